```python
import math
import jax
import jax.numpy as jnp
from jax import lax
import numpy as np

D_MODEL = 1024
BATCH = 32
SEQ = 256
DEPTH = 2
DEC_BATCH = 2
DEC_SEQ = 4096
PAST_LEN = 256

GRID_W = 64
EPS = 1e-6
ROPE_THETA = 10000.0
CHUNK = 128
Q_BLOCK = 128
ATT_HEAD_DIM = 64
ATT_WIDTH = D_MODEL // 2
ATT_HEADS = ATT_WIDTH // ATT_HEAD_DIM
ATT_KV_HEADS = ATT_HEADS // 4
ATT_KV_WIDTH = ATT_KV_HEADS * ATT_HEAD_DIM
SSD_WIDTH = D_MODEL // 2
SSD_HEAD_DIM = 64
SSD_HEADS = SSD_WIDTH // SSD_HEAD_DIM
SSD_GROUPS = 2
SSD_STATE = 64
SSD_CONV = 5
SSD_XBC = SSD_WIDTH + 2 * SSD_GROUPS * SSD_STATE
L0_IN = ATT_WIDTH + 2 * ATT_KV_WIDTH + SSD_WIDTH + SSD_XBC + 2 * SSD_HEADS
RET_HEADS = 4
RET_QK_WIDTH = D_MODEL
RET_V_WIDTH = 2 * D_MODEL
RET_DK = RET_QK_WIDTH // RET_HEADS
RET_DV = RET_V_WIDTH // RET_HEADS
L1_IN = 2 * RET_QK_WIDTH + 2 * RET_V_WIDTH
D_FF = -(-8 * D_MODEL // (3 * 256)) * 256

kernel_name = "hybrid_ssd_gqa_retention_diffusion_step"

F32 = jnp.float32


def rmsnorm(x, w):
    xf = x.astype(F32)
    y = xf * lax.rsqrt(jnp.mean(xf * xf, axis=-1, keepdims=True) + EPS)
    return (y * w.astype(F32)).astype(x.dtype)


def axial_rope(n_tokens, dim):
    rows = n_tokens // GRID_W
    row = jnp.repeat(jnp.arange(rows), GRID_W).astype(F32)
    col = jnp.tile(jnp.arange(GRID_W), rows).astype(F32)
    n_freq = dim // 4
    inv = ROPE_THETA ** (-jnp.arange(n_freq, dtype=F32) / n_freq)
    ang = jnp.concatenate([row[:, None] * inv, col[:, None] * inv], axis=-1)
    return jnp.cos(ang)[:, None, :], jnp.sin(ang)[:, None, :]


def apply_rope(x, rope):
    cos, sin = rope[0].astype(x.dtype), rope[1].astype(x.dtype)
    x1, x2 = jnp.split(x, 2, axis=-1)
    return jnp.concatenate([x1 * cos - x2 * sin, x2 * cos + x1 * sin], axis=-1)


def block_attention(q, k, v):
    b, lq, nkv, g, d = q.shape
    nb = lq // Q_BLOCK
    qb = q.reshape(b, nb, Q_BLOCK, nkv, g, d).swapaxes(0, 1)
    scale = d ** -0.5

    def one(qblk):
        s = jnp.einsum("bqkgd,btkd->bkgqt", qblk, k).astype(F32) * scale
        p = jax.nn.softmax(s, axis=-1).astype(v.dtype)
        return jnp.einsum("bkgqt,btkd->bqkgd", p, v)

    o = lax.map(one, qb)
    return o.swapaxes(0, 1).reshape(b, lq, nkv * g * d)


def dwconv_centred(x, w, bias):
    kw, ch = w.shape
    y = lax.conv_general_dilated(x, w[:, None, :].astype(x.dtype), window_strides=(1,),
                                 padding=[(kw // 2, kw // 2)],
                                 dimension_numbers=("NWC", "WIO", "NWC"),
                                 feature_group_count=ch)
    return y + bias


def chunked_decay_scan(q, k, v, log_a, s0):
    b, l, h, dk = q.shape
    dv = v.shape[-1]
    nc = l // CHUNK

    def to_chunks(t):
        return t.reshape((b, nc, CHUNK) + t.shape[2:]).swapaxes(0, 1)

    idx = jnp.arange(CHUNK)
    lower = idx[:, None] >= idx[None, :]

    def step(s, xs):
        qc, kc, vc, ac = xs
        cum = jnp.cumsum(ac.astype(F32), axis=1)
        cum_h = cum.transpose(0, 2, 1)
        decay = jnp.exp(jnp.where(lower, cum_h[..., :, None] - cum_h[..., None, :], -jnp.inf)).astype(qc.dtype)
        scores = jnp.einsum("bihd,bjhd->bhij", qc, kc) * decay
        y = (jnp.einsum("bhij,bjhe->bihe", scores, vc)
             + jnp.einsum("bihd,bhde->bihe", qc * jnp.exp(cum)[..., None].astype(qc.dtype), s))
        total = cum[:, -1]
        w_in = jnp.exp(total[:, None, :] - cum).astype(kc.dtype)
        s_new = (s * jnp.exp(total)[..., None, None].astype(s.dtype)
                 + jnp.einsum("bjhd,bjhe->bhde", kc * w_in[..., None], vc)).astype(s.dtype)
        return s_new, y

    s_fin, ys = lax.scan(step, s0, (to_chunks(q), to_chunks(k), to_chunks(v), to_chunks(log_a)))
    return ys.swapaxes(0, 1).reshape(b, l, h, dv), s_fin


def bidir_scan(q, k, v_f, v_b, a_f, a_b, s0):
    flip = lambda t: jnp.flip(t, axis=1)
    y_f, s_f = chunked_decay_scan(q, k, v_f, a_f, s0[:, 0])
    y_b, s_b = chunked_decay_scan(flip(q), flip(k), flip(v_b), flip(a_b), s0[:, 1])
    return y_f + flip(y_b), jnp.stack([s_f, s_b], axis=1)


def mixer_ab(h, w_in, w_out, q_gain, k_gain, conv_w, conv_b, dt_bias, a_log, d_skip, ssd_gain,
             s0, rope, cache_k, cache_v):
    b, l, _ = h.shape
    o1 = ATT_WIDTH
    o2 = o1 + ATT_KV_WIDTH
    o3 = o2 + ATT_KV_WIDTH
    o4 = o3 + SSD_WIDTH
    o5 = o4 + SSD_XBC
    q, k, v, z, xbc, dt = jnp.split(h @ w_in, [o1, o2, o3, o4, o5], axis=-1)
    q = rmsnorm(q.reshape(b, l, ATT_HEADS, ATT_HEAD_DIM), q_gain)
    k = rmsnorm(k.reshape(b, l, ATT_KV_HEADS, ATT_HEAD_DIM), k_gain)
    v = v.reshape(b, l, ATT_KV_HEADS, ATT_HEAD_DIM)
    k_ctx, v_ctx = k, v
    if rope is None:
        k_all, v_all = k, v
    else:
        q = apply_rope(q, rope)
        k_all = jnp.concatenate([cache_k, apply_rope(k, rope)], axis=1)
        v_all = jnp.concatenate([cache_v, v], axis=1)
    att = block_attention(q.reshape(b, l, ATT_KV_HEADS, ATT_HEADS // ATT_KV_HEADS, ATT_HEAD_DIM), k_all, v_all)
    xbc = jax.nn.silu(dwconv_centred(xbc, conv_w, conv_b))
    xs, bm, cm = jnp.split(xbc, [SSD_WIDTH, SSD_WIDTH + SSD_GROUPS * SSD_STATE], axis=-1)
    xs = xs.reshape(b, l, SSD_HEADS, SSD_HEAD_DIM)
    rep = SSD_HEADS // SSD_GROUPS
    bm = jnp.repeat(bm.reshape(b, l, SSD_GROUPS, SSD_STATE), rep, axis=2)
    cm = jnp.repeat(cm.reshape(b, l, SSD_GROUPS, SSD_STATE), rep, axis=2)
    dt = jax.nn.softplus(dt.reshape(b, l, 2, SSD_HEADS) + dt_bias)
    a = dt * (-jnp.exp(a_log))
    y, s_fin = bidir_scan(cm, bm, xs * dt[:, :, 0, :, None], xs * dt[:, :, 1, :, None],
                          a[:, :, 0], a[:, :, 1], s0)
    y = y + d_skip[:, None] * xs
    y = rmsnorm(y.reshape(b, l, SSD_WIDTH) * jax.nn.silu(z), ssd_gain)
    out = jnp.concatenate([att, y], axis=-1) @ w_out
    return out, k_ctx, v_ctx, s_fin


def mixer_c(h, w_in, w_out, decay, ret_gain, s0, rope):
    b, l, _ = h.shape
    q, k, v, g = jnp.split(h @ w_in, [RET_QK_WIDTH, 2 * RET_QK_WIDTH, 2 * RET_QK_WIDTH + RET_V_WIDTH], axis=-1)
    q = q.reshape(b, l, RET_HEADS, RET_DK)
    k = k.reshape(b, l, RET_HEADS, RET_DK) * (RET_DK ** -0.5)
    v = v.reshape(b, l, RET_HEADS, RET_DV)
    if rope is not None:
        q = apply_rope(q, rope)
        k = apply_rope(k, rope)
    log_g = (-jnp.exp(decay)).astype(h.dtype)
    a_f = jnp.broadcast_to(log_g[0], (b, l, RET_HEADS))
    a_b = jnp.broadcast_to(log_g[1], (b, l, RET_HEADS))
    y, s_fin = bidir_scan(q, k, v, v, a_f, a_b, s0)
    y = rmsnorm(y, ret_gain.reshape(RET_HEADS, RET_DV))
    out = (jax.nn.silu(g) * y.reshape(b, l, RET_V_WIDTH)) @ w_out
    return out, s_fin


def swiglu(h, w_gate, w_up, w_down):
    return (jax.nn.silu(h @ w_gate) * (h @ w_up)) @ w_down


def setup_inputs(seed: int = 0) -> dict:
    key = jax.random.key(seed)
    ks = iter(jax.random.split(key, 64))
    d = D_MODEL

    def nrm(shape, scale=1.0):
        return jax.random.normal(next(ks), shape, F32) * scale

    def gain(n):
        return 1.0 + nrm((n,), 0.02)

    dt0 = jnp.exp(jax.random.uniform(next(ks), (2, SSD_HEADS), F32, math.log(1e-3), math.log(1e-1)))
    ret_base = jnp.log(-jnp.log1p(-(2.0 ** (-5.0 - jnp.arange(RET_HEADS, dtype=F32)))))
    inp = {}
    inp["x_prompt"] = nrm((BATCH, SEQ, d))
    inp["x_sample"] = nrm((DEC_BATCH, DEC_SEQ, d))
    inp["c"] = nrm((DEC_BATCH, d))
    inp["cache_k0"] = nrm((DEC_BATCH, PAST_LEN, ATT_KV_HEADS, ATT_HEAD_DIM))
    inp["cache_v0"] = nrm((DEC_BATCH, PAST_LEN, ATT_KV_HEADS, ATT_HEAD_DIM))
    inp["state_ssd0"] = nrm((DEC_BATCH, 2, SSD_HEADS, SSD_STATE, SSD_HEAD_DIM), 0.5)
    inp["state_ret1"] = nrm((DEC_BATCH, 2, RET_HEADS, RET_DK, RET_DV), 0.5)
    inp["c_ctx"] = nrm((d,))
    inp["l0_w_ada"] = nrm((d, 6 * d), d ** -0.5)
    inp["l0_b_ada"] = nrm((6 * d,), 0.02)
    inp["l0_norm_mix"] = gain(d)
    inp["l0_norm_ffn"] = gain(d)
    inp["l0_w_in"] = nrm((d, L0_IN), d ** -0.5)
    inp["l0_w_out"] = nrm((ATT_WIDTH + SSD_WIDTH, d), (ATT_WIDTH + SSD_WIDTH) ** -0.5)
    inp["l0_q_gain"] = gain(ATT_HEAD_DIM)
    inp["l0_k_gain"] = gain(ATT_HEAD_DIM)
    inp["l0_conv_w"] = nrm((SSD_CONV, SSD_XBC), SSD_CONV ** -0.5)
    inp["l0_conv_b"] = nrm((SSD_XBC,), 0.02)
    inp["l0_dt_bias"] = dt0 + jnp.log(-jnp.expm1(-dt0))
    inp["l0_a_log"] = jnp.log(jax.random.uniform(next(ks), (2, SSD_HEADS), F32, 1.0, 16.0))
    inp["l0_d_skip"] = 1.0 + nrm((SSD_HEADS,), 0.1)
    inp["l0_ssd_gain"] = gain(SSD_WIDTH)
    inp["l0_w_gate"] = nrm((d, D_FF), d ** -0.5)
    inp["l0_w_up"] = nrm((d, D_FF), d ** -0.5)
    inp["l0_w_down"] = nrm((D_FF, d), D_FF ** -0.5)
    inp["l1_w_ada"] = nrm((d, 6 * d), d ** -0.5)
    inp["l1_b_ada"] = nrm((6 * d,), 0.02)
    inp["l1_norm_mix"] = gain(d)
    inp["l1_norm_ffn"] = gain(d)
    inp["l1_w_in"] = nrm((d, L1_IN), d ** -0.5)
    inp["l1_w_out"] = nrm((RET_V_WIDTH, d), RET_V_WIDTH ** -0.5)
    inp["l1_decay"] = ret_base[None, :] + nrm((2, RET_HEADS), 0.1)
    inp["l1_ret_gain"] = gain(RET_V_WIDTH)
    inp["l1_w_gate"] = nrm((d, D_FF), d ** -0.5)
    inp["l1_w_up"] = nrm((d, D_FF), d ** -0.5)
    inp["l1_w_down"] = nrm((D_FF, d), D_FF ** -0.5)
    inp["final_norm"] = gain(d)
    return inp


def reference(x_prompt, x_sample, c, cache_k0, cache_v0, state_ssd0, state_ret1, c_ctx,
              l0_w_ada, l0_b_ada, l0_norm_mix, l0_norm_ffn, l0_w_in, l0_w_out, l0_q_gain, l0_k_gain,
              l0_conv_w, l0_conv_b, l0_dt_bias, l0_a_log, l0_d_skip, l0_ssd_gain,
              l0_w_gate, l0_w_up, l0_w_down,
              l1_w_ada, l1_b_ada, l1_norm_mix, l1_norm_ffn, l1_w_in, l1_w_out, l1_decay, l1_ret_gain,
              l1_w_gate, l1_w_up, l1_w_down, final_norm):
    layers = (
        dict(w_ada=l0_w_ada, b_ada=l0_b_ada, norm_mix=l0_norm_mix, norm_ffn=l0_norm_ffn,
             w_gate=l0_w_gate, w_up=l0_w_up, w_down=l0_w_down),
        dict(w_ada=l1_w_ada, b_ada=l1_b_ada, norm_mix=l1_norm_mix, norm_ffn=l1_norm_ffn,
             w_gate=l1_w_gate, w_up=l1_w_up, w_down=l1_w_down),
    )

    def trunk(x, cond, rope_att, rope_ret, caches):
        b = x.shape[0]
        ctx_out = []
        for i in range(DEPTH):
            p = layers[i]
            mod = (jax.nn.silu(cond) @ p["w_ada"] + p["b_ada"])[:, None, :]
            sh1, sc1, g1, sh2, sc2, g2 = jnp.split(mod, 6, axis=-1)
            h = rmsnorm(x, p["norm_mix"]) * (1.0 + sc1) + sh1
            if i % 2 == 0:
                if caches is None:
                    ck, cv = None, None
                    s0 = jnp.zeros((b, 2, SSD_HEADS, SSD_STATE, SSD_HEAD_DIM), x.dtype)
                else:
                    ck, cv, s0 = caches[i]
                mix, k_ctx, v_ctx, s_fin = mixer_ab(h, l0_w_in, l0_w_out, l0_q_gain, l0_k_gain, l0_conv_w,
                                                    l0_conv_b, l0_dt_bias, l0_a_log, l0_d_skip, l0_ssd_gain,
                                                    s0, rope_att, ck, cv)
                ctx_out += [k_ctx, v_ctx, s_fin]
            else:
                if caches is None:
                    s0 = jnp.zeros((b, 2, RET_HEADS, RET_DK, RET_DV), x.dtype)
                else:
                    s0 = caches[i][0]
                mix, s_fin = mixer_c(h, l1_w_in, l1_w_out, l1_decay, l1_ret_gain, s0, rope_ret)
                ctx_out += [s_fin]
            x = x + g1 * mix
            h = rmsnorm(x, p["norm_ffn"]) * (1.0 + sc2) + sh2
            x = x + g2 * swiglu(h, p["w_gate"], p["w_up"], p["w_down"])
        return rmsnorm(x, final_norm), ctx_out

    y_prompt, ctx = trunk(x_prompt, c_ctx[None, :], None, None, None)
    new_k0, new_v0, new_ssd0, new_ret1 = ctx
    n_lat = x_sample.shape[1]
    rope_att = axial_rope(n_lat, ATT_HEAD_DIM)
    rope_ret = axial_rope(n_lat, RET_DK)
    caches = ((cache_k0, cache_v0, state_ssd0), (state_ret1,))
    y_sample, _ = trunk(x_sample, c, rope_att, rope_ret, caches)
    return (y_prompt, y_sample, new_k0, new_v0, new_ssd0, new_ret1)
```

```python
import functools

import jax
import jax.numpy as jnp
from jax import lax
from jax.experimental import pallas as pl
from jax.experimental.pallas import tpu as pltpu

F32 = jnp.float32
BF16 = jnp.bfloat16

EPS = 1e-6
ROPE_THETA = 10000.0
GRID_W = 64
D_MODEL = 1024
ATT_HEAD_DIM = 64
ATT_HEADS = 8
ATT_KV_HEADS = 2
ATT_GROUP = ATT_HEADS // ATT_KV_HEADS
ATT_WIDTH = ATT_HEADS * ATT_HEAD_DIM
ATT_KV_WIDTH = ATT_KV_HEADS * ATT_HEAD_DIM
SSD_WIDTH = 512
SSD_HEADS = 8
SSD_HEAD_DIM = 64
SSD_STATE = 64
SSD_GROUPS = 2
SSD_CONV = 5
SSD_XBC = SSD_WIDTH + 2 * SSD_GROUPS * SSD_STATE
L0_IN = ATT_WIDTH + 2 * ATT_KV_WIDTH + SSD_WIDTH + SSD_XBC + 2 * SSD_HEADS
RET_HEADS = 4
RET_DK = 256
RET_DV = 512
RET_QK_WIDTH = RET_HEADS * RET_DK
RET_V_WIDTH = RET_HEADS * RET_DV
D_FF = 2816

LANES = 128
HALO = 8
L0_IN_PAD = -(-L0_IN // LANES) * LANES
SSD_CHUNK = 128
FF_SPLIT = 2
VMEM_LIMIT = 56 * 1024 * 1024


def _dot(a, b):
    return jnp.dot(a, b, preferred_element_type=F32)


def _dot_nt(a, b):
    return lax.dot_general(a, b, (((1,), (1,)), ((), ())), preferred_element_type=F32)


def _dot_tn(a, b):
    return lax.dot_general(a, b, (((0,), (0,)), ((), ())), preferred_element_type=F32)


def _silu(x):
    return x / (1.0 + jnp.exp(-x))


def _softplus(x):
    return jnp.maximum(x, 0.0) + jnp.log1p(jnp.exp(-jnp.abs(x)))


def _rms(x):
    return x * lax.rsqrt(jnp.mean(x * x, axis=-1, keepdims=True) + EPS)


def _split3(x):
    hi = x.astype(BF16)
    r = x - hi.astype(F32)
    mid = r.astype(BF16)
    lo = (r - mid.astype(F32)).astype(BF16)
    return hi, mid, lo


def _const_spec(shape):
    return pl.BlockSpec(shape, lambda *_: (0,) * len(shape))


def _resident_spec(shape):
    return pl.BlockSpec(shape, lambda *_: (0,) * len(shape), pipeline_mode=pl.Buffered(1))


def _params(n_axes, vmem=VMEM_LIMIT):
    return pltpu.CompilerParams(dimension_semantics=("arbitrary",) * n_axes, vmem_limit_bytes=vmem)


def _ada_kernel(c_ref, w_ref, b_ref, o_ref):
    s = _silu(c_ref[...])
    o_ref[...] = _dot(s.astype(BF16), w_ref[...].astype(BF16)) + b_ref[...]


def _ada(conds, w, b):
    n = w.shape[1]
    tn = 1536
    return pl.pallas_call(
        _ada_kernel,
        grid=(n // tn,),
        in_specs=[_const_spec(conds.shape),
                  pl.BlockSpec((D_MODEL, tn), lambda j: (0, j)),
                  pl.BlockSpec((1, tn), lambda j: (0, j))],
        out_specs=pl.BlockSpec((conds.shape[0], tn), lambda j: (0, j)),
        out_shape=jax.ShapeDtypeStruct((conds.shape[0], n), F32),
        compiler_params=_params(1),
        name="ada",
    )(conds, w, b.reshape(1, n))


def _mod_specs(tm, rows_per_mod, which):
    return [pl.BlockSpec((1, 1, D_MODEL), lambda i, j=j: ((i * tm) // rows_per_mod, 0, j)) for j in which]


def _head_rms(x, p_ref, gain):
    x2 = x * x
    hi = x2.astype(BF16)
    lo = (x2 - hi.astype(F32)).astype(BF16)
    ms = _dot(hi, p_ref[...]) + _dot(lo, p_ref[...])
    return x * lax.rsqrt(ms + EPS) * gain


def _rope64(x, cos, sin):
    n = x.shape[1]
    lane = lax.broadcasted_iota(jnp.int32, x.shape, 1)
    first_half = (lane % ATT_HEAD_DIM) < (ATT_HEAD_DIM // 2)
    partner = jnp.where(first_half, pltpu.roll(x, n - ATT_HEAD_DIM // 2, 1), pltpu.roll(x, ATT_HEAD_DIM // 2, 1))
    return x * cos + partner * sin


def _l0_in_kernel(*refs, use_rope):
    if use_rope:
        (x_ref, nw_ref, sh_ref, sc_ref, w_ref, qg_ref, kg_ref, pq_ref, pk_ref, cos_ref, sin_ref,
         q_ref, ka_ref, va_ref, kf_ref, vf_ref, z_ref, xbc_ref, dt_ref) = refs
    else:
        (x_ref, nw_ref, sh_ref, sc_ref, w_ref, qg_ref, kg_ref, pq_ref, pk_ref,
         q_ref, ka_ref, va_ref, kf_ref, vf_ref, z_ref, xbc_ref, dt_ref) = refs
    h = _rms(x_ref[...]) * nw_ref[...] * (1.0 + sc_ref[0]) + sh_ref[0]
    proj = _dot(h.astype(BF16), w_ref[...])
    o1 = ATT_WIDTH
    o2 = o1 + ATT_KV_WIDTH
    o3 = o2 + ATT_KV_WIDTH
    o4 = o3 + SSD_WIDTH
    o5 = o4 + SSD_XBC
    q = _head_rms(proj[:, :o1], pq_ref, qg_ref[...])
    k = _head_rms(proj[:, o1:o2], pk_ref, kg_ref[...])
    v = proj[:, o2:o3]
    kf_ref[...] = k
    vf_ref[...] = v
    if use_rope:
        cos = cos_ref[...]
        sin = sin_ref[...]
        k = _rope64(k, cos, sin)
        reps = ATT_WIDTH // LANES
        q = _rope64(q, jnp.concatenate([cos] * reps, axis=1), jnp.concatenate([sin] * reps, axis=1))
    q_ref[...] = (q * (ATT_HEAD_DIM ** -0.5)).astype(BF16)
    ka_ref[...] = k.astype(BF16)
    va_ref[...] = v.astype(BF16)
    z_ref[...] = proj[:, o3:o4]
    xbc_ref[...] = proj[:, o4:o5]
    dt_ref[...] = proj[:, o5:]


def _l0_in(x, mod, rows_per_mod, p, rope, tm):
    m = x.shape[0]
    use_rope = rope is not None
    row = lambda w: pl.BlockSpec((tm, w), lambda i: (i, 0))
    in_specs = ([row(D_MODEL), _const_spec((1, D_MODEL))] + _mod_specs(tm, rows_per_mod, (0, 1))
                + [_resident_spec((D_MODEL, L0_IN_PAD)), _const_spec((1, ATT_WIDTH)), _const_spec((1, ATT_KV_WIDTH)),
                   _resident_spec((ATT_WIDTH, ATT_WIDTH)), _resident_spec((ATT_KV_WIDTH, ATT_KV_WIDTH))])
    args = [x, p["norm_mix"], mod, mod, p["w_in"], p["q_gain"], p["k_gain"], p["pq"], p["pk"]]
    if use_rope:
        rows = rope[0].shape[0]
        in_specs += [pl.BlockSpec((tm, LANES), lambda i: (i % (rows // tm), 0))] * 2
        args += list(rope)
    widths = [(ATT_WIDTH, BF16), (ATT_KV_WIDTH, BF16), (ATT_KV_WIDTH, BF16), (ATT_KV_WIDTH, F32), (ATT_KV_WIDTH, F32),
              (SSD_WIDTH, F32), (SSD_XBC, F32), (LANES, F32)]
    return pl.pallas_call(
        functools.partial(_l0_in_kernel, use_rope=use_rope),
        grid=(m // tm,),
        in_specs=in_specs,
        out_specs=[row(w) for w, _ in widths],
        out_shape=[jax.ShapeDtypeStruct((m, w), dt) for w, dt in widths],
        compiler_params=_params(1),
        name="l0_in",
    )(*args)


def _attn_kernel(q_ref, k_ref, v_ref, o_ref, *, tk, nk):
    q = q_ref[0, 0, 0]

    def tile(t):
        k = k_ref[0, 0, pl.ds(t * tk, tk), :]
        v = v_ref[0, 0, pl.ds(t * tk, tk), :]
        return _dot_nt(q, k), v

    s, v = tile(0)
    m = jnp.max(s, axis=1, keepdims=True)
    p = jnp.exp(s - m)
    l = jnp.sum(p, axis=1, keepdims=True)
    acc = _dot(p.astype(BF16), v)

    def body(t, carry):
        m, l, acc = carry
        s, v = tile(t)
        m_new = jnp.maximum(m, jnp.max(s, axis=1, keepdims=True))
        alpha = jnp.exp(m - m_new)
        p = jnp.exp(s - m_new)
        l = alpha * l + jnp.sum(p, axis=1, keepdims=True)
        acc = alpha * acc + _dot(p.astype(BF16), v)
        return m_new, l, acc

    if nk > 1:
        m, l, acc = lax.fori_loop(1, nk, body, (m, l, acc))
    o_ref[0, 0, 0] = (acc / l).astype(o_ref.dtype)


def _attention(q, k, v, tq, tk):
    b, nkv, nq, rows, d = q.shape
    lk = k.shape[2]
    return pl.pallas_call(
        functools.partial(_attn_kernel, tk=tk, nk=lk // tk),
        grid=(b, nkv, nq),
        in_specs=[pl.BlockSpec((1, 1, 1, rows, d), lambda i, j, t: (i, j, t, 0, 0)),
                  pl.BlockSpec((1, 1, lk, d), lambda i, j, t: (i, j, 0, 0)),
                  pl.BlockSpec((1, 1, lk, d), lambda i, j, t: (i, j, 0, 0))],
        out_specs=pl.BlockSpec((1, 1, 1, rows, d), lambda i, j, t: (i, j, t, 0, 0)),
        out_shape=jax.ShapeDtypeStruct(q.shape, BF16),
        compiler_params=_params(3),
        name="attention",
    )(q, k, v)


def _ssd_group_rows(pair):
    g = pair // (SSD_HEADS // 2 // SSD_GROUPS)
    return slice(g * SSD_STATE, (g + 1) * SSD_STATE)


def _ssd_stream(cur_ref, prev_ref, next_ref, dt_ref, is_first, is_last, s_ref, y_ref, ext_ref,
                cw_ref, cb_ref, dtb_ref, alog_ref, dsk_ref, *, forward):
    cn = SSD_CHUNK
    ext_ref[0:HALO, :] = jnp.where(is_first, 0.0, prev_ref[...])
    ext_ref[HALO:HALO + cn, :] = cur_ref[...]
    ext_ref[HALO + cn:, :] = jnp.where(is_last, 0.0, next_ref[...])
    u = cb_ref[...]
    for t in range(SSD_CONV):
        u = u + cw_ref[t:t + 1, :] * ext_ref[pl.ds(HALO - SSD_CONV // 2 + t, cn), :]
    act = _silu(u)
    xs = act[:, :SSD_WIDTH]
    bm = act[:, SSD_WIDTH:SSD_WIDTH + LANES]
    cm = act[:, SSD_WIDTH + LANES:]
    dt = _softplus(dt_ref[...] + dtb_ref[...])
    a = dt * (-jnp.exp(alog_ref[...]))
    ii = lax.broadcasted_iota(jnp.int32, (cn, cn), 0)
    jj = lax.broadcasted_iota(jnp.int32, (cn, cn), 1)
    keep = (ii >= jj) if forward else (ii <= jj)
    tri = keep.astype(BF16)
    tri_t = ((ii <= jj) if forward else (ii >= jj)).astype(BF16)
    a3 = jnp.concatenate(_split3(a), axis=1)
    c3 = _dot(tri, a3)
    cum = c3[:, :LANES] + c3[:, LANES:2 * LANES] + c3[:, 2 * LANES:]
    ct3 = _dot_tn(a3, tri_t)
    cum_t = ct3[:LANES] + ct3[LANES:2 * LANES] + ct3[2 * LANES:]
    off = 0 if forward else SSD_HEADS
    tot_col = cn - 1 if forward else 0
    lane = lax.broadcasted_iota(jnp.int32, (1, LANES), 1)
    lo = lane < SSD_HEAD_DIM
    cm_b = cm.astype(BF16)
    bm_b = bm.astype(BF16)
    gmat = {}
    for pair in range(SSD_HEADS // 2):
        g = pair // (SSD_HEADS // 2 // SSD_GROUPS)
        in_group = lo if g == 0 else jnp.logical_not(lo)
        if g not in gmat:
            gmat[g] = _dot_nt(jnp.where(in_group, cm, 0.0).astype(BF16), bm_b)
        bg = jnp.where(in_group, bm, 0.0)
        h0 = off + 2 * pair
        h1 = h0 + 1
        ci0 = cum[:, h0:h0 + 1]
        ci1 = cum[:, h1:h1 + 1]
        m0 = (gmat[g] * jnp.exp(jnp.where(keep, ci0 - cum_t[h0:h0 + 1, :], -1e30))).astype(BF16)
        m1 = (gmat[g] * jnp.exp(jnp.where(keep, ci1 - cum_t[h1:h1 + 1, :], -1e30))).astype(BF16)
        xs_p = xs[:, pair * LANES:(pair + 1) * LANES]
        vp = xs_p * jnp.where(lo, dt[:, h0:h0 + 1], dt[:, h1:h1 + 1])
        v0 = jnp.where(lo, vp, 0.0).astype(BF16)
        v1 = jnp.where(lo, 0.0, vp).astype(BF16)
        s_old = s_ref[pair]
        y = _dot(m0, v0) + _dot(m1, v1) + jnp.exp(jnp.where(lo, ci0, ci1)) * _dot(cm_b, s_old.astype(BF16))
        tot0 = cum_t[h0:h0 + 1, tot_col:tot_col + 1]
        tot1 = cum_t[h1:h1 + 1, tot_col:tot_col + 1]
        bw0 = (bg * jnp.exp(tot0 - ci0)).astype(BF16)
        bw1 = (bg * jnp.exp(tot1 - ci1)).astype(BF16)
        s_ref[pair] = s_old * jnp.exp(jnp.where(lo, tot0, tot1)) + _dot_tn(bw0, v0) + _dot_tn(bw1, v1)
        if forward:
            y = y + dsk_ref[:, pair * LANES:(pair + 1) * LANES] * xs_p
        y_ref[:, pair * LANES:(pair + 1) * LANES] = y


def _ssd_kernel(*refs, nc, has_init, want_fin):
    refs = list(refs)
    cf_ref, pf_ref, nf_ref, dtf_ref, cb_ref, pb_ref, nb_ref, dtb_ref = refs[:8]
    cw_ref, cbias_ref, dtbias_ref, alog_ref, dsk_ref = refs[8:13]
    pos = 13
    s0_ref = None
    if has_init:
        s0_ref = refs[pos]
        pos += 1
    yf_ref, yb_ref = refs[pos:pos + 2]
    pos += 2
    sfin_ref = None
    if want_fin:
        sfin_ref = refs[pos]
        pos += 1
    ext_ref, sf_ref, sb_ref = refs[pos:]
    c = pl.program_id(1)

    @pl.when(c == 0)
    def _():
        sf_ref[...] = jnp.zeros_like(sf_ref)
        sb_ref[...] = jnp.zeros_like(sb_ref)
        if has_init:
            for pair in range(SSD_HEADS // 2):
                rows = _ssd_group_rows(pair)
                sf_ref[pair, rows, :] = s0_ref[0, 0, pair]
                sb_ref[pair, rows, :] = s0_ref[0, 1, pair]

    shared = (cw_ref, cbias_ref, dtbias_ref, alog_ref, dsk_ref)
    _ssd_stream(cf_ref, pf_ref, nf_ref, dtf_ref, c == 0, c == nc - 1, sf_ref, yf_ref, ext_ref, *shared,
                forward=True)
    _ssd_stream(cb_ref, pb_ref, nb_ref, dtb_ref, c == nc - 1, c == 0, sb_ref, yb_ref, ext_ref, *shared,
                forward=False)

    if want_fin:
        @pl.when(c == nc - 1)
        def _():
            for pair in range(SSD_HEADS // 2):
                rows = _ssd_group_rows(pair)
                sfin_ref[0, 0, pair] = sf_ref[pair, rows, :]
                sfin_ref[0, 1, pair] = sb_ref[pair, rows, :]


def _ssd(xbc, dt, p, s0, batch, want_fin):
    m = xbc.shape[0]
    cn = SSD_CHUNK
    nc = m // batch // cn
    per = cn // HALO
    n_halo = m // HALO
    has_init = s0 is not None
    npair = SSD_HEADS // 2

    def fwd(b, c):
        return b * nc + c

    def bwd(b, c):
        return b * nc + nc - 1 - c

    def stream_specs(chunk):
        return [pl.BlockSpec((cn, SSD_XBC), lambda b, c: (chunk(b, c), 0)),
                pl.BlockSpec((HALO, SSD_XBC), lambda b, c: (jnp.maximum(chunk(b, c) * per - 1, 0), 0)),
                pl.BlockSpec((HALO, SSD_XBC), lambda b, c: (jnp.minimum(chunk(b, c) * per + per, n_halo - 1), 0)),
                pl.BlockSpec((cn, LANES), lambda b, c: (chunk(b, c), 0))]

    in_specs = stream_specs(fwd) + stream_specs(bwd) + [
        _const_spec((HALO, SSD_XBC)), _const_spec((1, SSD_XBC)), _const_spec((1, LANES)), _const_spec((1, LANES)),
        _const_spec((1, SSD_WIDTH))]
    args = [xbc, xbc, xbc, dt, xbc, xbc, xbc, dt, p["conv_w"], p["conv_b"], p["dt_bias"], p["a_log"], p["d_skip"]]
    state_block = (1, 2, npair, SSD_STATE, LANES)
    if has_init:
        in_specs.append(pl.BlockSpec(state_block, lambda b, c: (b, 0, 0, 0, 0)))
        args.append(s0)
    out_specs = [pl.BlockSpec((cn, SSD_WIDTH), lambda b, c: (fwd(b, c), 0)),
                 pl.BlockSpec((cn, SSD_WIDTH), lambda b, c: (bwd(b, c), 0))]
    out_shape = [jax.ShapeDtypeStruct((m, SSD_WIDTH), F32)] * 2
    if want_fin:
        out_specs.append(pl.BlockSpec(state_block, lambda b, c: (b, 0, 0, 0, 0)))
        out_shape.append(jax.ShapeDtypeStruct((batch,) + state_block[1:], F32))
    return pl.pallas_call(
        functools.partial(_ssd_kernel, nc=nc, has_init=has_init, want_fin=want_fin),
        grid=(batch, nc),
        in_specs=in_specs,
        out_specs=out_specs,
        out_shape=out_shape,
        scratch_shapes=[pltpu.VMEM((cn + 2 * HALO, SSD_XBC), F32),
                        pltpu.VMEM((npair, LANES, LANES), F32), pltpu.VMEM((npair, LANES, LANES), F32)],
        compiler_params=_params(2),
        name="ssd",
    )(*args)


def _ssd_state_to_pairs(s):
    b = s.shape[0]
    npair = SSD_HEADS // 2
    s = s.reshape(b, 2, npair, 2, SSD_STATE, SSD_HEAD_DIM).transpose(0, 1, 2, 4, 3, 5)
    return s.reshape(b, 2, npair, SSD_STATE, 2 * SSD_HEAD_DIM)


def _ssd_state_from_pairs(s):
    b = s.shape[0]
    npair = SSD_HEADS // 2
    s = s.reshape(b, 2, npair, SSD_STATE, 2, SSD_HEAD_DIM).transpose(0, 1, 2, 4, 3, 5)
    return s.reshape(b, 2, SSD_HEADS, SSD_STATE, SSD_HEAD_DIM)


def _l1_in_kernel(*refs, use_rope):
    if use_rope:
        x_ref, nw_ref, sh_ref, sc_ref, w_ref, cos_ref, sin_ref, q_ref, k_ref, v_ref, g_ref = refs
    else:
        x_ref, nw_ref, sh_ref, sc_ref, w_ref, q_ref, k_ref, v_ref, g_ref = refs
    h = (_rms(x_ref[...]) * nw_ref[...] * (1.0 + sc_ref[0]) + sh_ref[0]).astype(BF16)
    half = RET_DK // 2

    def rope(t):
        if not use_rope:
            return t
        cos = cos_ref[...]
        sin = sin_ref[...]
        parts = []
        for hd in range(RET_HEADS):
            x1 = t[:, hd * RET_DK:hd * RET_DK + half]
            x2 = t[:, hd * RET_DK + half:(hd + 1) * RET_DK]
            parts += [x1 * cos - x2 * sin, x2 * cos + x1 * sin]
        return jnp.concatenate(parts, axis=1)

    o1 = RET_QK_WIDTH
    o2 = 2 * RET_QK_WIDTH
    o3 = o2 + RET_V_WIDTH
    q_ref[...] = rope(_dot(h, w_ref[:, :o1])).astype(BF16)
    k_ref[...] = rope(_dot(h, w_ref[:, o1:o2])) * (RET_DK ** -0.5)
    v_ref[...] = _dot(h, w_ref[:, o2:o3]).astype(BF16)
    g_ref[...] = _dot(h, w_ref[:, o3:])


def _l1_in(x, mod, rows_per_mod, p, rope, tm):
    m = x.shape[0]
    use_rope = rope is not None
    row = lambda w: pl.BlockSpec((tm, w), lambda i: (i, 0))
    n = 2 * RET_QK_WIDTH + 2 * RET_V_WIDTH
    in_specs = ([row(D_MODEL), _const_spec((1, D_MODEL))] + _mod_specs(tm, rows_per_mod, (0, 1))
                + [_resident_spec((D_MODEL, n))])
    args = [x, p["norm_mix"], mod, mod, p["w_in"]]
    if use_rope:
        rows = rope[0].shape[0]
        in_specs += [pl.BlockSpec((tm, LANES), lambda i: (i % (rows // tm), 0))] * 2
        args += list(rope)
    widths = [(RET_QK_WIDTH, BF16), (RET_QK_WIDTH, F32), (RET_V_WIDTH, BF16), (RET_V_WIDTH, F32)]
    return pl.pallas_call(
        functools.partial(_l1_in_kernel, use_rope=use_rope),
        grid=(m // tm,),
        in_specs=in_specs,
        out_specs=[row(w) for w, _ in widths],
        out_shape=[jax.ShapeDtypeStruct((m, w), dt) for w, dt in widths],
        compiler_params=_params(1),
        name="l1_in",
    )(*args)


def _ret_kernel(*refs, cn, nc, has_init, want_fin):
    refs = list(refs)
    dec_ref, qf_ref, kf_ref, vf_ref, qb_ref, kb_ref, vb_ref = refs[:7]
    pos = 7
    s0_ref = None
    if has_init:
        s0_ref = refs[pos]
        pos += 1
    yf_ref, yb_ref = refs[pos:pos + 2]
    pos += 2
    sfin_ref = None
    if want_fin:
        sfin_ref = refs[pos]
        pos += 1
    sf_ref, sb_ref = refs[pos:]
    c = pl.program_id(1)
    stateless = (not has_init) and nc == 1

    if not stateless:
        @pl.when(c == 0)
        def _():
            if has_init:
                sf_ref[...] = s0_ref[0, 0]
                sb_ref[...] = s0_ref[0, 1]
            else:
                sf_ref[...] = jnp.zeros_like(sf_ref)
                sb_ref[...] = jnp.zeros_like(sb_ref)

    log_g = -jnp.exp(dec_ref[...])
    ii = lax.broadcasted_iota(jnp.int32, (cn, cn), 0)
    jj = lax.broadcasted_iota(jnp.int32, (cn, cn), 1)
    dist = (ii - jj).astype(F32)
    ri = lax.broadcasted_iota(jnp.int32, (cn, 1), 0).astype(F32)
    for hd in range(RET_HEADS):
        gf = log_g[0:1, hd:hd + 1]
        gb = log_g[1:2, hd:hd + 1]
        decay = (jnp.where(dist >= 0, jnp.exp(gf * jnp.maximum(dist, 0.0)), 0.0)
                 + jnp.where(dist <= 0, jnp.exp(gb * jnp.maximum(-dist, 0.0)), 0.0))
        qs = slice(hd * RET_DK, (hd + 1) * RET_DK)
        vs = slice(hd * RET_DV, (hd + 1) * RET_DV)
        q = qf_ref[:, qs]
        k = kf_ref[:, qs]
        v = vf_ref[:, vs]
        y = _dot((_dot_nt(q, k.astype(BF16)) * decay).astype(BF16), v)
        upd_f = _dot_tn((k * jnp.exp(gf * (cn - 1.0 - ri))).astype(BF16), v)
        if stateless:
            new_f = upd_f
        else:
            s_old = sf_ref[hd]
            y = y + jnp.exp(gf * (ri + 1.0)) * _dot(q, s_old.astype(BF16))
            new_f = s_old * jnp.exp(gf * cn) + upd_f
            sf_ref[hd] = new_f
        yf_ref[:, vs] = y
        q = qb_ref[:, qs]
        k = kb_ref[:, qs]
        v = vb_ref[:, vs]
        upd_b = _dot_tn((k * jnp.exp(gb * ri)).astype(BF16), v)
        if stateless:
            new_b = upd_b
            yb_ref[:, vs] = jnp.zeros((cn, RET_DV), F32)
        else:
            s_old = sb_ref[hd]
            yb_ref[:, vs] = jnp.exp(gb * (cn - ri)) * _dot(q, s_old.astype(BF16))
            new_b = s_old * jnp.exp(gb * cn) + upd_b
            sb_ref[hd] = new_b
        if want_fin:
            if stateless:
                sfin_ref[0, 0, hd] = new_f
                sfin_ref[0, 1, hd] = new_b
            else:
                @pl.when(c == nc - 1)
                def _(new_f=new_f, new_b=new_b, hd=hd):
                    sfin_ref[0, 0, hd] = new_f
                    sfin_ref[0, 1, hd] = new_b


def _retention(q, k, v, decay, s0, batch, cn, want_fin):
    m = q.shape[0]
    nc = m // batch // cn
    has_init = s0 is not None

    def fwd(b, c):
        return b * nc + c

    def bwd(b, c):
        return b * nc + nc - 1 - c

    def stream_specs(chunk):
        return [pl.BlockSpec((cn, RET_QK_WIDTH), lambda b, c: (chunk(b, c), 0)),
                pl.BlockSpec((cn, RET_QK_WIDTH), lambda b, c: (chunk(b, c), 0)),
                pl.BlockSpec((cn, RET_V_WIDTH), lambda b, c: (chunk(b, c), 0))]

    in_specs = [_const_spec((8, LANES))] + stream_specs(fwd) + stream_specs(bwd)
    args = [decay, q, k, v, q, k, v]
    state_block = (1, 2, RET_HEADS, RET_DK, RET_DV)
    if has_init:
        in_specs.append(pl.BlockSpec(state_block, lambda b, c: (b, 0, 0, 0, 0)))
        args.append(s0)
    out_specs = [pl.BlockSpec((cn, RET_V_WIDTH), lambda b, c: (fwd(b, c), 0)),
                 pl.BlockSpec((cn, RET_V_WIDTH), lambda b, c: (bwd(b, c), 0))]
    out_shape = [jax.ShapeDtypeStruct((m, RET_V_WIDTH), F32)] * 2
    if want_fin:
        out_specs.append(pl.BlockSpec(state_block, lambda b, c: (b, 0, 0, 0, 0)))
        out_shape.append(jax.ShapeDtypeStruct((batch,) + state_block[1:], F32))
    return pl.pallas_call(
        functools.partial(_ret_kernel, cn=cn, nc=nc, has_init=has_init, want_fin=want_fin),
        grid=(batch, nc),
        in_specs=in_specs,
        out_specs=out_specs,
        out_shape=out_shape,
        scratch_shapes=[pltpu.VMEM((RET_HEADS, RET_DK, RET_DV), F32)] * 2,
        compiler_params=_params(2),
        name="retention",
    )(*args)


def _post_kernel(*refs, mixer, final):
    refs = list(refs)
    x_ref, g1_ref, nw_ref, sh2_ref, sc2_ref, g2_ref, wout_ref, wg_ref, wu_ref, wd_ref = refs[:10]
    pos = 10
    fn_ref = None
    if final:
        fn_ref = refs[pos]
        pos += 1
    if mixer == "ab":
        att_ref, yf_ref, yb_ref, z_ref, gain_ref, o_ref = refs[pos:]
        y = (yf_ref[...] + yb_ref[...]) * _silu(z_ref[...])
        y = _rms(y) * gain_ref[...]
        mix = _dot(att_ref[...], wout_ref[:ATT_WIDTH, :]) + _dot(y.astype(BF16), wout_ref[ATT_WIDTH:, :])
    else:
        yf_ref, yb_ref, gate_ref, gain_ref, o_ref = refs[pos:]
        mix = None
        for hd in range(RET_HEADS):
            vs = slice(hd * RET_DV, (hd + 1) * RET_DV)
            y = _rms(yf_ref[:, vs] + yb_ref[:, vs]) * gain_ref[:, vs]
            part = _dot((_silu(gate_ref[:, vs]) * y).astype(BF16), wout_ref[vs, :])
            mix = part if mix is None else mix + part
    x1 = x_ref[...] + g1_ref[0] * mix
    h = (_rms(x1) * nw_ref[...] * (1.0 + sc2_ref[0]) + sh2_ref[0]).astype(BF16)
    wcols = D_FF // FF_SPLIT
    ffn = None
    for j in range(FF_SPLIT):
        cs = slice(j * wcols, (j + 1) * wcols)
        act = (_silu(_dot(h, wg_ref[:, cs])) * _dot(h, wu_ref[:, cs])).astype(BF16)
        part = _dot(act, wd_ref[cs, :])
        ffn = part if ffn is None else ffn + part
    x2 = x1 + g2_ref[0] * ffn
    if final:
        x2 = _rms(x2) * fn_ref[...]
    o_ref[...] = x2


def _post(x, mod, rows_per_mod, p, mixer, mix_inputs, gain, final_norm, tm):
    m = x.shape[0]
    row = lambda w: pl.BlockSpec((tm, w), lambda i: (i, 0))
    mixw = p["w_out"].shape[0]
    (g1,) = _mod_specs(tm, rows_per_mod, (2,))
    sh2, sc2, g2 = _mod_specs(tm, rows_per_mod, (3, 4, 5))
    in_specs = [row(D_MODEL), g1, _const_spec((1, D_MODEL)), sh2, sc2, g2,
                _resident_spec((mixw, D_MODEL)), _resident_spec((D_MODEL, D_FF)), _resident_spec((D_MODEL, D_FF)),
                _resident_spec((D_FF, D_MODEL))]
    args = [x, mod, p["norm_ffn"], mod, mod, mod, p["w_out"], p["w_gate"], p["w_up"], p["w_down"]]
    final = final_norm is not None
    if final:
        in_specs.append(_const_spec((1, D_MODEL)))
        args.append(final_norm)
    in_specs += [row(a.shape[1]) for a in mix_inputs] + [_const_spec(gain.shape)]
    args += list(mix_inputs) + [gain]
    return pl.pallas_call(
        functools.partial(_post_kernel, mixer=mixer, final=final),
        grid=(m // tm,),
        in_specs=in_specs,
        out_specs=row(D_MODEL),
        out_shape=jax.ShapeDtypeStruct((m, D_MODEL), F32),
        compiler_params=_params(1),
        name="post_" + mixer,
    )(*args)


def _axial_angles(n_tokens, dim):
    rows = n_tokens // GRID_W
    row = jnp.repeat(jnp.arange(rows), GRID_W).astype(F32)
    col = jnp.tile(jnp.arange(GRID_W), rows).astype(F32)
    n_freq = dim // 4
    inv = ROPE_THETA ** (-jnp.arange(n_freq, dtype=F32) / n_freq)
    return jnp.concatenate([row[:, None] * inv, col[:, None] * inv], axis=-1)


def _head_mean_matrix(width, head):
    idx = jnp.arange(width) // head
    return jnp.where(idx[:, None] == idx[None, :], 1.0 / head, 0.0).astype(BF16)


def _pad_lanes(v, width=LANES):
    v = v.reshape(1, -1)
    return jnp.pad(v, ((0, 0), (0, width - v.shape[1])))


def _trunk(x, mods, rows_per_mod, p0, p1, final_norm, rope_att, rope_ret, caches, seq):
    m = x.shape[0]
    batch = m // seq
    sample = caches is not None
    tm = 512
    q, ka, va, kf, vf, z, xbc, dt = _l0_in(x, mods[0], rows_per_mod, p0, rope_att, tm)
    tq = 128 if sample else seq
    nq = seq // tq
    qh = q.reshape(batch, nq, tq, ATT_KV_HEADS, ATT_GROUP, ATT_HEAD_DIM).transpose(0, 3, 1, 4, 2, 5)
    qh = qh.reshape(batch, ATT_KV_HEADS, nq, ATT_GROUP * tq, ATT_HEAD_DIM)
    kh = ka.reshape(batch, seq, ATT_KV_HEADS, ATT_HEAD_DIM).transpose(0, 2, 1, 3)
    vh = va.reshape(batch, seq, ATT_KV_HEADS, ATT_HEAD_DIM).transpose(0, 2, 1, 3)
    s0_ssd = None
    s0_ret = None
    if sample:
        cache_k, cache_v, state_ssd, state_ret = caches
        kh = jnp.concatenate([cache_k.astype(BF16).transpose(0, 2, 1, 3), kh], axis=2)
        vh = jnp.concatenate([cache_v.astype(BF16).transpose(0, 2, 1, 3), vh], axis=2)
        s0_ssd = _ssd_state_to_pairs(state_ssd)
        s0_ret = state_ret
    att = _attention(qh, kh, vh, tq, 256)
    att = att.reshape(batch, ATT_KV_HEADS, nq, ATT_GROUP, tq, ATT_HEAD_DIM).transpose(0, 2, 4, 1, 3, 5)
    att = att.reshape(m, ATT_WIDTH)
    ssd_out = _ssd(xbc, dt, p0, s0_ssd, batch, want_fin=not sample)
    x = _post(x, mods[0], rows_per_mod, p0, "ab", [att, ssd_out[0], ssd_out[1], z], p0["ssd_gain"], None, tm)
    q1, k1, v1, g1 = _l1_in(x, mods[1], rows_per_mod, p1, rope_ret, tm)
    ret_out = _retention(q1, k1, v1, p1["decay"], s0_ret, batch, 128, want_fin=not sample)
    y = _post(x, mods[1], rows_per_mod, p1, "c", [ret_out[0], ret_out[1], g1], p1["ret_gain"], final_norm, 256)
    if sample:
        return y, None
    return y, (kf, vf, _ssd_state_from_pairs(ssd_out[2]), ret_out[2])


def kernel(x_prompt, x_sample, c, cache_k0, cache_v0, state_ssd0, state_ret1, c_ctx, l0_w_ada, l0_b_ada, l0_norm_mix, l0_norm_ffn, l0_w_in, l0_w_out, l0_q_gain, l0_k_gain, l0_conv_w, l0_conv_b, l0_dt_bias, l0_a_log, l0_d_skip, l0_ssd_gain, l0_w_gate, l0_w_up, l0_w_down, l1_w_ada, l1_b_ada, l1_norm_mix, l1_norm_ffn, l1_w_in, l1_w_out, l1_decay, l1_ret_gain, l1_w_gate, l1_w_up, l1_w_down, final_norm):
    b_ctx, seq_ctx, d = x_prompt.shape
    b_lat, seq_lat, _ = x_sample.shape
    assert d == D_MODEL and l0_w_in.shape == (D_MODEL, L0_IN) and l0_w_gate.shape == (D_MODEL, D_FF)
    row = lambda v: v.reshape(1, -1)

    p0 = dict(
        norm_mix=row(l0_norm_mix), norm_ffn=row(l0_norm_ffn),
        w_in=jnp.pad(l0_w_in, ((0, 0), (0, L0_IN_PAD - L0_IN))).astype(BF16),
        w_out=l0_w_out.astype(BF16), w_gate=l0_w_gate.astype(BF16), w_up=l0_w_up.astype(BF16),
        w_down=l0_w_down.astype(BF16),
        q_gain=row(jnp.tile(l0_q_gain, ATT_HEADS)), k_gain=row(jnp.tile(l0_k_gain, ATT_KV_HEADS)),
        pq=_head_mean_matrix(ATT_WIDTH, ATT_HEAD_DIM), pk=_head_mean_matrix(ATT_KV_WIDTH, ATT_HEAD_DIM),
        conv_w=jnp.pad(l0_conv_w, ((0, HALO - SSD_CONV), (0, 0))), conv_b=row(l0_conv_b),
        dt_bias=_pad_lanes(l0_dt_bias), a_log=_pad_lanes(l0_a_log),
        d_skip=row(jnp.repeat(l0_d_skip, SSD_HEAD_DIM)), ssd_gain=row(l0_ssd_gain),
    )
    p1 = dict(
        norm_mix=row(l1_norm_mix), norm_ffn=row(l1_norm_ffn),
        w_in=l1_w_in.astype(BF16), w_out=l1_w_out.astype(BF16), w_gate=l1_w_gate.astype(BF16),
        w_up=l1_w_up.astype(BF16), w_down=l1_w_down.astype(BF16),
        decay=jnp.pad(l1_decay, ((0, 8 - l1_decay.shape[0]), (0, LANES - l1_decay.shape[1]))),
        ret_gain=row(l1_ret_gain),
    )
    fnorm = row(final_norm)

    n_cond = 8
    conds = jnp.concatenate([c_ctx[None, :], c, jnp.zeros((n_cond - 1 - b_lat, d), F32)], axis=0)
    mod0 = _ada(conds, l0_w_ada, l0_b_ada)
    mod1 = _ada(conds, l1_w_ada, l1_b_ada)
    mods_ctx = [mod[0:1].reshape(1, 1, 6 * d) for mod in (mod0, mod1)]
    mods_lat = [mod[1:1 + b_lat].reshape(b_lat, 1, 6 * d) for mod in (mod0, mod1)]

    m_ctx = b_ctx * seq_ctx
    y_prompt, ctx = _trunk(x_prompt.reshape(m_ctx, d), mods_ctx, m_ctx, p0, p1, fnorm, None, None, None, seq_ctx)
    new_k0, new_v0, new_ssd0, new_ret1 = ctx
    ang = _axial_angles(seq_lat, ATT_HEAD_DIM)
    cos, sin = jnp.cos(ang), jnp.sin(ang)
    reps = LANES // ATT_HEAD_DIM
    rope_att = (jnp.tile(jnp.concatenate([cos, cos], axis=1), (1, reps)),
                jnp.tile(jnp.concatenate([-sin, sin], axis=1), (1, reps)))
    ang = _axial_angles(seq_lat, RET_DK)
    rope_ret = (jnp.cos(ang), jnp.sin(ang))
    caches = (cache_k0, cache_v0, state_ssd0, state_ret1)
    y_sample, _ = _trunk(x_sample.reshape(b_lat * seq_lat, d), mods_lat, seq_lat, p0, p1, fnorm, rope_att, rope_ret,
                         caches, seq_lat)
    return (y_prompt.reshape(b_ctx, seq_ctx, d), y_sample.reshape(b_lat, seq_lat, d),
            new_k0.reshape(b_ctx, seq_ctx, ATT_KV_HEADS, ATT_HEAD_DIM),
            new_v0.reshape(b_ctx, seq_ctx, ATT_KV_HEADS, ATT_HEAD_DIM), new_ssd0, new_ret1)
```

```python
import functools

import jax
import jax.numpy as jnp
from jax import lax
from jax.experimental import pallas as pl
from jax.experimental.pallas import tpu as pltpu

F32 = jnp.float32
BF16 = jnp.bfloat16

EPS = 1e-6
ROPE_THETA = 10000.0
GRID_W = 64
D_MODEL = 1024
ATT_HEAD_DIM = 64
ATT_HEADS = 8
ATT_KV_HEADS = 2
ATT_GROUP = ATT_HEADS // ATT_KV_HEADS
ATT_WIDTH = ATT_HEADS * ATT_HEAD_DIM
ATT_KV_WIDTH = ATT_KV_HEADS * ATT_HEAD_DIM
SSD_WIDTH = 512
SSD_HEADS = 8
SSD_HEAD_DIM = 64
SSD_STATE = 64
SSD_GROUPS = 2
SSD_CONV = 5
SSD_XBC = SSD_WIDTH + 2 * SSD_GROUPS * SSD_STATE
L0_IN = ATT_WIDTH + 2 * ATT_KV_WIDTH + SSD_WIDTH + SSD_XBC + 2 * SSD_HEADS
RET_HEADS = 4
RET_DK = 256
RET_DV = 512
RET_QK_WIDTH = RET_HEADS * RET_DK
RET_V_WIDTH = RET_HEADS * RET_DV
D_FF = 2816

LANES = 128
HALO = 8
L0_IN_PAD = -(-L0_IN // LANES) * LANES
SSD_CHUNK = 128
FF_SPLIT = 2
VMEM_LIMIT = 56 * 1024 * 1024


def _dot(a, b):
    return jnp.dot(a, b, preferred_element_type=F32)


def _dot_nt(a, b):
    return lax.dot_general(a, b, (((1,), (1,)), ((), ())), preferred_element_type=F32)


def _dot_tn(a, b):
    return lax.dot_general(a, b, (((0,), (0,)), ((), ())), preferred_element_type=F32)


def _silu(x):
    return x / (1.0 + jnp.exp(-x))


def _softplus(x):
    return jnp.maximum(x, 0.0) + jnp.log1p(jnp.exp(-jnp.abs(x)))


def _rms(x):
    return x * lax.rsqrt(jnp.mean(x * x, axis=-1, keepdims=True) + EPS)


def _split3(x):
    hi = x.astype(BF16)
    r = x - hi.astype(F32)
    mid = r.astype(BF16)
    lo = (r - mid.astype(F32)).astype(BF16)
    return hi, mid, lo


def _const_spec(shape):
    return pl.BlockSpec(shape, lambda *_: (0,) * len(shape))


def _resident_spec(shape):
    return pl.BlockSpec(shape, lambda *_: (0,) * len(shape), pipeline_mode=pl.Buffered(1))


def _params(n_axes, vmem=VMEM_LIMIT):
    return pltpu.CompilerParams(dimension_semantics=("arbitrary",) * n_axes, vmem_limit_bytes=vmem)


def _ada_kernel(c_ref, w_ref, b_ref, o_ref):
    s = _silu(c_ref[...])
    o_ref[...] = _dot(s.astype(BF16), w_ref[...].astype(BF16)) + b_ref[...]


def _ada(conds, w, b):
    n = w.shape[1]
    tn = 1536
    return pl.pallas_call(
        _ada_kernel,
        grid=(n // tn,),
        in_specs=[_const_spec(conds.shape),
                  pl.BlockSpec((D_MODEL, tn), lambda j: (0, j)),
                  pl.BlockSpec((1, tn), lambda j: (0, j))],
        out_specs=pl.BlockSpec((conds.shape[0], tn), lambda j: (0, j)),
        out_shape=jax.ShapeDtypeStruct((conds.shape[0], n), F32),
        compiler_params=_params(1),
        name="ada",
    )(conds, w, b.reshape(1, n))


def _mod_specs(tm, rows_per_mod, which):
    return [pl.BlockSpec((1, 1, D_MODEL), lambda i, j=j: ((i * tm) // rows_per_mod, 0, j)) for j in which]


def _head_rms(x, p_ref, gain):
    x2 = x * x
    hi = x2.astype(BF16)
    lo = (x2 - hi.astype(F32)).astype(BF16)
    ms = _dot(hi, p_ref[...]) + _dot(lo, p_ref[...])
    return x * lax.rsqrt(ms + EPS) * gain


def _rope64(x, cos, sin):
    n = x.shape[1]
    lane = lax.broadcasted_iota(jnp.int32, x.shape, 1)
    first_half = (lane % ATT_HEAD_DIM) < (ATT_HEAD_DIM // 2)
    partner = jnp.where(first_half, pltpu.roll(x, n - ATT_HEAD_DIM // 2, 1), pltpu.roll(x, ATT_HEAD_DIM // 2, 1))
    return x * cos + partner * sin


def _l0_in_kernel(*refs, use_rope):
    if use_rope:
        (x_ref, nw_ref, sh_ref, sc_ref, w_ref, qg_ref, kg_ref, pq_ref, pk_ref, cos_ref, sin_ref,
         q_ref, ka_ref, va_ref, kf_ref, vf_ref, z_ref, xbc_ref, dt_ref) = refs
    else:
        (x_ref, nw_ref, sh_ref, sc_ref, w_ref, qg_ref, kg_ref, pq_ref, pk_ref,
         q_ref, ka_ref, va_ref, kf_ref, vf_ref, z_ref, xbc_ref, dt_ref) = refs
    h = _rms(x_ref[...]) * nw_ref[...] * (1.0 + sc_ref[0]) + sh_ref[0]
    proj = _dot(h.astype(BF16), w_ref[...])
    o1 = ATT_WIDTH
    o2 = o1 + ATT_KV_WIDTH
    o3 = o2 + ATT_KV_WIDTH
    o4 = o3 + SSD_WIDTH
    o5 = o4 + SSD_XBC
    q = _head_rms(proj[:, :o1], pq_ref, qg_ref[...])
    k = _head_rms(proj[:, o1:o2], pk_ref, kg_ref[...])
    v = proj[:, o2:o3]
    kf_ref[...] = k
    vf_ref[...] = v
    if use_rope:
        cos = cos_ref[...]
        sin = sin_ref[...]
        k = _rope64(k, cos, sin)
        reps = ATT_WIDTH // LANES
        q = _rope64(q, jnp.concatenate([cos] * reps, axis=1), jnp.concatenate([sin] * reps, axis=1))
    q_ref[...] = (q * (ATT_HEAD_DIM ** -0.5)).astype(BF16)
    ka_ref[...] = k.astype(BF16)
    va_ref[...] = v.astype(BF16)
    z_ref[...] = proj[:, o3:o4]
    xbc_ref[...] = proj[:, o4:o5]
    dt_ref[...] = proj[:, o5:]


def _l0_in(x, mod, rows_per_mod, p, rope, tm):
    m = x.shape[0]
    use_rope = rope is not None
    row = lambda w: pl.BlockSpec((tm, w), lambda i: (i, 0))
    in_specs = ([row(D_MODEL), _const_spec((1, D_MODEL))] + _mod_specs(tm, rows_per_mod, (0, 1))
                + [_resident_spec((D_MODEL, L0_IN_PAD)), _const_spec((1, ATT_WIDTH)), _const_spec((1, ATT_KV_WIDTH)),
                   _resident_spec((ATT_WIDTH, ATT_WIDTH)), _resident_spec((ATT_KV_WIDTH, ATT_KV_WIDTH))])
    args = [x, p["norm_mix"], mod, mod, p["w_in"], p["q_gain"], p["k_gain"], p["pq"], p["pk"]]
    if use_rope:
        rows = rope[0].shape[0]
        in_specs += [pl.BlockSpec((tm, LANES), lambda i: (i % (rows // tm), 0))] * 2
        args += list(rope)
    widths = [(ATT_WIDTH, BF16), (ATT_KV_WIDTH, BF16), (ATT_KV_WIDTH, BF16), (ATT_KV_WIDTH, F32), (ATT_KV_WIDTH, F32),
              (SSD_WIDTH, F32), (SSD_XBC, F32), (LANES, F32)]
    return pl.pallas_call(
        functools.partial(_l0_in_kernel, use_rope=use_rope),
        grid=(m // tm,),
        in_specs=in_specs,
        out_specs=[row(w) for w, _ in widths],
        out_shape=[jax.ShapeDtypeStruct((m, w), dt) for w, dt in widths],
        compiler_params=_params(1),
        name="l0_in",
    )(*args)


def _attn_kernel(q_ref, k_ref, v_ref, o_ref, s0_ref, s1_ref, m0_ref, m1_ref, *, ck, sb):
    nchunk = k_ref.shape[2] // ck
    nsub = q_ref.shape[2] // sb
    slots = ((s0_ref, m0_ref), (s1_ref, m1_ref))

    def scores(i, slot):
        s_ref, m_ref = slots[slot]
        q = q_ref[0, 0, pl.ds(pl.multiple_of(i * sb, sb), sb), :]
        mx = None
        for j in range(nchunk):
            s = _dot_nt(q, k_ref[0, 0, j * ck:(j + 1) * ck, :])
            s_ref[j] = s
            for t in range(ck // LANES):
                part = s[:, t * LANES:(t + 1) * LANES]
                mx = part if mx is None else jnp.maximum(mx, part)
        m_ref[...] = jnp.broadcast_to(jnp.max(mx, axis=1, keepdims=True), (sb, LANES))

    def values(i, slot):
        s_ref, m_ref = slots[slot]
        m = jnp.concatenate([m_ref[...]] * (ck // LANES), axis=1)
        acc = None
        for j in range(nchunk):
            p = jnp.exp(s_ref[j] - m).astype(BF16)
            part = _dot(p, v_ref[0, 0, j * ck:(j + 1) * ck, :])
            acc = part if acc is None else acc + part
        out = acc[:, :ATT_HEAD_DIM] / acc[:, ATT_HEAD_DIM:ATT_HEAD_DIM + 1]
        o_ref[0, 0, pl.ds(pl.multiple_of(i * sb, sb), sb), :] = out.astype(o_ref.dtype)

    scores(0, 0)

    def body(h, carry):
        scores(2 * h + 1, 1)
        values(2 * h, 0)
        scores(2 * h + 2, 0)
        values(2 * h + 1, 1)
        return carry

    lax.fori_loop(0, nsub // 2 - 1, body, 0)
    scores(nsub - 1, 1)
    values(nsub - 2, 0)
    values(nsub - 1, 1)


def _attention(q, k, v, ck, sb):
    b, nkv, rows, d = q.shape
    lk = k.shape[2]
    assert rows % (2 * sb) == 0 and lk % ck == 0
    return pl.pallas_call(
        functools.partial(_attn_kernel, ck=ck, sb=sb),
        grid=(b, nkv),
        in_specs=[pl.BlockSpec((1, 1, rows, d), lambda i, j: (i, j, 0, 0)),
                  pl.BlockSpec((1, 1, lk, d), lambda i, j: (i, j, 0, 0)),
                  pl.BlockSpec((1, 1, lk, LANES), lambda i, j: (i, j, 0, 0))],
        out_specs=pl.BlockSpec((1, 1, rows, d), lambda i, j: (i, j, 0, 0)),
        out_shape=jax.ShapeDtypeStruct(q.shape, BF16),
        scratch_shapes=[pltpu.VMEM((lk // ck, sb, ck), F32)] * 2 + [pltpu.VMEM((sb, LANES), F32)] * 2,
        compiler_params=_params(2),
        name="attention",
    )(q, k, v)


def _ssd_group_rows(pair):
    g = pair // (SSD_HEADS // 2 // SSD_GROUPS)
    return slice(g * SSD_STATE, (g + 1) * SSD_STATE)


def _ssd_stream(cur_ref, prev_ref, next_ref, dt_ref, is_first, is_last, s_ref, y_ref, ext_ref,
                cw_ref, cb_ref, dtb_ref, alog_ref, dsk_ref, *, forward):
    cn = SSD_CHUNK
    ext_ref[0:HALO, :] = jnp.where(is_first, 0.0, prev_ref[...])
    ext_ref[HALO:HALO + cn, :] = cur_ref[...]
    ext_ref[HALO + cn:, :] = jnp.where(is_last, 0.0, next_ref[...])
    u = cb_ref[...]
    for t in range(SSD_CONV):
        u = u + cw_ref[t:t + 1, :] * ext_ref[pl.ds(HALO - SSD_CONV // 2 + t, cn), :]
    act = _silu(u)
    xs = act[:, :SSD_WIDTH]
    bm = act[:, SSD_WIDTH:SSD_WIDTH + LANES]
    cm = act[:, SSD_WIDTH + LANES:]
    dt = _softplus(dt_ref[...] + dtb_ref[...])
    a = dt * (-jnp.exp(alog_ref[...]))
    ii = lax.broadcasted_iota(jnp.int32, (cn, cn), 0)
    jj = lax.broadcasted_iota(jnp.int32, (cn, cn), 1)
    keep = (ii >= jj) if forward else (ii <= jj)
    tri = keep.astype(BF16)
    tri_t = ((ii <= jj) if forward else (ii >= jj)).astype(BF16)
    a3 = jnp.concatenate(_split3(a), axis=1)
    c3 = _dot(tri, a3)
    cum = c3[:, :LANES] + c3[:, LANES:2 * LANES] + c3[:, 2 * LANES:]
    ct3 = _dot_tn(a3, tri_t)
    cum_t = ct3[:LANES] + ct3[LANES:2 * LANES] + ct3[2 * LANES:]
    off = 0 if forward else SSD_HEADS
    tot_col = cn - 1 if forward else 0
    lane = lax.broadcasted_iota(jnp.int32, (1, LANES), 1)
    lo = lane < SSD_HEAD_DIM
    cm_b = cm.astype(BF16)
    bm_b = bm.astype(BF16)
    gmat = {}
    for pair in range(SSD_HEADS // 2):
        g = pair // (SSD_HEADS // 2 // SSD_GROUPS)
        in_group = lo if g == 0 else jnp.logical_not(lo)
        if g not in gmat:
            gmat[g] = _dot_nt(jnp.where(in_group, cm, 0.0).astype(BF16), bm_b)
        bg = jnp.where(in_group, bm, 0.0)
        h0 = off + 2 * pair
        h1 = h0 + 1
        ci0 = cum[:, h0:h0 + 1]
        ci1 = cum[:, h1:h1 + 1]
        m0 = (gmat[g] * jnp.exp(jnp.where(keep, ci0 - cum_t[h0:h0 + 1, :], -1e30))).astype(BF16)
        m1 = (gmat[g] * jnp.exp(jnp.where(keep, ci1 - cum_t[h1:h1 + 1, :], -1e30))).astype(BF16)
        xs_p = xs[:, pair * LANES:(pair + 1) * LANES]
        vp = xs_p * jnp.where(lo, dt[:, h0:h0 + 1], dt[:, h1:h1 + 1])
        v0 = jnp.where(lo, vp, 0.0).astype(BF16)
        v1 = jnp.where(lo, 0.0, vp).astype(BF16)
        s_old = s_ref[pair]
        y = _dot(m0, v0) + _dot(m1, v1) + jnp.exp(jnp.where(lo, ci0, ci1)) * _dot(cm_b, s_old.astype(BF16))
        tot0 = cum_t[h0:h0 + 1, tot_col:tot_col + 1]
        tot1 = cum_t[h1:h1 + 1, tot_col:tot_col + 1]
        bw0 = (bg * jnp.exp(tot0 - ci0)).astype(BF16)
        bw1 = (bg * jnp.exp(tot1 - ci1)).astype(BF16)
        s_ref[pair] = s_old * jnp.exp(jnp.where(lo, tot0, tot1)) + _dot_tn(bw0, v0) + _dot_tn(bw1, v1)
        if forward:
            y = y + dsk_ref[:, pair * LANES:(pair + 1) * LANES] * xs_p
        y_ref[:, pair * LANES:(pair + 1) * LANES] = y


def _ssd_kernel(*refs, nc, has_init, want_fin):
    refs = list(refs)
    cf_ref, pf_ref, nf_ref, dtf_ref, cb_ref, pb_ref, nb_ref, dtb_ref = refs[:8]
    cw_ref, cbias_ref, dtbias_ref, alog_ref, dsk_ref = refs[8:13]
    pos = 13
    s0_ref = None
    if has_init:
        s0_ref = refs[pos]
        pos += 1
    yf_ref, yb_ref = refs[pos:pos + 2]
    pos += 2
    sfin_ref = None
    if want_fin:
        sfin_ref = refs[pos]
        pos += 1
    ext_ref, sf_ref, sb_ref = refs[pos:]
    c = pl.program_id(1)

    @pl.when(c == 0)
    def _():
        sf_ref[...] = jnp.zeros_like(sf_ref)
        sb_ref[...] = jnp.zeros_like(sb_ref)
        if has_init:
            for pair in range(SSD_HEADS // 2):
                rows = _ssd_group_rows(pair)
                sf_ref[pair, rows, :] = s0_ref[0, 0, pair]
                sb_ref[pair, rows, :] = s0_ref[0, 1, pair]

    shared = (cw_ref, cbias_ref, dtbias_ref, alog_ref, dsk_ref)
    _ssd_stream(cf_ref, pf_ref, nf_ref, dtf_ref, c == 0, c == nc - 1, sf_ref, yf_ref, ext_ref, *shared,
                forward=True)
    _ssd_stream(cb_ref, pb_ref, nb_ref, dtb_ref, c == nc - 1, c == 0, sb_ref, yb_ref, ext_ref, *shared,
                forward=False)

    if want_fin:
        @pl.when(c == nc - 1)
        def _():
            for pair in range(SSD_HEADS // 2):
                rows = _ssd_group_rows(pair)
                sfin_ref[0, 0, pair] = sf_ref[pair, rows, :]
                sfin_ref[0, 1, pair] = sb_ref[pair, rows, :]


def _ssd(xbc, dt, p, s0, batch, want_fin):
    m = xbc.shape[0]
    cn = SSD_CHUNK
    nc = m // batch // cn
    per = cn // HALO
    n_halo = m // HALO
    has_init = s0 is not None
    npair = SSD_HEADS // 2

    def fwd(b, c):
        return b * nc + c

    def bwd(b, c):
        return b * nc + nc - 1 - c

    def stream_specs(chunk):
        return [pl.BlockSpec((cn, SSD_XBC), lambda b, c: (chunk(b, c), 0)),
                pl.BlockSpec((HALO, SSD_XBC), lambda b, c: (jnp.maximum(chunk(b, c) * per - 1, 0), 0)),
                pl.BlockSpec((HALO, SSD_XBC), lambda b, c: (jnp.minimum(chunk(b, c) * per + per, n_halo - 1), 0)),
                pl.BlockSpec((cn, LANES), lambda b, c: (chunk(b, c), 0))]

    in_specs = stream_specs(fwd) + stream_specs(bwd) + [
        _const_spec((HALO, SSD_XBC)), _const_spec((1, SSD_XBC)), _const_spec((1, LANES)), _const_spec((1, LANES)),
        _const_spec((1, SSD_WIDTH))]
    args = [xbc, xbc, xbc, dt, xbc, xbc, xbc, dt, p["conv_w"], p["conv_b"], p["dt_bias"], p["a_log"], p["d_skip"]]
    state_block = (1, 2, npair, SSD_STATE, LANES)
    if has_init:
        in_specs.append(pl.BlockSpec(state_block, lambda b, c: (b, 0, 0, 0, 0)))
        args.append(s0)
    out_specs = [pl.BlockSpec((cn, SSD_WIDTH), lambda b, c: (fwd(b, c), 0)),
                 pl.BlockSpec((cn, SSD_WIDTH), lambda b, c: (bwd(b, c), 0))]
    out_shape = [jax.ShapeDtypeStruct((m, SSD_WIDTH), F32)] * 2
    if want_fin:
        out_specs.append(pl.BlockSpec(state_block, lambda b, c: (b, 0, 0, 0, 0)))
        out_shape.append(jax.ShapeDtypeStruct((batch,) + state_block[1:], F32))
    return pl.pallas_call(
        functools.partial(_ssd_kernel, nc=nc, has_init=has_init, want_fin=want_fin),
        grid=(batch, nc),
        in_specs=in_specs,
        out_specs=out_specs,
        out_shape=out_shape,
        scratch_shapes=[pltpu.VMEM((cn + 2 * HALO, SSD_XBC), F32),
                        pltpu.VMEM((npair, LANES, LANES), F32), pltpu.VMEM((npair, LANES, LANES), F32)],
        compiler_params=_params(2),
        name="ssd",
    )(*args)


def _ssd_state_to_pairs(s):
    b = s.shape[0]
    npair = SSD_HEADS // 2
    s = s.reshape(b, 2, npair, 2, SSD_STATE, SSD_HEAD_DIM).transpose(0, 1, 2, 4, 3, 5)
    return s.reshape(b, 2, npair, SSD_STATE, 2 * SSD_HEAD_DIM)


def _ssd_state_from_pairs(s):
    b = s.shape[0]
    npair = SSD_HEADS // 2
    s = s.reshape(b, 2, npair, SSD_STATE, 2, SSD_HEAD_DIM).transpose(0, 1, 2, 4, 3, 5)
    return s.reshape(b, 2, SSD_HEADS, SSD_STATE, SSD_HEAD_DIM)


def _l1_in_kernel(*refs, use_rope):
    if use_rope:
        x_ref, nw_ref, sh_ref, sc_ref, w_ref, cos_ref, sin_ref, q_ref, k_ref, v_ref, g_ref = refs
    else:
        x_ref, nw_ref, sh_ref, sc_ref, w_ref, q_ref, k_ref, v_ref, g_ref = refs
    h = (_rms(x_ref[...]) * nw_ref[...] * (1.0 + sc_ref[0]) + sh_ref[0]).astype(BF16)
    half = RET_DK // 2

    def rope(t):
        if not use_rope:
            return t
        cos = cos_ref[...]
        sin = sin_ref[...]
        parts = []
        for hd in range(RET_HEADS):
            x1 = t[:, hd * RET_DK:hd * RET_DK + half]
            x2 = t[:, hd * RET_DK + half:(hd + 1) * RET_DK]
            parts += [x1 * cos - x2 * sin, x2 * cos + x1 * sin]
        return jnp.concatenate(parts, axis=1)

    o1 = RET_QK_WIDTH
    o2 = 2 * RET_QK_WIDTH
    o3 = o2 + RET_V_WIDTH
    q_ref[...] = rope(_dot(h, w_ref[:, :o1])).astype(BF16)
    k_ref[...] = rope(_dot(h, w_ref[:, o1:o2])) * (RET_DK ** -0.5)
    v_ref[...] = _dot(h, w_ref[:, o2:o3]).astype(BF16)
    g_ref[...] = _dot(h, w_ref[:, o3:])


def _l1_in(x, mod, rows_per_mod, p, rope, tm):
    m = x.shape[0]
    use_rope = rope is not None
    row = lambda w: pl.BlockSpec((tm, w), lambda i: (i, 0))
    n = 2 * RET_QK_WIDTH + 2 * RET_V_WIDTH
    in_specs = ([row(D_MODEL), _const_spec((1, D_MODEL))] + _mod_specs(tm, rows_per_mod, (0, 1))
                + [_resident_spec((D_MODEL, n))])
    args = [x, p["norm_mix"], mod, mod, p["w_in"]]
    if use_rope:
        rows = rope[0].shape[0]
        in_specs += [pl.BlockSpec((tm, LANES), lambda i: (i % (rows // tm), 0))] * 2
        args += list(rope)
    widths = [(RET_QK_WIDTH, BF16), (RET_QK_WIDTH, F32), (RET_V_WIDTH, BF16), (RET_V_WIDTH, F32)]
    return pl.pallas_call(
        functools.partial(_l1_in_kernel, use_rope=use_rope),
        grid=(m // tm,),
        in_specs=in_specs,
        out_specs=[row(w) for w, _ in widths],
        out_shape=[jax.ShapeDtypeStruct((m, w), dt) for w, dt in widths],
        compiler_params=_params(1),
        name="l1_in",
    )(*args)


def _ret_kernel(*refs, cn, nc, has_init, want_fin):
    refs = list(refs)
    dec_ref, qf_ref, kf_ref, vf_ref, qb_ref, kb_ref, vb_ref = refs[:7]
    pos = 7
    s0_ref = None
    if has_init:
        s0_ref = refs[pos]
        pos += 1
    yf_ref, yb_ref = refs[pos:pos + 2]
    pos += 2
    sfin_ref = None
    if want_fin:
        sfin_ref = refs[pos]
        pos += 1
    sf_ref, sb_ref = refs[pos:]
    c = pl.program_id(1)
    stateless = (not has_init) and nc == 1

    if not stateless:
        @pl.when(c == 0)
        def _():
            if has_init:
                sf_ref[...] = s0_ref[0, 0]
                sb_ref[...] = s0_ref[0, 1]
            else:
                sf_ref[...] = jnp.zeros_like(sf_ref)
                sb_ref[...] = jnp.zeros_like(sb_ref)

    log_g = -jnp.exp(dec_ref[...])
    ii = lax.broadcasted_iota(jnp.int32, (cn, cn), 0)
    jj = lax.broadcasted_iota(jnp.int32, (cn, cn), 1)
    dist = (ii - jj).astype(F32)
    ri = lax.broadcasted_iota(jnp.int32, (cn, 1), 0).astype(F32)
    for hd in range(RET_HEADS):
        gf = log_g[0:1, hd:hd + 1]
        gb = log_g[1:2, hd:hd + 1]
        decay = (jnp.where(dist >= 0, jnp.exp(gf * jnp.maximum(dist, 0.0)), 0.0)
                 + jnp.where(dist <= 0, jnp.exp(gb * jnp.maximum(-dist, 0.0)), 0.0))
        qs = slice(hd * RET_DK, (hd + 1) * RET_DK)
        vs = slice(hd * RET_DV, (hd + 1) * RET_DV)
        q = qf_ref[:, qs]
        k = kf_ref[:, qs]
        v = vf_ref[:, vs]
        y = _dot((_dot_nt(q, k.astype(BF16)) * decay).astype(BF16), v)
        upd_f = _dot_tn((k * jnp.exp(gf * (cn - 1.0 - ri))).astype(BF16), v)
        if stateless:
            new_f = upd_f
        else:
            s_old = sf_ref[hd]
            y = y + jnp.exp(gf * (ri + 1.0)) * _dot(q, s_old.astype(BF16))
            new_f = s_old * jnp.exp(gf * cn) + upd_f
            sf_ref[hd] = new_f
        yf_ref[:, vs] = y
        q = qb_ref[:, qs]
        k = kb_ref[:, qs]
        v = vb_ref[:, vs]
        upd_b = _dot_tn((k * jnp.exp(gb * ri)).astype(BF16), v)
        if stateless:
            new_b = upd_b
            yb_ref[:, vs] = jnp.zeros((cn, RET_DV), F32)
        else:
            s_old = sb_ref[hd]
            yb_ref[:, vs] = jnp.exp(gb * (cn - ri)) * _dot(q, s_old.astype(BF16))
            new_b = s_old * jnp.exp(gb * cn) + upd_b
            sb_ref[hd] = new_b
        if want_fin:
            if stateless:
                sfin_ref[0, 0, hd] = new_f
                sfin_ref[0, 1, hd] = new_b
            else:
                @pl.when(c == nc - 1)
                def _(new_f=new_f, new_b=new_b, hd=hd):
                    sfin_ref[0, 0, hd] = new_f
                    sfin_ref[0, 1, hd] = new_b


def _retention(q, k, v, decay, s0, batch, cn, want_fin):
    m = q.shape[0]
    nc = m // batch // cn
    has_init = s0 is not None

    def fwd(b, c):
        return b * nc + c

    def bwd(b, c):
        return b * nc + nc - 1 - c

    def stream_specs(chunk):
        return [pl.BlockSpec((cn, RET_QK_WIDTH), lambda b, c: (chunk(b, c), 0)),
                pl.BlockSpec((cn, RET_QK_WIDTH), lambda b, c: (chunk(b, c), 0)),
                pl.BlockSpec((cn, RET_V_WIDTH), lambda b, c: (chunk(b, c), 0))]

    in_specs = [_const_spec((8, LANES))] + stream_specs(fwd) + stream_specs(bwd)
    args = [decay, q, k, v, q, k, v]
    state_block = (1, 2, RET_HEADS, RET_DK, RET_DV)
    if has_init:
        in_specs.append(pl.BlockSpec(state_block, lambda b, c: (b, 0, 0, 0, 0)))
        args.append(s0)
    out_specs = [pl.BlockSpec((cn, RET_V_WIDTH), lambda b, c: (fwd(b, c), 0)),
                 pl.BlockSpec((cn, RET_V_WIDTH), lambda b, c: (bwd(b, c), 0))]
    out_shape = [jax.ShapeDtypeStruct((m, RET_V_WIDTH), F32)] * 2
    if want_fin:
        out_specs.append(pl.BlockSpec(state_block, lambda b, c: (b, 0, 0, 0, 0)))
        out_shape.append(jax.ShapeDtypeStruct((batch,) + state_block[1:], F32))
    return pl.pallas_call(
        functools.partial(_ret_kernel, cn=cn, nc=nc, has_init=has_init, want_fin=want_fin),
        grid=(batch, nc),
        in_specs=in_specs,
        out_specs=out_specs,
        out_shape=out_shape,
        scratch_shapes=[pltpu.VMEM((RET_HEADS, RET_DK, RET_DV), F32)] * 2,
        compiler_params=_params(2),
        name="retention",
    )(*args)


def _post_kernel(*refs, mixer, final):
    refs = list(refs)
    x_ref, g1_ref, nw_ref, sh2_ref, sc2_ref, g2_ref, wout_ref, wg_ref, wu_ref, wd_ref = refs[:10]
    pos = 10
    fn_ref = None
    if final:
        fn_ref = refs[pos]
        pos += 1
    if mixer == "ab":
        att_ref, yf_ref, yb_ref, z_ref, gain_ref, o_ref = refs[pos:]
        y = (yf_ref[...] + yb_ref[...]) * _silu(z_ref[...])
        y = _rms(y) * gain_ref[...]
        mix = _dot(att_ref[...], wout_ref[:ATT_WIDTH, :]) + _dot(y.astype(BF16), wout_ref[ATT_WIDTH:, :])
    else:
        yf_ref, yb_ref, gate_ref, gain_ref, o_ref = refs[pos:]
        mix = None
        for hd in range(RET_HEADS):
            vs = slice(hd * RET_DV, (hd + 1) * RET_DV)
            y = _rms(yf_ref[:, vs] + yb_ref[:, vs]) * gain_ref[:, vs]
            part = _dot((_silu(gate_ref[:, vs]) * y).astype(BF16), wout_ref[vs, :])
            mix = part if mix is None else mix + part
    x1 = x_ref[...] + g1_ref[0] * mix
    h = (_rms(x1) * nw_ref[...] * (1.0 + sc2_ref[0]) + sh2_ref[0]).astype(BF16)
    wcols = D_FF // FF_SPLIT
    ffn = None
    for j in range(FF_SPLIT):
        cs = slice(j * wcols, (j + 1) * wcols)
        act = (_silu(_dot(h, wg_ref[:, cs])) * _dot(h, wu_ref[:, cs])).astype(BF16)
        part = _dot(act, wd_ref[cs, :])
        ffn = part if ffn is None else ffn + part
    x2 = x1 + g2_ref[0] * ffn
    if final:
        x2 = _rms(x2) * fn_ref[...]
    o_ref[...] = x2


def _post(x, mod, rows_per_mod, p, mixer, mix_inputs, gain, final_norm, tm):
    m = x.shape[0]
    row = lambda w: pl.BlockSpec((tm, w), lambda i: (i, 0))
    mixw = p["w_out"].shape[0]
    (g1,) = _mod_specs(tm, rows_per_mod, (2,))
    sh2, sc2, g2 = _mod_specs(tm, rows_per_mod, (3, 4, 5))
    in_specs = [row(D_MODEL), g1, _const_spec((1, D_MODEL)), sh2, sc2, g2,
                _resident_spec((mixw, D_MODEL)), _resident_spec((D_MODEL, D_FF)), _resident_spec((D_MODEL, D_FF)),
                _resident_spec((D_FF, D_MODEL))]
    args = [x, mod, p["norm_ffn"], mod, mod, mod, p["w_out"], p["w_gate"], p["w_up"], p["w_down"]]
    final = final_norm is not None
    if final:
        in_specs.append(_const_spec((1, D_MODEL)))
        args.append(final_norm)
    in_specs += [row(a.shape[1]) for a in mix_inputs] + [_const_spec(gain.shape)]
    args += list(mix_inputs) + [gain]
    return pl.pallas_call(
        functools.partial(_post_kernel, mixer=mixer, final=final),
        grid=(m // tm,),
        in_specs=in_specs,
        out_specs=row(D_MODEL),
        out_shape=jax.ShapeDtypeStruct((m, D_MODEL), F32),
        compiler_params=_params(1),
        name="post_" + mixer,
    )(*args)


def _axial_angles(n_tokens, dim):
    rows = n_tokens // GRID_W
    row = jnp.repeat(jnp.arange(rows), GRID_W).astype(F32)
    col = jnp.tile(jnp.arange(GRID_W), rows).astype(F32)
    n_freq = dim // 4
    inv = ROPE_THETA ** (-jnp.arange(n_freq, dtype=F32) / n_freq)
    return jnp.concatenate([row[:, None] * inv, col[:, None] * inv], axis=-1)


def _head_mean_matrix(width, head):
    idx = jnp.arange(width) // head
    return jnp.where(idx[:, None] == idx[None, :], 1.0 / head, 0.0).astype(BF16)


def _pad_lanes(v, width=LANES):
    v = v.reshape(1, -1)
    return jnp.pad(v, ((0, 0), (0, width - v.shape[1])))


def _trunk(x, mods, rows_per_mod, p0, p1, final_norm, rope_att, rope_ret, caches, seq):
    m = x.shape[0]
    batch = m // seq
    sample = caches is not None
    tm = 512
    q, ka, va, kf, vf, z, xbc, dt = _l0_in(x, mods[0], rows_per_mod, p0, rope_att, tm)
    qh = q.reshape(batch, seq, ATT_KV_HEADS, ATT_GROUP * ATT_HEAD_DIM).transpose(0, 2, 1, 3)
    qh = qh.reshape(batch, ATT_KV_HEADS, seq * ATT_GROUP, ATT_HEAD_DIM)
    kh = ka.reshape(batch, seq, ATT_KV_HEADS, ATT_HEAD_DIM).transpose(0, 2, 1, 3)
    vh = va.reshape(batch, seq, ATT_KV_HEADS, ATT_HEAD_DIM).transpose(0, 2, 1, 3)
    s0_ssd = None
    s0_ret = None
    if sample:
        cache_k, cache_v, state_ssd, state_ret = caches
        kh = jnp.concatenate([cache_k.astype(BF16).transpose(0, 2, 1, 3), kh], axis=2)
        vh = jnp.concatenate([cache_v.astype(BF16).transpose(0, 2, 1, 3), vh], axis=2)
        s0_ssd = _ssd_state_to_pairs(state_ssd)
        s0_ret = state_ret
    ones_col = jnp.zeros(vh.shape[:3] + (LANES - ATT_HEAD_DIM,), BF16).at[..., 0].set(1.0)
    att = _attention(qh, kh, jnp.concatenate([vh, ones_col], axis=-1), 256, 256)
    att = att.reshape(batch, ATT_KV_HEADS, seq, ATT_GROUP * ATT_HEAD_DIM).transpose(0, 2, 1, 3)
    att = att.reshape(m, ATT_WIDTH)
    ssd_out = _ssd(xbc, dt, p0, s0_ssd, batch, want_fin=not sample)
    x = _post(x, mods[0], rows_per_mod, p0, "ab", [att, ssd_out[0], ssd_out[1], z], p0["ssd_gain"], None, tm)
    q1, k1, v1, g1 = _l1_in(x, mods[1], rows_per_mod, p1, rope_ret, tm)
    ret_out = _retention(q1, k1, v1, p1["decay"], s0_ret, batch, 128, want_fin=not sample)
    y = _post(x, mods[1], rows_per_mod, p1, "c", [ret_out[0], ret_out[1], g1], p1["ret_gain"], final_norm, 256)
    if sample:
        return y, None
    return y, (kf, vf, _ssd_state_from_pairs(ssd_out[2]), ret_out[2])


def kernel(x_prompt, x_sample, c, cache_k0, cache_v0, state_ssd0, state_ret1, c_ctx, l0_w_ada, l0_b_ada, l0_norm_mix, l0_norm_ffn, l0_w_in, l0_w_out, l0_q_gain, l0_k_gain, l0_conv_w, l0_conv_b, l0_dt_bias, l0_a_log, l0_d_skip, l0_ssd_gain, l0_w_gate, l0_w_up, l0_w_down, l1_w_ada, l1_b_ada, l1_norm_mix, l1_norm_ffn, l1_w_in, l1_w_out, l1_decay, l1_ret_gain, l1_w_gate, l1_w_up, l1_w_down, final_norm):
    b_ctx, seq_ctx, d = x_prompt.shape
    b_lat, seq_lat, _ = x_sample.shape
    assert d == D_MODEL and l0_w_in.shape == (D_MODEL, L0_IN) and l0_w_gate.shape == (D_MODEL, D_FF)
    row = lambda v: v.reshape(1, -1)

    p0 = dict(
        norm_mix=row(l0_norm_mix), norm_ffn=row(l0_norm_ffn),
        w_in=jnp.pad(l0_w_in, ((0, 0), (0, L0_IN_PAD - L0_IN))).astype(BF16),
        w_out=l0_w_out.astype(BF16), w_gate=l0_w_gate.astype(BF16), w_up=l0_w_up.astype(BF16),
        w_down=l0_w_down.astype(BF16),
        q_gain=row(jnp.tile(l0_q_gain, ATT_HEADS)), k_gain=row(jnp.tile(l0_k_gain, ATT_KV_HEADS)),
        pq=_head_mean_matrix(ATT_WIDTH, ATT_HEAD_DIM), pk=_head_mean_matrix(ATT_KV_WIDTH, ATT_HEAD_DIM),
        conv_w=jnp.pad(l0_conv_w, ((0, HALO - SSD_CONV), (0, 0))), conv_b=row(l0_conv_b),
        dt_bias=_pad_lanes(l0_dt_bias), a_log=_pad_lanes(l0_a_log),
        d_skip=row(jnp.repeat(l0_d_skip, SSD_HEAD_DIM)), ssd_gain=row(l0_ssd_gain),
    )
    p1 = dict(
        norm_mix=row(l1_norm_mix), norm_ffn=row(l1_norm_ffn),
        w_in=l1_w_in.astype(BF16), w_out=l1_w_out.astype(BF16), w_gate=l1_w_gate.astype(BF16),
        w_up=l1_w_up.astype(BF16), w_down=l1_w_down.astype(BF16),
        decay=jnp.pad(l1_decay, ((0, 8 - l1_decay.shape[0]), (0, LANES - l1_decay.shape[1]))),
        ret_gain=row(l1_ret_gain),
    )
    fnorm = row(final_norm)

    n_cond = 8
    conds = jnp.concatenate([c_ctx[None, :], c, jnp.zeros((n_cond - 1 - b_lat, d), F32)], axis=0)
    mod0 = _ada(conds, l0_w_ada, l0_b_ada)
    mod1 = _ada(conds, l1_w_ada, l1_b_ada)
    mods_ctx = [mod[0:1].reshape(1, 1, 6 * d) for mod in (mod0, mod1)]
    mods_lat = [mod[1:1 + b_lat].reshape(b_lat, 1, 6 * d) for mod in (mod0, mod1)]

    m_ctx = b_ctx * seq_ctx
    y_prompt, ctx = _trunk(x_prompt.reshape(m_ctx, d), mods_ctx, m_ctx, p0, p1, fnorm, None, None, None, seq_ctx)
    new_k0, new_v0, new_ssd0, new_ret1 = ctx
    ang = _axial_angles(seq_lat, ATT_HEAD_DIM)
    cos, sin = jnp.cos(ang), jnp.sin(ang)
    reps = LANES // ATT_HEAD_DIM
    rope_att = (jnp.tile(jnp.concatenate([cos, cos], axis=1), (1, reps)),
                jnp.tile(jnp.concatenate([-sin, sin], axis=1), (1, reps)))
    ang = _axial_angles(seq_lat, RET_DK)
    rope_ret = (jnp.cos(ang), jnp.sin(ang))
    caches = (cache_k0, cache_v0, state_ssd0, state_ret1)
    y_sample, _ = _trunk(x_sample.reshape(b_lat * seq_lat, d), mods_lat, seq_lat, p0, p1, fnorm, rope_att, rope_ret,
                         caches, seq_lat)
    return (y_prompt.reshape(b_ctx, seq_ctx, d), y_sample.reshape(b_lat, seq_lat, d),
            new_k0.reshape(b_ctx, seq_ctx, ATT_KV_HEADS, ATT_HEAD_DIM),
            new_v0.reshape(b_ctx, seq_ctx, ATT_KV_HEADS, ATT_HEAD_DIM), new_ssd0, new_ret1)
```

```python
import functools

import jax
import jax.numpy as jnp
from jax import lax
from jax.experimental import pallas as pl
from jax.experimental.pallas import tpu as pltpu

F32 = jnp.float32
BF16 = jnp.bfloat16

EPS = 1e-6
ROPE_THETA = 10000.0
GRID_W = 64
D_MODEL = 1024
ATT_HEAD_DIM = 64
ATT_HEADS = 8
ATT_KV_HEADS = 2
ATT_GROUP = ATT_HEADS // ATT_KV_HEADS
ATT_WIDTH = ATT_HEADS * ATT_HEAD_DIM
ATT_KV_WIDTH = ATT_KV_HEADS * ATT_HEAD_DIM
SSD_WIDTH = 512
SSD_HEADS = 8
SSD_HEAD_DIM = 64
SSD_STATE = 64
SSD_GROUPS = 2
SSD_CONV = 5
SSD_XBC = SSD_WIDTH + 2 * SSD_GROUPS * SSD_STATE
L0_IN = ATT_WIDTH + 2 * ATT_KV_WIDTH + SSD_WIDTH + SSD_XBC + 2 * SSD_HEADS
RET_HEADS = 4
RET_DK = 256
RET_DV = 512
RET_QK_WIDTH = RET_HEADS * RET_DK
RET_V_WIDTH = RET_HEADS * RET_DV
D_FF = 2816

LANES = 128
HALO = 8
L0_IN_PAD = -(-L0_IN // LANES) * LANES
SSD_CHUNK = 128
RET_CHUNK = 256
FF_SPLIT = 2
VMEM_LIMIT = 56 * 1024 * 1024


def _dot(a, b):
    return jnp.dot(a, b, preferred_element_type=F32)


def _dot_nt(a, b):
    return lax.dot_general(a, b, (((1,), (1,)), ((), ())), preferred_element_type=F32)


def _dot_tn(a, b):
    return lax.dot_general(a, b, (((0,), (0,)), ((), ())), preferred_element_type=F32)


def _silu(x):
    return x / (1.0 + jnp.exp(-x))


def _softplus(x):
    return jnp.maximum(x, 0.0) + jnp.log1p(jnp.exp(-jnp.abs(x)))


def _rms(x):
    return x * lax.rsqrt(jnp.mean(x * x, axis=-1, keepdims=True) + EPS)


def _split3(x):
    hi = x.astype(BF16)
    r = x - hi.astype(F32)
    mid = r.astype(BF16)
    lo = (r - mid.astype(F32)).astype(BF16)
    return hi, mid, lo


def _const_spec(shape):
    return pl.BlockSpec(shape, lambda *_: (0,) * len(shape))


def _resident_spec(shape):
    return pl.BlockSpec(shape, lambda *_: (0,) * len(shape), pipeline_mode=pl.Buffered(1))


def _params(n_axes, vmem=VMEM_LIMIT):
    return pltpu.CompilerParams(dimension_semantics=("arbitrary",) * n_axes, vmem_limit_bytes=vmem)


def _ada_kernel(c_ref, w_ref, b_ref, o_ref):
    s = _silu(c_ref[...])
    o_ref[...] = _dot(s.astype(BF16), w_ref[...].astype(BF16)) + b_ref[...]


def _ada(conds, w, b):
    n = w.shape[1]
    tn = 1536
    return pl.pallas_call(
        _ada_kernel,
        grid=(n // tn,),
        in_specs=[_const_spec(conds.shape),
                  pl.BlockSpec((D_MODEL, tn), lambda j: (0, j)),
                  pl.BlockSpec((1, tn), lambda j: (0, j))],
        out_specs=pl.BlockSpec((conds.shape[0], tn), lambda j: (0, j)),
        out_shape=jax.ShapeDtypeStruct((conds.shape[0], n), F32),
        compiler_params=_params(1),
        name="ada",
    )(conds, w, b.reshape(1, n))


def _mod_specs(tm, rows_per_mod, which):
    return [pl.BlockSpec((1, 1, D_MODEL), lambda i, j=j: ((i * tm) // rows_per_mod, 0, j)) for j in which]


def _head_rms(x, p_ref, gain):
    x2 = x * x
    hi = x2.astype(BF16)
    lo = (x2 - hi.astype(F32)).astype(BF16)
    ms = _dot(hi, p_ref[...]) + _dot(lo, p_ref[...])
    return x * lax.rsqrt(ms + EPS) * gain


def _rope64(x, cos, sin):
    n = x.shape[1]
    lane = lax.broadcasted_iota(jnp.int32, x.shape, 1)
    first_half = (lane % ATT_HEAD_DIM) < (ATT_HEAD_DIM // 2)
    partner = jnp.where(first_half, pltpu.roll(x, n - ATT_HEAD_DIM // 2, 1), pltpu.roll(x, ATT_HEAD_DIM // 2, 1))
    return x * cos + partner * sin


def _l0_in_kernel(*refs, use_rope):
    if use_rope:
        (x_ref, nw_ref, sh_ref, sc_ref, w_ref, qg_ref, kg_ref, pq_ref, pk_ref, cos_ref, sin_ref,
         q_ref, ka_ref, va_ref, kf_ref, vf_ref, z_ref, xbc_ref, dt_ref) = refs
    else:
        (x_ref, nw_ref, sh_ref, sc_ref, w_ref, qg_ref, kg_ref, pq_ref, pk_ref,
         q_ref, ka_ref, va_ref, kf_ref, vf_ref, z_ref, xbc_ref, dt_ref) = refs
    h = _rms(x_ref[...]) * nw_ref[...] * (1.0 + sc_ref[0]) + sh_ref[0]
    proj = _dot(h.astype(BF16), w_ref[...])
    o1 = ATT_WIDTH
    o2 = o1 + ATT_KV_WIDTH
    o3 = o2 + ATT_KV_WIDTH
    o4 = o3 + SSD_WIDTH
    o5 = o4 + SSD_XBC
    q = _head_rms(proj[:, :o1], pq_ref, qg_ref[...])
    k = _head_rms(proj[:, o1:o2], pk_ref, kg_ref[...])
    v = proj[:, o2:o3]
    kf_ref[...] = k
    vf_ref[...] = v
    if use_rope:
        cos = cos_ref[...]
        sin = sin_ref[...]
        k = _rope64(k, cos, sin)
        reps = ATT_WIDTH // LANES
        q = _rope64(q, jnp.concatenate([cos] * reps, axis=1), jnp.concatenate([sin] * reps, axis=1))
    q_ref[...] = (q * (ATT_HEAD_DIM ** -0.5)).astype(BF16)
    ka_ref[...] = k.astype(BF16)
    va_ref[...] = v.astype(BF16)
    z_ref[...] = proj[:, o3:o4].astype(z_ref.dtype)
    xbc_ref[...] = proj[:, o4:o5]
    dt_ref[...] = proj[:, o5:]


def _l0_in(x, mod, rows_per_mod, p, rope, tm):
    m = x.shape[0]
    use_rope = rope is not None
    row = lambda w: pl.BlockSpec((tm, w), lambda i: (i, 0))
    in_specs = ([row(D_MODEL), _const_spec((1, D_MODEL))] + _mod_specs(tm, rows_per_mod, (0, 1))
                + [_resident_spec((D_MODEL, L0_IN_PAD)), _const_spec((1, ATT_WIDTH)), _const_spec((1, ATT_KV_WIDTH)),
                   _resident_spec((ATT_WIDTH, ATT_WIDTH)), _resident_spec((ATT_KV_WIDTH, ATT_KV_WIDTH))])
    args = [x, p["norm_mix"], mod, mod, p["w_in"], p["q_gain"], p["k_gain"], p["pq"], p["pk"]]
    if use_rope:
        rows = rope[0].shape[0]
        in_specs += [pl.BlockSpec((tm, LANES), lambda i: (i % (rows // tm), 0))] * 2
        args += list(rope)
    widths = [(ATT_WIDTH, BF16), (ATT_KV_WIDTH, BF16), (ATT_KV_WIDTH, BF16), (ATT_KV_WIDTH, F32), (ATT_KV_WIDTH, F32),
              (SSD_WIDTH, BF16), (SSD_XBC, F32), (LANES, F32)]
    return pl.pallas_call(
        functools.partial(_l0_in_kernel, use_rope=use_rope),
        grid=(m // tm,),
        in_specs=in_specs,
        out_specs=[row(w) for w, _ in widths],
        out_shape=[jax.ShapeDtypeStruct((m, w), dt) for w, dt in widths],
        compiler_params=_params(1),
        name="l0_in",
    )(*args)


def _attn_kernel(q_ref, k_ref, v_ref, o_ref, s0_ref, s1_ref, m0_ref, m1_ref, *, ck, sb):
    nchunk = k_ref.shape[2] // ck
    tt = sb // ATT_GROUP
    nsub = q_ref.shape[1] // tt
    slots = ((s0_ref, m0_ref), (s1_ref, m1_ref))

    def scores(i, slot):
        s_ref, m_ref = slots[slot]
        q4 = q_ref[0, pl.ds(pl.multiple_of(i * tt, tt), tt), :]
        q = jnp.concatenate([q4[:, g * ATT_HEAD_DIM:(g + 1) * ATT_HEAD_DIM] for g in range(ATT_GROUP)], axis=0)
        mx = None
        for j in range(nchunk):
            s = _dot_nt(q, k_ref[0, 0, j * ck:(j + 1) * ck, :])
            s_ref[j] = s
            for t in range(ck // LANES):
                part = s[:, t * LANES:(t + 1) * LANES]
                mx = part if mx is None else jnp.maximum(mx, part)
        m_ref[...] = jnp.broadcast_to(jnp.max(mx, axis=1, keepdims=True), (sb, LANES))

    def values(i, slot):
        s_ref, m_ref = slots[slot]
        m = jnp.concatenate([m_ref[...]] * (ck // LANES), axis=1)
        acc = None
        for j in range(nchunk):
            p = jnp.exp(s_ref[j] - m).astype(BF16)
            part = _dot(p, v_ref[0, 0, j * ck:(j + 1) * ck, :])
            acc = part if acc is None else acc + part
        out = acc[:, :ATT_HEAD_DIM] / acc[:, ATT_HEAD_DIM:ATT_HEAD_DIM + 1]
        out = jnp.concatenate([out[g * tt:(g + 1) * tt] for g in range(ATT_GROUP)], axis=1)
        o_ref[0, pl.ds(pl.multiple_of(i * tt, tt), tt), :] = out.astype(o_ref.dtype)

    scores(0, 0)

    def body(h, carry):
        scores(2 * h + 1, 1)
        values(2 * h, 0)
        scores(2 * h + 2, 0)
        values(2 * h + 1, 1)
        return carry

    lax.fori_loop(0, nsub // 2 - 1, body, 0)
    scores(nsub - 1, 1)
    values(nsub - 2, 0)
    values(nsub - 1, 1)


def _attention(q, k, v, ck, sb):
    b, seq, width = q.shape
    nkv, lk, d = k.shape[1:]
    gw = width // nkv
    assert (seq * ATT_GROUP) % (2 * sb) == 0 and lk % ck == 0
    return pl.pallas_call(
        functools.partial(_attn_kernel, ck=ck, sb=sb),
        grid=(b, nkv),
        in_specs=[pl.BlockSpec((1, seq, gw), lambda i, j: (i, 0, j)),
                  pl.BlockSpec((1, 1, lk, d), lambda i, j: (i, j, 0, 0)),
                  pl.BlockSpec((1, 1, lk, LANES), lambda i, j: (i, j, 0, 0))],
        out_specs=pl.BlockSpec((1, seq, gw), lambda i, j: (i, 0, j)),
        out_shape=jax.ShapeDtypeStruct(q.shape, BF16),
        scratch_shapes=[pltpu.VMEM((lk // ck, sb, ck), F32)] * 2 + [pltpu.VMEM((sb, LANES), F32)] * 2,
        compiler_params=_params(2),
        name="attention",
    )(q, k, v)


def _ssd_group_rows(pair):
    g = pair // (SSD_HEADS // 2 // SSD_GROUPS)
    return slice(g * SSD_STATE, (g + 1) * SSD_STATE)


def _ssd_stream(cur_ref, prev_ref, next_ref, dt_ref, is_first, is_last, s_ref, y_ref, ext_ref,
                cw_ref, cb_ref, dtb_ref, alog_ref, dsk_ref, *, forward):
    cn = SSD_CHUNK
    ext_ref[0:HALO, :] = jnp.where(is_first, 0.0, prev_ref[...])
    ext_ref[HALO:HALO + cn, :] = cur_ref[...]
    ext_ref[HALO + cn:, :] = jnp.where(is_last, 0.0, next_ref[...])
    u = cb_ref[...]
    for t in range(SSD_CONV):
        u = u + cw_ref[t:t + 1, :] * ext_ref[pl.ds(HALO - SSD_CONV // 2 + t, cn), :]
    act = _silu(u)
    xs = act[:, :SSD_WIDTH]
    bm = act[:, SSD_WIDTH:SSD_WIDTH + LANES]
    cm = act[:, SSD_WIDTH + LANES:]
    nh = 2 * SSD_HEADS
    dt_t = _softplus(dt_ref[...].T[:nh] + dtb_ref[...])
    a_t = dt_t * (-jnp.exp(alog_ref[...]))
    ii = lax.broadcasted_iota(jnp.int32, (cn, cn), 0)
    jj = lax.broadcasted_iota(jnp.int32, (cn, cn), 1)
    keep = (ii >= jj) if forward else (ii <= jj)
    tri_t = ((ii <= jj) if forward else (ii >= jj)).astype(BF16)
    c3 = _dot(jnp.concatenate(_split3(a_t), axis=0), tri_t)
    cum_t = c3[:nh] + c3[nh:2 * nh] + c3[2 * nh:]
    cols = jnp.concatenate([dt_t, cum_t, jnp.zeros((LANES - 2 * nh, cn), F32)], axis=0).T

    def col(lane_idx):
        return jnp.broadcast_to(cols[:, lane_idx:lane_idx + 1], (cn, LANES))

    off = 0 if forward else SSD_HEADS
    tot_col = cn - 1 if forward else 0
    lane = lax.broadcasted_iota(jnp.int32, (1, LANES), 1)
    lo = lane < SSD_HEAD_DIM
    cm_b = cm.astype(BF16)
    bm_b = bm.astype(BF16)
    gmat = {}
    bg_b = {}
    for pair in range(SSD_HEADS // 2):
        g = pair // (SSD_HEADS // 2 // SSD_GROUPS)
        in_group = lo if g == 0 else jnp.logical_not(lo)
        if g not in gmat:
            gmat[g] = _dot_nt(jnp.where(in_group, cm, 0.0).astype(BF16), bm_b)
            bg_b[g] = jnp.where(in_group, bm, 0.0).astype(BF16)
        h0 = off + 2 * pair
        h1 = h0 + 1
        ci0 = col(nh + h0)
        ci1 = col(nh + h1)
        cip = jnp.where(lo, ci0, ci1)
        m0 = (gmat[g] * jnp.exp(jnp.where(keep, ci0 - cum_t[h0:h0 + 1, :], -1e30))).astype(BF16)
        m1 = (gmat[g] * jnp.exp(jnp.where(keep, ci1 - cum_t[h1:h1 + 1, :], -1e30))).astype(BF16)
        xs_p = xs[:, pair * LANES:(pair + 1) * LANES]
        vp = xs_p * jnp.where(lo, col(h0), col(h1))
        v0 = jnp.where(lo, vp, 0.0).astype(BF16)
        v1 = jnp.where(lo, 0.0, vp).astype(BF16)
        s_old = s_ref[pair]
        y = (_dot(jnp.concatenate([m0, m1], axis=1), jnp.concatenate([v0, v1], axis=0))
             + jnp.exp(cip) * _dot(cm_b, s_old.astype(BF16)))
        totp = jnp.where(lo, cum_t[h0:h0 + 1, tot_col:tot_col + 1], cum_t[h1:h1 + 1, tot_col:tot_col + 1])
        s_ref[pair] = s_old * jnp.exp(totp) + _dot_tn(bg_b[g], (vp * jnp.exp(totp - cip)).astype(BF16))
        if forward:
            y = y + dsk_ref[:, pair * LANES:(pair + 1) * LANES] * xs_p
        y_ref[:, pair * LANES:(pair + 1) * LANES] = y.astype(y_ref.dtype)


def _ssd_kernel(*refs, nc, has_init, want_fin):
    refs = list(refs)
    cf_ref, pf_ref, nf_ref, dtf_ref, cb_ref, pb_ref, nb_ref, dtb_ref = refs[:8]
    cw_ref, cbias_ref, dtbias_ref, alog_ref, dsk_ref = refs[8:13]
    pos = 13
    s0_ref = None
    if has_init:
        s0_ref = refs[pos]
        pos += 1
    yf_ref, yb_ref = refs[pos:pos + 2]
    pos += 2
    sfin_ref = None
    if want_fin:
        sfin_ref = refs[pos]
        pos += 1
    ext_ref, sf_ref, sb_ref = refs[pos:]
    c = pl.program_id(1)

    @pl.when(c == 0)
    def _():
        sf_ref[...] = jnp.zeros_like(sf_ref)
        sb_ref[...] = jnp.zeros_like(sb_ref)
        if has_init:
            for pair in range(SSD_HEADS // 2):
                rows = _ssd_group_rows(pair)
                sf_ref[pair, rows, :] = s0_ref[0, 0, pair]
                sb_ref[pair, rows, :] = s0_ref[0, 1, pair]

    shared = (cw_ref, cbias_ref, dtbias_ref, alog_ref, dsk_ref)
    _ssd_stream(cf_ref, pf_ref, nf_ref, dtf_ref, c == 0, c == nc - 1, sf_ref, yf_ref, ext_ref, *shared,
                forward=True)
    _ssd_stream(cb_ref, pb_ref, nb_ref, dtb_ref, c == nc - 1, c == 0, sb_ref, yb_ref, ext_ref, *shared,
                forward=False)

    if want_fin:
        @pl.when(c == nc - 1)
        def _():
            for pair in range(SSD_HEADS // 2):
                rows = _ssd_group_rows(pair)
                sfin_ref[0, 0, pair] = sf_ref[pair, rows, :]
                sfin_ref[0, 1, pair] = sb_ref[pair, rows, :]


def _ssd(xbc, dt, p, s0, batch, want_fin):
    m = xbc.shape[0]
    cn = SSD_CHUNK
    nc = m // batch // cn
    per = cn // HALO
    n_halo = m // HALO
    has_init = s0 is not None
    npair = SSD_HEADS // 2

    def fwd(b, c):
        return b * nc + c

    def bwd(b, c):
        return b * nc + nc - 1 - c

    def stream_specs(chunk):
        return [pl.BlockSpec((cn, SSD_XBC), lambda b, c: (chunk(b, c), 0)),
                pl.BlockSpec((HALO, SSD_XBC), lambda b, c: (jnp.maximum(chunk(b, c) * per - 1, 0), 0)),
                pl.BlockSpec((HALO, SSD_XBC), lambda b, c: (jnp.minimum(chunk(b, c) * per + per, n_halo - 1), 0)),
                pl.BlockSpec((cn, LANES), lambda b, c: (chunk(b, c), 0))]

    in_specs = stream_specs(fwd) + stream_specs(bwd) + [
        _const_spec((HALO, SSD_XBC)), _const_spec((1, SSD_XBC)), _const_spec((2 * SSD_HEADS, cn)),
        _const_spec((2 * SSD_HEADS, cn)),
        _const_spec((1, SSD_WIDTH))]
    args = [xbc, xbc, xbc, dt, xbc, xbc, xbc, dt, p["conv_w"], p["conv_b"], p["dt_bias"], p["a_log"], p["d_skip"]]
    state_block = (1, 2, npair, SSD_STATE, LANES)
    if has_init:
        in_specs.append(pl.BlockSpec(state_block, lambda b, c: (b, 0, 0, 0, 0)))
        args.append(s0)
    out_specs = [pl.BlockSpec((cn, SSD_WIDTH), lambda b, c: (fwd(b, c), 0)),
                 pl.BlockSpec((cn, SSD_WIDTH), lambda b, c: (bwd(b, c), 0))]
    out_shape = [jax.ShapeDtypeStruct((m, SSD_WIDTH), BF16)] * 2
    if want_fin:
        out_specs.append(pl.BlockSpec(state_block, lambda b, c: (b, 0, 0, 0, 0)))
        out_shape.append(jax.ShapeDtypeStruct((batch,) + state_block[1:], F32))
    return pl.pallas_call(
        functools.partial(_ssd_kernel, nc=nc, has_init=has_init, want_fin=want_fin),
        grid=(batch, nc),
        in_specs=in_specs,
        out_specs=out_specs,
        out_shape=out_shape,
        scratch_shapes=[pltpu.VMEM((cn + 2 * HALO, SSD_XBC), F32),
                        pltpu.VMEM((npair, LANES, LANES), F32), pltpu.VMEM((npair, LANES, LANES), F32)],
        compiler_params=_params(2),
        name="ssd",
    )(*args)


def _ssd_state_to_pairs(s):
    b = s.shape[0]
    npair = SSD_HEADS // 2
    s = s.reshape(b, 2, npair, 2, SSD_STATE, SSD_HEAD_DIM).transpose(0, 1, 2, 4, 3, 5)
    return s.reshape(b, 2, npair, SSD_STATE, 2 * SSD_HEAD_DIM)


def _ssd_state_from_pairs(s):
    b = s.shape[0]
    npair = SSD_HEADS // 2
    s = s.reshape(b, 2, npair, SSD_STATE, 2, SSD_HEAD_DIM).transpose(0, 1, 2, 4, 3, 5)
    return s.reshape(b, 2, SSD_HEADS, SSD_STATE, SSD_HEAD_DIM)


def _l1_in_kernel(*refs, use_rope):
    if use_rope:
        x_ref, nw_ref, sh_ref, sc_ref, w_ref, cos_ref, sin_ref, q_ref, k_ref, v_ref, g_ref = refs
    else:
        x_ref, nw_ref, sh_ref, sc_ref, w_ref, q_ref, k_ref, v_ref, g_ref = refs
    h = (_rms(x_ref[...]) * nw_ref[...] * (1.0 + sc_ref[0]) + sh_ref[0]).astype(BF16)
    half = RET_DK // 2

    def rope(t):
        if not use_rope:
            return t
        cos = cos_ref[...]
        sin = sin_ref[...]
        parts = []
        for hd in range(RET_HEADS):
            x1 = t[:, hd * RET_DK:hd * RET_DK + half]
            x2 = t[:, hd * RET_DK + half:(hd + 1) * RET_DK]
            parts += [x1 * cos - x2 * sin, x2 * cos + x1 * sin]
        return jnp.concatenate(parts, axis=1)

    o1 = RET_QK_WIDTH
    o2 = 2 * RET_QK_WIDTH
    o3 = o2 + RET_V_WIDTH
    q_ref[...] = rope(_dot(h, w_ref[:, :o1])).astype(BF16)
    k_ref[...] = rope(_dot(h, w_ref[:, o1:o2])) * (RET_DK ** -0.5)
    v_ref[...] = _dot(h, w_ref[:, o2:o3]).astype(BF16)
    g_ref[...] = _dot(h, w_ref[:, o3:]).astype(g_ref.dtype)


def _l1_in(x, mod, rows_per_mod, p, rope, tm):
    m = x.shape[0]
    use_rope = rope is not None
    row = lambda w: pl.BlockSpec((tm, w), lambda i: (i, 0))
    n = 2 * RET_QK_WIDTH + 2 * RET_V_WIDTH
    in_specs = ([row(D_MODEL), _const_spec((1, D_MODEL))] + _mod_specs(tm, rows_per_mod, (0, 1))
                + [_resident_spec((D_MODEL, n))])
    args = [x, p["norm_mix"], mod, mod, p["w_in"]]
    if use_rope:
        rows = rope[0].shape[0]
        in_specs += [pl.BlockSpec((tm, LANES), lambda i: (i % (rows // tm), 0))] * 2
        args += list(rope)
    widths = [(RET_QK_WIDTH, BF16), (RET_QK_WIDTH, F32), (RET_V_WIDTH, BF16), (RET_V_WIDTH, BF16)]
    return pl.pallas_call(
        functools.partial(_l1_in_kernel, use_rope=use_rope),
        grid=(m // tm,),
        in_specs=in_specs,
        out_specs=[row(w) for w, _ in widths],
        out_shape=[jax.ShapeDtypeStruct((m, w), dt) for w, dt in widths],
        compiler_params=_params(1),
        name="l1_in",
    )(*args)


def _ret_kernel(*refs, cn, nc, has_init, want_fin):
    stateless = (not has_init) and nc == 1
    refs = list(refs)
    dec_ref, qf_ref, kf_ref, vf_ref = refs[:4]
    pos = 4
    if stateless:
        qb_ref, kb_ref, vb_ref = qf_ref, kf_ref, vf_ref
    else:
        qb_ref, kb_ref, vb_ref = refs[pos:pos + 3]
        pos += 3
    s0_ref = None
    if has_init:
        s0_ref = refs[pos]
        pos += 1
    yf_ref = refs[pos]
    pos += 1
    yb_ref = None
    if not stateless:
        yb_ref = refs[pos]
        pos += 1
    sfin_ref = None
    if want_fin:
        sfin_ref = refs[pos]
        pos += 1
    decay_ref, sf_ref, sb_ref = refs[pos:]
    c = pl.program_id(1)
    log_g = -jnp.exp(dec_ref[...])

    @pl.when((pl.program_id(0) == 0) & (c == 0))
    def _():
        ii = lax.broadcasted_iota(jnp.int32, (cn, cn), 0)
        jj = lax.broadcasted_iota(jnp.int32, (cn, cn), 1)
        dist = (ii - jj).astype(F32)
        for hd in range(RET_HEADS):
            gf = log_g[0:1, hd:hd + 1]
            gb = log_g[1:2, hd:hd + 1]
            decay_ref[hd] = (jnp.where(dist >= 0, jnp.exp(gf * jnp.maximum(dist, 0.0)), 0.0)
                             + jnp.where(dist <= 0, jnp.exp(gb * jnp.maximum(-dist, 0.0)), 0.0))

    if not stateless:
        @pl.when(c == 0)
        def _():
            if has_init:
                sf_ref[...] = s0_ref[0, 0]
                sb_ref[...] = s0_ref[0, 1]
            else:
                sf_ref[...] = jnp.zeros_like(sf_ref)
                sb_ref[...] = jnp.zeros_like(sb_ref)

    ri = lax.broadcasted_iota(jnp.int32, (cn, 1), 0).astype(F32)
    for hd in range(RET_HEADS):
        gf = log_g[0:1, hd:hd + 1]
        gb = log_g[1:2, hd:hd + 1]
        qs = slice(hd * RET_DK, (hd + 1) * RET_DK)
        vs = slice(hd * RET_DV, (hd + 1) * RET_DV)
        q = qf_ref[:, qs]
        k = kf_ref[:, qs]
        v = vf_ref[:, vs]
        y = _dot((_dot_nt(q, k.astype(BF16)) * decay_ref[hd]).astype(BF16), v)
        upd_f = _dot_tn((k * jnp.exp(gf * (cn - 1.0 - ri))).astype(BF16), v)
        if stateless:
            new_f = upd_f
        else:
            s_old = sf_ref[hd]
            y = y + jnp.exp(gf * (ri + 1.0)) * _dot(q, s_old.astype(BF16))
            new_f = s_old * jnp.exp(gf * cn) + upd_f
            sf_ref[hd] = new_f
        yf_ref[:, vs] = y.astype(yf_ref.dtype)
        q = qb_ref[:, qs]
        k = kb_ref[:, qs]
        v = vb_ref[:, vs]
        upd_b = _dot_tn((k * jnp.exp(gb * ri)).astype(BF16), v)
        if stateless:
            new_b = upd_b
        else:
            s_old = sb_ref[hd]
            yb_ref[:, vs] = (jnp.exp(gb * (cn - ri)) * _dot(q, s_old.astype(BF16))).astype(yb_ref.dtype)
            new_b = s_old * jnp.exp(gb * cn) + upd_b
            sb_ref[hd] = new_b
        if want_fin:
            if stateless:
                sfin_ref[0, 0, hd] = new_f
                sfin_ref[0, 1, hd] = new_b
            else:
                @pl.when(c == nc - 1)
                def _(new_f=new_f, new_b=new_b, hd=hd):
                    sfin_ref[0, 0, hd] = new_f
                    sfin_ref[0, 1, hd] = new_b


def _retention(q, k, v, decay, s0, batch, cn, want_fin):
    m = q.shape[0]
    nc = m // batch // cn
    has_init = s0 is not None
    stateless = (not has_init) and nc == 1

    def fwd(b, c):
        return b * nc + c

    def bwd(b, c):
        return b * nc + nc - 1 - c

    def stream_specs(chunk):
        return [pl.BlockSpec((cn, RET_QK_WIDTH), lambda b, c: (chunk(b, c), 0)),
                pl.BlockSpec((cn, RET_QK_WIDTH), lambda b, c: (chunk(b, c), 0)),
                pl.BlockSpec((cn, RET_V_WIDTH), lambda b, c: (chunk(b, c), 0))]

    in_specs = [_const_spec((8, LANES))] + stream_specs(fwd)
    args = [decay, q, k, v]
    if not stateless:
        in_specs += stream_specs(bwd)
        args += [q, k, v]
    state_block = (1, 2, RET_HEADS, RET_DK, RET_DV)
    if has_init:
        in_specs.append(pl.BlockSpec(state_block, lambda b, c: (b, 0, 0, 0, 0)))
        args.append(s0)
    out_specs = [pl.BlockSpec((cn, RET_V_WIDTH), lambda b, c: (fwd(b, c), 0))]
    if not stateless:
        out_specs.append(pl.BlockSpec((cn, RET_V_WIDTH), lambda b, c: (bwd(b, c), 0)))
    out_shape = [jax.ShapeDtypeStruct((m, RET_V_WIDTH), BF16)] * len(out_specs)
    if want_fin:
        out_specs.append(pl.BlockSpec(state_block, lambda b, c: (b, 0, 0, 0, 0)))
        out_shape.append(jax.ShapeDtypeStruct((batch,) + state_block[1:], F32))
    return pl.pallas_call(
        functools.partial(_ret_kernel, cn=cn, nc=nc, has_init=has_init, want_fin=want_fin),
        grid=(batch, nc),
        in_specs=in_specs,
        out_specs=out_specs,
        out_shape=out_shape,
        scratch_shapes=[pltpu.VMEM((RET_HEADS, cn, cn), F32)] + [pltpu.VMEM((RET_HEADS, RET_DK, RET_DV), F32)] * 2,
        compiler_params=_params(2),
        name="retention",
    )(*args)


def _post_kernel(*refs, mixer, final):
    refs = list(refs)
    x_ref, g1_ref, nw_ref, sh2_ref, sc2_ref, g2_ref, wout_ref, wg_ref, wu_ref, wd_ref = refs[:10]
    pos = 10
    fn_ref = None
    if final:
        fn_ref = refs[pos]
        pos += 1
    if mixer == "ab":
        att_ref, yf_ref, yb_ref, z_ref, gain_ref, o_ref = refs[pos:]
        y = (yf_ref[...].astype(F32) + yb_ref[...].astype(F32)) * _silu(z_ref[...].astype(F32))
        y = _rms(y) * gain_ref[...]
        mix = _dot(att_ref[...], wout_ref[:ATT_WIDTH, :]) + _dot(y.astype(BF16), wout_ref[ATT_WIDTH:, :])
    else:
        parts = refs[pos:-3]
        gate_ref, gain_ref, o_ref = refs[-3:]
        mix = None
        for hd in range(RET_HEADS):
            vs = slice(hd * RET_DV, (hd + 1) * RET_DV)
            y = parts[0][:, vs].astype(F32)
            for extra in parts[1:]:
                y = y + extra[:, vs].astype(F32)
            y = _rms(y) * gain_ref[:, vs]
            part = _dot((_silu(gate_ref[:, vs].astype(F32)) * y).astype(BF16), wout_ref[vs, :])
            mix = part if mix is None else mix + part
    x1 = x_ref[...] + g1_ref[0] * mix
    h = (_rms(x1) * nw_ref[...] * (1.0 + sc2_ref[0]) + sh2_ref[0]).astype(BF16)
    wcols = D_FF // FF_SPLIT
    ffn = None
    for j in range(FF_SPLIT):
        cs = slice(j * wcols, (j + 1) * wcols)
        act = (_silu(_dot(h, wg_ref[:, cs])) * _dot(h, wu_ref[:, cs])).astype(BF16)
        part = _dot(act, wd_ref[cs, :])
        ffn = part if ffn is None else ffn + part
    x2 = x1 + g2_ref[0] * ffn
    if final:
        x2 = _rms(x2) * fn_ref[...]
    o_ref[...] = x2


def _post(x, mod, rows_per_mod, p, mixer, mix_inputs, gain, final_norm, tm):
    m = x.shape[0]
    row = lambda w: pl.BlockSpec((tm, w), lambda i: (i, 0))
    mixw = p["w_out"].shape[0]
    (g1,) = _mod_specs(tm, rows_per_mod, (2,))
    sh2, sc2, g2 = _mod_specs(tm, rows_per_mod, (3, 4, 5))
    in_specs = [row(D_MODEL), g1, _const_spec((1, D_MODEL)), sh2, sc2, g2,
                _resident_spec((mixw, D_MODEL)), _resident_spec((D_MODEL, D_FF)), _resident_spec((D_MODEL, D_FF)),
                _resident_spec((D_FF, D_MODEL))]
    args = [x, mod, p["norm_ffn"], mod, mod, mod, p["w_out"], p["w_gate"], p["w_up"], p["w_down"]]
    final = final_norm is not None
    if final:
        in_specs.append(_const_spec((1, D_MODEL)))
        args.append(final_norm)
    in_specs += [row(a.shape[1]) for a in mix_inputs] + [_const_spec(gain.shape)]
    args += list(mix_inputs) + [gain]
    return pl.pallas_call(
        functools.partial(_post_kernel, mixer=mixer, final=final),
        grid=(m // tm,),
        in_specs=in_specs,
        out_specs=row(D_MODEL),
        out_shape=jax.ShapeDtypeStruct((m, D_MODEL), F32),
        compiler_params=_params(1),
        name="post_" + mixer,
    )(*args)


def _axial_angles(n_tokens, dim):
    rows = n_tokens // GRID_W
    row = jnp.repeat(jnp.arange(rows), GRID_W).astype(F32)
    col = jnp.tile(jnp.arange(GRID_W), rows).astype(F32)
    n_freq = dim // 4
    inv = ROPE_THETA ** (-jnp.arange(n_freq, dtype=F32) / n_freq)
    return jnp.concatenate([row[:, None] * inv, col[:, None] * inv], axis=-1)


def _head_mean_matrix(width, head):
    idx = jnp.arange(width) // head
    return jnp.where(idx[:, None] == idx[None, :], 1.0 / head, 0.0).astype(BF16)


def _rows_bcast(v, width):
    return jnp.broadcast_to(v.reshape(-1, 1), (v.size, width))


def _trunk(x, mods, rows_per_mod, p0, p1, final_norm, rope_att, rope_ret, caches, seq):
    m = x.shape[0]
    batch = m // seq
    sample = caches is not None
    tm = 512
    q, ka, va, kf, vf, z, xbc, dt = _l0_in(x, mods[0], rows_per_mod, p0, rope_att, tm)
    kh = ka.reshape(batch, seq, ATT_KV_HEADS, ATT_HEAD_DIM).transpose(0, 2, 1, 3)
    vh = va.reshape(batch, seq, ATT_KV_HEADS, ATT_HEAD_DIM).transpose(0, 2, 1, 3)
    s0_ssd = None
    s0_ret = None
    if sample:
        cache_k, cache_v, state_ssd, state_ret = caches
        kh = jnp.concatenate([cache_k.astype(BF16).transpose(0, 2, 1, 3), kh], axis=2)
        vh = jnp.concatenate([cache_v.astype(BF16).transpose(0, 2, 1, 3), vh], axis=2)
        s0_ssd = _ssd_state_to_pairs(state_ssd)
        s0_ret = state_ret
    ones_col = jnp.zeros(vh.shape[:3] + (LANES - ATT_HEAD_DIM,), BF16).at[..., 0].set(1.0)
    att = _attention(q.reshape(batch, seq, ATT_WIDTH), kh, jnp.concatenate([vh, ones_col], axis=-1), 256, 256)
    att = att.reshape(m, ATT_WIDTH)
    ssd_out = _ssd(xbc, dt, p0, s0_ssd, batch, want_fin=not sample)
    x = _post(x, mods[0], rows_per_mod, p0, "ab", [att, ssd_out[0], ssd_out[1], z], p0["ssd_gain"], None, tm)
    q1, k1, v1, g1 = _l1_in(x, mods[1], rows_per_mod, p1, rope_ret, tm)
    ret_out = _retention(q1, k1, v1, p1["decay"], s0_ret, batch, RET_CHUNK, want_fin=not sample)
    y_parts = ret_out if sample else ret_out[:-1]
    y = _post(x, mods[1], rows_per_mod, p1, "c", list(y_parts) + [g1], p1["ret_gain"], final_norm, tm)
    if sample:
        return y, None
    return y, (kf, vf, _ssd_state_from_pairs(ssd_out[2]), ret_out[-1])


def kernel(x_prompt, x_sample, c, cache_k0, cache_v0, state_ssd0, state_ret1, c_ctx, l0_w_ada, l0_b_ada, l0_norm_mix, l0_norm_ffn, l0_w_in, l0_w_out, l0_q_gain, l0_k_gain, l0_conv_w, l0_conv_b, l0_dt_bias, l0_a_log, l0_d_skip, l0_ssd_gain, l0_w_gate, l0_w_up, l0_w_down, l1_w_ada, l1_b_ada, l1_norm_mix, l1_norm_ffn, l1_w_in, l1_w_out, l1_decay, l1_ret_gain, l1_w_gate, l1_w_up, l1_w_down, final_norm):
    b_ctx, seq_ctx, d = x_prompt.shape
    b_lat, seq_lat, _ = x_sample.shape
    assert d == D_MODEL and l0_w_in.shape == (D_MODEL, L0_IN) and l0_w_gate.shape == (D_MODEL, D_FF)
    row = lambda v: v.reshape(1, -1)

    p0 = dict(
        norm_mix=row(l0_norm_mix), norm_ffn=row(l0_norm_ffn),
        w_in=jnp.pad(l0_w_in, ((0, 0), (0, L0_IN_PAD - L0_IN))).astype(BF16),
        w_out=l0_w_out.astype(BF16), w_gate=l0_w_gate.astype(BF16), w_up=l0_w_up.astype(BF16),
        w_down=l0_w_down.astype(BF16),
        q_gain=row(jnp.tile(l0_q_gain, ATT_HEADS)), k_gain=row(jnp.tile(l0_k_gain, ATT_KV_HEADS)),
        pq=_head_mean_matrix(ATT_WIDTH, ATT_HEAD_DIM), pk=_head_mean_matrix(ATT_KV_WIDTH, ATT_HEAD_DIM),
        conv_w=jnp.pad(l0_conv_w, ((0, HALO - SSD_CONV), (0, 0))), conv_b=row(l0_conv_b),
        dt_bias=_rows_bcast(l0_dt_bias, SSD_CHUNK), a_log=_rows_bcast(l0_a_log, SSD_CHUNK),
        d_skip=row(jnp.repeat(l0_d_skip, SSD_HEAD_DIM)), ssd_gain=row(l0_ssd_gain),
    )
    p1 = dict(
        norm_mix=row(l1_norm_mix), norm_ffn=row(l1_norm_ffn),
        w_in=l1_w_in.astype(BF16), w_out=l1_w_out.astype(BF16), w_gate=l1_w_gate.astype(BF16),
        w_up=l1_w_up.astype(BF16), w_down=l1_w_down.astype(BF16),
        decay=jnp.pad(l1_decay, ((0, 8 - l1_decay.shape[0]), (0, LANES - l1_decay.shape[1]))),
        ret_gain=row(l1_ret_gain),
    )
    fnorm = row(final_norm)

    n_cond = 8
    conds = jnp.concatenate([c_ctx[None, :], c, jnp.zeros((n_cond - 1 - b_lat, d), F32)], axis=0)
    mod0 = _ada(conds, l0_w_ada, l0_b_ada)
    mod1 = _ada(conds, l1_w_ada, l1_b_ada)
    mods_ctx = [mod[0:1].reshape(1, 1, 6 * d) for mod in (mod0, mod1)]
    mods_lat = [mod[1:1 + b_lat].reshape(b_lat, 1, 6 * d) for mod in (mod0, mod1)]

    m_ctx = b_ctx * seq_ctx
    y_prompt, ctx = _trunk(x_prompt.reshape(m_ctx, d), mods_ctx, m_ctx, p0, p1, fnorm, None, None, None, seq_ctx)
    new_k0, new_v0, new_ssd0, new_ret1 = ctx
    ang = _axial_angles(seq_lat, ATT_HEAD_DIM)
    cos, sin = jnp.cos(ang), jnp.sin(ang)
    reps = LANES // ATT_HEAD_DIM
    rope_att = (jnp.tile(jnp.concatenate([cos, cos], axis=1), (1, reps)),
                jnp.tile(jnp.concatenate([-sin, sin], axis=1), (1, reps)))
    ang = _axial_angles(seq_lat, RET_DK)
    rope_ret = (jnp.cos(ang), jnp.sin(ang))
    caches = (cache_k0, cache_v0, state_ssd0, state_ret1)
    y_sample, _ = _trunk(x_sample.reshape(b_lat * seq_lat, d), mods_lat, seq_lat, p0, p1, fnorm, rope_att, rope_ret,
                         caches, seq_lat)
    return (y_prompt.reshape(b_ctx, seq_ctx, d), y_sample.reshape(b_lat, seq_lat, d),
            new_k0.reshape(b_ctx, seq_ctx, ATT_KV_HEADS, ATT_HEAD_DIM),
            new_v0.reshape(b_ctx, seq_ctx, ATT_KV_HEADS, ATT_HEAD_DIM), new_ssd0, new_ret1)
```

```python
import functools

import jax
import jax.numpy as jnp
import numpy as np
from jax import lax
from jax.experimental import pallas as pl
from jax.experimental.pallas import tpu as pltpu

F32 = jnp.float32
BF16 = jnp.bfloat16

EPS = 1e-6
ROPE_THETA = 10000.0
GRID_W = 64
D_MODEL = 1024
ATT_HEAD_DIM = 64
ATT_HEADS = 8
ATT_KV_HEADS = 2
ATT_GROUP = ATT_HEADS // ATT_KV_HEADS
ATT_WIDTH = ATT_HEADS * ATT_HEAD_DIM
ATT_KV_WIDTH = ATT_KV_HEADS * ATT_HEAD_DIM
SSD_WIDTH = 512
SSD_HEADS = 8
SSD_HEAD_DIM = 64
SSD_STATE = 64
SSD_GROUPS = 2
SSD_CONV = 5
SSD_XBC = SSD_WIDTH + 2 * SSD_GROUPS * SSD_STATE
L0_IN = ATT_WIDTH + 2 * ATT_KV_WIDTH + SSD_WIDTH + SSD_XBC + 2 * SSD_HEADS
RET_HEADS = 4
RET_DK = 256
RET_DV = 512
RET_QK_WIDTH = RET_HEADS * RET_DK
RET_V_WIDTH = RET_HEADS * RET_DV
D_FF = 2816

LANES = 128
HALO = 8
L0_IN_PAD = -(-L0_IN // LANES) * LANES
SSD_CHUNK = 128
RET_CHUNK = 256
FF_SPLIT = 2
VMEM_LIMIT = 56 * 1024 * 1024


def _dot(a, b):
    return jnp.dot(a, b, preferred_element_type=F32)


def _dot_nt(a, b):
    return lax.dot_general(a, b, (((1,), (1,)), ((), ())), preferred_element_type=F32)


def _dot_tn(a, b):
    return lax.dot_general(a, b, (((0,), (0,)), ((), ())), preferred_element_type=F32)


def _silu(x):
    return x / (1.0 + jnp.exp(-x))


def _softplus(x):
    return jnp.maximum(x, 0.0) + jnp.log1p(jnp.exp(-jnp.abs(x)))


def _rms(x):
    return x * lax.rsqrt(jnp.mean(x * x, axis=-1, keepdims=True) + EPS)


def _split3(x):
    hi = x.astype(BF16)
    r = x - hi.astype(F32)
    mid = r.astype(BF16)
    lo = (r - mid.astype(F32)).astype(BF16)
    return hi, mid, lo


def _const_spec(shape):
    return pl.BlockSpec(shape, lambda *_: (0,) * len(shape))


def _resident_spec(shape):
    return pl.BlockSpec(shape, lambda *_: (0,) * len(shape), pipeline_mode=pl.Buffered(1))


def _params(n_axes, vmem=VMEM_LIMIT):
    return pltpu.CompilerParams(dimension_semantics=("arbitrary",) * n_axes, vmem_limit_bytes=vmem)


def _ada_kernel(c_ref, w_ref, b_ref, o_ref):
    s = _silu(c_ref[...])
    o_ref[...] = _dot(s.astype(BF16), w_ref[...].astype(BF16)) + b_ref[...]


def _ada(conds, w, b):
    n = w.shape[1]
    tn = 1536
    return pl.pallas_call(
        _ada_kernel,
        grid=(n // tn,),
        in_specs=[_const_spec(conds.shape),
                  pl.BlockSpec((D_MODEL, tn), lambda j: (0, j)),
                  pl.BlockSpec((1, tn), lambda j: (0, j))],
        out_specs=pl.BlockSpec((conds.shape[0], tn), lambda j: (0, j)),
        out_shape=jax.ShapeDtypeStruct((conds.shape[0], n), F32),
        compiler_params=_params(1),
        name="ada",
    )(conds, w, b.reshape(1, n))


def _mod_specs(tm, rows_per_mod, which):
    return [pl.BlockSpec((1, 1, D_MODEL), lambda i, j=j: ((i * tm) // rows_per_mod, 0, j)) for j in which]


def _head_rms(x, p_ref, gain):
    x2 = x * x
    hi = x2.astype(BF16)
    lo = (x2 - hi.astype(F32)).astype(BF16)
    ms = _dot(hi, p_ref[...]) + _dot(lo, p_ref[...])
    return x * lax.rsqrt(ms + EPS) * gain


def _rope64(x, cos, sin):
    n = x.shape[1]
    lane = lax.broadcasted_iota(jnp.int32, x.shape, 1)
    first_half = (lane % ATT_HEAD_DIM) < (ATT_HEAD_DIM // 2)
    partner = jnp.where(first_half, pltpu.roll(x, n - ATT_HEAD_DIM // 2, 1), pltpu.roll(x, ATT_HEAD_DIM // 2, 1))
    return x * cos + partner * sin


def _l0_in_kernel(*refs, use_rope, ctx_seq):
    refs = list(refs)
    x_ref, nw_ref, sh_ref, sc_ref, w_ref, qg_ref, kg_ref, pq_ref, pk_ref = refs[:9]
    pos = 9
    if use_rope:
        cos_ref, sin_ref = refs[pos:pos + 2]
        pos += 2
    q_ref, ka_ref, va_ref, z_ref, xbc_ref, dt_ref = refs[pos:pos + 6]
    pos += 6
    h = _rms(x_ref[...]) * nw_ref[...] * (1.0 + sc_ref[0]) + sh_ref[0]
    proj = _dot(h.astype(BF16), w_ref[...])
    o1 = ATT_WIDTH
    o2 = o1 + ATT_KV_WIDTH
    o3 = o2 + ATT_KV_WIDTH
    o4 = o3 + SSD_WIDTH
    o5 = o4 + SSD_XBC
    q = _head_rms(proj[:, :o1], pq_ref, qg_ref[...])
    k = _head_rms(proj[:, o1:o2], pk_ref, kg_ref[...])
    v = proj[:, o2:o3]
    if ctx_seq:
        kt_ref, vt_ref = refs[pos:]
        for s in range(x_ref.shape[0] // ctx_seq):
            kt = k[s * ctx_seq:(s + 1) * ctx_seq].T
            vt = v[s * ctx_seq:(s + 1) * ctx_seq].T
            for kv in range(ATT_KV_HEADS):
                kt_ref[s, kv] = kt[kv * ATT_HEAD_DIM:(kv + 1) * ATT_HEAD_DIM]
                vt_ref[s, kv] = vt[kv * ATT_HEAD_DIM:(kv + 1) * ATT_HEAD_DIM]
    if use_rope:
        cos = cos_ref[...]
        sin = sin_ref[...]
        k = _rope64(k, cos, sin)
        reps = ATT_WIDTH // LANES
        q = _rope64(q, jnp.concatenate([cos] * reps, axis=1), jnp.concatenate([sin] * reps, axis=1))
    q_ref[...] = (q * (ATT_HEAD_DIM ** -0.5)).astype(BF16)
    kb = k.astype(BF16)
    lane = lax.broadcasted_iota(jnp.int32, (1, LANES), 1)
    ones_col = jnp.where(lane == ATT_HEAD_DIM, 1.0, 0.0)
    for kv in range(ATT_KV_HEADS):
        ka_ref[kv] = kb[:, kv * ATT_HEAD_DIM:(kv + 1) * ATT_HEAD_DIM]
        vv = v if kv == 0 else pltpu.roll(v, (LANES - kv * ATT_HEAD_DIM) % LANES, 1)
        va_ref[kv] = jnp.where(lane < ATT_HEAD_DIM, vv, ones_col).astype(BF16)
    z_ref[...] = proj[:, o3:o4].astype(z_ref.dtype)
    xbc_ref[...] = proj[:, o4:o5]
    dt_ref[...] = proj[:, o5:]


def _l0_in(x, mod, rows_per_mod, p, rope, tm, ctx_seq):
    m = x.shape[0]
    use_rope = rope is not None
    row = lambda w: pl.BlockSpec((tm, w), lambda i: (i, 0))
    in_specs = ([row(D_MODEL), _const_spec((1, D_MODEL))] + _mod_specs(tm, rows_per_mod, (0, 1))
                + [_resident_spec((D_MODEL, L0_IN_PAD)), _const_spec((1, ATT_WIDTH)), _const_spec((1, ATT_KV_WIDTH)),
                   _resident_spec((ATT_WIDTH, ATT_WIDTH)), _resident_spec((ATT_KV_WIDTH, ATT_KV_WIDTH))])
    args = [x, p["norm_mix"], mod, mod, p["w_in"], p["q_gain"], p["k_gain"], p["pq"], p["pk"]]
    if use_rope:
        rows = rope[0].shape[0]
        in_specs += [pl.BlockSpec((tm, LANES), lambda i: (i % (rows // tm), 0))] * 2
        args += list(rope)
    head = lambda w: pl.BlockSpec((ATT_KV_HEADS, tm, w), lambda i: (0, i, 0))
    out_specs = [row(ATT_WIDTH), head(ATT_HEAD_DIM), head(LANES), row(SSD_WIDTH), row(SSD_XBC), row(LANES)]
    out_shape = [jax.ShapeDtypeStruct((m, ATT_WIDTH), BF16),
                 jax.ShapeDtypeStruct((ATT_KV_HEADS, m, ATT_HEAD_DIM), BF16),
                 jax.ShapeDtypeStruct((ATT_KV_HEADS, m, LANES), BF16),
                 jax.ShapeDtypeStruct((m, SSD_WIDTH), BF16), jax.ShapeDtypeStruct((m, SSD_XBC), F32),
                 jax.ShapeDtypeStruct((m, LANES), F32)]
    if ctx_seq:
        assert tm % ctx_seq == 0
        cache_block = (tm // ctx_seq, ATT_KV_HEADS, ATT_HEAD_DIM, ctx_seq)
        out_specs += [pl.BlockSpec(cache_block, lambda i: (i, 0, 0, 0))] * 2
        out_shape += [jax.ShapeDtypeStruct((m // ctx_seq,) + cache_block[1:], F32)] * 2
    return pl.pallas_call(
        functools.partial(_l0_in_kernel, use_rope=use_rope, ctx_seq=ctx_seq),
        grid=(m // tm,),
        in_specs=in_specs,
        out_specs=out_specs,
        out_shape=out_shape,
        compiler_params=_params(1),
        name="l0_in",
    )(*args)


def _attn_kernel(*refs, ck, sb, has_cache):
    if has_cache:
        q_ref, k_ref, v_ref, kc_ref, vc_ref, o_ref, s0_ref, s1_ref, m0_ref, m1_ref = refs
    else:
        q_ref, k_ref, v_ref, o_ref, s0_ref, s1_ref, m0_ref, m1_ref = refs
    chunks = []
    if has_cache:
        chunks += [(kc_ref.at[0, 0], vc_ref.at[0, 0], j * ck) for j in range(kc_ref.shape[2] // ck)]
    chunks += [(k_ref.at[0], v_ref.at[0], j * ck) for j in range(k_ref.shape[1] // ck)]
    tt = sb // ATT_GROUP
    nsub = q_ref.shape[1] // tt
    slots = ((s0_ref, m0_ref), (s1_ref, m1_ref))

    def scores(i, slot):
        s_ref, m_ref = slots[slot]
        q4 = q_ref[0, pl.ds(pl.multiple_of(i * tt, tt), tt), :]
        q = jnp.concatenate([q4[:, g * ATT_HEAD_DIM:(g + 1) * ATT_HEAD_DIM] for g in range(ATT_GROUP)], axis=0)
        mx = None
        for j, (kr, _, r0) in enumerate(chunks):
            s = _dot_nt(q, kr[r0:r0 + ck, :])
            s_ref[j] = s
            for t in range(ck // LANES):
                part = s[:, t * LANES:(t + 1) * LANES]
                mx = part if mx is None else jnp.maximum(mx, part)
        m_ref[...] = jnp.broadcast_to(jnp.max(mx, axis=1, keepdims=True), (sb, LANES))

    def values(i, slot):
        s_ref, m_ref = slots[slot]
        m = jnp.concatenate([m_ref[...]] * (ck // LANES), axis=1)
        acc = None
        for j, (_, vr, r0) in enumerate(chunks):
            p = jnp.exp(s_ref[j] - m).astype(BF16)
            part = _dot(p, vr[r0:r0 + ck, :])
            acc = part if acc is None else acc + part
        out = acc[:, :ATT_HEAD_DIM] / acc[:, ATT_HEAD_DIM:ATT_HEAD_DIM + 1]
        out = jnp.concatenate([out[g * tt:(g + 1) * tt] for g in range(ATT_GROUP)], axis=1)
        o_ref[0, pl.ds(pl.multiple_of(i * tt, tt), tt), :] = out.astype(o_ref.dtype)

    scores(0, 0)

    def body(h, carry):
        scores(2 * h + 1, 1)
        values(2 * h, 0)
        scores(2 * h + 2, 0)
        values(2 * h + 1, 1)
        return carry

    lax.fori_loop(0, nsub // 2 - 1, body, 0)
    scores(nsub - 1, 1)
    values(nsub - 2, 0)
    values(nsub - 1, 1)


def _attention(q, k, v, cache, ck, sb):
    b, seq, width = q.shape
    nkv = k.shape[0]
    gw = width // nkv
    lk = seq
    in_specs = [pl.BlockSpec((1, seq, gw), lambda i, j: (i, 0, j)),
                pl.BlockSpec((1, seq, ATT_HEAD_DIM), lambda i, j: (j, i, 0)),
                pl.BlockSpec((1, seq, LANES), lambda i, j: (j, i, 0))]
    args = [q, k, v]
    if cache is not None:
        past = cache[0].shape[2]
        assert past % ck == 0
        lk += past
        in_specs += [pl.BlockSpec((1, 1, past, ATT_HEAD_DIM), lambda i, j: (i, j, 0, 0)),
                     pl.BlockSpec((1, 1, past, LANES), lambda i, j: (i, j, 0, 0))]
        args += list(cache)
    assert (seq * ATT_GROUP) % (2 * sb) == 0 and seq % ck == 0
    return pl.pallas_call(
        functools.partial(_attn_kernel, ck=ck, sb=sb, has_cache=cache is not None),
        grid=(b, nkv),
        in_specs=in_specs,
        out_specs=pl.BlockSpec((1, seq, gw), lambda i, j: (i, 0, j)),
        out_shape=jax.ShapeDtypeStruct(q.shape, BF16),
        scratch_shapes=[pltpu.VMEM((lk // ck, sb, ck), F32)] * 2 + [pltpu.VMEM((sb, LANES), F32)] * 2,
        compiler_params=_params(2),
        name="attention",
    )(*args)


def _ssd_group_rows(pair):
    g = pair // (SSD_HEADS // 2 // SSD_GROUPS)
    return slice(g * SSD_STATE, (g + 1) * SSD_STATE)


def _ssd_stream(cur_ref, prev_ref, next_ref, dt_ref, is_first, is_last, s_ref, y_ref, ext_ref,
                cw_ref, cb_ref, dtb_ref, alog_ref, dsk_ref, *, forward):
    cn = SSD_CHUNK
    ext_ref[0:HALO, :] = jnp.where(is_first, 0.0, prev_ref[...])
    ext_ref[HALO:HALO + cn, :] = cur_ref[...]
    ext_ref[HALO + cn:, :] = jnp.where(is_last, 0.0, next_ref[...])
    u = cb_ref[...]
    for t in range(SSD_CONV):
        u = u + cw_ref[t:t + 1, :] * ext_ref[pl.ds(HALO - SSD_CONV // 2 + t, cn), :]
    act = _silu(u)
    xs = act[:, :SSD_WIDTH]
    bm = act[:, SSD_WIDTH:SSD_WIDTH + LANES]
    cm = act[:, SSD_WIDTH + LANES:]
    nh = 2 * SSD_HEADS
    dt_t = _softplus(dt_ref[...].T[:nh] + dtb_ref[...])
    a_t = dt_t * (-jnp.exp(alog_ref[...]))
    ii = lax.broadcasted_iota(jnp.int32, (cn, cn), 0)
    jj = lax.broadcasted_iota(jnp.int32, (cn, cn), 1)
    keep = (ii >= jj) if forward else (ii <= jj)
    tri_t = ((ii <= jj) if forward else (ii >= jj)).astype(BF16)
    c3 = _dot(jnp.concatenate(_split3(a_t), axis=0), tri_t)
    cum_t = c3[:nh] + c3[nh:2 * nh] + c3[2 * nh:]
    cols = jnp.concatenate([dt_t, cum_t, jnp.zeros((LANES - 2 * nh, cn), F32)], axis=0).T

    def col(lane_idx):
        return jnp.broadcast_to(cols[:, lane_idx:lane_idx + 1], (cn, LANES))

    off = 0 if forward else SSD_HEADS
    tot_col = cn - 1 if forward else 0
    lane = lax.broadcasted_iota(jnp.int32, (1, LANES), 1)
    lo = lane < SSD_HEAD_DIM
    cm_b = cm.astype(BF16)
    bm_b = bm.astype(BF16)
    gmat = {}
    bg_b = {}
    for pair in range(SSD_HEADS // 2):
        g = pair // (SSD_HEADS // 2 // SSD_GROUPS)
        in_group = lo if g == 0 else jnp.logical_not(lo)
        if g not in gmat:
            gmat[g] = _dot_nt(jnp.where(in_group, cm, 0.0).astype(BF16), bm_b)
            bg_b[g] = jnp.where(in_group, bm, 0.0).astype(BF16)
        h0 = off + 2 * pair
        h1 = h0 + 1
        ci0 = col(nh + h0)
        ci1 = col(nh + h1)
        cip = jnp.where(lo, ci0, ci1)
        m0 = (gmat[g] * jnp.exp(jnp.where(keep, ci0 - cum_t[h0:h0 + 1, :], -1e30))).astype(BF16)
        m1 = (gmat[g] * jnp.exp(jnp.where(keep, ci1 - cum_t[h1:h1 + 1, :], -1e30))).astype(BF16)
        xs_p = xs[:, pair * LANES:(pair + 1) * LANES]
        vp = xs_p * jnp.where(lo, col(h0), col(h1))
        v0 = jnp.where(lo, vp, 0.0).astype(BF16)
        v1 = jnp.where(lo, 0.0, vp).astype(BF16)
        s_old = s_ref[pair]
        y = (_dot(jnp.concatenate([m0, m1], axis=1), jnp.concatenate([v0, v1], axis=0))
             + jnp.exp(cip) * _dot(cm_b, s_old.astype(BF16)))
        totp = jnp.where(lo, cum_t[h0:h0 + 1, tot_col:tot_col + 1], cum_t[h1:h1 + 1, tot_col:tot_col + 1])
        s_ref[pair] = s_old * jnp.exp(totp) + _dot_tn(bg_b[g], (vp * jnp.exp(totp - cip)).astype(BF16))
        if forward:
            y = y + dsk_ref[:, pair * LANES:(pair + 1) * LANES] * xs_p
        y_ref[:, pair * LANES:(pair + 1) * LANES] = y.astype(y_ref.dtype)


def _ssd_kernel(*refs, nc, has_init, want_fin):
    refs = list(refs)
    cf_ref, pf_ref, nf_ref, dtf_ref, cb_ref, pb_ref, nb_ref, dtb_ref = refs[:8]
    cw_ref, cbias_ref, dtbias_ref, alog_ref, dsk_ref = refs[8:13]
    pos = 13
    s0_ref = None
    if has_init:
        s0_ref = refs[pos]
        pos += 1
    yf_ref, yb_ref = refs[pos:pos + 2]
    pos += 2
    sfin_ref = None
    if want_fin:
        sfin_ref = refs[pos]
        pos += 1
    ext_ref, sf_ref, sb_ref = refs[pos:]
    c = pl.program_id(1)

    @pl.when(c == 0)
    def _():
        sf_ref[...] = jnp.zeros_like(sf_ref)
        sb_ref[...] = jnp.zeros_like(sb_ref)
        if has_init:
            for pair in range(SSD_HEADS // 2):
                rows = _ssd_group_rows(pair)
                sf_ref[pair, rows, :] = s0_ref[0, 0, pair]
                sb_ref[pair, rows, :] = s0_ref[0, 1, pair]

    shared = (cw_ref, cbias_ref, dtbias_ref, alog_ref, dsk_ref)
    _ssd_stream(cf_ref, pf_ref, nf_ref, dtf_ref, c == 0, c == nc - 1, sf_ref, yf_ref, ext_ref, *shared,
                forward=True)
    _ssd_stream(cb_ref, pb_ref, nb_ref, dtb_ref, c == nc - 1, c == 0, sb_ref, yb_ref, ext_ref, *shared,
                forward=False)

    if want_fin:
        @pl.when(c == nc - 1)
        def _():
            for pair in range(SSD_HEADS // 2):
                rows = _ssd_group_rows(pair)
                for d, st_ref in enumerate((sf_ref, sb_ref)):
                    both = st_ref[pair, rows, :]
                    sfin_ref[0, d, 2 * pair] = both[:, :SSD_HEAD_DIM]
                    sfin_ref[0, d, 2 * pair + 1] = both[:, SSD_HEAD_DIM:]


def _ssd(xbc, dt, p, s0, batch, want_fin):
    m = xbc.shape[0]
    cn = SSD_CHUNK
    nc = m // batch // cn
    per = cn // HALO
    n_halo = m // HALO
    has_init = s0 is not None
    npair = SSD_HEADS // 2

    def fwd(b, c):
        return b * nc + c

    def bwd(b, c):
        return b * nc + nc - 1 - c

    def stream_specs(chunk):
        return [pl.BlockSpec((cn, SSD_XBC), lambda b, c: (chunk(b, c), 0)),
                pl.BlockSpec((HALO, SSD_XBC), lambda b, c: (jnp.maximum(chunk(b, c) * per - 1, 0), 0)),
                pl.BlockSpec((HALO, SSD_XBC), lambda b, c: (jnp.minimum(chunk(b, c) * per + per, n_halo - 1), 0)),
                pl.BlockSpec((cn, LANES), lambda b, c: (chunk(b, c), 0))]

    in_specs = stream_specs(fwd) + stream_specs(bwd) + [
        _const_spec((HALO, SSD_XBC)), _const_spec((1, SSD_XBC)), _const_spec((2 * SSD_HEADS, cn)),
        _const_spec((2 * SSD_HEADS, cn)),
        _const_spec((1, SSD_WIDTH))]
    args = [xbc, xbc, xbc, dt, xbc, xbc, xbc, dt, p["conv_w"], p["conv_b"], p["dt_bias"], p["a_log"], p["d_skip"]]
    state_block = (1, 2, npair, SSD_STATE, LANES)
    if has_init:
        in_specs.append(pl.BlockSpec(state_block, lambda b, c: (b, 0, 0, 0, 0)))
        args.append(s0)
    out_specs = [pl.BlockSpec((cn, SSD_WIDTH), lambda b, c: (fwd(b, c), 0)),
                 pl.BlockSpec((cn, SSD_WIDTH), lambda b, c: (bwd(b, c), 0))]
    out_shape = [jax.ShapeDtypeStruct((m, SSD_WIDTH), BF16)] * 2
    if want_fin:
        fin_block = (1, 2, SSD_HEADS, SSD_STATE, SSD_HEAD_DIM)
        out_specs.append(pl.BlockSpec(fin_block, lambda b, c: (b, 0, 0, 0, 0)))
        out_shape.append(jax.ShapeDtypeStruct((batch,) + fin_block[1:], F32))
    return pl.pallas_call(
        functools.partial(_ssd_kernel, nc=nc, has_init=has_init, want_fin=want_fin),
        grid=(batch, nc),
        in_specs=in_specs,
        out_specs=out_specs,
        out_shape=out_shape,
        scratch_shapes=[pltpu.VMEM((cn + 2 * HALO, SSD_XBC), F32),
                        pltpu.VMEM((npair, LANES, LANES), F32), pltpu.VMEM((npair, LANES, LANES), F32)],
        compiler_params=_params(2),
        name="ssd",
    )(*args)


def _ssd_state_to_pairs(s):
    b = s.shape[0]
    npair = SSD_HEADS // 2
    s = s.reshape(b, 2, npair, 2, SSD_STATE, SSD_HEAD_DIM).transpose(0, 1, 2, 4, 3, 5)
    return s.reshape(b, 2, npair, SSD_STATE, 2 * SSD_HEAD_DIM)


def _l1_in_kernel(*refs, use_rope):
    if use_rope:
        x_ref, nw_ref, sh_ref, sc_ref, w_ref, cos_ref, sin_ref, q_ref, k_ref, v_ref, g_ref = refs
    else:
        x_ref, nw_ref, sh_ref, sc_ref, w_ref, q_ref, k_ref, v_ref, g_ref = refs
    h = (_rms(x_ref[...]) * nw_ref[...] * (1.0 + sc_ref[0]) + sh_ref[0]).astype(BF16)
    half = RET_DK // 2

    def rope(t):
        if not use_rope:
            return t
        cos = cos_ref[...]
        sin = sin_ref[...]
        parts = []
        for hd in range(RET_HEADS):
            x1 = t[:, hd * RET_DK:hd * RET_DK + half]
            x2 = t[:, hd * RET_DK + half:(hd + 1) * RET_DK]
            parts += [x1 * cos - x2 * sin, x2 * cos + x1 * sin]
        return jnp.concatenate(parts, axis=1)

    o1 = RET_QK_WIDTH
    o2 = 2 * RET_QK_WIDTH
    o3 = o2 + RET_V_WIDTH
    q_ref[...] = rope(_dot(h, w_ref[:, :o1])).astype(BF16)
    k_ref[...] = rope(_dot(h, w_ref[:, o1:o2])) * (RET_DK ** -0.5)
    v_ref[...] = _dot(h, w_ref[:, o2:o3]).astype(BF16)
    g_ref[...] = _dot(h, w_ref[:, o3:]).astype(g_ref.dtype)


def _l1_in(x, mod, rows_per_mod, p, rope, tm):
    m = x.shape[0]
    use_rope = rope is not None
    row = lambda w: pl.BlockSpec((tm, w), lambda i: (i, 0))
    n = 2 * RET_QK_WIDTH + 2 * RET_V_WIDTH
    in_specs = ([row(D_MODEL), _const_spec((1, D_MODEL))] + _mod_specs(tm, rows_per_mod, (0, 1))
                + [_resident_spec((D_MODEL, n))])
    args = [x, p["norm_mix"], mod, mod, p["w_in"]]
    if use_rope:
        rows = rope[0].shape[0]
        in_specs += [pl.BlockSpec((tm, LANES), lambda i: (i % (rows // tm), 0))] * 2
        args += list(rope)
    widths = [(RET_QK_WIDTH, BF16), (RET_QK_WIDTH, F32), (RET_V_WIDTH, BF16), (RET_V_WIDTH, BF16)]
    return pl.pallas_call(
        functools.partial(_l1_in_kernel, use_rope=use_rope),
        grid=(m // tm,),
        in_specs=in_specs,
        out_specs=[row(w) for w, _ in widths],
        out_shape=[jax.ShapeDtypeStruct((m, w), dt) for w, dt in widths],
        compiler_params=_params(1),
        name="l1_in",
    )(*args)


def _ret_kernel(*refs, cn, nc, has_init, want_fin):
    stateless = (not has_init) and nc == 1
    refs = list(refs)
    dec_ref, qf_ref, kf_ref, vf_ref = refs[:4]
    pos = 4
    if stateless:
        qb_ref, kb_ref, vb_ref = qf_ref, kf_ref, vf_ref
    else:
        qb_ref, kb_ref, vb_ref = refs[pos:pos + 3]
        pos += 3
    s0_ref = None
    if has_init:
        s0_ref = refs[pos]
        pos += 1
    yf_ref = refs[pos]
    pos += 1
    yb_ref = None
    if not stateless:
        yb_ref = refs[pos]
        pos += 1
    sfin_ref = None
    if want_fin:
        sfin_ref = refs[pos]
        pos += 1
    decay_ref, sf_ref, sb_ref = refs[pos:]
    c = pl.program_id(1)
    log_g = -jnp.exp(dec_ref[...])

    @pl.when((pl.program_id(0) == 0) & (c == 0))
    def _():
        ii = lax.broadcasted_iota(jnp.int32, (cn, cn), 0)
        jj = lax.broadcasted_iota(jnp.int32, (cn, cn), 1)
        dist = (ii - jj).astype(F32)
        for hd in range(RET_HEADS):
            gf = log_g[0:1, hd:hd + 1]
            gb = log_g[1:2, hd:hd + 1]
            decay_ref[hd] = (jnp.where(dist >= 0, jnp.exp(gf * jnp.maximum(dist, 0.0)), 0.0)
                             + jnp.where(dist <= 0, jnp.exp(gb * jnp.maximum(-dist, 0.0)), 0.0))

    if not stateless:
        @pl.when(c == 0)
        def _():
            if has_init:
                sf_ref[...] = s0_ref[0, 0]
                sb_ref[...] = s0_ref[0, 1]
            else:
                sf_ref[...] = jnp.zeros_like(sf_ref)
                sb_ref[...] = jnp.zeros_like(sb_ref)

    ri = lax.broadcasted_iota(jnp.int32, (cn, 1), 0).astype(F32)
    for hd in range(RET_HEADS):
        gf = log_g[0:1, hd:hd + 1]
        gb = log_g[1:2, hd:hd + 1]
        qs = slice(hd * RET_DK, (hd + 1) * RET_DK)
        vs = slice(hd * RET_DV, (hd + 1) * RET_DV)
        q = qf_ref[:, qs]
        k = kf_ref[:, qs]
        v = vf_ref[:, vs]
        y = _dot((_dot_nt(q, k.astype(BF16)) * decay_ref[hd]).astype(BF16), v)
        upd_f = _dot_tn((k * jnp.exp(gf * (cn - 1.0 - ri))).astype(BF16), v)
        if stateless:
            new_f = upd_f
        else:
            s_old = sf_ref[hd]
            y = y + jnp.exp(gf * (ri + 1.0)) * _dot(q, s_old.astype(BF16))
            new_f = s_old * jnp.exp(gf * cn) + upd_f
            sf_ref[hd] = new_f
        yf_ref[:, vs] = y.astype(yf_ref.dtype)
        q = qb_ref[:, qs]
        k = kb_ref[:, qs]
        v = vb_ref[:, vs]
        upd_b = _dot_tn((k * jnp.exp(gb * ri)).astype(BF16), v)
        if stateless:
            new_b = upd_b
        else:
            s_old = sb_ref[hd]
            yb_ref[:, vs] = (jnp.exp(gb * (cn - ri)) * _dot(q, s_old.astype(BF16))).astype(yb_ref.dtype)
            new_b = s_old * jnp.exp(gb * cn) + upd_b
            sb_ref[hd] = new_b
        if want_fin:
            if stateless:
                sfin_ref[0, 0, hd] = new_f
                sfin_ref[0, 1, hd] = new_b
            else:
                @pl.when(c == nc - 1)
                def _(new_f=new_f, new_b=new_b, hd=hd):
                    sfin_ref[0, 0, hd] = new_f
                    sfin_ref[0, 1, hd] = new_b


def _retention(q, k, v, decay, s0, batch, cn, want_fin):
    m = q.shape[0]
    nc = m // batch // cn
    has_init = s0 is not None
    stateless = (not has_init) and nc == 1

    def fwd(b, c):
        return b * nc + c

    def bwd(b, c):
        return b * nc + nc - 1 - c

    def stream_specs(chunk):
        return [pl.BlockSpec((cn, RET_QK_WIDTH), lambda b, c: (chunk(b, c), 0)),
                pl.BlockSpec((cn, RET_QK_WIDTH), lambda b, c: (chunk(b, c), 0)),
                pl.BlockSpec((cn, RET_V_WIDTH), lambda b, c: (chunk(b, c), 0))]

    in_specs = [_const_spec((8, LANES))] + stream_specs(fwd)
    args = [decay, q, k, v]
    if not stateless:
        in_specs += stream_specs(bwd)
        args += [q, k, v]
    state_block = (1, 2, RET_HEADS, RET_DK, RET_DV)
    if has_init:
        in_specs.append(pl.BlockSpec(state_block, lambda b, c: (b, 0, 0, 0, 0)))
        args.append(s0)
    out_specs = [pl.BlockSpec((cn, RET_V_WIDTH), lambda b, c: (fwd(b, c), 0))]
    if not stateless:
        out_specs.append(pl.BlockSpec((cn, RET_V_WIDTH), lambda b, c: (bwd(b, c), 0)))
    out_shape = [jax.ShapeDtypeStruct((m, RET_V_WIDTH), BF16)] * len(out_specs)
    if want_fin:
        out_specs.append(pl.BlockSpec(state_block, lambda b, c: (b, 0, 0, 0, 0)))
        out_shape.append(jax.ShapeDtypeStruct((batch,) + state_block[1:], F32))
    return pl.pallas_call(
        functools.partial(_ret_kernel, cn=cn, nc=nc, has_init=has_init, want_fin=want_fin),
        grid=(batch, nc),
        in_specs=in_specs,
        out_specs=out_specs,
        out_shape=out_shape,
        scratch_shapes=[pltpu.VMEM((RET_HEADS, cn, cn), F32)] + [pltpu.VMEM((RET_HEADS, RET_DK, RET_DV), F32)] * 2,
        compiler_params=_params(2),
        name="retention",
    )(*args)


def _post_kernel(*refs, mixer, final):
    refs = list(refs)
    x_ref, g1_ref, nw_ref, sh2_ref, sc2_ref, g2_ref, wout_ref, wg_ref, wu_ref, wd_ref = refs[:10]
    pos = 10
    fn_ref = None
    if final:
        fn_ref = refs[pos]
        pos += 1
    if mixer == "ab":
        att_ref, yf_ref, yb_ref, z_ref, gain_ref, o_ref = refs[pos:]
        y = (yf_ref[...].astype(F32) + yb_ref[...].astype(F32)) * _silu(z_ref[...].astype(F32))
        y = _rms(y) * gain_ref[...]
        mix = _dot(att_ref[...], wout_ref[:ATT_WIDTH, :]) + _dot(y.astype(BF16), wout_ref[ATT_WIDTH:, :])
    else:
        parts = refs[pos:-3]
        gate_ref, gain_ref, o_ref = refs[-3:]
        mix = None
        for hd in range(RET_HEADS):
            vs = slice(hd * RET_DV, (hd + 1) * RET_DV)
            y = parts[0][:, vs].astype(F32)
            for extra in parts[1:]:
                y = y + extra[:, vs].astype(F32)
            y = _rms(y) * gain_ref[:, vs]
            part = _dot((_silu(gate_ref[:, vs].astype(F32)) * y).astype(BF16), wout_ref[vs, :])
            mix = part if mix is None else mix + part
    x1 = x_ref[...] + g1_ref[0] * mix
    h = (_rms(x1) * nw_ref[...] * (1.0 + sc2_ref[0]) + sh2_ref[0]).astype(BF16)
    wcols = D_FF // FF_SPLIT
    ffn = None
    for j in range(FF_SPLIT):
        cs = slice(j * wcols, (j + 1) * wcols)
        act = (_silu(_dot(h, wg_ref[:, cs])) * _dot(h, wu_ref[:, cs])).astype(BF16)
        part = _dot(act, wd_ref[cs, :])
        ffn = part if ffn is None else ffn + part
    x2 = x1 + g2_ref[0] * ffn
    if final:
        x2 = _rms(x2) * fn_ref[...]
    o_ref[...] = x2


def _post(x, mod, rows_per_mod, p, mixer, mix_inputs, gain, final_norm, tm):
    m = x.shape[0]
    row = lambda w: pl.BlockSpec((tm, w), lambda i: (i, 0))
    mixw = p["w_out"].shape[0]
    (g1,) = _mod_specs(tm, rows_per_mod, (2,))
    sh2, sc2, g2 = _mod_specs(tm, rows_per_mod, (3, 4, 5))
    in_specs = [row(D_MODEL), g1, _const_spec((1, D_MODEL)), sh2, sc2, g2,
                _resident_spec((mixw, D_MODEL)), _resident_spec((D_MODEL, D_FF)), _resident_spec((D_MODEL, D_FF)),
                _resident_spec((D_FF, D_MODEL))]
    args = [x, mod, p["norm_ffn"], mod, mod, mod, p["w_out"], p["w_gate"], p["w_up"], p["w_down"]]
    final = final_norm is not None
    if final:
        in_specs.append(_const_spec((1, D_MODEL)))
        args.append(final_norm)
    in_specs += [row(a.shape[1]) for a in mix_inputs] + [_const_spec(gain.shape)]
    args += list(mix_inputs) + [gain]
    return pl.pallas_call(
        functools.partial(_post_kernel, mixer=mixer, final=final),
        grid=(m // tm,),
        in_specs=in_specs,
        out_specs=row(D_MODEL),
        out_shape=jax.ShapeDtypeStruct((m, D_MODEL), F32),
        compiler_params=_params(1),
        name="post_" + mixer,
    )(*args)


def _axial_angles(n_tokens, dim):
    rows = n_tokens // GRID_W
    row = np.repeat(np.arange(rows), GRID_W).astype(np.float64)
    col = np.tile(np.arange(GRID_W), rows).astype(np.float64)
    n_freq = dim // 4
    inv = ROPE_THETA ** (-np.arange(n_freq, dtype=np.float64) / n_freq)
    return np.concatenate([row[:, None] * inv, col[:, None] * inv], axis=-1)


def _head_mean_matrix(width, head):
    idx = jnp.arange(width) // head
    return jnp.where(idx[:, None] == idx[None, :], 1.0 / head, 0.0).astype(BF16)


def _rows_bcast(v, width):
    return jnp.broadcast_to(v.reshape(-1, 1), (v.size, width))


def _trunk(x, mods, rows_per_mod, p0, p1, final_norm, rope_att, rope_ret, caches, seq):
    m = x.shape[0]
    batch = m // seq
    sample = caches is not None
    tm = 512
    l0 = _l0_in(x, mods[0], rows_per_mod, p0, rope_att, tm, None if sample else seq)
    q, ka, va, z, xbc, dt = l0[:6]
    s0_ssd = None
    s0_ret = None
    kv_cache = None
    if sample:
        cache_k, cache_v, state_ssd, state_ret = caches
        ck = cache_k.astype(BF16).transpose(0, 2, 1, 3)
        cv = cache_v.astype(BF16).transpose(0, 2, 1, 3)
        ones_col = jnp.zeros(cv.shape[:3] + (LANES - ATT_HEAD_DIM,), BF16).at[..., 0].set(1.0)
        kv_cache = (ck, jnp.concatenate([cv, ones_col], axis=-1))
        s0_ssd = _ssd_state_to_pairs(state_ssd)
        s0_ret = state_ret
    att = _attention(q.reshape(batch, seq, ATT_WIDTH), ka, va, kv_cache, 256, 256)
    att = att.reshape(m, ATT_WIDTH)
    ssd_out = _ssd(xbc, dt, p0, s0_ssd, batch, want_fin=not sample)
    x = _post(x, mods[0], rows_per_mod, p0, "ab", [att, ssd_out[0], ssd_out[1], z], p0["ssd_gain"], None, tm)
    q1, k1, v1, g1 = _l1_in(x, mods[1], rows_per_mod, p1, rope_ret, tm)
    ret_out = _retention(q1, k1, v1, p1["decay"], s0_ret, batch, RET_CHUNK, want_fin=not sample)
    y_parts = ret_out if sample else ret_out[:-1]
    y = _post(x, mods[1], rows_per_mod, p1, "c", list(y_parts) + [g1], p1["ret_gain"], final_norm, tm)
    if sample:
        return y, None
    new_k = l0[6].transpose(0, 3, 1, 2)
    new_v = l0[7].transpose(0, 3, 1, 2)
    return y, (new_k, new_v, ssd_out[2], ret_out[-1])


def kernel(x_prompt, x_sample, c, cache_k0, cache_v0, state_ssd0, state_ret1, c_ctx, l0_w_ada, l0_b_ada, l0_norm_mix, l0_norm_ffn, l0_w_in, l0_w_out, l0_q_gain, l0_k_gain, l0_conv_w, l0_conv_b, l0_dt_bias, l0_a_log, l0_d_skip, l0_ssd_gain, l0_w_gate, l0_w_up, l0_w_down, l1_w_ada, l1_b_ada, l1_norm_mix, l1_norm_ffn, l1_w_in, l1_w_out, l1_decay, l1_ret_gain, l1_w_gate, l1_w_up, l1_w_down, final_norm):
    b_ctx, seq_ctx, d = x_prompt.shape
    b_lat, seq_lat, _ = x_sample.shape
    assert d == D_MODEL and l0_w_in.shape == (D_MODEL, L0_IN) and l0_w_gate.shape == (D_MODEL, D_FF)
    row = lambda v: v.reshape(1, -1)

    p0 = dict(
        norm_mix=row(l0_norm_mix), norm_ffn=row(l0_norm_ffn),
        w_in=jnp.pad(l0_w_in, ((0, 0), (0, L0_IN_PAD - L0_IN))).astype(BF16),
        w_out=l0_w_out.astype(BF16), w_gate=l0_w_gate.astype(BF16), w_up=l0_w_up.astype(BF16),
        w_down=l0_w_down.astype(BF16),
        q_gain=row(jnp.tile(l0_q_gain, ATT_HEADS)), k_gain=row(jnp.tile(l0_k_gain, ATT_KV_HEADS)),
        pq=_head_mean_matrix(ATT_WIDTH, ATT_HEAD_DIM), pk=_head_mean_matrix(ATT_KV_WIDTH, ATT_HEAD_DIM),
        conv_w=jnp.pad(l0_conv_w, ((0, HALO - SSD_CONV), (0, 0))), conv_b=row(l0_conv_b),
        dt_bias=_rows_bcast(l0_dt_bias, SSD_CHUNK), a_log=_rows_bcast(l0_a_log, SSD_CHUNK),
        d_skip=row(jnp.repeat(l0_d_skip, SSD_HEAD_DIM)), ssd_gain=row(l0_ssd_gain),
    )
    p1 = dict(
        norm_mix=row(l1_norm_mix), norm_ffn=row(l1_norm_ffn),
        w_in=l1_w_in.astype(BF16), w_out=l1_w_out.astype(BF16), w_gate=l1_w_gate.astype(BF16),
        w_up=l1_w_up.astype(BF16), w_down=l1_w_down.astype(BF16),
        decay=jnp.pad(l1_decay, ((0, 8 - l1_decay.shape[0]), (0, LANES - l1_decay.shape[1]))),
        ret_gain=row(l1_ret_gain),
    )
    fnorm = row(final_norm)

    n_cond = 8
    conds = jnp.concatenate([c_ctx[None, :], c, jnp.zeros((n_cond - 1 - b_lat, d), F32)], axis=0)
    mod0 = _ada(conds, l0_w_ada, l0_b_ada)
    mod1 = _ada(conds, l1_w_ada, l1_b_ada)
    mods_ctx = [mod[0:1].reshape(1, 1, 6 * d) for mod in (mod0, mod1)]
    mods_lat = [mod[1:1 + b_lat].reshape(b_lat, 1, 6 * d) for mod in (mod0, mod1)]

    m_ctx = b_ctx * seq_ctx
    y_prompt, ctx = _trunk(x_prompt.reshape(m_ctx, d), mods_ctx, m_ctx, p0, p1, fnorm, None, None, None, seq_ctx)
    new_k0, new_v0, new_ssd0, new_ret1 = ctx
    ang = _axial_angles(seq_lat, ATT_HEAD_DIM)
    cos, sin = np.cos(ang), np.sin(ang)
    reps = LANES // ATT_HEAD_DIM
    rope_att = (jnp.asarray(np.tile(np.concatenate([cos, cos], axis=1), (1, reps)), F32),
                jnp.asarray(np.tile(np.concatenate([-sin, sin], axis=1), (1, reps)), F32))
    ang = _axial_angles(seq_lat, RET_DK)
    rope_ret = (jnp.asarray(np.cos(ang), F32), jnp.asarray(np.sin(ang), F32))
    caches = (cache_k0, cache_v0, state_ssd0, state_ret1)
    y_sample, _ = _trunk(x_sample.reshape(b_lat * seq_lat, d), mods_lat, seq_lat, p0, p1, fnorm, rope_att, rope_ret,
                         caches, seq_lat)
    return (y_prompt.reshape(b_ctx, seq_ctx, d), y_sample.reshape(b_lat, seq_lat, d),
            new_k0, new_v0, new_ssd0, new_ret1)
```

```python
import functools

import jax
import jax.numpy as jnp
import numpy as np
from jax import lax
from jax.experimental import pallas as pl
from jax.experimental.pallas import tpu as pltpu

F32 = jnp.float32
BF16 = jnp.bfloat16

EPS = 1e-6
ROPE_THETA = 10000.0
GRID_W = 64
D_MODEL = 1024
ATT_HEAD_DIM = 64
ATT_HEADS = 8
ATT_KV_HEADS = 2
ATT_GROUP = ATT_HEADS // ATT_KV_HEADS
ATT_WIDTH = ATT_HEADS * ATT_HEAD_DIM
ATT_KV_WIDTH = ATT_KV_HEADS * ATT_HEAD_DIM
SSD_WIDTH = 512
SSD_HEADS = 8
SSD_HEAD_DIM = 64
SSD_STATE = 64
SSD_GROUPS = 2
SSD_CONV = 5
SSD_XBC = SSD_WIDTH + 2 * SSD_GROUPS * SSD_STATE
L0_IN = ATT_WIDTH + 2 * ATT_KV_WIDTH + SSD_WIDTH + SSD_XBC + 2 * SSD_HEADS
RET_HEADS = 4
RET_DK = 256
RET_DV = 512
RET_QK_WIDTH = RET_HEADS * RET_DK
RET_V_WIDTH = RET_HEADS * RET_DV
D_FF = 2816

LANES = 128
HALO = 8
L0_IN_PAD = -(-L0_IN // LANES) * LANES
SSD_CHUNK = 128
RET_CHUNK = 256
FF_SPLIT = 2
VMEM_LIMIT = 56 * 1024 * 1024


def _dot(a, b):
    return jnp.dot(a, b, preferred_element_type=F32)


def _dot_nt(a, b):
    return lax.dot_general(a, b, (((1,), (1,)), ((), ())), preferred_element_type=F32)


def _dot_tn(a, b):
    return lax.dot_general(a, b, (((0,), (0,)), ((), ())), preferred_element_type=F32)


def _silu(x):
    half = 0.5 * x
    return half + half * jnp.tanh(half)


def _softplus(x):
    return jnp.maximum(x, 0.0) + jnp.log1p(jnp.exp(-jnp.abs(x)))


def _rms(x):
    return x * lax.rsqrt(jnp.mean(x * x, axis=-1, keepdims=True) + EPS)


def _split3(x):
    hi = x.astype(BF16)
    r = x - hi.astype(F32)
    mid = r.astype(BF16)
    lo = (r - mid.astype(F32)).astype(BF16)
    return hi, mid, lo


def _const_spec(shape):
    return pl.BlockSpec(shape, lambda *_: (0,) * len(shape))


def _resident_spec(shape):
    return pl.BlockSpec(shape, lambda *_: (0,) * len(shape), pipeline_mode=pl.Buffered(1))


def _params(n_axes, vmem=VMEM_LIMIT):
    return pltpu.CompilerParams(dimension_semantics=("arbitrary",) * n_axes, vmem_limit_bytes=vmem)


def _ada_kernel(c_ref, w_ref, b_ref, o_ref):
    s = _silu(c_ref[...])
    o_ref[...] = _dot(s.astype(BF16), w_ref[...].astype(BF16)) + b_ref[...]


def _ada(conds, w, b):
    n = w.shape[1]
    tn = 1536
    return pl.pallas_call(
        _ada_kernel,
        grid=(n // tn,),
        in_specs=[_const_spec(conds.shape),
                  pl.BlockSpec((D_MODEL, tn), lambda j: (0, j)),
                  pl.BlockSpec((1, tn), lambda j: (0, j))],
        out_specs=pl.BlockSpec((conds.shape[0], tn), lambda j: (0, j)),
        out_shape=jax.ShapeDtypeStruct((conds.shape[0], n), F32),
        compiler_params=_params(1),
        name="ada",
    )(conds, w, b.reshape(1, n))


def _mod_specs(tm, rows_per_mod, which):
    return [pl.BlockSpec((1, 1, D_MODEL), lambda i, j=j: ((i * tm) // rows_per_mod, 0, j)) for j in which]


def _head_rms(x, p_ref, gain):
    x2 = x * x
    hi = x2.astype(BF16)
    lo = (x2 - hi.astype(F32)).astype(BF16)
    ms = _dot(hi, p_ref[...]) + _dot(lo, p_ref[...])
    return x * lax.rsqrt(ms + EPS) * gain


def _rope64(x, cos, sin):
    n = x.shape[1]
    lane = lax.broadcasted_iota(jnp.int32, x.shape, 1)
    first_half = (lane % ATT_HEAD_DIM) < (ATT_HEAD_DIM // 2)
    partner = jnp.where(first_half, pltpu.roll(x, n - ATT_HEAD_DIM // 2, 1), pltpu.roll(x, ATT_HEAD_DIM // 2, 1))
    return x * cos + partner * sin


def _l0_in_kernel(*refs, use_rope, ctx_seq):
    refs = list(refs)
    x_ref, nw_ref, sh_ref, sc_ref, w_ref, qg_ref, kg_ref, pq_ref, pk_ref = refs[:9]
    pos = 9
    if use_rope:
        cos_ref, sin_ref = refs[pos:pos + 2]
        pos += 2
    q_ref, ka_ref, va_ref, z_ref, xbc_ref, dt_ref = refs[pos:pos + 6]
    pos += 6
    h = _rms(x_ref[...]) * nw_ref[...] * (1.0 + sc_ref[0]) + sh_ref[0]
    proj = _dot(h.astype(BF16), w_ref[...])
    o1 = ATT_WIDTH
    o2 = o1 + ATT_KV_WIDTH
    o3 = o2 + ATT_KV_WIDTH
    o4 = o3 + SSD_WIDTH
    o5 = o4 + SSD_XBC
    q = _head_rms(proj[:, :o1], pq_ref, qg_ref[...])
    k = _head_rms(proj[:, o1:o2], pk_ref, kg_ref[...])
    v = proj[:, o2:o3]
    if ctx_seq:
        kt_ref, vt_ref = refs[pos:]
        for s in range(x_ref.shape[0] // ctx_seq):
            kt = k[s * ctx_seq:(s + 1) * ctx_seq].T
            vt = v[s * ctx_seq:(s + 1) * ctx_seq].T
            for kv in range(ATT_KV_HEADS):
                kt_ref[s, kv] = kt[kv * ATT_HEAD_DIM:(kv + 1) * ATT_HEAD_DIM]
                vt_ref[s, kv] = vt[kv * ATT_HEAD_DIM:(kv + 1) * ATT_HEAD_DIM]
    if use_rope:
        cos = cos_ref[...]
        sin = sin_ref[...]
        k = _rope64(k, cos, sin)
        reps = ATT_WIDTH // LANES
        q = _rope64(q, jnp.concatenate([cos] * reps, axis=1), jnp.concatenate([sin] * reps, axis=1))
    q_ref[...] = (q * (ATT_HEAD_DIM ** -0.5)).astype(BF16)
    kb = k.astype(BF16)
    lane = lax.broadcasted_iota(jnp.int32, (1, LANES), 1)
    ones_col = jnp.where(lane == ATT_HEAD_DIM, 1.0, 0.0)
    for kv in range(ATT_KV_HEADS):
        ka_ref[kv] = kb[:, kv * ATT_HEAD_DIM:(kv + 1) * ATT_HEAD_DIM]
        vv = v if kv == 0 else pltpu.roll(v, (LANES - kv * ATT_HEAD_DIM) % LANES, 1)
        va_ref[kv] = jnp.where(lane < ATT_HEAD_DIM, vv, ones_col).astype(BF16)
    z_ref[...] = proj[:, o3:o4].astype(z_ref.dtype)
    xbc_ref[...] = proj[:, o4:o5]
    dt_ref[...] = proj[:, o5:]


def _l0_in(x, mod, rows_per_mod, p, rope, tm, ctx_seq):
    m = x.shape[0]
    use_rope = rope is not None
    row = lambda w: pl.BlockSpec((tm, w), lambda i: (i, 0))
    in_specs = ([row(D_MODEL), _const_spec((1, D_MODEL))] + _mod_specs(tm, rows_per_mod, (0, 1))
                + [_resident_spec((D_MODEL, L0_IN_PAD)), _const_spec((1, ATT_WIDTH)), _const_spec((1, ATT_KV_WIDTH)),
                   _resident_spec((ATT_WIDTH, ATT_WIDTH)), _resident_spec((ATT_KV_WIDTH, ATT_KV_WIDTH))])
    args = [x, p["norm_mix"], mod, mod, p["w_in"], p["q_gain"], p["k_gain"], p["pq"], p["pk"]]
    if use_rope:
        rows = rope[0].shape[0]
        in_specs += [pl.BlockSpec((tm, LANES), lambda i: (i % (rows // tm), 0))] * 2
        args += list(rope)
    head = lambda w: pl.BlockSpec((ATT_KV_HEADS, tm, w), lambda i: (0, i, 0))
    out_specs = [row(ATT_WIDTH), head(ATT_HEAD_DIM), head(LANES), row(SSD_WIDTH), row(SSD_XBC), row(LANES)]
    out_shape = [jax.ShapeDtypeStruct((m, ATT_WIDTH), BF16),
                 jax.ShapeDtypeStruct((ATT_KV_HEADS, m, ATT_HEAD_DIM), BF16),
                 jax.ShapeDtypeStruct((ATT_KV_HEADS, m, LANES), BF16),
                 jax.ShapeDtypeStruct((m, SSD_WIDTH), BF16), jax.ShapeDtypeStruct((m, SSD_XBC), F32),
                 jax.ShapeDtypeStruct((m, LANES), F32)]
    if ctx_seq:
        assert tm % ctx_seq == 0
        cache_block = (tm // ctx_seq, ATT_KV_HEADS, ATT_HEAD_DIM, ctx_seq)
        out_specs += [pl.BlockSpec(cache_block, lambda i: (i, 0, 0, 0))] * 2
        out_shape += [jax.ShapeDtypeStruct((m // ctx_seq,) + cache_block[1:], F32)] * 2
    return pl.pallas_call(
        functools.partial(_l0_in_kernel, use_rope=use_rope, ctx_seq=ctx_seq),
        grid=(m // tm,),
        in_specs=in_specs,
        out_specs=out_specs,
        out_shape=out_shape,
        compiler_params=_params(1),
        name="l0_in",
    )(*args)


def _attn_kernel(*refs, ck, sb, has_cache):
    if has_cache:
        q_ref, k_ref, v_ref, kc_ref, vc_ref, o_ref, s0_ref, s1_ref, m0_ref, m1_ref = refs
    else:
        q_ref, k_ref, v_ref, o_ref, s0_ref, s1_ref, m0_ref, m1_ref = refs
    chunks = []
    if has_cache:
        chunks += [(kc_ref.at[0, 0], vc_ref.at[0, 0], j * ck) for j in range(kc_ref.shape[2] // ck)]
    chunks += [(k_ref.at[0], v_ref.at[0], j * ck) for j in range(k_ref.shape[1] // ck)]
    tt = sb // ATT_GROUP
    nsub = q_ref.shape[1] // tt
    slots = ((s0_ref, m0_ref), (s1_ref, m1_ref))

    def scores(i, slot):
        s_ref, m_ref = slots[slot]
        q4 = q_ref[0, pl.ds(pl.multiple_of(i * tt, tt), tt), :]
        q = jnp.concatenate([q4[:, g * ATT_HEAD_DIM:(g + 1) * ATT_HEAD_DIM] for g in range(ATT_GROUP)], axis=0)
        mx = None
        for j, (kr, _, r0) in enumerate(chunks):
            s = _dot_nt(q, kr[r0:r0 + ck, :])
            s_ref[j] = s
            for t in range(ck // LANES):
                part = s[:, t * LANES:(t + 1) * LANES]
                mx = part if mx is None else jnp.maximum(mx, part)
        m_ref[...] = jnp.broadcast_to(jnp.max(mx, axis=1, keepdims=True), (sb, LANES))

    def values(i, slot):
        s_ref, m_ref = slots[slot]
        m = jnp.concatenate([m_ref[...]] * (ck // LANES), axis=1)
        acc = None
        for j, (_, vr, r0) in enumerate(chunks):
            p = jnp.exp(s_ref[j] - m).astype(BF16)
            part = _dot(p, vr[r0:r0 + ck, :])
            acc = part if acc is None else acc + part
        out = acc[:, :ATT_HEAD_DIM] / acc[:, ATT_HEAD_DIM:ATT_HEAD_DIM + 1]
        out = jnp.concatenate([out[g * tt:(g + 1) * tt] for g in range(ATT_GROUP)], axis=1)
        o_ref[0, pl.ds(pl.multiple_of(i * tt, tt), tt), :] = out.astype(o_ref.dtype)

    scores(0, 0)

    def body(h, carry):
        scores(2 * h + 1, 1)
        values(2 * h, 0)
        scores(2 * h + 2, 0)
        values(2 * h + 1, 1)
        return carry

    lax.fori_loop(0, nsub // 2 - 1, body, 0)
    scores(nsub - 1, 1)
    values(nsub - 2, 0)
    values(nsub - 1, 1)


def _attention(q, k, v, cache, ck, sb):
    b, seq, width = q.shape
    nkv = k.shape[0]
    gw = width // nkv
    lk = seq
    in_specs = [pl.BlockSpec((1, seq, gw), lambda i, j: (i, 0, j)),
                pl.BlockSpec((1, seq, ATT_HEAD_DIM), lambda i, j: (j, i, 0)),
                pl.BlockSpec((1, seq, LANES), lambda i, j: (j, i, 0))]
    args = [q, k, v]
    if cache is not None:
        past = cache[0].shape[2]
        assert past % ck == 0
        lk += past
        in_specs += [pl.BlockSpec((1, 1, past, ATT_HEAD_DIM), lambda i, j: (i, j, 0, 0)),
                     pl.BlockSpec((1, 1, past, LANES), lambda i, j: (i, j, 0, 0))]
        args += list(cache)
    assert (seq * ATT_GROUP) % (2 * sb) == 0 and seq % ck == 0
    return pl.pallas_call(
        functools.partial(_attn_kernel, ck=ck, sb=sb, has_cache=cache is not None),
        grid=(b, nkv),
        in_specs=in_specs,
        out_specs=pl.BlockSpec((1, seq, gw), lambda i, j: (i, 0, j)),
        out_shape=jax.ShapeDtypeStruct(q.shape, BF16),
        scratch_shapes=[pltpu.VMEM((lk // ck, sb, ck), F32)] * 2 + [pltpu.VMEM((sb, LANES), F32)] * 2,
        compiler_params=_params(2),
        name="attention",
    )(*args)


def _ssd_group_rows(pair):
    g = pair // (SSD_HEADS // 2 // SSD_GROUPS)
    return slice(g * SSD_STATE, (g + 1) * SSD_STATE)


def _ssd_conv(cur_ref, prev_ref, next_ref, is_first, is_last, ext_ref, cw_ref, cb_ref):
    cn = SSD_CHUNK
    ext_ref[0:HALO, :] = jnp.where(is_first, 0.0, prev_ref[...])
    ext_ref[HALO:HALO + cn, :] = cur_ref[...]
    ext_ref[HALO + cn:, :] = jnp.where(is_last, 0.0, next_ref[...])
    u = cb_ref[...]
    for t in range(SSD_CONV):
        u = u + cw_ref[t:t + 1, :] * ext_ref[pl.ds(HALO - SSD_CONV // 2 + t, cn), :]
    return _silu(u)


def _ssd_stream(act, dt_ref, s_ref, y_ref, dtb_ref, alog_ref, dsk_ref, *, forward):
    cn = SSD_CHUNK
    xs = act[:, :SSD_WIDTH]
    bm = act[:, SSD_WIDTH:SSD_WIDTH + LANES]
    cm = act[:, SSD_WIDTH + LANES:]
    nh = 2 * SSD_HEADS
    dt_t = _softplus(dt_ref[...].T[:nh] + dtb_ref[...])
    a_t = dt_t * (-jnp.exp(alog_ref[...]))
    ii = lax.broadcasted_iota(jnp.int32, (cn, cn), 0)
    jj = lax.broadcasted_iota(jnp.int32, (cn, cn), 1)
    keep = (ii >= jj) if forward else (ii <= jj)
    tri_t = ((ii <= jj) if forward else (ii >= jj)).astype(BF16)
    c3 = _dot(jnp.concatenate(_split3(a_t), axis=0), tri_t)
    cum_t = c3[:nh] + c3[nh:2 * nh] + c3[2 * nh:]
    cols = jnp.concatenate([dt_t, cum_t, jnp.zeros((LANES - 2 * nh, cn), F32)], axis=0).T

    def col(lane_idx):
        return jnp.broadcast_to(cols[:, lane_idx:lane_idx + 1], (cn, LANES))

    off = 0 if forward else SSD_HEADS
    tot_col = cn - 1 if forward else 0
    lane = lax.broadcasted_iota(jnp.int32, (1, LANES), 1)
    lo = lane < SSD_HEAD_DIM
    cm_b = cm.astype(BF16)
    bm_b = bm.astype(BF16)
    gmat = {}
    bg_b = {}
    for pair in range(SSD_HEADS // 2):
        g = pair // (SSD_HEADS // 2 // SSD_GROUPS)
        in_group = lo if g == 0 else jnp.logical_not(lo)
        if g not in gmat:
            gmat[g] = _dot_nt(jnp.where(in_group, cm, 0.0).astype(BF16), bm_b)
            bg_b[g] = jnp.where(in_group, bm, 0.0).astype(BF16)
        h0 = off + 2 * pair
        h1 = h0 + 1
        ci0 = col(nh + h0)
        ci1 = col(nh + h1)
        cip = jnp.where(lo, ci0, ci1)
        m0 = (gmat[g] * jnp.exp(jnp.where(keep, ci0 - cum_t[h0:h0 + 1, :], -1e30))).astype(BF16)
        m1 = (gmat[g] * jnp.exp(jnp.where(keep, ci1 - cum_t[h1:h1 + 1, :], -1e30))).astype(BF16)
        xs_p = xs[:, pair * LANES:(pair + 1) * LANES]
        vp = xs_p * jnp.where(lo, col(h0), col(h1))
        v0 = jnp.where(lo, vp, 0.0).astype(BF16)
        v1 = jnp.where(lo, 0.0, vp).astype(BF16)
        s_old = s_ref[pair]
        y = (_dot(jnp.concatenate([m0, m1], axis=1), jnp.concatenate([v0, v1], axis=0))
             + jnp.exp(cip) * _dot(cm_b, s_old.astype(BF16)))
        totp = jnp.where(lo, cum_t[h0:h0 + 1, tot_col:tot_col + 1], cum_t[h1:h1 + 1, tot_col:tot_col + 1])
        s_ref[pair] = s_old * jnp.exp(totp) + _dot_tn(bg_b[g], (vp * jnp.exp(totp - cip)).astype(BF16))
        if forward:
            y = y + dsk_ref[:, pair * LANES:(pair + 1) * LANES] * xs_p
        y_ref[:, pair * LANES:(pair + 1) * LANES] = y.astype(y_ref.dtype)


def _ssd_kernel(*refs, nc, has_init, want_fin):
    refs = list(refs)
    cf_ref, pf_ref, nf_ref, dtf_ref, cb_ref, pb_ref, nb_ref, dtb_ref = refs[:8]
    cw_ref, cbias_ref, dtbias_ref, alog_ref, dsk_ref = refs[8:13]
    pos = 13
    s0_ref = None
    if has_init:
        s0_ref = refs[pos]
        pos += 1
    yf_ref, yb_ref = refs[pos:pos + 2]
    pos += 2
    sfin_ref = None
    if want_fin:
        sfin_ref = refs[pos]
        pos += 1
    ext_ref, act_ref, sf_ref, sb_ref = refs[pos:]
    c = pl.program_id(1)
    c_fwd = c
    c_bwd = nc - 1 - c

    @pl.when(c == 0)
    def _():
        sf_ref[...] = jnp.zeros_like(sf_ref)
        sb_ref[...] = jnp.zeros_like(sb_ref)
        if has_init:
            for pair in range(SSD_HEADS // 2):
                rows = _ssd_group_rows(pair)
                sf_ref[pair, rows, :] = s0_ref[0, 0, pair]
                sb_ref[pair, rows, :] = s0_ref[0, 1, pair]

    def streams(act_f, act_b):
        shared = (dtbias_ref, alog_ref, dsk_ref)
        _ssd_stream(act_f, dtf_ref, sf_ref, yf_ref, *shared, forward=True)
        _ssd_stream(act_b, dtb_ref, sb_ref, yb_ref, *shared, forward=False)

    @pl.when(c < nc // 2)
    def _():
        act_f = _ssd_conv(cf_ref, pf_ref, nf_ref, c_fwd == 0, c_fwd == nc - 1, ext_ref, cw_ref, cbias_ref)
        act_b = _ssd_conv(cb_ref, pb_ref, nb_ref, c_bwd == 0, c_bwd == nc - 1, ext_ref, cw_ref, cbias_ref)
        act_ref[c_fwd] = act_f
        act_ref[c_bwd] = act_b
        streams(act_f, act_b)

    @pl.when(c >= nc // 2)
    def _():
        streams(act_ref[c_fwd], act_ref[c_bwd])

    if want_fin:
        @pl.when(c == nc - 1)
        def _():
            for pair in range(SSD_HEADS // 2):
                rows = _ssd_group_rows(pair)
                for d, st_ref in enumerate((sf_ref, sb_ref)):
                    both = st_ref[pair, rows, :]
                    sfin_ref[0, d, 2 * pair] = both[:, :SSD_HEAD_DIM]
                    sfin_ref[0, d, 2 * pair + 1] = both[:, SSD_HEAD_DIM:]


def _ssd(xbc, dt, p, s0, batch, want_fin):
    m = xbc.shape[0]
    cn = SSD_CHUNK
    nc = m // batch // cn
    assert nc % 2 == 0
    per = cn // HALO
    n_halo = m // HALO
    has_init = s0 is not None
    npair = SSD_HEADS // 2

    def fwd(b, c):
        return b * nc + c

    def bwd(b, c):
        return b * nc + nc - 1 - c

    def stream_specs(chunk):
        return [pl.BlockSpec((cn, SSD_XBC), lambda b, c: (chunk(b, c), 0)),
                pl.BlockSpec((HALO, SSD_XBC), lambda b, c: (jnp.maximum(chunk(b, c) * per - 1, 0), 0)),
                pl.BlockSpec((HALO, SSD_XBC), lambda b, c: (jnp.minimum(chunk(b, c) * per + per, n_halo - 1), 0)),
                pl.BlockSpec((cn, LANES), lambda b, c: (chunk(b, c), 0))]

    in_specs = stream_specs(fwd) + stream_specs(bwd) + [
        _const_spec((HALO, SSD_XBC)), _const_spec((1, SSD_XBC)), _const_spec((2 * SSD_HEADS, cn)),
        _const_spec((2 * SSD_HEADS, cn)),
        _const_spec((1, SSD_WIDTH))]
    args = [xbc, xbc, xbc, dt, xbc, xbc, xbc, dt, p["conv_w"], p["conv_b"], p["dt_bias"], p["a_log"], p["d_skip"]]
    state_block = (1, 2, npair, SSD_STATE, LANES)
    if has_init:
        in_specs.append(pl.BlockSpec(state_block, lambda b, c: (b, 0, 0, 0, 0)))
        args.append(s0)
    out_specs = [pl.BlockSpec((cn, SSD_WIDTH), lambda b, c: (fwd(b, c), 0)),
                 pl.BlockSpec((cn, SSD_WIDTH), lambda b, c: (bwd(b, c), 0))]
    out_shape = [jax.ShapeDtypeStruct((m, SSD_WIDTH), BF16)] * 2
    if want_fin:
        fin_block = (1, 2, SSD_HEADS, SSD_STATE, SSD_HEAD_DIM)
        out_specs.append(pl.BlockSpec(fin_block, lambda b, c: (b, 0, 0, 0, 0)))
        out_shape.append(jax.ShapeDtypeStruct((batch,) + fin_block[1:], F32))
    return pl.pallas_call(
        functools.partial(_ssd_kernel, nc=nc, has_init=has_init, want_fin=want_fin),
        grid=(batch, nc),
        in_specs=in_specs,
        out_specs=out_specs,
        out_shape=out_shape,
        scratch_shapes=[pltpu.VMEM((cn + 2 * HALO, SSD_XBC), F32), pltpu.VMEM((nc, cn, SSD_XBC), F32),
                        pltpu.VMEM((npair, LANES, LANES), F32), pltpu.VMEM((npair, LANES, LANES), F32)],
        compiler_params=_params(2),
        name="ssd",
    )(*args)


def _ssd_state_to_pairs(s):
    b = s.shape[0]
    npair = SSD_HEADS // 2
    s = s.reshape(b, 2, npair, 2, SSD_STATE, SSD_HEAD_DIM).transpose(0, 1, 2, 4, 3, 5)
    return s.reshape(b, 2, npair, SSD_STATE, 2 * SSD_HEAD_DIM)


def _l1_in_kernel(*refs, use_rope):
    if use_rope:
        x_ref, nw_ref, sh_ref, sc_ref, w_ref, cos_ref, sin_ref, q_ref, k_ref, v_ref, g_ref = refs
    else:
        x_ref, nw_ref, sh_ref, sc_ref, w_ref, q_ref, k_ref, v_ref, g_ref = refs
    h = (_rms(x_ref[...]) * nw_ref[...] * (1.0 + sc_ref[0]) + sh_ref[0]).astype(BF16)
    half = RET_DK // 2

    def rope(t):
        if not use_rope:
            return t
        cos = cos_ref[...]
        sin = sin_ref[...]
        parts = []
        for hd in range(RET_HEADS):
            x1 = t[:, hd * RET_DK:hd * RET_DK + half]
            x2 = t[:, hd * RET_DK + half:(hd + 1) * RET_DK]
            parts += [x1 * cos - x2 * sin, x2 * cos + x1 * sin]
        return jnp.concatenate(parts, axis=1)

    o1 = RET_QK_WIDTH
    o2 = 2 * RET_QK_WIDTH
    o3 = o2 + RET_V_WIDTH
    q_ref[...] = rope(_dot(h, w_ref[:, :o1])).astype(BF16)
    k_ref[...] = (rope(_dot(h, w_ref[:, o1:o2])) * (RET_DK ** -0.5)).astype(k_ref.dtype)
    v_ref[...] = _dot(h, w_ref[:, o2:o3]).astype(BF16)
    g_ref[...] = _dot(h, w_ref[:, o3:]).astype(g_ref.dtype)


def _l1_in(x, mod, rows_per_mod, p, rope, tm):
    m = x.shape[0]
    use_rope = rope is not None
    row = lambda w: pl.BlockSpec((tm, w), lambda i: (i, 0))
    n = 2 * RET_QK_WIDTH + 2 * RET_V_WIDTH
    in_specs = ([row(D_MODEL), _const_spec((1, D_MODEL))] + _mod_specs(tm, rows_per_mod, (0, 1))
                + [_resident_spec((D_MODEL, n))])
    args = [x, p["norm_mix"], mod, mod, p["w_in"]]
    if use_rope:
        rows = rope[0].shape[0]
        in_specs += [pl.BlockSpec((tm, LANES), lambda i: (i % (rows // tm), 0))] * 2
        args += list(rope)
    widths = [(RET_QK_WIDTH, BF16), (RET_QK_WIDTH, BF16), (RET_V_WIDTH, BF16), (RET_V_WIDTH, BF16)]
    return pl.pallas_call(
        functools.partial(_l1_in_kernel, use_rope=use_rope),
        grid=(m // tm,),
        in_specs=in_specs,
        out_specs=[row(w) for w, _ in widths],
        out_shape=[jax.ShapeDtypeStruct((m, w), dt) for w, dt in widths],
        compiler_params=_params(1),
        name="l1_in",
    )(*args)


def _ret_kernel(*refs, cn, nc, has_init, want_fin):
    stateless = (not has_init) and nc == 1
    refs = list(refs)
    dec_ref, qf_ref, kf_ref, vf_ref = refs[:4]
    pos = 4
    if stateless:
        qb_ref, kb_ref, vb_ref = qf_ref, kf_ref, vf_ref
    else:
        qb_ref, kb_ref, vb_ref = refs[pos:pos + 3]
        pos += 3
    s0_ref = None
    if has_init:
        s0_ref = refs[pos]
        pos += 1
    yf_ref = refs[pos]
    pos += 1
    yb_ref = None
    if not stateless:
        yb_ref = refs[pos]
        pos += 1
    sfin_ref = None
    if want_fin:
        sfin_ref = refs[pos]
        pos += 1
    decay_ref, sf_ref, sb_ref = refs[pos:]
    c = pl.program_id(1)
    log_g = -jnp.exp(dec_ref[...])

    @pl.when((pl.program_id(0) == 0) & (c == 0))
    def _():
        ii = lax.broadcasted_iota(jnp.int32, (cn, cn), 0)
        jj = lax.broadcasted_iota(jnp.int32, (cn, cn), 1)
        dist = (ii - jj).astype(F32)
        for hd in range(RET_HEADS):
            gf = log_g[0:1, hd:hd + 1]
            gb = log_g[1:2, hd:hd + 1]
            decay_ref[hd] = (jnp.where(dist >= 0, jnp.exp(gf * jnp.maximum(dist, 0.0)), 0.0)
                             + jnp.where(dist <= 0, jnp.exp(gb * jnp.maximum(-dist, 0.0)), 0.0))

    if not stateless:
        @pl.when(c == 0)
        def _():
            if has_init:
                sf_ref[...] = s0_ref[0, 0]
                sb_ref[...] = s0_ref[0, 1]
            else:
                sf_ref[...] = jnp.zeros_like(sf_ref)
                sb_ref[...] = jnp.zeros_like(sb_ref)

    ri = lax.broadcasted_iota(jnp.int32, (cn, 1), 0).astype(F32)
    for hd in range(RET_HEADS):
        gf = log_g[0:1, hd:hd + 1]
        gb = log_g[1:2, hd:hd + 1]
        qs = slice(hd * RET_DK, (hd + 1) * RET_DK)
        vs = slice(hd * RET_DV, (hd + 1) * RET_DV)
        q = qf_ref[:, qs]
        k = kf_ref[:, qs]
        v = vf_ref[:, vs]
        y = _dot((_dot_nt(q, k.astype(BF16)) * decay_ref[hd]).astype(BF16), v)
        upd_f = _dot_tn((k * jnp.exp(gf * (cn - 1.0 - ri))).astype(BF16), v)
        if stateless:
            new_f = upd_f
        else:
            s_old = sf_ref[hd]
            y = y + jnp.exp(gf * (ri + 1.0)) * _dot(q, s_old.astype(BF16))
            new_f = s_old * jnp.exp(gf * cn) + upd_f
            sf_ref[hd] = new_f
        yf_ref[:, vs] = y.astype(yf_ref.dtype)
        q = qb_ref[:, qs]
        k = kb_ref[:, qs]
        v = vb_ref[:, vs]
        upd_b = _dot_tn((k * jnp.exp(gb * ri)).astype(BF16), v)
        if stateless:
            new_b = upd_b
        else:
            s_old = sb_ref[hd]
            yb_ref[:, vs] = (jnp.exp(gb * (cn - ri)) * _dot(q, s_old.astype(BF16))).astype(yb_ref.dtype)
            new_b = s_old * jnp.exp(gb * cn) + upd_b
            sb_ref[hd] = new_b
        if want_fin:
            if stateless:
                sfin_ref[0, 0, hd] = new_f
                sfin_ref[0, 1, hd] = new_b
            else:
                @pl.when(c == nc - 1)
                def _(new_f=new_f, new_b=new_b, hd=hd):
                    sfin_ref[0, 0, hd] = new_f
                    sfin_ref[0, 1, hd] = new_b


def _retention(q, k, v, decay, s0, batch, cn, want_fin):
    m = q.shape[0]
    nc = m // batch // cn
    has_init = s0 is not None
    stateless = (not has_init) and nc == 1

    def fwd(b, c):
        return b * nc + c

    def bwd(b, c):
        return b * nc + nc - 1 - c

    def stream_specs(chunk):
        return [pl.BlockSpec((cn, RET_QK_WIDTH), lambda b, c: (chunk(b, c), 0)),
                pl.BlockSpec((cn, RET_QK_WIDTH), lambda b, c: (chunk(b, c), 0)),
                pl.BlockSpec((cn, RET_V_WIDTH), lambda b, c: (chunk(b, c), 0))]

    in_specs = [_const_spec((8, LANES))] + stream_specs(fwd)
    args = [decay, q, k, v]
    if not stateless:
        in_specs += stream_specs(bwd)
        args += [q, k, v]
    state_block = (1, 2, RET_HEADS, RET_DK, RET_DV)
    if has_init:
        in_specs.append(pl.BlockSpec(state_block, lambda b, c: (b, 0, 0, 0, 0)))
        args.append(s0)
    out_specs = [pl.BlockSpec((cn, RET_V_WIDTH), lambda b, c: (fwd(b, c), 0))]
    if not stateless:
        out_specs.append(pl.BlockSpec((cn, RET_V_WIDTH), lambda b, c: (bwd(b, c), 0)))
    out_shape = [jax.ShapeDtypeStruct((m, RET_V_WIDTH), BF16)] * len(out_specs)
    if want_fin:
        out_specs.append(pl.BlockSpec(state_block, lambda b, c: (b, 0, 0, 0, 0)))
        out_shape.append(jax.ShapeDtypeStruct((batch,) + state_block[1:], F32))
    return pl.pallas_call(
        functools.partial(_ret_kernel, cn=cn, nc=nc, has_init=has_init, want_fin=want_fin),
        grid=(batch, nc),
        in_specs=in_specs,
        out_specs=out_specs,
        out_shape=out_shape,
        scratch_shapes=[pltpu.VMEM((RET_HEADS, cn, cn), F32)] + [pltpu.VMEM((RET_HEADS, RET_DK, RET_DV), F32)] * 2,
        compiler_params=_params(2),
        name="retention",
    )(*args)


def _post_kernel(*refs, mixer, final):
    refs = list(refs)
    x_ref, g1_ref, nw_ref, sh2_ref, sc2_ref, g2_ref, wout_ref, wg_ref, wu_ref, wd_ref = refs[:10]
    pos = 10
    fn_ref = None
    if final:
        fn_ref = refs[pos]
        pos += 1
    if mixer == "ab":
        att_ref, yf_ref, yb_ref, z_ref, gain_ref, o_ref = refs[pos:]
        y = (yf_ref[...].astype(F32) + yb_ref[...].astype(F32)) * _silu(z_ref[...].astype(F32))
        y = _rms(y) * gain_ref[...]
        mix = _dot(att_ref[...], wout_ref[:ATT_WIDTH, :]) + _dot(y.astype(BF16), wout_ref[ATT_WIDTH:, :])
    else:
        parts = refs[pos:-3]
        gate_ref, gain_ref, o_ref = refs[-3:]
        mix = None
        for hd in range(RET_HEADS):
            vs = slice(hd * RET_DV, (hd + 1) * RET_DV)
            y = parts[0][:, vs].astype(F32)
            for extra in parts[1:]:
                y = y + extra[:, vs].astype(F32)
            y = _rms(y) * gain_ref[:, vs]
            part = _dot((_silu(gate_ref[:, vs].astype(F32)) * y).astype(BF16), wout_ref[vs, :])
            mix = part if mix is None else mix + part
    x1 = x_ref[...] + g1_ref[0] * mix
    h = (_rms(x1) * nw_ref[...] * (1.0 + sc2_ref[0]) + sh2_ref[0]).astype(BF16)
    wcols = D_FF // FF_SPLIT
    ffn = None
    for j in range(FF_SPLIT):
        cs = slice(j * wcols, (j + 1) * wcols)
        act = (_silu(_dot(h, wg_ref[:, cs])) * _dot(h, wu_ref[:, cs])).astype(BF16)
        part = _dot(act, wd_ref[cs, :])
        ffn = part if ffn is None else ffn + part
    x2 = x1 + g2_ref[0] * ffn
    if final:
        x2 = _rms(x2) * fn_ref[...]
    o_ref[...] = x2


def _post(x, mod, rows_per_mod, p, mixer, mix_inputs, gain, final_norm, tm):
    m = x.shape[0]
    row = lambda w: pl.BlockSpec((tm, w), lambda i: (i, 0))
    mixw = p["w_out"].shape[0]
    (g1,) = _mod_specs(tm, rows_per_mod, (2,))
    sh2, sc2, g2 = _mod_specs(tm, rows_per_mod, (3, 4, 5))
    in_specs = [row(D_MODEL), g1, _const_spec((1, D_MODEL)), sh2, sc2, g2,
                _resident_spec((mixw, D_MODEL)), _resident_spec((D_MODEL, D_FF)), _resident_spec((D_MODEL, D_FF)),
                _resident_spec((D_FF, D_MODEL))]
    args = [x, mod, p["norm_ffn"], mod, mod, mod, p["w_out"], p["w_gate"], p["w_up"], p["w_down"]]
    final = final_norm is not None
    if final:
        in_specs.append(_const_spec((1, D_MODEL)))
        args.append(final_norm)
    in_specs += [row(a.shape[1]) for a in mix_inputs] + [_const_spec(gain.shape)]
    args += list(mix_inputs) + [gain]
    return pl.pallas_call(
        functools.partial(_post_kernel, mixer=mixer, final=final),
        grid=(m // tm,),
        in_specs=in_specs,
        out_specs=row(D_MODEL),
        out_shape=jax.ShapeDtypeStruct((m, D_MODEL), F32),
        compiler_params=_params(1),
        name="post_" + mixer,
    )(*args)


def _axial_angles(n_tokens, dim):
    rows = n_tokens // GRID_W
    row = np.repeat(np.arange(rows), GRID_W).astype(np.float64)
    col = np.tile(np.arange(GRID_W), rows).astype(np.float64)
    n_freq = dim // 4
    inv = ROPE_THETA ** (-np.arange(n_freq, dtype=np.float64) / n_freq)
    return np.concatenate([row[:, None] * inv, col[:, None] * inv], axis=-1)


def _head_mean_matrix(width, head):
    idx = jnp.arange(width) // head
    return jnp.where(idx[:, None] == idx[None, :], 1.0 / head, 0.0).astype(BF16)


def _rows_bcast(v, width):
    return jnp.broadcast_to(v.reshape(-1, 1), (v.size, width))


def _trunk(x, mods, rows_per_mod, p0, p1, final_norm, rope_att, rope_ret, caches, seq):
    m = x.shape[0]
    batch = m // seq
    sample = caches is not None
    tm = 512
    l0 = _l0_in(x, mods[0], rows_per_mod, p0, rope_att, tm, None if sample else seq)
    q, ka, va, z, xbc, dt = l0[:6]
    s0_ssd = None
    s0_ret = None
    kv_cache = None
    if sample:
        cache_k, cache_v, state_ssd, state_ret = caches
        ck = cache_k.astype(BF16).transpose(0, 2, 1, 3)
        cv = cache_v.astype(BF16).transpose(0, 2, 1, 3)
        ones_col = jnp.zeros(cv.shape[:3] + (LANES - ATT_HEAD_DIM,), BF16).at[..., 0].set(1.0)
        kv_cache = (ck, jnp.concatenate([cv, ones_col], axis=-1))
        s0_ssd = _ssd_state_to_pairs(state_ssd)
        s0_ret = state_ret
    att = _attention(q.reshape(batch, seq, ATT_WIDTH), ka, va, kv_cache, 256, 512)
    att = att.reshape(m, ATT_WIDTH)
    ssd_out = _ssd(xbc, dt, p0, s0_ssd, batch, want_fin=not sample)
    x = _post(x, mods[0], rows_per_mod, p0, "ab", [att, ssd_out[0], ssd_out[1], z], p0["ssd_gain"], None, tm)
    q1, k1, v1, g1 = _l1_in(x, mods[1], rows_per_mod, p1, rope_ret, tm)
    ret_out = _retention(q1, k1, v1, p1["decay"], s0_ret, batch, RET_CHUNK, want_fin=not sample)
    y_parts = ret_out if sample else ret_out[:-1]
    y = _post(x, mods[1], rows_per_mod, p1, "c", list(y_parts) + [g1], p1["ret_gain"], final_norm, tm)
    if sample:
        return y, None
    new_k = l0[6].transpose(0, 3, 1, 2)
    new_v = l0[7].transpose(0, 3, 1, 2)
    return y, (new_k, new_v, ssd_out[2], ret_out[-1])


def kernel(x_prompt, x_sample, c, cache_k0, cache_v0, state_ssd0, state_ret1, c_ctx, l0_w_ada, l0_b_ada, l0_norm_mix, l0_norm_ffn, l0_w_in, l0_w_out, l0_q_gain, l0_k_gain, l0_conv_w, l0_conv_b, l0_dt_bias, l0_a_log, l0_d_skip, l0_ssd_gain, l0_w_gate, l0_w_up, l0_w_down, l1_w_ada, l1_b_ada, l1_norm_mix, l1_norm_ffn, l1_w_in, l1_w_out, l1_decay, l1_ret_gain, l1_w_gate, l1_w_up, l1_w_down, final_norm):
    b_ctx, seq_ctx, d = x_prompt.shape
    b_lat, seq_lat, _ = x_sample.shape
    assert d == D_MODEL and l0_w_in.shape == (D_MODEL, L0_IN) and l0_w_gate.shape == (D_MODEL, D_FF)
    row = lambda v: v.reshape(1, -1)

    p0 = dict(
        norm_mix=row(l0_norm_mix), norm_ffn=row(l0_norm_ffn),
        w_in=jnp.pad(l0_w_in, ((0, 0), (0, L0_IN_PAD - L0_IN))).astype(BF16),
        w_out=l0_w_out.astype(BF16), w_gate=l0_w_gate.astype(BF16), w_up=l0_w_up.astype(BF16),
        w_down=l0_w_down.astype(BF16),
        q_gain=row(jnp.tile(l0_q_gain, ATT_HEADS)), k_gain=row(jnp.tile(l0_k_gain, ATT_KV_HEADS)),
        pq=_head_mean_matrix(ATT_WIDTH, ATT_HEAD_DIM), pk=_head_mean_matrix(ATT_KV_WIDTH, ATT_HEAD_DIM),
        conv_w=jnp.pad(l0_conv_w, ((0, HALO - SSD_CONV), (0, 0))), conv_b=row(l0_conv_b),
        dt_bias=_rows_bcast(l0_dt_bias, SSD_CHUNK), a_log=_rows_bcast(l0_a_log, SSD_CHUNK),
        d_skip=row(jnp.repeat(l0_d_skip, SSD_HEAD_DIM)), ssd_gain=row(l0_ssd_gain),
    )
    p1 = dict(
        norm_mix=row(l1_norm_mix), norm_ffn=row(l1_norm_ffn),
        w_in=l1_w_in.astype(BF16), w_out=l1_w_out.astype(BF16), w_gate=l1_w_gate.astype(BF16),
        w_up=l1_w_up.astype(BF16), w_down=l1_w_down.astype(BF16),
        decay=jnp.pad(l1_decay, ((0, 8 - l1_decay.shape[0]), (0, LANES - l1_decay.shape[1]))),
        ret_gain=row(l1_ret_gain),
    )
    fnorm = row(final_norm)

    n_cond = 8
    conds = jnp.concatenate([c_ctx[None, :], c, jnp.zeros((n_cond - 1 - b_lat, d), F32)], axis=0)
    mod0 = _ada(conds, l0_w_ada, l0_b_ada)
    mod1 = _ada(conds, l1_w_ada, l1_b_ada)
    mods_ctx = [mod[0:1].reshape(1, 1, 6 * d) for mod in (mod0, mod1)]
    mods_lat = [mod[1:1 + b_lat].reshape(b_lat, 1, 6 * d) for mod in (mod0, mod1)]

    m_ctx = b_ctx * seq_ctx
    y_prompt, ctx = _trunk(x_prompt.reshape(m_ctx, d), mods_ctx, m_ctx, p0, p1, fnorm, None, None, None, seq_ctx)
    new_k0, new_v0, new_ssd0, new_ret1 = ctx
    ang = _axial_angles(seq_lat, ATT_HEAD_DIM)
    cos, sin = np.cos(ang), np.sin(ang)
    reps = LANES // ATT_HEAD_DIM
    rope_att = (jnp.asarray(np.tile(np.concatenate([cos, cos], axis=1), (1, reps)), F32),
                jnp.asarray(np.tile(np.concatenate([-sin, sin], axis=1), (1, reps)), F32))
    ang = _axial_angles(seq_lat, RET_DK)
    rope_ret = (jnp.asarray(np.cos(ang), F32), jnp.asarray(np.sin(ang), F32))
    caches = (cache_k0, cache_v0, state_ssd0, state_ret1)
    y_sample, _ = _trunk(x_sample.reshape(b_lat * seq_lat, d), mods_lat, seq_lat, p0, p1, fnorm, rope_att, rope_ret,
                         caches, seq_lat)
    return (y_prompt.reshape(b_ctx, seq_ctx, d), y_sample.reshape(b_lat, seq_lat, d),
            new_k0, new_v0, new_ssd0, new_ret1)
```

```python
import functools

import jax
import jax.numpy as jnp
import numpy as np
from jax import lax
from jax.experimental import pallas as pl
from jax.experimental.pallas import tpu as pltpu

F32 = jnp.float32
BF16 = jnp.bfloat16

EPS = 1e-6
ROPE_THETA = 10000.0
GRID_W = 64
D_MODEL = 1024
ATT_HEAD_DIM = 64
ATT_HEADS = 8
ATT_KV_HEADS = 2
ATT_GROUP = ATT_HEADS // ATT_KV_HEADS
ATT_WIDTH = ATT_HEADS * ATT_HEAD_DIM
ATT_KV_WIDTH = ATT_KV_HEADS * ATT_HEAD_DIM
SSD_WIDTH = 512
SSD_HEADS = 8
SSD_HEAD_DIM = 64
SSD_STATE = 64
SSD_GROUPS = 2
SSD_CONV = 5
SSD_XBC = SSD_WIDTH + 2 * SSD_GROUPS * SSD_STATE
L0_IN = ATT_WIDTH + 2 * ATT_KV_WIDTH + SSD_WIDTH + SSD_XBC + 2 * SSD_HEADS
RET_HEADS = 4
RET_DK = 256
RET_DV = 512
RET_QK_WIDTH = RET_HEADS * RET_DK
RET_V_WIDTH = RET_HEADS * RET_DV
D_FF = 2816

LANES = 128
HALO = 8
L0_IN_PAD = -(-L0_IN // LANES) * LANES
SSD_CHUNK = 128
SSD_SEQS = 2
RET_CHUNK = 256
FF_SPLIT = 2
VMEM_LIMIT = 56 * 1024 * 1024


def _dot(a, b):
    return jnp.dot(a, b, preferred_element_type=F32)


def _dot_nt(a, b):
    return lax.dot_general(a, b, (((1,), (1,)), ((), ())), preferred_element_type=F32)


def _dot_tn(a, b):
    return lax.dot_general(a, b, (((0,), (0,)), ((), ())), preferred_element_type=F32)


def _silu(x):
    half = 0.5 * x
    return half + half * jnp.tanh(half)


def _softplus(x):
    return jnp.maximum(x, 0.0) + jnp.log1p(jnp.exp(-jnp.abs(x)))


def _rms(x):
    return x * lax.rsqrt(jnp.mean(x * x, axis=-1, keepdims=True) + EPS)


def _split3(x):
    hi = x.astype(BF16)
    r = x - hi.astype(F32)
    mid = r.astype(BF16)
    lo = (r - mid.astype(F32)).astype(BF16)
    return hi, mid, lo


def _const_spec(shape):
    return pl.BlockSpec(shape, lambda *_: (0,) * len(shape))


def _resident_spec(shape):
    return pl.BlockSpec(shape, lambda *_: (0,) * len(shape), pipeline_mode=pl.Buffered(1))


def _params(n_axes, vmem=VMEM_LIMIT):
    return pltpu.CompilerParams(dimension_semantics=("arbitrary",) * n_axes, vmem_limit_bytes=vmem)


def _ada_kernel(c_ref, w_ref, b_ref, o_ref):
    s = _silu(c_ref[...])
    o_ref[...] = _dot(s.astype(BF16), w_ref[...].astype(BF16)) + b_ref[...]


def _ada(conds, w, b):
    n = w.shape[1]
    tn = 1536
    return pl.pallas_call(
        _ada_kernel,
        grid=(n // tn,),
        in_specs=[_const_spec(conds.shape),
                  pl.BlockSpec((D_MODEL, tn), lambda j: (0, j)),
                  pl.BlockSpec((1, tn), lambda j: (0, j))],
        out_specs=pl.BlockSpec((conds.shape[0], tn), lambda j: (0, j)),
        out_shape=jax.ShapeDtypeStruct((conds.shape[0], n), F32),
        compiler_params=_params(1),
        name="ada",
    )(conds, w, b.reshape(1, n))


def _mod_specs(tm, rows_per_mod, which):
    return [pl.BlockSpec((1, 1, D_MODEL), lambda i, j=j: ((i * tm) // rows_per_mod, 0, j)) for j in which]


def _head_rms(x, p_ref, gain):
    x2 = x * x
    hi = x2.astype(BF16)
    lo = (x2 - hi.astype(F32)).astype(BF16)
    ms = _dot(hi, p_ref[...]) + _dot(lo, p_ref[...])
    return x * lax.rsqrt(ms + EPS) * gain


def _rope64(x, cos, sin):
    n = x.shape[1]
    lane = lax.broadcasted_iota(jnp.int32, x.shape, 1)
    first_half = (lane % ATT_HEAD_DIM) < (ATT_HEAD_DIM // 2)
    partner = jnp.where(first_half, pltpu.roll(x, n - ATT_HEAD_DIM // 2, 1), pltpu.roll(x, ATT_HEAD_DIM // 2, 1))
    return x * cos + partner * sin


def _l0_in_kernel(*refs, use_rope, ctx_seq):
    refs = list(refs)
    x_ref, nw_ref, sh_ref, sc_ref, w_ref, qg_ref, kg_ref, pq_ref, pk_ref = refs[:9]
    pos = 9
    if use_rope:
        cos_ref, sin_ref = refs[pos:pos + 2]
        pos += 2
    q_ref, ka_ref, va_ref, z_ref, xbc_ref, dt_ref = refs[pos:pos + 6]
    pos += 6
    h = _rms(x_ref[...]) * nw_ref[...] * (1.0 + sc_ref[0]) + sh_ref[0]
    proj = _dot(h.astype(BF16), w_ref[...])
    o1 = ATT_WIDTH
    o2 = o1 + ATT_KV_WIDTH
    o3 = o2 + ATT_KV_WIDTH
    o4 = o3 + SSD_WIDTH
    o5 = o4 + SSD_XBC
    q = _head_rms(proj[:, :o1], pq_ref, qg_ref[...])
    k = _head_rms(proj[:, o1:o2], pk_ref, kg_ref[...])
    v = proj[:, o2:o3]
    if ctx_seq:
        kt_ref, vt_ref = refs[pos:]
        for s in range(x_ref.shape[0] // ctx_seq):
            kt = k[s * ctx_seq:(s + 1) * ctx_seq].T
            vt = v[s * ctx_seq:(s + 1) * ctx_seq].T
            for kv in range(ATT_KV_HEADS):
                kt_ref[s, kv] = kt[kv * ATT_HEAD_DIM:(kv + 1) * ATT_HEAD_DIM]
                vt_ref[s, kv] = vt[kv * ATT_HEAD_DIM:(kv + 1) * ATT_HEAD_DIM]
    if use_rope:
        cos = cos_ref[...]
        sin = sin_ref[...]
        k = _rope64(k, cos, sin)
        reps = ATT_WIDTH // LANES
        q = _rope64(q, jnp.concatenate([cos] * reps, axis=1), jnp.concatenate([sin] * reps, axis=1))
    q_ref[...] = (q * (ATT_HEAD_DIM ** -0.5)).astype(BF16)
    kb = k.astype(BF16)
    lane = lax.broadcasted_iota(jnp.int32, (1, LANES), 1)
    ones_col = jnp.where(lane == ATT_HEAD_DIM, 1.0, 0.0)
    for kv in range(ATT_KV_HEADS):
        ka_ref[kv] = kb[:, kv * ATT_HEAD_DIM:(kv + 1) * ATT_HEAD_DIM]
        vv = v if kv == 0 else pltpu.roll(v, (LANES - kv * ATT_HEAD_DIM) % LANES, 1)
        va_ref[kv] = jnp.where(lane < ATT_HEAD_DIM, vv, ones_col).astype(BF16)
    z_ref[...] = proj[:, o3:o4].astype(z_ref.dtype)
    xbc_ref[...] = proj[:, o4:o5]
    dt_ref[...] = proj[:, o5:]


def _l0_in(x, mod, rows_per_mod, p, rope, tm, ctx_seq):
    m = x.shape[0]
    use_rope = rope is not None
    row = lambda w: pl.BlockSpec((tm, w), lambda i: (i, 0))
    in_specs = ([row(D_MODEL), _const_spec((1, D_MODEL))] + _mod_specs(tm, rows_per_mod, (0, 1))
                + [_resident_spec((D_MODEL, L0_IN_PAD)), _const_spec((1, ATT_WIDTH)), _const_spec((1, ATT_KV_WIDTH)),
                   _resident_spec((ATT_WIDTH, ATT_WIDTH)), _resident_spec((ATT_KV_WIDTH, ATT_KV_WIDTH))])
    args = [x, p["norm_mix"], mod, mod, p["w_in"], p["q_gain"], p["k_gain"], p["pq"], p["pk"]]
    if use_rope:
        rows = rope[0].shape[0]
        in_specs += [pl.BlockSpec((tm, LANES), lambda i: (i % (rows // tm), 0))] * 2
        args += list(rope)
    head = lambda w: pl.BlockSpec((ATT_KV_HEADS, tm, w), lambda i: (0, i, 0))
    out_specs = [row(ATT_WIDTH), head(ATT_HEAD_DIM), head(LANES), row(SSD_WIDTH), row(SSD_XBC), row(LANES)]
    out_shape = [jax.ShapeDtypeStruct((m, ATT_WIDTH), BF16),
                 jax.ShapeDtypeStruct((ATT_KV_HEADS, m, ATT_HEAD_DIM), BF16),
                 jax.ShapeDtypeStruct((ATT_KV_HEADS, m, LANES), BF16),
                 jax.ShapeDtypeStruct((m, SSD_WIDTH), BF16), jax.ShapeDtypeStruct((m, SSD_XBC), F32),
                 jax.ShapeDtypeStruct((m, LANES), F32)]
    if ctx_seq:
        assert tm % ctx_seq == 0
        cache_block = (tm // ctx_seq, ATT_KV_HEADS, ATT_HEAD_DIM, ctx_seq)
        out_specs += [pl.BlockSpec(cache_block, lambda i: (i, 0, 0, 0))] * 2
        out_shape += [jax.ShapeDtypeStruct((m // ctx_seq,) + cache_block[1:], F32)] * 2
    return pl.pallas_call(
        functools.partial(_l0_in_kernel, use_rope=use_rope, ctx_seq=ctx_seq),
        grid=(m // tm,),
        in_specs=in_specs,
        out_specs=out_specs,
        out_shape=out_shape,
        compiler_params=_params(1),
        name="l0_in",
    )(*args)


def _attn_kernel(*refs, ck, sb, has_cache):
    if has_cache:
        q_ref, k_ref, v_ref, kc_ref, vc_ref, o_ref, s0_ref, s1_ref, m0_ref, m1_ref = refs
    else:
        q_ref, k_ref, v_ref, o_ref, s0_ref, s1_ref, m0_ref, m1_ref = refs
    chunks = []
    if has_cache:
        chunks += [(kc_ref.at[0, 0], vc_ref.at[0, 0], j * ck) for j in range(kc_ref.shape[2] // ck)]
    chunks += [(k_ref.at[0], v_ref.at[0], j * ck) for j in range(k_ref.shape[1] // ck)]
    tt = sb // ATT_GROUP
    nsub = q_ref.shape[1] // tt
    slots = ((s0_ref, m0_ref), (s1_ref, m1_ref))

    def scores(i, slot):
        s_ref, m_ref = slots[slot]
        q4 = q_ref[0, pl.ds(pl.multiple_of(i * tt, tt), tt), :]
        q = jnp.concatenate([q4[:, g * ATT_HEAD_DIM:(g + 1) * ATT_HEAD_DIM] for g in range(ATT_GROUP)], axis=0)
        mx = None
        for j, (kr, _, r0) in enumerate(chunks):
            s = _dot_nt(q, kr[r0:r0 + ck, :])
            s_ref[j] = s
            for t in range(ck // LANES):
                part = s[:, t * LANES:(t + 1) * LANES]
                mx = part if mx is None else jnp.maximum(mx, part)
        m_ref[...] = jnp.broadcast_to(jnp.max(mx, axis=1, keepdims=True), (sb, LANES))

    def values(i, slot):
        s_ref, m_ref = slots[slot]
        m = jnp.concatenate([m_ref[...]] * (ck // LANES), axis=1)
        acc = None
        for j, (_, vr, r0) in enumerate(chunks):
            p = jnp.exp(s_ref[j] - m).astype(BF16)
            part = _dot(p, vr[r0:r0 + ck, :])
            acc = part if acc is None else acc + part
        out = acc[:, :ATT_HEAD_DIM] / acc[:, ATT_HEAD_DIM:ATT_HEAD_DIM + 1]
        out = jnp.concatenate([out[g * tt:(g + 1) * tt] for g in range(ATT_GROUP)], axis=1)
        o_ref[0, pl.ds(pl.multiple_of(i * tt, tt), tt), :] = out.astype(o_ref.dtype)

    scores(0, 0)

    def body(h, carry):
        scores(2 * h + 1, 1)
        values(2 * h, 0)
        scores(2 * h + 2, 0)
        values(2 * h + 1, 1)
        return carry

    lax.fori_loop(0, nsub // 2 - 1, body, 0)
    scores(nsub - 1, 1)
    values(nsub - 2, 0)
    values(nsub - 1, 1)


def _attention(q, k, v, cache, ck, sb):
    b, seq, width = q.shape
    nkv = k.shape[0]
    gw = width // nkv
    lk = seq
    in_specs = [pl.BlockSpec((1, seq, gw), lambda i, j: (i, 0, j)),
                pl.BlockSpec((1, seq, ATT_HEAD_DIM), lambda i, j: (j, i, 0)),
                pl.BlockSpec((1, seq, LANES), lambda i, j: (j, i, 0))]
    args = [q, k, v]
    if cache is not None:
        past = cache[0].shape[2]
        assert past % ck == 0
        lk += past
        in_specs += [pl.BlockSpec((1, 1, past, ATT_HEAD_DIM), lambda i, j: (i, j, 0, 0)),
                     pl.BlockSpec((1, 1, past, LANES), lambda i, j: (i, j, 0, 0))]
        args += list(cache)
    assert (seq * ATT_GROUP) % (2 * sb) == 0 and seq % ck == 0
    return pl.pallas_call(
        functools.partial(_attn_kernel, ck=ck, sb=sb, has_cache=cache is not None),
        grid=(b, nkv),
        in_specs=in_specs,
        out_specs=pl.BlockSpec((1, seq, gw), lambda i, j: (i, 0, j)),
        out_shape=jax.ShapeDtypeStruct(q.shape, BF16),
        scratch_shapes=[pltpu.VMEM((lk // ck, sb, ck), F32)] * 2 + [pltpu.VMEM((sb, LANES), F32)] * 2,
        compiler_params=_params(2),
        name="attention",
    )(*args)


def _ssd_group_rows(pair):
    g = pair // (SSD_HEADS // 2 // SSD_GROUPS)
    return slice(g * SSD_STATE, (g + 1) * SSD_STATE)


def _ssd_conv(cur_ref, prev_ref, next_ref, is_first, is_last, ext_ref, cw_ref, cb_ref):
    cn = SSD_CHUNK
    ext_ref[0:HALO, :] = jnp.where(is_first, 0.0, prev_ref[...])
    ext_ref[HALO:HALO + cn, :] = cur_ref[...]
    ext_ref[HALO + cn:, :] = jnp.where(is_last, 0.0, next_ref[...])
    u = cb_ref[...]
    for t in range(SSD_CONV):
        u = u + cw_ref[t:t + 1, :] * ext_ref[pl.ds(HALO - SSD_CONV // 2 + t, cn), :]
    return _silu(u)


def _ssd_stream(act, dt_ref, s_ref, y_ref, dtb_ref, alog_ref, dsk_ref, *, forward):
    cn = SSD_CHUNK
    xs = act[:, :SSD_WIDTH]
    bm = act[:, SSD_WIDTH:SSD_WIDTH + LANES]
    cm = act[:, SSD_WIDTH + LANES:]
    nh = 2 * SSD_HEADS
    dt_t = _softplus(dt_ref[...].T[:nh] + dtb_ref[...])
    a_t = dt_t * (-jnp.exp(alog_ref[...]))
    ii = lax.broadcasted_iota(jnp.int32, (cn, cn), 0)
    jj = lax.broadcasted_iota(jnp.int32, (cn, cn), 1)
    keep = (ii >= jj) if forward else (ii <= jj)
    tri_t = ((ii <= jj) if forward else (ii >= jj)).astype(BF16)
    c3 = _dot(jnp.concatenate(_split3(a_t), axis=0), tri_t)
    cum_t = c3[:nh] + c3[nh:2 * nh] + c3[2 * nh:]
    cols = jnp.concatenate([dt_t, cum_t, jnp.zeros((LANES - 2 * nh, cn), F32)], axis=0).T

    def col(lane_idx):
        return jnp.broadcast_to(cols[:, lane_idx:lane_idx + 1], (cn, LANES))

    off = 0 if forward else SSD_HEADS
    tot_col = cn - 1 if forward else 0
    lane = lax.broadcasted_iota(jnp.int32, (1, LANES), 1)
    lo = lane < SSD_HEAD_DIM
    cm_b = cm.astype(BF16)
    bm_b = bm.astype(BF16)
    gmat = {}
    bg_b = {}
    for pair in range(SSD_HEADS // 2):
        g = pair // (SSD_HEADS // 2 // SSD_GROUPS)
        in_group = lo if g == 0 else jnp.logical_not(lo)
        if g not in gmat:
            gmat[g] = _dot_nt(jnp.where(in_group, cm, 0.0).astype(BF16), bm_b)
            bg_b[g] = jnp.where(in_group, bm, 0.0).astype(BF16)
        h0 = off + 2 * pair
        h1 = h0 + 1
        ci0 = col(nh + h0)
        ci1 = col(nh + h1)
        cip = jnp.where(lo, ci0, ci1)
        m0 = (gmat[g] * jnp.exp(jnp.where(keep, ci0 - cum_t[h0:h0 + 1, :], -1e30))).astype(BF16)
        m1 = (gmat[g] * jnp.exp(jnp.where(keep, ci1 - cum_t[h1:h1 + 1, :], -1e30))).astype(BF16)
        xs_p = xs[:, pair * LANES:(pair + 1) * LANES]
        vp = xs_p * jnp.where(lo, col(h0), col(h1))
        v0 = jnp.where(lo, vp, 0.0).astype(BF16)
        v1 = jnp.where(lo, 0.0, vp).astype(BF16)
        s_old = s_ref[pair]
        y = (_dot(jnp.concatenate([m0, m1], axis=1), jnp.concatenate([v0, v1], axis=0))
             + jnp.exp(cip) * _dot(cm_b, s_old.astype(BF16)))
        totp = jnp.where(lo, cum_t[h0:h0 + 1, tot_col:tot_col + 1], cum_t[h1:h1 + 1, tot_col:tot_col + 1])
        s_ref[pair] = s_old * jnp.exp(totp) + _dot_tn(bg_b[g], (vp * jnp.exp(totp - cip)).astype(BF16))
        if forward:
            y = y + dsk_ref[:, pair * LANES:(pair + 1) * LANES] * xs_p
        y_ref[:, pair * LANES:(pair + 1) * LANES] = y.astype(y_ref.dtype)


def _ssd_kernel(*refs, nc, ns, has_init, want_fin):
    refs = list(refs)
    cf_ref, pf_ref, nf_ref, dtf_ref, cb_ref, pb_ref, nb_ref, dtb_ref = refs[:8]
    cw_ref, cbias_ref, dtbias_ref, alog_ref, dsk_ref = refs[8:13]
    pos = 13
    s0_ref = None
    if has_init:
        s0_ref = refs[pos]
        pos += 1
    yf_ref, yb_ref = refs[pos:pos + 2]
    pos += 2
    sfin_ref = None
    if want_fin:
        sfin_ref = refs[pos]
        pos += 1
    ext_ref, act_ref, sf_ref, sb_ref = refs[pos:]
    c = pl.program_id(1)
    c_fwd = c
    c_bwd = nc - 1 - c

    @pl.when(c == 0)
    def _():
        sf_ref[...] = jnp.zeros_like(sf_ref)
        sb_ref[...] = jnp.zeros_like(sb_ref)
        if has_init:
            for s in range(ns):
                for pair in range(SSD_HEADS // 2):
                    rows = _ssd_group_rows(pair)
                    sf_ref[s, pair, rows, :] = s0_ref[s, 0, pair]
                    sb_ref[s, pair, rows, :] = s0_ref[s, 1, pair]

    def streams(s, act_f, act_b):
        shared = (dtbias_ref, alog_ref, dsk_ref)
        _ssd_stream(act_f, dtf_ref.at[0, s], sf_ref.at[s], yf_ref.at[0, s], *shared, forward=True)
        _ssd_stream(act_b, dtb_ref.at[0, s], sb_ref.at[s], yb_ref.at[0, s], *shared, forward=False)

    @pl.when(c < nc // 2)
    def _():
        for s in range(ns):
            act_f = _ssd_conv(cf_ref.at[0, s], pf_ref.at[0, s], nf_ref.at[0, s], c_fwd == 0, c_fwd == nc - 1,
                              ext_ref, cw_ref, cbias_ref)
            act_b = _ssd_conv(cb_ref.at[0, s], pb_ref.at[0, s], nb_ref.at[0, s], c_bwd == 0, c_bwd == nc - 1,
                              ext_ref, cw_ref, cbias_ref)
            act_ref[s, c_fwd] = act_f
            act_ref[s, c_bwd] = act_b
            streams(s, act_f, act_b)

    @pl.when(c >= nc // 2)
    def _():
        for s in range(ns):
            streams(s, act_ref[s, c_fwd], act_ref[s, c_bwd])

    if want_fin:
        @pl.when(c == nc - 1)
        def _():
            for s in range(ns):
                for pair in range(SSD_HEADS // 2):
                    rows = _ssd_group_rows(pair)
                    for d, st_ref in enumerate((sf_ref, sb_ref)):
                        both = st_ref[s, pair, rows, :]
                        sfin_ref[s, d, 2 * pair] = both[:, :SSD_HEAD_DIM]
                        sfin_ref[s, d, 2 * pair + 1] = both[:, SSD_HEAD_DIM:]


def _ssd(xbc, dt, p, s0, batch, want_fin):
    m = xbc.shape[0]
    cn = SSD_CHUNK
    ns = SSD_SEQS
    seq = m // batch
    nc = seq // cn
    assert nc % 2 == 0
    assert batch % ns == 0
    per = cn // HALO
    n_halo = seq // HALO
    has_init = s0 is not None
    npair = SSD_HEADS // 2
    view = lambda a: a.reshape(batch // ns, ns, seq, a.shape[-1])

    def fwd(c):
        return c

    def bwd(c):
        return nc - 1 - c

    def stream_specs(chunk):
        return [pl.BlockSpec((1, ns, cn, SSD_XBC), lambda b, c: (b, 0, chunk(c), 0)),
                pl.BlockSpec((1, ns, HALO, SSD_XBC), lambda b, c: (b, 0, jnp.maximum(chunk(c) * per - 1, 0), 0)),
                pl.BlockSpec((1, ns, HALO, SSD_XBC),
                             lambda b, c: (b, 0, jnp.minimum(chunk(c) * per + per, n_halo - 1), 0)),
                pl.BlockSpec((1, ns, cn, LANES), lambda b, c: (b, 0, chunk(c), 0))]

    in_specs = stream_specs(fwd) + stream_specs(bwd) + [
        _const_spec((HALO, SSD_XBC)), _const_spec((1, SSD_XBC)), _const_spec((2 * SSD_HEADS, cn)),
        _const_spec((2 * SSD_HEADS, cn)), _const_spec((1, SSD_WIDTH))]
    xv, dv = view(xbc), view(dt)
    args = [xv, xv, xv, dv, xv, xv, xv, dv, p["conv_w"], p["conv_b"], p["dt_bias"], p["a_log"], p["d_skip"]]
    state_block = (ns, 2, npair, SSD_STATE, LANES)
    if has_init:
        in_specs.append(pl.BlockSpec(state_block, lambda b, c: (b, 0, 0, 0, 0)))
        args.append(s0)
    out_specs = [pl.BlockSpec((1, ns, cn, SSD_WIDTH), lambda b, c: (b, 0, fwd(c), 0)),
                 pl.BlockSpec((1, ns, cn, SSD_WIDTH), lambda b, c: (b, 0, bwd(c), 0))]
    out_shape = [jax.ShapeDtypeStruct((batch // ns, ns, seq, SSD_WIDTH), BF16)] * 2
    if want_fin:
        fin_block = (ns, 2, SSD_HEADS, SSD_STATE, SSD_HEAD_DIM)
        out_specs.append(pl.BlockSpec(fin_block, lambda b, c: (b, 0, 0, 0, 0)))
        out_shape.append(jax.ShapeDtypeStruct((batch,) + fin_block[1:], F32))
    out = pl.pallas_call(
        functools.partial(_ssd_kernel, nc=nc, ns=ns, has_init=has_init, want_fin=want_fin),
        grid=(batch // ns, nc),
        in_specs=in_specs,
        out_specs=out_specs,
        out_shape=out_shape,
        scratch_shapes=[pltpu.VMEM((cn + 2 * HALO, SSD_XBC), F32), pltpu.VMEM((ns, nc, cn, SSD_XBC), F32),
                        pltpu.VMEM((ns, npair, LANES, LANES), F32), pltpu.VMEM((ns, npair, LANES, LANES), F32)],
        compiler_params=_params(2),
        name="ssd",
    )(*args)
    return [out[0].reshape(m, SSD_WIDTH), out[1].reshape(m, SSD_WIDTH)] + list(out[2:])


def _ssd_state_to_pairs(s):
    b = s.shape[0]
    npair = SSD_HEADS // 2
    s = s.reshape(b, 2, npair, 2, SSD_STATE, SSD_HEAD_DIM).transpose(0, 1, 2, 4, 3, 5)
    return s.reshape(b, 2, npair, SSD_STATE, 2 * SSD_HEAD_DIM)


def _l1_in_kernel(*refs, use_rope):
    if use_rope:
        x_ref, nw_ref, sh_ref, sc_ref, w_ref, cos_ref, sin_ref, q_ref, k_ref, v_ref, g_ref = refs
    else:
        x_ref, nw_ref, sh_ref, sc_ref, w_ref, q_ref, k_ref, v_ref, g_ref = refs
    h = (_rms(x_ref[...]) * nw_ref[...] * (1.0 + sc_ref[0]) + sh_ref[0]).astype(BF16)
    half = RET_DK // 2

    def rope(t):
        if not use_rope:
            return t
        cos = cos_ref[...]
        sin = sin_ref[...]
        parts = []
        for hd in range(RET_HEADS):
            x1 = t[:, hd * RET_DK:hd * RET_DK + half]
            x2 = t[:, hd * RET_DK + half:(hd + 1) * RET_DK]
            parts += [x1 * cos - x2 * sin, x2 * cos + x1 * sin]
        return jnp.concatenate(parts, axis=1)

    o1 = RET_QK_WIDTH
    o2 = 2 * RET_QK_WIDTH
    o3 = o2 + RET_V_WIDTH
    q_ref[...] = rope(_dot(h, w_ref[:, :o1])).astype(BF16)
    k_ref[...] = (rope(_dot(h, w_ref[:, o1:o2])) * (RET_DK ** -0.5)).astype(k_ref.dtype)
    v_ref[...] = _dot(h, w_ref[:, o2:o3]).astype(BF16)
    g_ref[...] = _silu(_dot(h, w_ref[:, o3:])).astype(g_ref.dtype)


def _l1_in(x, mod, rows_per_mod, p, rope, tm):
    m = x.shape[0]
    use_rope = rope is not None
    row = lambda w: pl.BlockSpec((tm, w), lambda i: (i, 0))
    n = 2 * RET_QK_WIDTH + 2 * RET_V_WIDTH
    in_specs = ([row(D_MODEL), _const_spec((1, D_MODEL))] + _mod_specs(tm, rows_per_mod, (0, 1))
                + [_resident_spec((D_MODEL, n))])
    args = [x, p["norm_mix"], mod, mod, p["w_in"]]
    if use_rope:
        rows = rope[0].shape[0]
        in_specs += [pl.BlockSpec((tm, LANES), lambda i: (i % (rows // tm), 0))] * 2
        args += list(rope)
    widths = [(RET_QK_WIDTH, BF16), (RET_QK_WIDTH, BF16), (RET_V_WIDTH, BF16), (RET_V_WIDTH, BF16)]
    return pl.pallas_call(
        functools.partial(_l1_in_kernel, use_rope=use_rope),
        grid=(m // tm,),
        in_specs=in_specs,
        out_specs=[row(w) for w, _ in widths],
        out_shape=[jax.ShapeDtypeStruct((m, w), dt) for w, dt in widths],
        compiler_params=_params(1),
        name="l1_in",
    )(*args)


def _ret_kernel(*refs, cn, nc, has_init, want_fin):
    stateless = (not has_init) and nc == 1
    refs = list(refs)
    dec_ref, qf_ref, kf_ref, vf_ref = refs[:4]
    pos = 4
    if stateless:
        qb_ref, kb_ref, vb_ref = qf_ref, kf_ref, vf_ref
    else:
        qb_ref, kb_ref, vb_ref = refs[pos:pos + 3]
        pos += 3
    s0_ref = None
    if has_init:
        s0_ref = refs[pos]
        pos += 1
    yf_ref = refs[pos]
    pos += 1
    yb_ref = None
    if not stateless:
        yb_ref = refs[pos]
        pos += 1
    sfin_ref = None
    if want_fin:
        sfin_ref = refs[pos]
        pos += 1
    decay_ref, sf_ref, sb_ref = refs[pos:]
    c = pl.program_id(1)
    log_g = -jnp.exp(dec_ref[...])

    @pl.when((pl.program_id(0) == 0) & (c == 0))
    def _():
        ii = lax.broadcasted_iota(jnp.int32, (cn, cn), 0)
        jj = lax.broadcasted_iota(jnp.int32, (cn, cn), 1)
        dist = (ii - jj).astype(F32)
        for hd in range(RET_HEADS):
            gf = log_g[0:1, hd:hd + 1]
            gb = log_g[1:2, hd:hd + 1]
            decay_ref[hd] = (jnp.where(dist >= 0, jnp.exp(gf * jnp.maximum(dist, 0.0)), 0.0)
                             + jnp.where(dist <= 0, jnp.exp(gb * jnp.maximum(-dist, 0.0)), 0.0))

    if not stateless:
        @pl.when(c == 0)
        def _():
            if has_init:
                sf_ref[...] = s0_ref[0, 0]
                sb_ref[...] = s0_ref[0, 1]
            else:
                sf_ref[...] = jnp.zeros_like(sf_ref)
                sb_ref[...] = jnp.zeros_like(sb_ref)

    ri = lax.broadcasted_iota(jnp.int32, (cn, 1), 0).astype(F32)
    for hd in range(RET_HEADS):
        gf = log_g[0:1, hd:hd + 1]
        gb = log_g[1:2, hd:hd + 1]
        qs = slice(hd * RET_DK, (hd + 1) * RET_DK)
        vs = slice(hd * RET_DV, (hd + 1) * RET_DV)
        q = qf_ref[:, qs]
        k = kf_ref[:, qs]
        v = vf_ref[:, vs]
        y = _dot((_dot_nt(q, k.astype(BF16)) * decay_ref[hd]).astype(BF16), v)
        upd_f = _dot_tn((k * jnp.exp(gf * (cn - 1.0 - ri))).astype(BF16), v)
        if stateless:
            new_f = upd_f
        else:
            s_old = sf_ref[hd]
            y = y + jnp.exp(gf * (ri + 1.0)) * _dot(q, s_old.astype(BF16))
            new_f = s_old * jnp.exp(gf * cn) + upd_f
            sf_ref[hd] = new_f
        yf_ref[:, vs] = y.astype(yf_ref.dtype)
        q = qb_ref[:, qs]
        k = kb_ref[:, qs]
        v = vb_ref[:, vs]
        upd_b = _dot_tn((k * jnp.exp(gb * ri)).astype(BF16), v)
        if stateless:
            new_b = upd_b
        else:
            s_old = sb_ref[hd]
            yb_ref[:, vs] = (jnp.exp(gb * (cn - ri)) * _dot(q, s_old.astype(BF16))).astype(yb_ref.dtype)
            new_b = s_old * jnp.exp(gb * cn) + upd_b
            sb_ref[hd] = new_b
        if want_fin:
            if stateless:
                sfin_ref[0, 0, hd] = new_f
                sfin_ref[0, 1, hd] = new_b
            else:
                @pl.when(c == nc - 1)
                def _(new_f=new_f, new_b=new_b, hd=hd):
                    sfin_ref[0, 0, hd] = new_f
                    sfin_ref[0, 1, hd] = new_b


def _retention(q, k, v, decay, s0, batch, cn, want_fin):
    m = q.shape[0]
    nc = m // batch // cn
    has_init = s0 is not None
    stateless = (not has_init) and nc == 1

    def fwd(b, c):
        return b * nc + c

    def bwd(b, c):
        return b * nc + nc - 1 - c

    def stream_specs(chunk):
        return [pl.BlockSpec((cn, RET_QK_WIDTH), lambda b, c: (chunk(b, c), 0)),
                pl.BlockSpec((cn, RET_QK_WIDTH), lambda b, c: (chunk(b, c), 0)),
                pl.BlockSpec((cn, RET_V_WIDTH), lambda b, c: (chunk(b, c), 0))]

    in_specs = [_const_spec((8, LANES))] + stream_specs(fwd)
    args = [decay, q, k, v]
    if not stateless:
        in_specs += stream_specs(bwd)
        args += [q, k, v]
    state_block = (1, 2, RET_HEADS, RET_DK, RET_DV)
    if has_init:
        in_specs.append(pl.BlockSpec(state_block, lambda b, c: (b, 0, 0, 0, 0)))
        args.append(s0)
    out_specs = [pl.BlockSpec((cn, RET_V_WIDTH), lambda b, c: (fwd(b, c), 0))]
    if not stateless:
        out_specs.append(pl.BlockSpec((cn, RET_V_WIDTH), lambda b, c: (bwd(b, c), 0)))
    out_shape = [jax.ShapeDtypeStruct((m, RET_V_WIDTH), BF16)] * len(out_specs)
    if want_fin:
        out_specs.append(pl.BlockSpec(state_block, lambda b, c: (b, 0, 0, 0, 0)))
        out_shape.append(jax.ShapeDtypeStruct((batch,) + state_block[1:], F32))
    return pl.pallas_call(
        functools.partial(_ret_kernel, cn=cn, nc=nc, has_init=has_init, want_fin=want_fin),
        grid=(batch, nc),
        in_specs=in_specs,
        out_specs=out_specs,
        out_shape=out_shape,
        scratch_shapes=[pltpu.VMEM((RET_HEADS, cn, cn), F32)] + [pltpu.VMEM((RET_HEADS, RET_DK, RET_DV), F32)] * 2,
        compiler_params=_params(2),
        name="retention",
    )(*args)


def _post_kernel(*refs, mixer, final):
    refs = list(refs)
    x_ref, g1_ref, nw_ref, sh2_ref, sc2_ref, g2_ref, wout_ref, wg_ref, wu_ref, wd_ref = refs[:10]
    pos = 10
    fn_ref = None
    if final:
        fn_ref = refs[pos]
        pos += 1
    if mixer == "ab":
        att_ref, yf_ref, yb_ref, z_ref, gain_ref, o_ref = refs[pos:]
        y = (yf_ref[...].astype(F32) + yb_ref[...].astype(F32)) * _silu(z_ref[...].astype(F32))
        y = _rms(y) * gain_ref[...]
        mix = _dot(att_ref[...], wout_ref[:ATT_WIDTH, :]) + _dot(y.astype(BF16), wout_ref[ATT_WIDTH:, :])
    else:
        parts = refs[pos:-3]
        gate_ref, gain_ref, o_ref = refs[-3:]
        mix = None
        for hd in range(RET_HEADS):
            vs = slice(hd * RET_DV, (hd + 1) * RET_DV)
            y = parts[0][:, vs].astype(F32)
            for extra in parts[1:]:
                y = y + extra[:, vs].astype(F32)
            y = _rms(y) * gain_ref[:, vs]
            part = _dot((gate_ref[:, vs].astype(F32) * y).astype(BF16), wout_ref[vs, :])
            mix = part if mix is None else mix + part
    x1 = x_ref[...] + g1_ref[0] * mix
    h = (_rms(x1) * nw_ref[...] * (1.0 + sc2_ref[0]) + sh2_ref[0]).astype(BF16)
    wcols = D_FF // FF_SPLIT
    ffn = None
    for j in range(FF_SPLIT):
        cs = slice(j * wcols, (j + 1) * wcols)
        act = (_silu(_dot(h, wg_ref[:, cs])) * _dot(h, wu_ref[:, cs])).astype(BF16)
        part = _dot(act, wd_ref[cs, :])
        ffn = part if ffn is None else ffn + part
    x2 = x1 + g2_ref[0] * ffn
    if final:
        x2 = _rms(x2) * fn_ref[...]
    o_ref[...] = x2


def _post(x, mod, rows_per_mod, p, mixer, mix_inputs, gain, final_norm, tm):
    m = x.shape[0]
    row = lambda w: pl.BlockSpec((tm, w), lambda i: (i, 0))
    mixw = p["w_out"].shape[0]
    (g1,) = _mod_specs(tm, rows_per_mod, (2,))
    sh2, sc2, g2 = _mod_specs(tm, rows_per_mod, (3, 4, 5))
    in_specs = [row(D_MODEL), g1, _const_spec((1, D_MODEL)), sh2, sc2, g2,
                _resident_spec((mixw, D_MODEL)), _resident_spec((D_MODEL, D_FF)), _resident_spec((D_MODEL, D_FF)),
                _resident_spec((D_FF, D_MODEL))]
    args = [x, mod, p["norm_ffn"], mod, mod, mod, p["w_out"], p["w_gate"], p["w_up"], p["w_down"]]
    final = final_norm is not None
    if final:
        in_specs.append(_const_spec((1, D_MODEL)))
        args.append(final_norm)
    in_specs += [row(a.shape[1]) for a in mix_inputs] + [_const_spec(gain.shape)]
    args += list(mix_inputs) + [gain]
    return pl.pallas_call(
        functools.partial(_post_kernel, mixer=mixer, final=final),
        grid=(m // tm,),
        in_specs=in_specs,
        out_specs=row(D_MODEL),
        out_shape=jax.ShapeDtypeStruct((m, D_MODEL), F32),
        compiler_params=_params(1),
        name="post_" + mixer,
    )(*args)


def _axial_angles(n_tokens, dim):
    rows = n_tokens // GRID_W
    row = np.repeat(np.arange(rows), GRID_W).astype(np.float64)
    col = np.tile(np.arange(GRID_W), rows).astype(np.float64)
    n_freq = dim // 4
    inv = ROPE_THETA ** (-np.arange(n_freq, dtype=np.float64) / n_freq)
    return np.concatenate([row[:, None] * inv, col[:, None] * inv], axis=-1)


def _head_mean_matrix(width, head):
    idx = jnp.arange(width) // head
    return jnp.where(idx[:, None] == idx[None, :], 1.0 / head, 0.0).astype(BF16)


def _rows_bcast(v, width):
    return jnp.broadcast_to(v.reshape(-1, 1), (v.size, width))


def _trunk(x, mods, rows_per_mod, p0, p1, final_norm, rope_att, rope_ret, caches, seq):
    m = x.shape[0]
    batch = m // seq
    sample = caches is not None
    tm = 512
    l0 = _l0_in(x, mods[0], rows_per_mod, p0, rope_att, tm, None if sample else seq)
    q, ka, va, z, xbc, dt = l0[:6]
    s0_ssd = None
    s0_ret = None
    kv_cache = None
    if sample:
        cache_k, cache_v, state_ssd, state_ret = caches
        ck = cache_k.astype(BF16).transpose(0, 2, 1, 3)
        cv = cache_v.astype(BF16).transpose(0, 2, 1, 3)
        ones_col = jnp.zeros(cv.shape[:3] + (LANES - ATT_HEAD_DIM,), BF16).at[..., 0].set(1.0)
        kv_cache = (ck, jnp.concatenate([cv, ones_col], axis=-1))
        s0_ssd = _ssd_state_to_pairs(state_ssd)
        s0_ret = state_ret
    att = _attention(q.reshape(batch, seq, ATT_WIDTH), ka, va, kv_cache, 256, 512)
    att = att.reshape(m, ATT_WIDTH)
    ssd_out = _ssd(xbc, dt, p0, s0_ssd, batch, want_fin=not sample)
    x = _post(x, mods[0], rows_per_mod, p0, "ab", [att, ssd_out[0], ssd_out[1], z], p0["ssd_gain"], None, tm)
    q1, k1, v1, g1 = _l1_in(x, mods[1], rows_per_mod, p1, rope_ret, tm)
    ret_out = _retention(q1, k1, v1, p1["decay"], s0_ret, batch, RET_CHUNK, want_fin=not sample)
    y_parts = ret_out if sample else ret_out[:-1]
    y = _post(x, mods[1], rows_per_mod, p1, "c", list(y_parts) + [g1], p1["ret_gain"], final_norm, tm)
    if sample:
        return y, None
    new_k = l0[6].transpose(0, 3, 1, 2)
    new_v = l0[7].transpose(0, 3, 1, 2)
    return y, (new_k, new_v, ssd_out[2], ret_out[-1])


def kernel(x_prompt, x_sample, c, cache_k0, cache_v0, state_ssd0, state_ret1, c_ctx, l0_w_ada, l0_b_ada, l0_norm_mix, l0_norm_ffn, l0_w_in, l0_w_out, l0_q_gain, l0_k_gain, l0_conv_w, l0_conv_b, l0_dt_bias, l0_a_log, l0_d_skip, l0_ssd_gain, l0_w_gate, l0_w_up, l0_w_down, l1_w_ada, l1_b_ada, l1_norm_mix, l1_norm_ffn, l1_w_in, l1_w_out, l1_decay, l1_ret_gain, l1_w_gate, l1_w_up, l1_w_down, final_norm):
    b_ctx, seq_ctx, d = x_prompt.shape
    b_lat, seq_lat, _ = x_sample.shape
    assert d == D_MODEL and l0_w_in.shape == (D_MODEL, L0_IN) and l0_w_gate.shape == (D_MODEL, D_FF)
    row = lambda v: v.reshape(1, -1)

    p0 = dict(
        norm_mix=row(l0_norm_mix), norm_ffn=row(l0_norm_ffn),
        w_in=jnp.pad(l0_w_in, ((0, 0), (0, L0_IN_PAD - L0_IN))).astype(BF16),
        w_out=l0_w_out.astype(BF16), w_gate=l0_w_gate.astype(BF16), w_up=l0_w_up.astype(BF16),
        w_down=l0_w_down.astype(BF16),
        q_gain=row(jnp.tile(l0_q_gain, ATT_HEADS)), k_gain=row(jnp.tile(l0_k_gain, ATT_KV_HEADS)),
        pq=_head_mean_matrix(ATT_WIDTH, ATT_HEAD_DIM), pk=_head_mean_matrix(ATT_KV_WIDTH, ATT_HEAD_DIM),
        conv_w=jnp.pad(l0_conv_w, ((0, HALO - SSD_CONV), (0, 0))), conv_b=row(l0_conv_b),
        dt_bias=_rows_bcast(l0_dt_bias, SSD_CHUNK), a_log=_rows_bcast(l0_a_log, SSD_CHUNK),
        d_skip=row(jnp.repeat(l0_d_skip, SSD_HEAD_DIM)), ssd_gain=row(l0_ssd_gain),
    )
    p1 = dict(
        norm_mix=row(l1_norm_mix), norm_ffn=row(l1_norm_ffn),
        w_in=l1_w_in.astype(BF16), w_out=l1_w_out.astype(BF16), w_gate=l1_w_gate.astype(BF16),
        w_up=l1_w_up.astype(BF16), w_down=l1_w_down.astype(BF16),
        decay=jnp.pad(l1_decay, ((0, 8 - l1_decay.shape[0]), (0, LANES - l1_decay.shape[1]))),
        ret_gain=row(l1_ret_gain),
    )
    fnorm = row(final_norm)

    n_cond = 8
    conds = jnp.concatenate([c_ctx[None, :], c, jnp.zeros((n_cond - 1 - b_lat, d), F32)], axis=0)
    mod0 = _ada(conds, l0_w_ada, l0_b_ada)
    mod1 = _ada(conds, l1_w_ada, l1_b_ada)
    mods_ctx = [mod[0:1].reshape(1, 1, 6 * d) for mod in (mod0, mod1)]
    mods_lat = [mod[1:1 + b_lat].reshape(b_lat, 1, 6 * d) for mod in (mod0, mod1)]

    m_ctx = b_ctx * seq_ctx
    y_prompt, ctx = _trunk(x_prompt.reshape(m_ctx, d), mods_ctx, m_ctx, p0, p1, fnorm, None, None, None, seq_ctx)
    new_k0, new_v0, new_ssd0, new_ret1 = ctx
    ang = _axial_angles(seq_lat, ATT_HEAD_DIM)
    cos, sin = np.cos(ang), np.sin(ang)
    reps = LANES // ATT_HEAD_DIM
    rope_att = (jnp.asarray(np.tile(np.concatenate([cos, cos], axis=1), (1, reps)), F32),
                jnp.asarray(np.tile(np.concatenate([-sin, sin], axis=1), (1, reps)), F32))
    ang = _axial_angles(seq_lat, RET_DK)
    rope_ret = (jnp.asarray(np.cos(ang), F32), jnp.asarray(np.sin(ang), F32))
    caches = (cache_k0, cache_v0, state_ssd0, state_ret1)
    y_sample, _ = _trunk(x_sample.reshape(b_lat * seq_lat, d), mods_lat, seq_lat, p0, p1, fnorm, rope_att, rope_ret,
                         caches, seq_lat)
    return (y_prompt.reshape(b_ctx, seq_ctx, d), y_sample.reshape(b_lat, seq_lat, d),
            new_k0, new_v0, new_ssd0, new_ret1)
```

```python
import functools

import jax
import jax.numpy as jnp
import numpy as np
from jax import lax
from jax.experimental import pallas as pl
from jax.experimental.pallas import tpu as pltpu

F32 = jnp.float32
BF16 = jnp.bfloat16

EPS = 1e-6
ROPE_THETA = 10000.0
GRID_W = 64
D_MODEL = 1024
ATT_HEAD_DIM = 64
ATT_HEADS = 8
ATT_KV_HEADS = 2
ATT_GROUP = ATT_HEADS // ATT_KV_HEADS
ATT_WIDTH = ATT_HEADS * ATT_HEAD_DIM
ATT_KV_WIDTH = ATT_KV_HEADS * ATT_HEAD_DIM
SSD_WIDTH = 512
SSD_HEADS = 8
SSD_HEAD_DIM = 64
SSD_STATE = 64
SSD_GROUPS = 2
SSD_CONV = 5
SSD_XBC = SSD_WIDTH + 2 * SSD_GROUPS * SSD_STATE
L0_IN = ATT_WIDTH + 2 * ATT_KV_WIDTH + SSD_WIDTH + SSD_XBC + 2 * SSD_HEADS
RET_HEADS = 4
RET_DK = 256
RET_DV = 512
RET_QK_WIDTH = RET_HEADS * RET_DK
RET_V_WIDTH = RET_HEADS * RET_DV
D_FF = 2816

LANES = 128
HALO = 8
L0_IN_PAD = -(-L0_IN // LANES) * LANES
SSD_CHUNK = 128
SSD_SEQS = 2
RET_CHUNK = 256
FF_SPLIT = 2
VMEM_LIMIT = 56 * 1024 * 1024


def _dot(a, b):
    return jnp.dot(a, b, preferred_element_type=F32)


def _dot_nt(a, b):
    return lax.dot_general(a, b, (((1,), (1,)), ((), ())), preferred_element_type=F32)


def _dot_tn(a, b):
    return lax.dot_general(a, b, (((0,), (0,)), ((), ())), preferred_element_type=F32)


def _silu(x):
    half = 0.5 * x
    return half + half * jnp.tanh(half)


def _softplus(x):
    return jnp.maximum(x, 0.0) + jnp.log1p(jnp.exp(-jnp.abs(x)))


def _rms(x):
    return x * lax.rsqrt(jnp.mean(x * x, axis=-1, keepdims=True) + EPS)


def _split3(x):
    hi = x.astype(BF16)
    r = x - hi.astype(F32)
    mid = r.astype(BF16)
    lo = (r - mid.astype(F32)).astype(BF16)
    return hi, mid, lo


def _const_spec(shape):
    return pl.BlockSpec(shape, lambda *_: (0,) * len(shape))


def _resident_spec(shape):
    return pl.BlockSpec(shape, lambda *_: (0,) * len(shape), pipeline_mode=pl.Buffered(1))


def _params(n_axes, vmem=VMEM_LIMIT):
    return pltpu.CompilerParams(dimension_semantics=("arbitrary",) * n_axes, vmem_limit_bytes=vmem)


def _ada_kernel(c_ref, w_ref, b_ref, o_ref):
    s = _silu(c_ref[...])
    o_ref[...] = _dot(s.astype(BF16), w_ref[...].astype(BF16)) + b_ref[...]


def _ada(conds, w, b):
    n = w.shape[1]
    tn = 1536
    return pl.pallas_call(
        _ada_kernel,
        grid=(n // tn,),
        in_specs=[_const_spec(conds.shape),
                  pl.BlockSpec((D_MODEL, tn), lambda j: (0, j)),
                  pl.BlockSpec((1, tn), lambda j: (0, j))],
        out_specs=pl.BlockSpec((conds.shape[0], tn), lambda j: (0, j)),
        out_shape=jax.ShapeDtypeStruct((conds.shape[0], n), F32),
        compiler_params=_params(1),
        name="ada",
    )(conds, w, b.reshape(1, n))


def _mod_specs(tm, rows_per_mod, which):
    return [pl.BlockSpec((1, 1, D_MODEL), lambda i, j=j: ((i * tm) // rows_per_mod, 0, j)) for j in which]


def _head_rms(x, p_ref, gain):
    x2 = x * x
    hi = x2.astype(BF16)
    lo = (x2 - hi.astype(F32)).astype(BF16)
    ms = _dot(hi, p_ref[...]) + _dot(lo, p_ref[...])
    return x * lax.rsqrt(ms + EPS) * gain


def _rope64(x, cos, sin):
    n = x.shape[1]
    lane = lax.broadcasted_iota(jnp.int32, x.shape, 1)
    first_half = (lane % ATT_HEAD_DIM) < (ATT_HEAD_DIM // 2)
    partner = jnp.where(first_half, pltpu.roll(x, n - ATT_HEAD_DIM // 2, 1), pltpu.roll(x, ATT_HEAD_DIM // 2, 1))
    return x * cos + partner * sin


def _l0_in_kernel(*refs, use_rope, ctx_seq):
    refs = list(refs)
    x_ref, nw_ref, sh_ref, sc_ref, w_ref, qg_ref, kg_ref, pq_ref, pk_ref = refs[:9]
    pos = 9
    if use_rope:
        cos_ref, sin_ref = refs[pos:pos + 2]
        pos += 2
    q_ref, ka_ref, va_ref, z_ref, xbc_ref, dt_ref = refs[pos:pos + 6]
    pos += 6
    h = _rms(x_ref[...]) * nw_ref[...] * (1.0 + sc_ref[0]) + sh_ref[0]
    proj = _dot(h.astype(BF16), w_ref[...])
    o1 = ATT_WIDTH
    o2 = o1 + ATT_KV_WIDTH
    o3 = o2 + ATT_KV_WIDTH
    o4 = o3 + SSD_WIDTH
    o5 = o4 + SSD_XBC
    q = _head_rms(proj[:, :o1], pq_ref, qg_ref[...])
    k = _head_rms(proj[:, o1:o2], pk_ref, kg_ref[...])
    v = proj[:, o2:o3]
    if ctx_seq:
        kt_ref, vt_ref = refs[pos:]
        for s in range(x_ref.shape[0] // ctx_seq):
            kt = k[s * ctx_seq:(s + 1) * ctx_seq].T
            vt = v[s * ctx_seq:(s + 1) * ctx_seq].T
            for kv in range(ATT_KV_HEADS):
                kt_ref[s, kv] = kt[kv * ATT_HEAD_DIM:(kv + 1) * ATT_HEAD_DIM]
                vt_ref[s, kv] = vt[kv * ATT_HEAD_DIM:(kv + 1) * ATT_HEAD_DIM]
    if use_rope:
        cos = cos_ref[...]
        sin = sin_ref[...]
        k = _rope64(k, cos, sin)
        reps = ATT_WIDTH // LANES
        q = _rope64(q, jnp.concatenate([cos] * reps, axis=1), jnp.concatenate([sin] * reps, axis=1))
    q_ref[...] = (q * (ATT_HEAD_DIM ** -0.5)).astype(BF16)
    kb = k.astype(BF16)
    lane = lax.broadcasted_iota(jnp.int32, (1, LANES), 1)
    ones_col = jnp.where(lane == ATT_HEAD_DIM, 1.0, 0.0)
    for kv in range(ATT_KV_HEADS):
        ka_ref[kv] = kb[:, kv * ATT_HEAD_DIM:(kv + 1) * ATT_HEAD_DIM]
        vv = v if kv == 0 else pltpu.roll(v, (LANES - kv * ATT_HEAD_DIM) % LANES, 1)
        va_ref[kv] = jnp.where(lane < ATT_HEAD_DIM, vv, ones_col).astype(BF16)
    z_ref[...] = proj[:, o3:o4].astype(z_ref.dtype)
    xbc_ref[...] = proj[:, o4:o5]
    dt_ref[...] = proj[:, o5:]


def _l0_in(x, mod, rows_per_mod, p, rope, tm, ctx_seq):
    m = x.shape[0]
    use_rope = rope is not None
    row = lambda w: pl.BlockSpec((tm, w), lambda i: (i, 0))
    in_specs = ([row(D_MODEL), _const_spec((1, D_MODEL))] + _mod_specs(tm, rows_per_mod, (0, 1))
                + [_resident_spec((D_MODEL, L0_IN_PAD)), _const_spec((1, ATT_WIDTH)), _const_spec((1, ATT_KV_WIDTH)),
                   _resident_spec((ATT_WIDTH, ATT_WIDTH)), _resident_spec((ATT_KV_WIDTH, ATT_KV_WIDTH))])
    args = [x, p["norm_mix"], mod, mod, p["w_in"], p["q_gain"], p["k_gain"], p["pq"], p["pk"]]
    if use_rope:
        rows = rope[0].shape[0]
        in_specs += [pl.BlockSpec((tm, LANES), lambda i: (i % (rows // tm), 0))] * 2
        args += list(rope)
    head = lambda w: pl.BlockSpec((ATT_KV_HEADS, tm, w), lambda i: (0, i, 0))
    out_specs = [row(ATT_WIDTH), head(ATT_HEAD_DIM), head(LANES), row(SSD_WIDTH), row(SSD_XBC), row(LANES)]
    out_shape = [jax.ShapeDtypeStruct((m, ATT_WIDTH), BF16),
                 jax.ShapeDtypeStruct((ATT_KV_HEADS, m, ATT_HEAD_DIM), BF16),
                 jax.ShapeDtypeStruct((ATT_KV_HEADS, m, LANES), BF16),
                 jax.ShapeDtypeStruct((m, SSD_WIDTH), BF16), jax.ShapeDtypeStruct((m, SSD_XBC), F32),
                 jax.ShapeDtypeStruct((m, LANES), F32)]
    if ctx_seq:
        assert tm % ctx_seq == 0
        cache_block = (tm // ctx_seq, ATT_KV_HEADS, ATT_HEAD_DIM, ctx_seq)
        out_specs += [pl.BlockSpec(cache_block, lambda i: (i, 0, 0, 0))] * 2
        out_shape += [jax.ShapeDtypeStruct((m // ctx_seq,) + cache_block[1:], F32)] * 2
    return pl.pallas_call(
        functools.partial(_l0_in_kernel, use_rope=use_rope, ctx_seq=ctx_seq),
        grid=(m // tm,),
        in_specs=in_specs,
        out_specs=out_specs,
        out_shape=out_shape,
        compiler_params=_params(1),
        name="l0_in",
    )(*args)


def _attn_kernel(*refs, ck, sb, has_cache):
    if has_cache:
        q_ref, k_ref, v_ref, kc_ref, vc_ref, o_ref, s0_ref, s1_ref, m0_ref, m1_ref = refs
    else:
        q_ref, k_ref, v_ref, o_ref, s0_ref, s1_ref, m0_ref, m1_ref = refs
    chunks = []
    if has_cache:
        chunks += [(kc_ref.at[0, 0], vc_ref.at[0, 0], j * ck) for j in range(kc_ref.shape[2] // ck)]
    chunks += [(k_ref.at[0], v_ref.at[0], j * ck) for j in range(k_ref.shape[1] // ck)]
    tt = sb // ATT_GROUP
    nsub = q_ref.shape[1] // tt
    slots = ((s0_ref, m0_ref), (s1_ref, m1_ref))

    def scores(i, slot):
        s_ref, m_ref = slots[slot]
        q4 = q_ref[0, pl.ds(pl.multiple_of(i * tt, tt), tt), :]
        q = jnp.concatenate([q4[:, g * ATT_HEAD_DIM:(g + 1) * ATT_HEAD_DIM] for g in range(ATT_GROUP)], axis=0)
        mx = None
        for j, (kr, _, r0) in enumerate(chunks):
            s = _dot_nt(q, kr[r0:r0 + ck, :])
            s_ref[j] = s
            for t in range(ck // LANES):
                part = s[:, t * LANES:(t + 1) * LANES]
                mx = part if mx is None else jnp.maximum(mx, part)
        m_ref[...] = jnp.broadcast_to(jnp.max(mx, axis=1, keepdims=True), (sb, LANES))

    def values(i, slot):
        s_ref, m_ref = slots[slot]
        m = jnp.concatenate([m_ref[...]] * (ck // LANES), axis=1)
        acc = None
        for j, (_, vr, r0) in enumerate(chunks):
            p = jnp.exp(s_ref[j] - m).astype(BF16)
            part = _dot(p, vr[r0:r0 + ck, :])
            acc = part if acc is None else acc + part
        out = acc[:, :ATT_HEAD_DIM] / acc[:, ATT_HEAD_DIM:ATT_HEAD_DIM + 1]
        out = jnp.concatenate([out[g * tt:(g + 1) * tt] for g in range(ATT_GROUP)], axis=1)
        o_ref[0, pl.ds(pl.multiple_of(i * tt, tt), tt), :] = out.astype(o_ref.dtype)

    scores(0, 0)

    def body(h, carry):
        scores(2 * h + 1, 1)
        values(2 * h, 0)
        scores(2 * h + 2, 0)
        values(2 * h + 1, 1)
        return carry

    lax.fori_loop(0, nsub // 2 - 1, body, 0)
    scores(nsub - 1, 1)
    values(nsub - 2, 0)
    values(nsub - 1, 1)


def _attention(q, k, v, cache, ck, sb):
    b, seq, width = q.shape
    nkv = k.shape[0]
    gw = width // nkv
    lk = seq
    in_specs = [pl.BlockSpec((1, seq, gw), lambda i, j: (i, 0, j)),
                pl.BlockSpec((1, seq, ATT_HEAD_DIM), lambda i, j: (j, i, 0)),
                pl.BlockSpec((1, seq, LANES), lambda i, j: (j, i, 0))]
    args = [q, k, v]
    if cache is not None:
        past = cache[0].shape[2]
        assert past % ck == 0
        lk += past
        in_specs += [pl.BlockSpec((1, 1, past, ATT_HEAD_DIM), lambda i, j: (i, j, 0, 0)),
                     pl.BlockSpec((1, 1, past, LANES), lambda i, j: (i, j, 0, 0))]
        args += list(cache)
    assert (seq * ATT_GROUP) % (2 * sb) == 0 and seq % ck == 0
    return pl.pallas_call(
        functools.partial(_attn_kernel, ck=ck, sb=sb, has_cache=cache is not None),
        grid=(b, nkv),
        in_specs=in_specs,
        out_specs=pl.BlockSpec((1, seq, gw), lambda i, j: (i, 0, j)),
        out_shape=jax.ShapeDtypeStruct(q.shape, BF16),
        scratch_shapes=[pltpu.VMEM((lk // ck, sb, ck), F32)] * 2 + [pltpu.VMEM((sb, LANES), F32)] * 2,
        compiler_params=_params(2),
        name="attention",
    )(*args)


def _ssd_group_rows(pair):
    g = pair // (SSD_HEADS // 2 // SSD_GROUPS)
    return slice(g * SSD_STATE, (g + 1) * SSD_STATE)


def _ssd_conv(cur_ref, prev_ref, next_ref, is_first, is_last, ext_ref, cw_ref, cb_ref, shift_ref):
    cn = SSD_CHUNK
    ext_ref[0:HALO, :] = jnp.where(is_first, 0.0, prev_ref[...])
    ext_ref[HALO:HALO + cn, :] = cur_ref[...]
    ext_ref[HALO + cn:, :] = jnp.where(is_last, 0.0, next_ref[...])
    shifted = _dot(shift_ref[...], ext_ref[...].astype(BF16))
    centre = SSD_CONV // 2
    u = cb_ref[...] + cw_ref[centre:centre + 1, :] * cur_ref[...]
    for n, t in enumerate(t for t in range(SSD_CONV) if t != centre):
        u = u + cw_ref[t:t + 1, :] * shifted[n * cn:(n + 1) * cn]
    return _silu(u)


def _ssd_scalars(dt_ref, dtb_ref, alog_ref, *, forward):
    cn = SSD_CHUNK
    nh = 2 * SSD_HEADS
    dt_t = _softplus(dt_ref[...].T[:nh] + dtb_ref[...])
    a_t = dt_t * (-jnp.exp(alog_ref[...]))
    ii = lax.broadcasted_iota(jnp.int32, (cn, cn), 0)
    jj = lax.broadcasted_iota(jnp.int32, (cn, cn), 1)
    keep = (ii >= jj) if forward else (ii <= jj)
    tri_t = ((ii <= jj) if forward else (ii >= jj)).astype(BF16)
    c3 = _dot(jnp.concatenate(_split3(a_t), axis=0), tri_t)
    cum_t = c3[:nh] + c3[nh:2 * nh] + c3[2 * nh:]
    cols = jnp.concatenate([dt_t, cum_t, jnp.zeros((LANES - 2 * nh, cn), F32)], axis=0).T
    lane = lax.broadcasted_iota(jnp.int32, (1, LANES), 1)
    return dict(cum_t=cum_t, cols=cols, keep=keep, lo=lane < SSD_HEAD_DIM, forward=forward)


def _ssd_mats(act, prep):
    lo = prep["lo"]
    xs = act[:, :SSD_WIDTH]
    bm = act[:, SSD_WIDTH:SSD_WIDTH + LANES]
    cm = act[:, SSD_WIDTH + LANES:]
    bm_b = bm.astype(BF16)
    gmat = []
    bg_b = []
    for g in range(SSD_GROUPS):
        in_group = lo if g == 0 else jnp.logical_not(lo)
        gmat.append(_dot_nt(jnp.where(in_group, cm, 0.0).astype(BF16), bm_b))
        bg_b.append(jnp.where(in_group, bm, 0.0).astype(BF16))
    return dict(prep, xs=xs, cm_b=cm.astype(BF16), gmat=gmat, bg_b=bg_b)


def _ssd_pair(prep, pair, s_ref, y_ref, dsk_ref):
    cn = SSD_CHUNK
    nh = 2 * SSD_HEADS
    forward, cols, cum_t, keep, lo = prep["forward"], prep["cols"], prep["cum_t"], prep["keep"], prep["lo"]

    def col(lane_idx):
        return jnp.broadcast_to(cols[:, lane_idx:lane_idx + 1], (cn, LANES))

    off = 0 if forward else SSD_HEADS
    tot_col = cn - 1 if forward else 0
    g = pair // (SSD_HEADS // 2 // SSD_GROUPS)
    gmat = prep["gmat"][g]
    h0 = off + 2 * pair
    h1 = h0 + 1
    ci0 = col(nh + h0)
    ci1 = col(nh + h1)
    cip = jnp.where(lo, ci0, ci1)
    m0 = (gmat * jnp.exp(jnp.where(keep, ci0 - cum_t[h0:h0 + 1, :], -1e30))).astype(BF16)
    m1 = (gmat * jnp.exp(jnp.where(keep, ci1 - cum_t[h1:h1 + 1, :], -1e30))).astype(BF16)
    xs_p = prep["xs"][:, pair * LANES:(pair + 1) * LANES]
    vp = xs_p * jnp.where(lo, col(h0), col(h1))
    v0 = jnp.where(lo, vp, 0.0).astype(BF16)
    v1 = jnp.where(lo, 0.0, vp).astype(BF16)
    s_old = s_ref[pair]
    y = (_dot(jnp.concatenate([m0, m1], axis=1), jnp.concatenate([v0, v1], axis=0))
         + jnp.exp(cip) * _dot(prep["cm_b"], s_old.astype(BF16)))
    totp = jnp.where(lo, cum_t[h0:h0 + 1, tot_col:tot_col + 1], cum_t[h1:h1 + 1, tot_col:tot_col + 1])
    s_ref[pair] = s_old * jnp.exp(totp) + _dot_tn(prep["bg_b"][g], (vp * jnp.exp(totp - cip)).astype(BF16))
    if forward:
        y = y + dsk_ref[:, pair * LANES:(pair + 1) * LANES] * xs_p
    y_ref[:, pair * LANES:(pair + 1) * LANES] = y.astype(y_ref.dtype)


def _ssd_kernel(*refs, nc, ns, has_init, want_fin):
    refs = list(refs)
    cf_ref, pf_ref, nf_ref, dtf_ref, cb_ref, pb_ref, nb_ref, dtb_ref = refs[:8]
    cw_ref, cbias_ref, dtbias_ref, alog_ref, dsk_ref, shift_ref = refs[8:14]
    pos = 14
    s0_ref = None
    if has_init:
        s0_ref = refs[pos]
        pos += 1
    yf_ref, yb_ref = refs[pos:pos + 2]
    pos += 2
    sfin_ref = None
    if want_fin:
        sfin_ref = refs[pos]
        pos += 1
    ext_ref, act_ref, sf_ref, sb_ref = refs[pos:]
    c = pl.program_id(1)
    c_fwd = c
    c_bwd = nc - 1 - c

    @pl.when(c == 0)
    def _():
        sf_ref[...] = jnp.zeros_like(sf_ref)
        sb_ref[...] = jnp.zeros_like(sb_ref)
        if has_init:
            for s in range(ns):
                for pair in range(SSD_HEADS // 2):
                    rows = _ssd_group_rows(pair)
                    sf_ref[s, pair, rows, :] = s0_ref[s, 0, pair]
                    sb_ref[s, pair, rows, :] = s0_ref[s, 1, pair]

    def scalars():
        out = []
        for s in range(ns):
            out.append(_ssd_scalars(dtf_ref.at[0, s], dtbias_ref, alog_ref, forward=True))
            out.append(_ssd_scalars(dtb_ref.at[0, s], dtbias_ref, alog_ref, forward=False))
        return out

    def streams(preps, acts):
        work = []
        for s in range(ns):
            work.append((_ssd_mats(acts[2 * s], preps[2 * s]), sf_ref.at[s], yf_ref.at[0, s]))
            work.append((_ssd_mats(acts[2 * s + 1], preps[2 * s + 1]), sb_ref.at[s], yb_ref.at[0, s]))
        for pair in range(SSD_HEADS // 2):
            for prep, s_ref, y_ref in work:
                _ssd_pair(prep, pair, s_ref, y_ref, dsk_ref)

    @pl.when(c < nc // 2)
    def _():
        preps = scalars()
        acts = []
        for s in range(ns):
            act_f = _ssd_conv(cf_ref.at[0, s], pf_ref.at[0, s], nf_ref.at[0, s], c_fwd == 0, c_fwd == nc - 1,
                              ext_ref.at[2 * s], cw_ref, cbias_ref, shift_ref)
            act_b = _ssd_conv(cb_ref.at[0, s], pb_ref.at[0, s], nb_ref.at[0, s], c_bwd == 0, c_bwd == nc - 1,
                              ext_ref.at[2 * s + 1], cw_ref, cbias_ref, shift_ref)
            act_ref[s, c_fwd] = act_f
            act_ref[s, c_bwd] = act_b
            acts += [act_f, act_b]
        streams(preps, acts)

    @pl.when(c >= nc // 2)
    def _():
        acts = []
        for s in range(ns):
            acts += [act_ref[s, c_fwd], act_ref[s, c_bwd]]
        streams(scalars(), acts)

    if want_fin:
        @pl.when(c == nc - 1)
        def _():
            for s in range(ns):
                for pair in range(SSD_HEADS // 2):
                    rows = _ssd_group_rows(pair)
                    for d, st_ref in enumerate((sf_ref, sb_ref)):
                        both = st_ref[s, pair, rows, :]
                        sfin_ref[s, d, 2 * pair] = both[:, :SSD_HEAD_DIM]
                        sfin_ref[s, d, 2 * pair + 1] = both[:, SSD_HEAD_DIM:]


def _ssd(xbc, dt, p, s0, batch, want_fin):
    m = xbc.shape[0]
    cn = SSD_CHUNK
    ns = SSD_SEQS
    seq = m // batch
    nc = seq // cn
    assert nc % 2 == 0
    assert batch % ns == 0
    per = cn // HALO
    n_halo = seq // HALO
    has_init = s0 is not None
    npair = SSD_HEADS // 2
    view = lambda a: a.reshape(batch // ns, ns, seq, a.shape[-1])

    def fwd(c):
        return c

    def bwd(c):
        return nc - 1 - c

    def stream_specs(chunk):
        return [pl.BlockSpec((1, ns, cn, SSD_XBC), lambda b, c: (b, 0, chunk(c), 0)),
                pl.BlockSpec((1, ns, HALO, SSD_XBC), lambda b, c: (b, 0, jnp.maximum(chunk(c) * per - 1, 0), 0)),
                pl.BlockSpec((1, ns, HALO, SSD_XBC),
                             lambda b, c: (b, 0, jnp.minimum(chunk(c) * per + per, n_halo - 1), 0)),
                pl.BlockSpec((1, ns, cn, LANES), lambda b, c: (b, 0, chunk(c), 0))]

    in_specs = stream_specs(fwd) + stream_specs(bwd) + [
        _const_spec((HALO, SSD_XBC)), _const_spec((1, SSD_XBC)), _const_spec((2 * SSD_HEADS, cn)),
        _const_spec((2 * SSD_HEADS, cn)), _const_spec((1, SSD_WIDTH)),
        _const_spec(((SSD_CONV - 1) * cn, cn + 2 * HALO))]
    taps = [t for t in range(SSD_CONV) if t != SSD_CONV // 2]
    src = np.concatenate([HALO + np.arange(cn) + (t - SSD_CONV // 2) for t in taps])
    shift = jnp.asarray(src[:, None] == np.arange(cn + 2 * HALO)[None, :], BF16)
    xv, dv = view(xbc), view(dt)
    args = [xv, xv, xv, dv, xv, xv, xv, dv, p["conv_w"], p["conv_b"], p["dt_bias"], p["a_log"], p["d_skip"], shift]
    state_block = (ns, 2, npair, SSD_STATE, LANES)
    if has_init:
        in_specs.append(pl.BlockSpec(state_block, lambda b, c: (b, 0, 0, 0, 0)))
        args.append(s0)
    out_specs = [pl.BlockSpec((1, ns, cn, SSD_WIDTH), lambda b, c: (b, 0, fwd(c), 0)),
                 pl.BlockSpec((1, ns, cn, SSD_WIDTH), lambda b, c: (b, 0, bwd(c), 0))]
    out_shape = [jax.ShapeDtypeStruct((batch // ns, ns, seq, SSD_WIDTH), BF16)] * 2
    if want_fin:
        fin_block = (ns, 2, SSD_HEADS, SSD_STATE, SSD_HEAD_DIM)
        out_specs.append(pl.BlockSpec(fin_block, lambda b, c: (b, 0, 0, 0, 0)))
        out_shape.append(jax.ShapeDtypeStruct((batch,) + fin_block[1:], F32))
    out = pl.pallas_call(
        functools.partial(_ssd_kernel, nc=nc, ns=ns, has_init=has_init, want_fin=want_fin),
        grid=(batch // ns, nc),
        in_specs=in_specs,
        out_specs=out_specs,
        out_shape=out_shape,
        scratch_shapes=[pltpu.VMEM((2 * ns, cn + 2 * HALO, SSD_XBC), F32), pltpu.VMEM((ns, nc, cn, SSD_XBC), F32),
                        pltpu.VMEM((ns, npair, LANES, LANES), F32), pltpu.VMEM((ns, npair, LANES, LANES), F32)],
        compiler_params=_params(2),
        name="ssd",
    )(*args)
    return [out[0].reshape(m, SSD_WIDTH), out[1].reshape(m, SSD_WIDTH)] + list(out[2:])


def _ssd_state_to_pairs(s):
    b = s.shape[0]
    npair = SSD_HEADS // 2
    s = s.reshape(b, 2, npair, 2, SSD_STATE, SSD_HEAD_DIM).transpose(0, 1, 2, 4, 3, 5)
    return s.reshape(b, 2, npair, SSD_STATE, 2 * SSD_HEAD_DIM)


def _l1_in_kernel(*refs, use_rope):
    if use_rope:
        x_ref, nw_ref, sh_ref, sc_ref, w_ref, cos_ref, sin_ref, q_ref, k_ref, v_ref, g_ref = refs
    else:
        x_ref, nw_ref, sh_ref, sc_ref, w_ref, q_ref, k_ref, v_ref, g_ref = refs
    h = (_rms(x_ref[...]) * nw_ref[...] * (1.0 + sc_ref[0]) + sh_ref[0]).astype(BF16)
    half = RET_DK // 2

    def rope(t):
        if not use_rope:
            return t
        cos = cos_ref[...]
        sin = sin_ref[...]
        parts = []
        for hd in range(RET_HEADS):
            x1 = t[:, hd * RET_DK:hd * RET_DK + half]
            x2 = t[:, hd * RET_DK + half:(hd + 1) * RET_DK]
            parts += [x1 * cos - x2 * sin, x2 * cos + x1 * sin]
        return jnp.concatenate(parts, axis=1)

    o1 = RET_QK_WIDTH
    o2 = 2 * RET_QK_WIDTH
    o3 = o2 + RET_V_WIDTH
    q_ref[...] = rope(_dot(h, w_ref[:, :o1])).astype(BF16)
    k_ref[...] = (rope(_dot(h, w_ref[:, o1:o2])) * (RET_DK ** -0.5)).astype(k_ref.dtype)
    v_ref[...] = _dot(h, w_ref[:, o2:o3]).astype(BF16)
    g_ref[...] = _silu(_dot(h, w_ref[:, o3:])).astype(g_ref.dtype)


def _l1_in(x, mod, rows_per_mod, p, rope, tm):
    m = x.shape[0]
    use_rope = rope is not None
    row = lambda w: pl.BlockSpec((tm, w), lambda i: (i, 0))
    n = 2 * RET_QK_WIDTH + 2 * RET_V_WIDTH
    in_specs = ([row(D_MODEL), _const_spec((1, D_MODEL))] + _mod_specs(tm, rows_per_mod, (0, 1))
                + [_resident_spec((D_MODEL, n))])
    args = [x, p["norm_mix"], mod, mod, p["w_in"]]
    if use_rope:
        rows = rope[0].shape[0]
        in_specs += [pl.BlockSpec((tm, LANES), lambda i: (i % (rows // tm), 0))] * 2
        args += list(rope)
    widths = [(RET_QK_WIDTH, BF16), (RET_QK_WIDTH, BF16), (RET_V_WIDTH, BF16), (RET_V_WIDTH, BF16)]
    return pl.pallas_call(
        functools.partial(_l1_in_kernel, use_rope=use_rope),
        grid=(m // tm,),
        in_specs=in_specs,
        out_specs=[row(w) for w, _ in widths],
        out_shape=[jax.ShapeDtypeStruct((m, w), dt) for w, dt in widths],
        compiler_params=_params(1),
        name="l1_in",
    )(*args)


def _ret_kernel(*refs, cn, nc, has_init, want_fin):
    stateless = (not has_init) and nc == 1
    refs = list(refs)
    dec_ref, qf_ref, kf_ref, vf_ref = refs[:4]
    pos = 4
    if stateless:
        qb_ref, kb_ref, vb_ref = qf_ref, kf_ref, vf_ref
    else:
        qb_ref, kb_ref, vb_ref = refs[pos:pos + 3]
        pos += 3
    s0_ref = None
    if has_init:
        s0_ref = refs[pos]
        pos += 1
    yf_ref = refs[pos]
    pos += 1
    yb_ref = None
    if not stateless:
        yb_ref = refs[pos]
        pos += 1
    sfin_ref = None
    if want_fin:
        sfin_ref = refs[pos]
        pos += 1
    decay_ref, sf_ref, sb_ref = refs[pos:]
    c = pl.program_id(1)
    log_g = -jnp.exp(dec_ref[...])

    @pl.when((pl.program_id(0) == 0) & (c == 0))
    def _():
        ii = lax.broadcasted_iota(jnp.int32, (cn, cn), 0)
        jj = lax.broadcasted_iota(jnp.int32, (cn, cn), 1)
        dist = (ii - jj).astype(F32)
        for hd in range(RET_HEADS):
            gf = log_g[0:1, hd:hd + 1]
            gb = log_g[1:2, hd:hd + 1]
            decay_ref[hd] = (jnp.where(dist >= 0, jnp.exp(gf * jnp.maximum(dist, 0.0)), 0.0)
                             + jnp.where(dist <= 0, jnp.exp(gb * jnp.maximum(-dist, 0.0)), 0.0))

    if not stateless:
        @pl.when(c == 0)
        def _():
            if has_init:
                sf_ref[...] = s0_ref[0, 0]
                sb_ref[...] = s0_ref[0, 1]
            else:
                sf_ref[...] = jnp.zeros_like(sf_ref)
                sb_ref[...] = jnp.zeros_like(sb_ref)

    ri = lax.broadcasted_iota(jnp.int32, (cn, 1), 0).astype(F32)
    for hd in range(RET_HEADS):
        gf = log_g[0:1, hd:hd + 1]
        gb = log_g[1:2, hd:hd + 1]
        qs = slice(hd * RET_DK, (hd + 1) * RET_DK)
        vs = slice(hd * RET_DV, (hd + 1) * RET_DV)
        q = qf_ref[:, qs]
        k = kf_ref[:, qs]
        v = vf_ref[:, vs]
        y = _dot((_dot_nt(q, k.astype(BF16)) * decay_ref[hd]).astype(BF16), v)
        upd_f = _dot_tn((k * jnp.exp(gf * (cn - 1.0 - ri))).astype(BF16), v)
        if stateless:
            new_f = upd_f
        else:
            s_old = sf_ref[hd]
            y = y + jnp.exp(gf * (ri + 1.0)) * _dot(q, s_old.astype(BF16))
            new_f = s_old * jnp.exp(gf * cn) + upd_f
            sf_ref[hd] = new_f
        yf_ref[:, vs] = y.astype(yf_ref.dtype)
        q = qb_ref[:, qs]
        k = kb_ref[:, qs]
        v = vb_ref[:, vs]
        upd_b = _dot_tn((k * jnp.exp(gb * ri)).astype(BF16), v)
        if stateless:
            new_b = upd_b
        else:
            s_old = sb_ref[hd]
            yb_ref[:, vs] = (jnp.exp(gb * (cn - ri)) * _dot(q, s_old.astype(BF16))).astype(yb_ref.dtype)
            new_b = s_old * jnp.exp(gb * cn) + upd_b
            sb_ref[hd] = new_b
        if want_fin:
            if stateless:
                sfin_ref[0, 0, hd] = new_f
                sfin_ref[0, 1, hd] = new_b
            else:
                @pl.when(c == nc - 1)
                def _(new_f=new_f, new_b=new_b, hd=hd):
                    sfin_ref[0, 0, hd] = new_f
                    sfin_ref[0, 1, hd] = new_b


def _retention(q, k, v, decay, s0, batch, cn, want_fin):
    m = q.shape[0]
    nc = m // batch // cn
    has_init = s0 is not None
    stateless = (not has_init) and nc == 1

    def fwd(b, c):
        return b * nc + c

    def bwd(b, c):
        return b * nc + nc - 1 - c

    def stream_specs(chunk):
        return [pl.BlockSpec((cn, RET_QK_WIDTH), lambda b, c: (chunk(b, c), 0)),
                pl.BlockSpec((cn, RET_QK_WIDTH), lambda b, c: (chunk(b, c), 0)),
                pl.BlockSpec((cn, RET_V_WIDTH), lambda b, c: (chunk(b, c), 0))]

    in_specs = [_const_spec((8, LANES))] + stream_specs(fwd)
    args = [decay, q, k, v]
    if not stateless:
        in_specs += stream_specs(bwd)
        args += [q, k, v]
    state_block = (1, 2, RET_HEADS, RET_DK, RET_DV)
    if has_init:
        in_specs.append(pl.BlockSpec(state_block, lambda b, c: (b, 0, 0, 0, 0)))
        args.append(s0)
    out_specs = [pl.BlockSpec((cn, RET_V_WIDTH), lambda b, c: (fwd(b, c), 0))]
    if not stateless:
        out_specs.append(pl.BlockSpec((cn, RET_V_WIDTH), lambda b, c: (bwd(b, c), 0)))
    out_shape = [jax.ShapeDtypeStruct((m, RET_V_WIDTH), BF16)] * len(out_specs)
    if want_fin:
        out_specs.append(pl.BlockSpec(state_block, lambda b, c: (b, 0, 0, 0, 0)))
        out_shape.append(jax.ShapeDtypeStruct((batch,) + state_block[1:], F32))
    return pl.pallas_call(
        functools.partial(_ret_kernel, cn=cn, nc=nc, has_init=has_init, want_fin=want_fin),
        grid=(batch, nc),
        in_specs=in_specs,
        out_specs=out_specs,
        out_shape=out_shape,
        scratch_shapes=[pltpu.VMEM((RET_HEADS, cn, cn), F32)] + [pltpu.VMEM((RET_HEADS, RET_DK, RET_DV), F32)] * 2,
        compiler_params=_params(2),
        name="retention",
    )(*args)


def _post_kernel(*refs, mixer, final):
    refs = list(refs)
    x_ref, g1_ref, nw_ref, sh2_ref, sc2_ref, g2_ref, wout_ref, wg_ref, wu_ref, wd_ref = refs[:10]
    pos = 10
    fn_ref = None
    if final:
        fn_ref = refs[pos]
        pos += 1
    if mixer == "ab":
        att_ref, yf_ref, yb_ref, z_ref, gain_ref, o_ref = refs[pos:]
        y = (yf_ref[...].astype(F32) + yb_ref[...].astype(F32)) * _silu(z_ref[...].astype(F32))
        y = _rms(y) * gain_ref[...]
        mix = _dot(att_ref[...], wout_ref[:ATT_WIDTH, :]) + _dot(y.astype(BF16), wout_ref[ATT_WIDTH:, :])
    else:
        parts = refs[pos:-3]
        gate_ref, gain_ref, o_ref = refs[-3:]
        mix = None
        for hd in range(RET_HEADS):
            vs = slice(hd * RET_DV, (hd + 1) * RET_DV)
            y = parts[0][:, vs].astype(F32)
            for extra in parts[1:]:
                y = y + extra[:, vs].astype(F32)
            y = _rms(y) * gain_ref[:, vs]
            part = _dot((gate_ref[:, vs].astype(F32) * y).astype(BF16), wout_ref[vs, :])
            mix = part if mix is None else mix + part
    x1 = x_ref[...] + g1_ref[0] * mix
    h = (_rms(x1) * nw_ref[...] * (1.0 + sc2_ref[0]) + sh2_ref[0]).astype(BF16)
    wcols = D_FF // FF_SPLIT
    ffn = None
    for j in range(FF_SPLIT):
        cs = slice(j * wcols, (j + 1) * wcols)
        act = (_silu(_dot(h, wg_ref[:, cs])) * _dot(h, wu_ref[:, cs])).astype(BF16)
        part = _dot(act, wd_ref[cs, :])
        ffn = part if ffn is None else ffn + part
    x2 = x1 + g2_ref[0] * ffn
    if final:
        x2 = _rms(x2) * fn_ref[...]
    o_ref[...] = x2


def _post(x, mod, rows_per_mod, p, mixer, mix_inputs, gain, final_norm, tm):
    m = x.shape[0]
    row = lambda w: pl.BlockSpec((tm, w), lambda i: (i, 0))
    mixw = p["w_out"].shape[0]
    (g1,) = _mod_specs(tm, rows_per_mod, (2,))
    sh2, sc2, g2 = _mod_specs(tm, rows_per_mod, (3, 4, 5))
    in_specs = [row(D_MODEL), g1, _const_spec((1, D_MODEL)), sh2, sc2, g2,
                _resident_spec((mixw, D_MODEL)), _resident_spec((D_MODEL, D_FF)), _resident_spec((D_MODEL, D_FF)),
                _resident_spec((D_FF, D_MODEL))]
    args = [x, mod, p["norm_ffn"], mod, mod, mod, p["w_out"], p["w_gate"], p["w_up"], p["w_down"]]
    final = final_norm is not None
    if final:
        in_specs.append(_const_spec((1, D_MODEL)))
        args.append(final_norm)
    in_specs += [row(a.shape[1]) for a in mix_inputs] + [_const_spec(gain.shape)]
    args += list(mix_inputs) + [gain]
    return pl.pallas_call(
        functools.partial(_post_kernel, mixer=mixer, final=final),
        grid=(m // tm,),
        in_specs=in_specs,
        out_specs=row(D_MODEL),
        out_shape=jax.ShapeDtypeStruct((m, D_MODEL), F32),
        compiler_params=_params(1),
        name="post_" + mixer,
    )(*args)


def _axial_angles(n_tokens, dim):
    rows = n_tokens // GRID_W
    row = np.repeat(np.arange(rows), GRID_W).astype(np.float64)
    col = np.tile(np.arange(GRID_W), rows).astype(np.float64)
    n_freq = dim // 4
    inv = ROPE_THETA ** (-np.arange(n_freq, dtype=np.float64) / n_freq)
    return np.concatenate([row[:, None] * inv, col[:, None] * inv], axis=-1)


def _head_mean_matrix(width, head):
    idx = jnp.arange(width) // head
    return jnp.where(idx[:, None] == idx[None, :], 1.0 / head, 0.0).astype(BF16)


def _rows_bcast(v, width):
    return jnp.broadcast_to(v.reshape(-1, 1), (v.size, width))


def _trunk(x, mods, rows_per_mod, p0, p1, final_norm, rope_att, rope_ret, caches, seq):
    m = x.shape[0]
    batch = m // seq
    sample = caches is not None
    tm = 512
    l0 = _l0_in(x, mods[0], rows_per_mod, p0, rope_att, tm, None if sample else seq)
    q, ka, va, z, xbc, dt = l0[:6]
    s0_ssd = None
    s0_ret = None
    kv_cache = None
    if sample:
        cache_k, cache_v, state_ssd, state_ret = caches
        ck = cache_k.astype(BF16).transpose(0, 2, 1, 3)
        cv = cache_v.astype(BF16).transpose(0, 2, 1, 3)
        ones_col = jnp.zeros(cv.shape[:3] + (LANES - ATT_HEAD_DIM,), BF16).at[..., 0].set(1.0)
        kv_cache = (ck, jnp.concatenate([cv, ones_col], axis=-1))
        s0_ssd = _ssd_state_to_pairs(state_ssd)
        s0_ret = state_ret
    att = _attention(q.reshape(batch, seq, ATT_WIDTH), ka, va, kv_cache, 256, 512)
    att = att.reshape(m, ATT_WIDTH)
    ssd_out = _ssd(xbc, dt, p0, s0_ssd, batch, want_fin=not sample)
    x = _post(x, mods[0], rows_per_mod, p0, "ab", [att, ssd_out[0], ssd_out[1], z], p0["ssd_gain"], None, tm)
    q1, k1, v1, g1 = _l1_in(x, mods[1], rows_per_mod, p1, rope_ret, tm)
    ret_out = _retention(q1, k1, v1, p1["decay"], s0_ret, batch, RET_CHUNK, want_fin=not sample)
    y_parts = ret_out if sample else ret_out[:-1]
    y = _post(x, mods[1], rows_per_mod, p1, "c", list(y_parts) + [g1], p1["ret_gain"], final_norm, tm)
    if sample:
        return y, None
    new_k = l0[6].transpose(0, 3, 1, 2)
    new_v = l0[7].transpose(0, 3, 1, 2)
    return y, (new_k, new_v, ssd_out[2], ret_out[-1])


def kernel(x_prompt, x_sample, c, cache_k0, cache_v0, state_ssd0, state_ret1, c_ctx, l0_w_ada, l0_b_ada, l0_norm_mix, l0_norm_ffn, l0_w_in, l0_w_out, l0_q_gain, l0_k_gain, l0_conv_w, l0_conv_b, l0_dt_bias, l0_a_log, l0_d_skip, l0_ssd_gain, l0_w_gate, l0_w_up, l0_w_down, l1_w_ada, l1_b_ada, l1_norm_mix, l1_norm_ffn, l1_w_in, l1_w_out, l1_decay, l1_ret_gain, l1_w_gate, l1_w_up, l1_w_down, final_norm):
    b_ctx, seq_ctx, d = x_prompt.shape
    b_lat, seq_lat, _ = x_sample.shape
    assert d == D_MODEL and l0_w_in.shape == (D_MODEL, L0_IN) and l0_w_gate.shape == (D_MODEL, D_FF)
    row = lambda v: v.reshape(1, -1)

    p0 = dict(
        norm_mix=row(l0_norm_mix), norm_ffn=row(l0_norm_ffn),
        w_in=jnp.pad(l0_w_in, ((0, 0), (0, L0_IN_PAD - L0_IN))).astype(BF16),
        w_out=l0_w_out.astype(BF16), w_gate=l0_w_gate.astype(BF16), w_up=l0_w_up.astype(BF16),
        w_down=l0_w_down.astype(BF16),
        q_gain=row(jnp.tile(l0_q_gain, ATT_HEADS)), k_gain=row(jnp.tile(l0_k_gain, ATT_KV_HEADS)),
        pq=_head_mean_matrix(ATT_WIDTH, ATT_HEAD_DIM), pk=_head_mean_matrix(ATT_KV_WIDTH, ATT_HEAD_DIM),
        conv_w=jnp.pad(l0_conv_w, ((0, HALO - SSD_CONV), (0, 0))), conv_b=row(l0_conv_b),
        dt_bias=_rows_bcast(l0_dt_bias, SSD_CHUNK), a_log=_rows_bcast(l0_a_log, SSD_CHUNK),
        d_skip=row(jnp.repeat(l0_d_skip, SSD_HEAD_DIM)), ssd_gain=row(l0_ssd_gain),
    )
    p1 = dict(
        norm_mix=row(l1_norm_mix), norm_ffn=row(l1_norm_ffn),
        w_in=l1_w_in.astype(BF16), w_out=l1_w_out.astype(BF16), w_gate=l1_w_gate.astype(BF16),
        w_up=l1_w_up.astype(BF16), w_down=l1_w_down.astype(BF16),
        decay=jnp.pad(l1_decay, ((0, 8 - l1_decay.shape[0]), (0, LANES - l1_decay.shape[1]))),
        ret_gain=row(l1_ret_gain),
    )
    fnorm = row(final_norm)

    n_cond = 8
    conds = jnp.concatenate([c_ctx[None, :], c, jnp.zeros((n_cond - 1 - b_lat, d), F32)], axis=0)
    mod0 = _ada(conds, l0_w_ada, l0_b_ada)
    mod1 = _ada(conds, l1_w_ada, l1_b_ada)
    mods_ctx = [mod[0:1].reshape(1, 1, 6 * d) for mod in (mod0, mod1)]
    mods_lat = [mod[1:1 + b_lat].reshape(b_lat, 1, 6 * d) for mod in (mod0, mod1)]

    m_ctx = b_ctx * seq_ctx
    y_prompt, ctx = _trunk(x_prompt.reshape(m_ctx, d), mods_ctx, m_ctx, p0, p1, fnorm, None, None, None, seq_ctx)
    new_k0, new_v0, new_ssd0, new_ret1 = ctx
    ang = _axial_angles(seq_lat, ATT_HEAD_DIM)
    cos, sin = np.cos(ang), np.sin(ang)
    reps = LANES // ATT_HEAD_DIM
    rope_att = (jnp.asarray(np.tile(np.concatenate([cos, cos], axis=1), (1, reps)), F32),
                jnp.asarray(np.tile(np.concatenate([-sin, sin], axis=1), (1, reps)), F32))
    ang = _axial_angles(seq_lat, RET_DK)
    rope_ret = (jnp.asarray(np.cos(ang), F32), jnp.asarray(np.sin(ang), F32))
    caches = (cache_k0, cache_v0, state_ssd0, state_ret1)
    y_sample, _ = _trunk(x_sample.reshape(b_lat * seq_lat, d), mods_lat, seq_lat, p0, p1, fnorm, rope_att, rope_ret,
                         caches, seq_lat)
    return (y_prompt.reshape(b_ctx, seq_ctx, d), y_sample.reshape(b_lat, seq_lat, d),
            new_k0, new_v0, new_ssd0, new_ret1)
```

```python
import functools

import jax
import jax.numpy as jnp
import numpy as np
from jax import lax
from jax.experimental import pallas as pl
from jax.experimental.pallas import tpu as pltpu

F32 = jnp.float32
BF16 = jnp.bfloat16

EPS = 1e-6
ROPE_THETA = 10000.0
GRID_W = 64
D_MODEL = 1024
ATT_HEAD_DIM = 64
ATT_HEADS = 8
ATT_KV_HEADS = 2
ATT_GROUP = ATT_HEADS // ATT_KV_HEADS
ATT_WIDTH = ATT_HEADS * ATT_HEAD_DIM
ATT_KV_WIDTH = ATT_KV_HEADS * ATT_HEAD_DIM
SSD_WIDTH = 512
SSD_HEADS = 8
SSD_HEAD_DIM = 64
SSD_STATE = 64
SSD_GROUPS = 2
SSD_CONV = 5
SSD_XBC = SSD_WIDTH + 2 * SSD_GROUPS * SSD_STATE
L0_IN = ATT_WIDTH + 2 * ATT_KV_WIDTH + SSD_WIDTH + SSD_XBC + 2 * SSD_HEADS
RET_HEADS = 4
RET_DK = 256
RET_DV = 512
RET_QK_WIDTH = RET_HEADS * RET_DK
RET_V_WIDTH = RET_HEADS * RET_DV
D_FF = 2816

LANES = 128
HALO = 8
L0_IN_PAD = -(-L0_IN // LANES) * LANES
SSD_CHUNK = 128
SSD_SEQS = 2
RET_CHUNK = 256
VMEM_LIMIT = 56 * 1024 * 1024


def _dot(a, b):
    return jnp.dot(a, b, preferred_element_type=F32)


def _dot_nt(a, b):
    return lax.dot_general(a, b, (((1,), (1,)), ((), ())), preferred_element_type=F32)


def _dot_tn(a, b):
    return lax.dot_general(a, b, (((0,), (0,)), ((), ())), preferred_element_type=F32)


def _silu(x):
    half = 0.5 * x
    return half + half * jnp.tanh(half)


def _softplus(x):
    return jnp.maximum(x, 0.0) + jnp.log1p(jnp.exp(-jnp.abs(x)))


def _rms(x):
    return x * lax.rsqrt(jnp.mean(x * x, axis=-1, keepdims=True) + EPS)


def _split3(x):
    hi = x.astype(BF16)
    r = x - hi.astype(F32)
    mid = r.astype(BF16)
    lo = (r - mid.astype(F32)).astype(BF16)
    return hi, mid, lo


def _const_spec(shape):
    return pl.BlockSpec(shape, lambda *_: (0,) * len(shape))


def _resident_spec(shape):
    return pl.BlockSpec(shape, lambda *_: (0,) * len(shape), pipeline_mode=pl.Buffered(1))


def _params(n_axes, vmem=VMEM_LIMIT):
    return pltpu.CompilerParams(dimension_semantics=("arbitrary",) * n_axes, vmem_limit_bytes=vmem)


def _ada_kernel(c_ref, w_ref, b_ref, o_ref):
    s = _silu(c_ref[...])
    o_ref[...] = _dot(s.astype(BF16), w_ref[...].astype(BF16)) + b_ref[...]


def _ada(conds, w, b):
    n = w.shape[1]
    tn = 1536
    return pl.pallas_call(
        _ada_kernel,
        grid=(n // tn,),
        in_specs=[_const_spec(conds.shape),
                  pl.BlockSpec((D_MODEL, tn), lambda j: (0, j)),
                  pl.BlockSpec((1, tn), lambda j: (0, j))],
        out_specs=pl.BlockSpec((conds.shape[0], tn), lambda j: (0, j)),
        out_shape=jax.ShapeDtypeStruct((conds.shape[0], n), F32),
        compiler_params=_params(1),
        name="ada",
    )(conds, w, b.reshape(1, n))


def _mod_specs(tm, rows_per_mod, which):
    return [pl.BlockSpec((1, 1, D_MODEL), lambda i, j=j: ((i * tm) // rows_per_mod, 0, j)) for j in which]


def _head_rms(x, p_ref, gain):
    x2 = x * x
    hi = x2.astype(BF16)
    lo = (x2 - hi.astype(F32)).astype(BF16)
    ms = _dot(hi, p_ref[...]) + _dot(lo, p_ref[...])
    return x * lax.rsqrt(ms + EPS) * gain


def _rope64(x, cos, sin):
    n = x.shape[1]
    lane = lax.broadcasted_iota(jnp.int32, x.shape, 1)
    first_half = (lane % ATT_HEAD_DIM) < (ATT_HEAD_DIM // 2)
    partner = jnp.where(first_half, pltpu.roll(x, n - ATT_HEAD_DIM // 2, 1), pltpu.roll(x, ATT_HEAD_DIM // 2, 1))
    return x * cos + partner * sin


def _l0_in_kernel(*refs, use_rope, ctx_seq):
    refs = list(refs)
    x_ref, nw_ref, sh_ref, sc_ref, w_ref, qg_ref, kg_ref, pq_ref, pk_ref = refs[:9]
    pos = 9
    if use_rope:
        cos_ref, sin_ref = refs[pos:pos + 2]
        pos += 2
    q_ref, ka_ref, va_ref, z_ref, xbc_ref, dt_ref = refs[pos:pos + 6]
    pos += 6
    h = _rms(x_ref[...]) * nw_ref[...] * (1.0 + sc_ref[0]) + sh_ref[0]
    proj = _dot(h.astype(BF16), w_ref[...])
    o1 = ATT_WIDTH
    o2 = o1 + ATT_KV_WIDTH
    o3 = o2 + ATT_KV_WIDTH
    o4 = o3 + SSD_WIDTH
    o5 = o4 + SSD_XBC
    q = _head_rms(proj[:, :o1], pq_ref, qg_ref[...])
    k = _head_rms(proj[:, o1:o2], pk_ref, kg_ref[...])
    v = proj[:, o2:o3]
    if ctx_seq:
        kt_ref, vt_ref = refs[pos:]
        for s in range(x_ref.shape[0] // ctx_seq):
            kt = k[s * ctx_seq:(s + 1) * ctx_seq].T
            vt = v[s * ctx_seq:(s + 1) * ctx_seq].T
            for kv in range(ATT_KV_HEADS):
                kt_ref[s, kv] = kt[kv * ATT_HEAD_DIM:(kv + 1) * ATT_HEAD_DIM]
                vt_ref[s, kv] = vt[kv * ATT_HEAD_DIM:(kv + 1) * ATT_HEAD_DIM]
    if use_rope:
        cos = cos_ref[...]
        sin = sin_ref[...]
        k = _rope64(k, cos, sin)
        reps = ATT_WIDTH // LANES
        q = _rope64(q, jnp.concatenate([cos] * reps, axis=1), jnp.concatenate([sin] * reps, axis=1))
    q_ref[...] = (q * (ATT_HEAD_DIM ** -0.5)).astype(BF16)
    kb = k.astype(BF16)
    lane = lax.broadcasted_iota(jnp.int32, (1, LANES), 1)
    ones_col = jnp.where(lane == ATT_HEAD_DIM, 1.0, 0.0)
    for kv in range(ATT_KV_HEADS):
        ka_ref[kv] = kb[:, kv * ATT_HEAD_DIM:(kv + 1) * ATT_HEAD_DIM]
        vv = v if kv == 0 else pltpu.roll(v, (LANES - kv * ATT_HEAD_DIM) % LANES, 1)
        va_ref[kv] = jnp.where(lane < ATT_HEAD_DIM, vv, ones_col).astype(BF16)
    z_ref[...] = proj[:, o3:o4].astype(z_ref.dtype)
    xbc_ref[...] = proj[:, o4:o5]
    dt_ref[...] = proj[:, o5:]


def _l0_in(x, mod, rows_per_mod, p, rope, tm, ctx_seq):
    m = x.shape[0]
    use_rope = rope is not None
    row = lambda w: pl.BlockSpec((tm, w), lambda i: (i, 0))
    in_specs = ([row(D_MODEL), _const_spec((1, D_MODEL))] + _mod_specs(tm, rows_per_mod, (0, 1))
                + [_resident_spec((D_MODEL, L0_IN_PAD)), _const_spec((1, ATT_WIDTH)), _const_spec((1, ATT_KV_WIDTH)),
                   _resident_spec((ATT_WIDTH, ATT_WIDTH)), _resident_spec((ATT_KV_WIDTH, ATT_KV_WIDTH))])
    args = [x, p["norm_mix"], mod, mod, p["w_in"], p["q_gain"], p["k_gain"], p["pq"], p["pk"]]
    if use_rope:
        rows = rope[0].shape[0]
        in_specs += [pl.BlockSpec((tm, LANES), lambda i: (i % (rows // tm), 0))] * 2
        args += list(rope)
    head = lambda w: pl.BlockSpec((ATT_KV_HEADS, tm, w), lambda i: (0, i, 0))
    out_specs = [row(ATT_WIDTH), head(ATT_HEAD_DIM), head(LANES), row(SSD_WIDTH), row(SSD_XBC), row(LANES)]
    out_shape = [jax.ShapeDtypeStruct((m, ATT_WIDTH), BF16),
                 jax.ShapeDtypeStruct((ATT_KV_HEADS, m, ATT_HEAD_DIM), BF16),
                 jax.ShapeDtypeStruct((ATT_KV_HEADS, m, LANES), BF16),
                 jax.ShapeDtypeStruct((m, SSD_WIDTH), BF16), jax.ShapeDtypeStruct((m, SSD_XBC), F32),
                 jax.ShapeDtypeStruct((m, LANES), F32)]
    if ctx_seq:
        assert tm % ctx_seq == 0
        cache_block = (tm // ctx_seq, ATT_KV_HEADS, ATT_HEAD_DIM, ctx_seq)
        out_specs += [pl.BlockSpec(cache_block, lambda i: (i, 0, 0, 0))] * 2
        out_shape += [jax.ShapeDtypeStruct((m // ctx_seq,) + cache_block[1:], F32)] * 2
    return pl.pallas_call(
        functools.partial(_l0_in_kernel, use_rope=use_rope, ctx_seq=ctx_seq),
        grid=(m // tm,),
        in_specs=in_specs,
        out_specs=out_specs,
        out_shape=out_shape,
        compiler_params=_params(1),
        name="l0_in",
    )(*args)


def _attn_kernel(*refs, ck, sb, has_cache):
    if has_cache:
        q_ref, k_ref, v_ref, kc_ref, vc_ref, o_ref, s0_ref, s1_ref, m0_ref, m1_ref = refs
    else:
        q_ref, k_ref, v_ref, o_ref, s0_ref, s1_ref, m0_ref, m1_ref = refs
    chunks = []
    if has_cache:
        chunks += [(kc_ref.at[0, 0], vc_ref.at[0, 0], j * ck) for j in range(kc_ref.shape[2] // ck)]
    chunks += [(k_ref.at[0], v_ref.at[0], j * ck) for j in range(k_ref.shape[1] // ck)]
    tt = sb // ATT_GROUP
    nsub = q_ref.shape[1] // tt
    slots = ((s0_ref, m0_ref), (s1_ref, m1_ref))

    def scores(i, slot):
        s_ref, m_ref = slots[slot]
        q4 = q_ref[0, pl.ds(pl.multiple_of(i * tt, tt), tt), :]
        q = jnp.concatenate([q4[:, g * ATT_HEAD_DIM:(g + 1) * ATT_HEAD_DIM] for g in range(ATT_GROUP)], axis=0)
        mx = None
        for j, (kr, _, r0) in enumerate(chunks):
            s = _dot_nt(q, kr[r0:r0 + ck, :])
            s_ref[j] = s
            for t in range(ck // LANES):
                part = s[:, t * LANES:(t + 1) * LANES]
                mx = part if mx is None else jnp.maximum(mx, part)
        m_ref[...] = jnp.broadcast_to(jnp.max(mx, axis=1, keepdims=True), (sb, LANES))

    def values(i, slot):
        s_ref, m_ref = slots[slot]
        m = jnp.concatenate([m_ref[...]] * (ck // LANES), axis=1)
        acc = None
        for j, (_, vr, r0) in enumerate(chunks):
            p = jnp.exp(s_ref[j] - m).astype(BF16)
            part = _dot(p, vr[r0:r0 + ck, :])
            acc = part if acc is None else acc + part
        out = acc[:, :ATT_HEAD_DIM] / acc[:, ATT_HEAD_DIM:ATT_HEAD_DIM + 1]
        out = jnp.concatenate([out[g * tt:(g + 1) * tt] for g in range(ATT_GROUP)], axis=1)
        o_ref[0, pl.ds(pl.multiple_of(i * tt, tt), tt), :] = out.astype(o_ref.dtype)

    scores(0, 0)

    def body(h, carry):
        scores(2 * h + 1, 1)
        values(2 * h, 0)
        scores(2 * h + 2, 0)
        values(2 * h + 1, 1)
        return carry

    lax.fori_loop(0, nsub // 2 - 1, body, 0)
    scores(nsub - 1, 1)
    values(nsub - 2, 0)
    values(nsub - 1, 1)


def _attention(q, k, v, cache, ck, sb):
    b, seq, width = q.shape
    nkv = k.shape[0]
    gw = width // nkv
    lk = seq
    in_specs = [pl.BlockSpec((1, seq, gw), lambda i, j: (i, 0, j)),
                pl.BlockSpec((1, seq, ATT_HEAD_DIM), lambda i, j: (j, i, 0)),
                pl.BlockSpec((1, seq, LANES), lambda i, j: (j, i, 0))]
    args = [q, k, v]
    if cache is not None:
        past = cache[0].shape[2]
        assert past % ck == 0
        lk += past
        in_specs += [pl.BlockSpec((1, 1, past, ATT_HEAD_DIM), lambda i, j: (i, j, 0, 0)),
                     pl.BlockSpec((1, 1, past, LANES), lambda i, j: (i, j, 0, 0))]
        args += list(cache)
    assert (seq * ATT_GROUP) % (2 * sb) == 0 and seq % ck == 0
    return pl.pallas_call(
        functools.partial(_attn_kernel, ck=ck, sb=sb, has_cache=cache is not None),
        grid=(b, nkv),
        in_specs=in_specs,
        out_specs=pl.BlockSpec((1, seq, gw), lambda i, j: (i, 0, j)),
        out_shape=jax.ShapeDtypeStruct(q.shape, BF16),
        scratch_shapes=[pltpu.VMEM((lk // ck, sb, ck), F32)] * 2 + [pltpu.VMEM((sb, LANES), F32)] * 2,
        compiler_params=_params(2),
        name="attention",
    )(*args)


def _ssd_group_rows(pair):
    g = pair // (SSD_HEADS // 2 // SSD_GROUPS)
    return slice(g * SSD_STATE, (g + 1) * SSD_STATE)


def _ssd_conv(cur_ref, prev_ref, next_ref, is_first, is_last, ext_ref, cw_ref, cb_ref, shift_ref):
    cn = SSD_CHUNK
    ext_ref[0:HALO, :] = jnp.where(is_first, 0.0, prev_ref[...])
    ext_ref[HALO:HALO + cn, :] = cur_ref[...]
    ext_ref[HALO + cn:, :] = jnp.where(is_last, 0.0, next_ref[...])
    shifted = _dot(shift_ref[...], ext_ref[...].astype(BF16))
    centre = SSD_CONV // 2
    u = cb_ref[...] + cw_ref[centre:centre + 1, :] * cur_ref[...]
    for n, t in enumerate(t for t in range(SSD_CONV) if t != centre):
        u = u + cw_ref[t:t + 1, :] * shifted[n * cn:(n + 1) * cn]
    return _silu(u)


def _ssd_scalars(dt_ref, dtb_ref, alog_ref, *, forward):
    cn = SSD_CHUNK
    nh = 2 * SSD_HEADS
    dt_t = _softplus(dt_ref[...].T[:nh] + dtb_ref[...])
    a_t = dt_t * (-jnp.exp(alog_ref[...]))
    ii = lax.broadcasted_iota(jnp.int32, (cn, cn), 0)
    jj = lax.broadcasted_iota(jnp.int32, (cn, cn), 1)
    keep = (ii >= jj) if forward else (ii <= jj)
    tri_t = ((ii <= jj) if forward else (ii >= jj)).astype(BF16)
    c3 = _dot(jnp.concatenate(_split3(a_t), axis=0), tri_t)
    cum_t = c3[:nh] + c3[nh:2 * nh] + c3[2 * nh:]
    cols = jnp.concatenate([dt_t, cum_t, jnp.zeros((LANES - 2 * nh, cn), F32)], axis=0).T
    lane = lax.broadcasted_iota(jnp.int32, (1, LANES), 1)
    return dict(cum_t=cum_t, cols=cols, keep=keep, lo=lane < SSD_HEAD_DIM, forward=forward)


def _ssd_mats(act, prep):
    lo = prep["lo"]
    xs = act[:, :SSD_WIDTH]
    bm = act[:, SSD_WIDTH:SSD_WIDTH + LANES]
    cm = act[:, SSD_WIDTH + LANES:]
    bm_b = bm.astype(BF16)
    gmat = []
    bg_b = []
    for g in range(SSD_GROUPS):
        in_group = lo if g == 0 else jnp.logical_not(lo)
        gmat.append(_dot_nt(jnp.where(in_group, cm, 0.0).astype(BF16), bm_b))
        bg_b.append(jnp.where(in_group, bm, 0.0).astype(BF16))
    return dict(prep, xs=xs, cm_b=cm.astype(BF16), gmat=gmat, bg_b=bg_b)


def _ssd_pair(prep, pair, s_ref, y_ref, dsk_ref):
    cn = SSD_CHUNK
    nh = 2 * SSD_HEADS
    forward, cols, cum_t, keep, lo = prep["forward"], prep["cols"], prep["cum_t"], prep["keep"], prep["lo"]

    def col(lane_idx):
        return jnp.broadcast_to(cols[:, lane_idx:lane_idx + 1], (cn, LANES))

    off = 0 if forward else SSD_HEADS
    tot_col = cn - 1 if forward else 0
    g = pair // (SSD_HEADS // 2 // SSD_GROUPS)
    gmat = prep["gmat"][g]
    h0 = off + 2 * pair
    h1 = h0 + 1
    ci0 = col(nh + h0)
    ci1 = col(nh + h1)
    cip = jnp.where(lo, ci0, ci1)
    m0 = (gmat * jnp.exp(jnp.where(keep, ci0 - cum_t[h0:h0 + 1, :], -1e30))).astype(BF16)
    m1 = (gmat * jnp.exp(jnp.where(keep, ci1 - cum_t[h1:h1 + 1, :], -1e30))).astype(BF16)
    xs_p = prep["xs"][:, pair * LANES:(pair + 1) * LANES]
    vp = xs_p * jnp.where(lo, col(h0), col(h1))
    v0 = jnp.where(lo, vp, 0.0).astype(BF16)
    v1 = jnp.where(lo, 0.0, vp).astype(BF16)
    s_old = s_ref[pair]
    y = (_dot(jnp.concatenate([m0, m1], axis=1), jnp.concatenate([v0, v1], axis=0))
         + jnp.exp(cip) * _dot(prep["cm_b"], s_old.astype(BF16)))
    totp = jnp.where(lo, cum_t[h0:h0 + 1, tot_col:tot_col + 1], cum_t[h1:h1 + 1, tot_col:tot_col + 1])
    s_ref[pair] = s_old * jnp.exp(totp) + _dot_tn(prep["bg_b"][g], (vp * jnp.exp(totp - cip)).astype(BF16))
    if forward:
        y = y + dsk_ref[:, pair * LANES:(pair + 1) * LANES] * xs_p
    y_ref[:, pair * LANES:(pair + 1) * LANES] = y.astype(y_ref.dtype)


def _ssd_kernel(*refs, nc, ns, has_init, want_fin):
    refs = list(refs)
    cf_ref, pf_ref, nf_ref, dtf_ref, cb_ref, pb_ref, nb_ref, dtb_ref = refs[:8]
    cw_ref, cbias_ref, dtbias_ref, alog_ref, dsk_ref, shift_ref = refs[8:14]
    pos = 14
    s0_ref = None
    if has_init:
        s0_ref = refs[pos]
        pos += 1
    yf_ref, yb_ref = refs[pos:pos + 2]
    pos += 2
    sfin_ref = None
    if want_fin:
        sfin_ref = refs[pos]
        pos += 1
    ext_ref, act_ref, sf_ref, sb_ref = refs[pos:]
    c = pl.program_id(1)
    c_fwd = c
    c_bwd = nc - 1 - c

    @pl.when(c == 0)
    def _():
        sf_ref[...] = jnp.zeros_like(sf_ref)
        sb_ref[...] = jnp.zeros_like(sb_ref)
        if has_init:
            for s in range(ns):
                for pair in range(SSD_HEADS // 2):
                    rows = _ssd_group_rows(pair)
                    sf_ref[s, pair, rows, :] = s0_ref[s, 0, pair]
                    sb_ref[s, pair, rows, :] = s0_ref[s, 1, pair]

    def scalars():
        out = []
        for s in range(ns):
            out.append(_ssd_scalars(dtf_ref.at[0, s], dtbias_ref, alog_ref, forward=True))
            out.append(_ssd_scalars(dtb_ref.at[0, s], dtbias_ref, alog_ref, forward=False))
        return out

    def streams(preps, acts):
        work = []
        for s in range(ns):
            work.append((_ssd_mats(acts[2 * s], preps[2 * s]), sf_ref.at[s], yf_ref.at[0, s]))
            work.append((_ssd_mats(acts[2 * s + 1], preps[2 * s + 1]), sb_ref.at[s], yb_ref.at[0, s]))
        for pair in range(SSD_HEADS // 2):
            for prep, s_ref, y_ref in work:
                _ssd_pair(prep, pair, s_ref, y_ref, dsk_ref)

    @pl.when(c < nc // 2)
    def _():
        preps = scalars()
        acts = []
        for s in range(ns):
            act_f = _ssd_conv(cf_ref.at[0, s], pf_ref.at[0, s], nf_ref.at[0, s], c_fwd == 0, c_fwd == nc - 1,
                              ext_ref.at[2 * s], cw_ref, cbias_ref, shift_ref)
            act_b = _ssd_conv(cb_ref.at[0, s], pb_ref.at[0, s], nb_ref.at[0, s], c_bwd == 0, c_bwd == nc - 1,
                              ext_ref.at[2 * s + 1], cw_ref, cbias_ref, shift_ref)
            act_ref[s, c_fwd] = act_f
            act_ref[s, c_bwd] = act_b
            acts += [act_f, act_b]
        streams(preps, acts)

    @pl.when(c >= nc // 2)
    def _():
        acts = []
        for s in range(ns):
            acts += [act_ref[s, c_fwd], act_ref[s, c_bwd]]
        streams(scalars(), acts)

    if want_fin:
        @pl.when(c == nc - 1)
        def _():
            for s in range(ns):
                for pair in range(SSD_HEADS // 2):
                    rows = _ssd_group_rows(pair)
                    for d, st_ref in enumerate((sf_ref, sb_ref)):
                        both = st_ref[s, pair, rows, :]
                        sfin_ref[s, d, 2 * pair] = both[:, :SSD_HEAD_DIM]
                        sfin_ref[s, d, 2 * pair + 1] = both[:, SSD_HEAD_DIM:]


def _ssd(xbc, dt, p, s0, batch, want_fin):
    m = xbc.shape[0]
    cn = SSD_CHUNK
    ns = SSD_SEQS
    seq = m // batch
    nc = seq // cn
    assert nc % 2 == 0
    assert batch % ns == 0
    per = cn // HALO
    n_halo = seq // HALO
    has_init = s0 is not None
    npair = SSD_HEADS // 2
    view = lambda a: a.reshape(batch // ns, ns, seq, a.shape[-1])

    def fwd(c):
        return c

    def bwd(c):
        return nc - 1 - c

    def stream_specs(chunk):
        return [pl.BlockSpec((1, ns, cn, SSD_XBC), lambda b, c: (b, 0, chunk(c), 0)),
                pl.BlockSpec((1, ns, HALO, SSD_XBC), lambda b, c: (b, 0, jnp.maximum(chunk(c) * per - 1, 0), 0)),
                pl.BlockSpec((1, ns, HALO, SSD_XBC),
                             lambda b, c: (b, 0, jnp.minimum(chunk(c) * per + per, n_halo - 1), 0)),
                pl.BlockSpec((1, ns, cn, LANES), lambda b, c: (b, 0, chunk(c), 0))]

    in_specs = stream_specs(fwd) + stream_specs(bwd) + [
        _const_spec((HALO, SSD_XBC)), _const_spec((1, SSD_XBC)), _const_spec((2 * SSD_HEADS, cn)),
        _const_spec((2 * SSD_HEADS, cn)), _const_spec((1, SSD_WIDTH)),
        _const_spec(((SSD_CONV - 1) * cn, cn + 2 * HALO))]
    taps = [t for t in range(SSD_CONV) if t != SSD_CONV // 2]
    src = np.concatenate([HALO + np.arange(cn) + (t - SSD_CONV // 2) for t in taps])
    shift = jnp.asarray(src[:, None] == np.arange(cn + 2 * HALO)[None, :], BF16)
    xv, dv = view(xbc), view(dt)
    args = [xv, xv, xv, dv, xv, xv, xv, dv, p["conv_w"], p["conv_b"], p["dt_bias"], p["a_log"], p["d_skip"], shift]
    state_block = (ns, 2, npair, SSD_STATE, LANES)
    if has_init:
        in_specs.append(pl.BlockSpec(state_block, lambda b, c: (b, 0, 0, 0, 0)))
        args.append(s0)
    out_specs = [pl.BlockSpec((1, ns, cn, SSD_WIDTH), lambda b, c: (b, 0, fwd(c), 0)),
                 pl.BlockSpec((1, ns, cn, SSD_WIDTH), lambda b, c: (b, 0, bwd(c), 0))]
    out_shape = [jax.ShapeDtypeStruct((batch // ns, ns, seq, SSD_WIDTH), BF16)] * 2
    if want_fin:
        fin_block = (ns, 2, SSD_HEADS, SSD_STATE, SSD_HEAD_DIM)
        out_specs.append(pl.BlockSpec(fin_block, lambda b, c: (b, 0, 0, 0, 0)))
        out_shape.append(jax.ShapeDtypeStruct((batch,) + fin_block[1:], F32))
    out = pl.pallas_call(
        functools.partial(_ssd_kernel, nc=nc, ns=ns, has_init=has_init, want_fin=want_fin),
        grid=(batch // ns, nc),
        in_specs=in_specs,
        out_specs=out_specs,
        out_shape=out_shape,
        scratch_shapes=[pltpu.VMEM((2 * ns, cn + 2 * HALO, SSD_XBC), F32), pltpu.VMEM((ns, nc, cn, SSD_XBC), F32),
                        pltpu.VMEM((ns, npair, LANES, LANES), F32), pltpu.VMEM((ns, npair, LANES, LANES), F32)],
        compiler_params=_params(2),
        name="ssd",
    )(*args)
    return [out[0].reshape(m, SSD_WIDTH), out[1].reshape(m, SSD_WIDTH)] + list(out[2:])


def _ssd_state_to_pairs(s):
    b = s.shape[0]
    npair = SSD_HEADS // 2
    s = s.reshape(b, 2, npair, 2, SSD_STATE, SSD_HEAD_DIM).transpose(0, 1, 2, 4, 3, 5)
    return s.reshape(b, 2, npair, SSD_STATE, 2 * SSD_HEAD_DIM)


def _l1_in_kernel(*refs, use_rope):
    if use_rope:
        x_ref, nw_ref, sh_ref, sc_ref, w_ref, cos_ref, sin_ref, q_ref, k_ref, v_ref, g_ref = refs
    else:
        x_ref, nw_ref, sh_ref, sc_ref, w_ref, q_ref, k_ref, v_ref, g_ref = refs
    h = (_rms(x_ref[...]) * nw_ref[...] * (1.0 + sc_ref[0]) + sh_ref[0]).astype(BF16)
    half = RET_DK // 2

    def rope(t):
        if not use_rope:
            return t
        cos = cos_ref[...]
        sin = sin_ref[...]
        parts = []
        for hd in range(RET_HEADS):
            x1 = t[:, hd * RET_DK:hd * RET_DK + half]
            x2 = t[:, hd * RET_DK + half:(hd + 1) * RET_DK]
            parts += [x1 * cos - x2 * sin, x2 * cos + x1 * sin]
        return jnp.concatenate(parts, axis=1)

    o1 = RET_QK_WIDTH
    o2 = 2 * RET_QK_WIDTH
    o3 = o2 + RET_V_WIDTH
    q_ref[...] = rope(_dot(h, w_ref[:, :o1])).astype(BF16)
    k_ref[...] = (rope(_dot(h, w_ref[:, o1:o2])) * (RET_DK ** -0.5)).astype(k_ref.dtype)
    v_ref[...] = _dot(h, w_ref[:, o2:o3]).astype(BF16)
    g_ref[...] = _silu(_dot(h, w_ref[:, o3:])).astype(g_ref.dtype)


def _l1_in(x, mod, rows_per_mod, p, rope, tm):
    m = x.shape[0]
    use_rope = rope is not None
    row = lambda w: pl.BlockSpec((tm, w), lambda i: (i, 0))
    n = 2 * RET_QK_WIDTH + 2 * RET_V_WIDTH
    in_specs = ([row(D_MODEL), _const_spec((1, D_MODEL))] + _mod_specs(tm, rows_per_mod, (0, 1))
                + [_resident_spec((D_MODEL, n))])
    args = [x, p["norm_mix"], mod, mod, p["w_in"]]
    if use_rope:
        rows = rope[0].shape[0]
        in_specs += [pl.BlockSpec((tm, LANES), lambda i: (i % (rows // tm), 0))] * 2
        args += list(rope)
    widths = [(RET_QK_WIDTH, BF16), (RET_QK_WIDTH, BF16), (RET_V_WIDTH, BF16), (RET_V_WIDTH, BF16)]
    return pl.pallas_call(
        functools.partial(_l1_in_kernel, use_rope=use_rope),
        grid=(m // tm,),
        in_specs=in_specs,
        out_specs=[row(w) for w, _ in widths],
        out_shape=[jax.ShapeDtypeStruct((m, w), dt) for w, dt in widths],
        compiler_params=_params(1),
        name="l1_in",
    )(*args)


def _ret_kernel(*refs, cn, nc, has_init, want_fin):
    stateless = (not has_init) and nc == 1
    refs = list(refs)
    dec_ref, qf_ref, kf_ref, vf_ref = refs[:4]
    pos = 4
    if stateless:
        qb_ref, kb_ref, vb_ref = qf_ref, kf_ref, vf_ref
    else:
        qb_ref, kb_ref, vb_ref = refs[pos:pos + 3]
        pos += 3
    s0_ref = None
    if has_init:
        s0_ref = refs[pos]
        pos += 1
    yf_ref = refs[pos]
    pos += 1
    yb_ref = None
    if not stateless:
        yb_ref = refs[pos]
        pos += 1
    sfin_ref = None
    if want_fin:
        sfin_ref = refs[pos]
        pos += 1
    decay_ref, sf_ref, sb_ref = refs[pos:]
    c = pl.program_id(1)
    log_g = -jnp.exp(dec_ref[...])

    @pl.when((pl.program_id(0) == 0) & (c == 0))
    def _():
        ii = lax.broadcasted_iota(jnp.int32, (cn, cn), 0)
        jj = lax.broadcasted_iota(jnp.int32, (cn, cn), 1)
        dist = (ii - jj).astype(F32)
        for hd in range(RET_HEADS):
            gf = log_g[0:1, hd:hd + 1]
            gb = log_g[1:2, hd:hd + 1]
            decay_ref[hd] = (jnp.where(dist >= 0, jnp.exp(gf * jnp.maximum(dist, 0.0)), 0.0)
                             + jnp.where(dist <= 0, jnp.exp(gb * jnp.maximum(-dist, 0.0)), 0.0))

    if not stateless:
        @pl.when(c == 0)
        def _():
            if has_init:
                sf_ref[...] = s0_ref[0, 0]
                sb_ref[...] = s0_ref[0, 1]
            else:
                sf_ref[...] = jnp.zeros_like(sf_ref)
                sb_ref[...] = jnp.zeros_like(sb_ref)

    ri = lax.broadcasted_iota(jnp.int32, (cn, 1), 0).astype(F32)
    for hd in range(RET_HEADS):
        gf = log_g[0:1, hd:hd + 1]
        gb = log_g[1:2, hd:hd + 1]
        qs = slice(hd * RET_DK, (hd + 1) * RET_DK)
        vs = slice(hd * RET_DV, (hd + 1) * RET_DV)
        q = qf_ref[:, qs]
        k = kf_ref[:, qs]
        v = vf_ref[:, vs]
        y = _dot((_dot_nt(q, k.astype(BF16)) * decay_ref[hd]).astype(BF16), v)
        upd_f = _dot_tn((k * jnp.exp(gf * (cn - 1.0 - ri))).astype(BF16), v)
        if stateless:
            new_f = upd_f
        else:
            s_old = sf_ref[hd]
            y = y + jnp.exp(gf * (ri + 1.0)) * _dot(q, s_old.astype(BF16))
            new_f = s_old * jnp.exp(gf * cn) + upd_f
            sf_ref[hd] = new_f
        yf_ref[:, vs] = y.astype(yf_ref.dtype)
        q = qb_ref[:, qs]
        k = kb_ref[:, qs]
        v = vb_ref[:, vs]
        upd_b = _dot_tn((k * jnp.exp(gb * ri)).astype(BF16), v)
        if stateless:
            new_b = upd_b
        else:
            s_old = sb_ref[hd]
            yb_ref[:, vs] = (jnp.exp(gb * (cn - ri)) * _dot(q, s_old.astype(BF16))).astype(yb_ref.dtype)
            new_b = s_old * jnp.exp(gb * cn) + upd_b
            sb_ref[hd] = new_b
        if want_fin:
            if stateless:
                sfin_ref[0, 0, hd] = new_f
                sfin_ref[0, 1, hd] = new_b
            else:
                @pl.when(c == nc - 1)
                def _(new_f=new_f, new_b=new_b, hd=hd):
                    sfin_ref[0, 0, hd] = new_f
                    sfin_ref[0, 1, hd] = new_b


def _retention(q, k, v, decay, s0, batch, cn, want_fin):
    m = q.shape[0]
    nc = m // batch // cn
    has_init = s0 is not None
    stateless = (not has_init) and nc == 1

    def fwd(b, c):
        return b * nc + c

    def bwd(b, c):
        return b * nc + nc - 1 - c

    def stream_specs(chunk):
        return [pl.BlockSpec((cn, RET_QK_WIDTH), lambda b, c: (chunk(b, c), 0)),
                pl.BlockSpec((cn, RET_QK_WIDTH), lambda b, c: (chunk(b, c), 0)),
                pl.BlockSpec((cn, RET_V_WIDTH), lambda b, c: (chunk(b, c), 0))]

    in_specs = [_const_spec((8, LANES))] + stream_specs(fwd)
    args = [decay, q, k, v]
    if not stateless:
        in_specs += stream_specs(bwd)
        args += [q, k, v]
    state_block = (1, 2, RET_HEADS, RET_DK, RET_DV)
    if has_init:
        in_specs.append(pl.BlockSpec(state_block, lambda b, c: (b, 0, 0, 0, 0)))
        args.append(s0)
    out_specs = [pl.BlockSpec((cn, RET_V_WIDTH), lambda b, c: (fwd(b, c), 0))]
    if not stateless:
        out_specs.append(pl.BlockSpec((cn, RET_V_WIDTH), lambda b, c: (bwd(b, c), 0)))
    out_shape = [jax.ShapeDtypeStruct((m, RET_V_WIDTH), BF16)] * len(out_specs)
    if want_fin:
        out_specs.append(pl.BlockSpec(state_block, lambda b, c: (b, 0, 0, 0, 0)))
        out_shape.append(jax.ShapeDtypeStruct((batch,) + state_block[1:], F32))
    return pl.pallas_call(
        functools.partial(_ret_kernel, cn=cn, nc=nc, has_init=has_init, want_fin=want_fin),
        grid=(batch, nc),
        in_specs=in_specs,
        out_specs=out_specs,
        out_shape=out_shape,
        scratch_shapes=[pltpu.VMEM((RET_HEADS, cn, cn), F32)] + [pltpu.VMEM((RET_HEADS, RET_DK, RET_DV), F32)] * 2,
        compiler_params=_params(2),
        name="retention",
    )(*args)


def _post_kernel(*refs, mixer, final):
    refs = list(refs)
    x_ref, g1_ref, nw_ref, sh2_ref, sc2_ref, g2_ref, wout_ref, wg_ref, wu_ref, wd_ref = refs[:10]
    pos = 10
    fn_ref = None
    if final:
        fn_ref = refs[pos]
        pos += 1
    if mixer == "ab":
        att_ref, yf_ref, yb_ref, z_ref, gain_ref, o_ref = refs[pos:]
        y = (yf_ref[...].astype(F32) + yb_ref[...].astype(F32)) * _silu(z_ref[...].astype(F32))
        y = _rms(y) * gain_ref[...]
        mix = _dot(jnp.concatenate([att_ref[...], y.astype(BF16)], axis=1), wout_ref[...])
    else:
        parts = refs[pos:-3]
        gate_ref, gain_ref, o_ref = refs[-3:]
        mix = None
        for hd in range(RET_HEADS):
            vs = slice(hd * RET_DV, (hd + 1) * RET_DV)
            y = parts[0][:, vs].astype(F32)
            for extra in parts[1:]:
                y = y + extra[:, vs].astype(F32)
            y = _rms(y) * gain_ref[:, vs]
            part = _dot((gate_ref[:, vs].astype(F32) * y).astype(BF16), wout_ref[vs, :])
            mix = part if mix is None else mix + part
    x1 = x_ref[...] + g1_ref[0] * mix
    h = (_rms(x1) * nw_ref[...] * (1.0 + sc2_ref[0]) + sh2_ref[0]).astype(BF16)
    act = (_silu(_dot(h, wg_ref[...])) * _dot(h, wu_ref[...])).astype(BF16)
    x2 = x1 + g2_ref[0] * _dot(act, wd_ref[...])
    if final:
        x2 = _rms(x2) * fn_ref[...]
    o_ref[...] = x2


def _post(x, mod, rows_per_mod, p, mixer, mix_inputs, gain, final_norm, tm):
    m = x.shape[0]
    row = lambda w: pl.BlockSpec((tm, w), lambda i: (i, 0))
    mixw = p["w_out"].shape[0]
    (g1,) = _mod_specs(tm, rows_per_mod, (2,))
    sh2, sc2, g2 = _mod_specs(tm, rows_per_mod, (3, 4, 5))
    in_specs = [row(D_MODEL), g1, _const_spec((1, D_MODEL)), sh2, sc2, g2,
                _resident_spec((mixw, D_MODEL)), _resident_spec((D_MODEL, D_FF)), _resident_spec((D_MODEL, D_FF)),
                _resident_spec((D_FF, D_MODEL))]
    args = [x, mod, p["norm_ffn"], mod, mod, mod, p["w_out"], p["w_gate"], p["w_up"], p["w_down"]]
    final = final_norm is not None
    if final:
        in_specs.append(_const_spec((1, D_MODEL)))
        args.append(final_norm)
    in_specs += [row(a.shape[1]) for a in mix_inputs] + [_const_spec(gain.shape)]
    args += list(mix_inputs) + [gain]
    return pl.pallas_call(
        functools.partial(_post_kernel, mixer=mixer, final=final),
        grid=(m // tm,),
        in_specs=in_specs,
        out_specs=row(D_MODEL),
        out_shape=jax.ShapeDtypeStruct((m, D_MODEL), F32),
        compiler_params=_params(1),
        name="post_" + mixer,
    )(*args)


def _axial_angles(n_tokens, dim):
    rows = n_tokens // GRID_W
    row = np.repeat(np.arange(rows), GRID_W).astype(np.float64)
    col = np.tile(np.arange(GRID_W), rows).astype(np.float64)
    n_freq = dim // 4
    inv = ROPE_THETA ** (-np.arange(n_freq, dtype=np.float64) / n_freq)
    return np.concatenate([row[:, None] * inv, col[:, None] * inv], axis=-1)


def _head_mean_matrix(width, head):
    idx = jnp.arange(width) // head
    return jnp.where(idx[:, None] == idx[None, :], 1.0 / head, 0.0).astype(BF16)


def _rows_bcast(v, width):
    return jnp.broadcast_to(v.reshape(-1, 1), (v.size, width))


def _trunk(x, mods, rows_per_mod, p0, p1, final_norm, rope_att, rope_ret, caches, seq):
    m = x.shape[0]
    batch = m // seq
    sample = caches is not None
    tm = 512
    l0 = _l0_in(x, mods[0], rows_per_mod, p0, rope_att, tm, None if sample else seq)
    q, ka, va, z, xbc, dt = l0[:6]
    s0_ssd = None
    s0_ret = None
    kv_cache = None
    if sample:
        cache_k, cache_v, state_ssd, state_ret = caches
        ck = cache_k.astype(BF16).transpose(0, 2, 1, 3)
        cv = cache_v.astype(BF16).transpose(0, 2, 1, 3)
        ones_col = jnp.zeros(cv.shape[:3] + (LANES - ATT_HEAD_DIM,), BF16).at[..., 0].set(1.0)
        kv_cache = (ck, jnp.concatenate([cv, ones_col], axis=-1))
        s0_ssd = _ssd_state_to_pairs(state_ssd)
        s0_ret = state_ret
    att = _attention(q.reshape(batch, seq, ATT_WIDTH), ka, va, kv_cache, 256, 512)
    att = att.reshape(m, ATT_WIDTH)
    ssd_out = _ssd(xbc, dt, p0, s0_ssd, batch, want_fin=not sample)
    x = _post(x, mods[0], rows_per_mod, p0, "ab", [att, ssd_out[0], ssd_out[1], z], p0["ssd_gain"], None, tm)
    q1, k1, v1, g1 = _l1_in(x, mods[1], rows_per_mod, p1, rope_ret, tm)
    ret_out = _retention(q1, k1, v1, p1["decay"], s0_ret, batch, RET_CHUNK, want_fin=not sample)
    y_parts = ret_out if sample else ret_out[:-1]
    y = _post(x, mods[1], rows_per_mod, p1, "c", list(y_parts) + [g1], p1["ret_gain"], final_norm, tm)
    if sample:
        return y, None
    new_k = l0[6].transpose(0, 3, 1, 2)
    new_v = l0[7].transpose(0, 3, 1, 2)
    return y, (new_k, new_v, ssd_out[2], ret_out[-1])


def kernel(x_prompt, x_sample, c, cache_k0, cache_v0, state_ssd0, state_ret1, c_ctx, l0_w_ada, l0_b_ada, l0_norm_mix, l0_norm_ffn, l0_w_in, l0_w_out, l0_q_gain, l0_k_gain, l0_conv_w, l0_conv_b, l0_dt_bias, l0_a_log, l0_d_skip, l0_ssd_gain, l0_w_gate, l0_w_up, l0_w_down, l1_w_ada, l1_b_ada, l1_norm_mix, l1_norm_ffn, l1_w_in, l1_w_out, l1_decay, l1_ret_gain, l1_w_gate, l1_w_up, l1_w_down, final_norm):
    b_ctx, seq_ctx, d = x_prompt.shape
    b_lat, seq_lat, _ = x_sample.shape
    assert d == D_MODEL and l0_w_in.shape == (D_MODEL, L0_IN) and l0_w_gate.shape == (D_MODEL, D_FF)
    row = lambda v: v.reshape(1, -1)

    p0 = dict(
        norm_mix=row(l0_norm_mix), norm_ffn=row(l0_norm_ffn),
        w_in=jnp.pad(l0_w_in, ((0, 0), (0, L0_IN_PAD - L0_IN))).astype(BF16),
        w_out=l0_w_out.astype(BF16), w_gate=l0_w_gate.astype(BF16), w_up=l0_w_up.astype(BF16),
        w_down=l0_w_down.astype(BF16),
        q_gain=row(jnp.tile(l0_q_gain, ATT_HEADS)), k_gain=row(jnp.tile(l0_k_gain, ATT_KV_HEADS)),
        pq=_head_mean_matrix(ATT_WIDTH, ATT_HEAD_DIM), pk=_head_mean_matrix(ATT_KV_WIDTH, ATT_HEAD_DIM),
        conv_w=jnp.pad(l0_conv_w, ((0, HALO - SSD_CONV), (0, 0))), conv_b=row(l0_conv_b),
        dt_bias=_rows_bcast(l0_dt_bias, SSD_CHUNK), a_log=_rows_bcast(l0_a_log, SSD_CHUNK),
        d_skip=row(jnp.repeat(l0_d_skip, SSD_HEAD_DIM)), ssd_gain=row(l0_ssd_gain),
    )
    p1 = dict(
        norm_mix=row(l1_norm_mix), norm_ffn=row(l1_norm_ffn),
        w_in=l1_w_in.astype(BF16), w_out=l1_w_out.astype(BF16), w_gate=l1_w_gate.astype(BF16),
        w_up=l1_w_up.astype(BF16), w_down=l1_w_down.astype(BF16),
        decay=jnp.pad(l1_decay, ((0, 8 - l1_decay.shape[0]), (0, LANES - l1_decay.shape[1]))),
        ret_gain=row(l1_ret_gain),
    )
    fnorm = row(final_norm)

    n_cond = 8
    conds = jnp.concatenate([c_ctx[None, :], c, jnp.zeros((n_cond - 1 - b_lat, d), F32)], axis=0)
    mod0 = _ada(conds, l0_w_ada, l0_b_ada)
    mod1 = _ada(conds, l1_w_ada, l1_b_ada)
    mods_ctx = [mod[0:1].reshape(1, 1, 6 * d) for mod in (mod0, mod1)]
    mods_lat = [mod[1:1 + b_lat].reshape(b_lat, 1, 6 * d) for mod in (mod0, mod1)]

    m_ctx = b_ctx * seq_ctx
    y_prompt, ctx = _trunk(x_prompt.reshape(m_ctx, d), mods_ctx, m_ctx, p0, p1, fnorm, None, None, None, seq_ctx)
    new_k0, new_v0, new_ssd0, new_ret1 = ctx
    ang = _axial_angles(seq_lat, ATT_HEAD_DIM)
    cos, sin = np.cos(ang), np.sin(ang)
    reps = LANES // ATT_HEAD_DIM
    rope_att = (jnp.asarray(np.tile(np.concatenate([cos, cos], axis=1), (1, reps)), F32),
                jnp.asarray(np.tile(np.concatenate([-sin, sin], axis=1), (1, reps)), F32))
    ang = _axial_angles(seq_lat, RET_DK)
    rope_ret = (jnp.asarray(np.cos(ang), F32), jnp.asarray(np.sin(ang), F32))
    caches = (cache_k0, cache_v0, state_ssd0, state_ret1)
    y_sample, _ = _trunk(x_sample.reshape(b_lat * seq_lat, d), mods_lat, seq_lat, p0, p1, fnorm, rope_att, rope_ret,
                         caches, seq_lat)
    return (y_prompt.reshape(b_ctx, seq_ctx, d), y_sample.reshape(b_lat, seq_lat, d),
            new_k0, new_v0, new_ssd0, new_ret1)
```

```python
import functools

import jax
import jax.numpy as jnp
import numpy as np
from jax import lax
from jax.experimental import pallas as pl
from jax.experimental.pallas import tpu as pltpu

F32 = jnp.float32
BF16 = jnp.bfloat16

EPS = 1e-6
ROPE_THETA = 10000.0
GRID_W = 64
D_MODEL = 1024
ATT_HEAD_DIM = 64
ATT_HEADS = 8
ATT_KV_HEADS = 2
ATT_GROUP = ATT_HEADS // ATT_KV_HEADS
ATT_WIDTH = ATT_HEADS * ATT_HEAD_DIM
ATT_KV_WIDTH = ATT_KV_HEADS * ATT_HEAD_DIM
SSD_WIDTH = 512
SSD_HEADS = 8
SSD_HEAD_DIM = 64
SSD_STATE = 64
SSD_GROUPS = 2
SSD_CONV = 5
SSD_XBC = SSD_WIDTH + 2 * SSD_GROUPS * SSD_STATE
L0_IN = ATT_WIDTH + 2 * ATT_KV_WIDTH + SSD_WIDTH + SSD_XBC + 2 * SSD_HEADS
RET_HEADS = 4
RET_DK = 256
RET_DV = 512
RET_QK_WIDTH = RET_HEADS * RET_DK
RET_V_WIDTH = RET_HEADS * RET_DV
D_FF = 2816

LANES = 128
HALO = 8
L0_IN_PAD = -(-L0_IN // LANES) * LANES
ATT_MAX_UNROLL = 8
SSD_CHUNK = 128
SSD_SEQS = 2
RET_CHUNK = 256
VMEM_LIMIT = 56 * 1024 * 1024


def _dot(a, b):
    return jnp.dot(a, b, preferred_element_type=F32)


def _dot_nt(a, b):
    return lax.dot_general(a, b, (((1,), (1,)), ((), ())), preferred_element_type=F32)


def _dot_tn(a, b):
    return lax.dot_general(a, b, (((0,), (0,)), ((), ())), preferred_element_type=F32)


def _silu(x):
    half = 0.5 * x
    return half + half * jnp.tanh(half)


def _softplus(x):
    return jnp.maximum(x, 0.0) + jnp.log1p(jnp.exp(-jnp.abs(x)))


def _rms(x):
    return x * lax.rsqrt(jnp.mean(x * x, axis=-1, keepdims=True) + EPS)


def _split3(x):
    hi = x.astype(BF16)
    r = x - hi.astype(F32)
    mid = r.astype(BF16)
    lo = (r - mid.astype(F32)).astype(BF16)
    return hi, mid, lo


def _const_spec(shape):
    return pl.BlockSpec(shape, lambda *_: (0,) * len(shape))


def _resident_spec(shape):
    return pl.BlockSpec(shape, lambda *_: (0,) * len(shape), pipeline_mode=pl.Buffered(1))


def _params(n_axes, vmem=VMEM_LIMIT):
    return pltpu.CompilerParams(dimension_semantics=("arbitrary",) * n_axes, vmem_limit_bytes=vmem)


def _ada_kernel(c_ref, w_ref, b_ref, o_ref):
    s = _silu(c_ref[...])
    o_ref[...] = _dot(s.astype(BF16), w_ref[...].astype(BF16)) + b_ref[...]


def _ada(conds, w, b):
    n = w.shape[1]
    tn = 1536
    return pl.pallas_call(
        _ada_kernel,
        grid=(n // tn,),
        in_specs=[_const_spec(conds.shape),
                  pl.BlockSpec((D_MODEL, tn), lambda j: (0, j)),
                  pl.BlockSpec((1, tn), lambda j: (0, j))],
        out_specs=pl.BlockSpec((conds.shape[0], tn), lambda j: (0, j)),
        out_shape=jax.ShapeDtypeStruct((conds.shape[0], n), F32),
        compiler_params=_params(1),
        name="ada",
    )(conds, w, b.reshape(1, n))


def _mod_specs(tm, rows_per_mod, which):
    return [pl.BlockSpec((1, 1, D_MODEL), lambda i, j=j: ((i * tm) // rows_per_mod, 0, j)) for j in which]


def _head_rms(x, p_ref, gain):
    x2 = x * x
    hi = x2.astype(BF16)
    lo = (x2 - hi.astype(F32)).astype(BF16)
    ms = _dot(hi, p_ref[...]) + _dot(lo, p_ref[...])
    return x * lax.rsqrt(ms + EPS) * gain


def _rope64(x, cos, sin):
    n = x.shape[1]
    lane = lax.broadcasted_iota(jnp.int32, x.shape, 1)
    first_half = (lane % ATT_HEAD_DIM) < (ATT_HEAD_DIM // 2)
    partner = jnp.where(first_half, pltpu.roll(x, n - ATT_HEAD_DIM // 2, 1), pltpu.roll(x, ATT_HEAD_DIM // 2, 1))
    return x * cos + partner * sin


def _l0_in_kernel(*refs, use_rope, ctx_seq):
    refs = list(refs)
    x_ref, nw_ref, sh_ref, sc_ref, w_ref, qg_ref, kg_ref, pq_ref, pk_ref = refs[:9]
    pos = 9
    if use_rope:
        cos_ref, sin_ref = refs[pos:pos + 2]
        pos += 2
    q_ref, ka_ref, va_ref, z_ref, xbc_ref, dt_ref = refs[pos:pos + 6]
    pos += 6
    h = _rms(x_ref[...]) * nw_ref[...] * (1.0 + sc_ref[0]) + sh_ref[0]
    proj = _dot(h.astype(BF16), w_ref[...])
    o1 = ATT_WIDTH
    o2 = o1 + ATT_KV_WIDTH
    o3 = o2 + ATT_KV_WIDTH
    o4 = o3 + SSD_WIDTH
    o5 = o4 + SSD_XBC
    q = _head_rms(proj[:, :o1], pq_ref, qg_ref[...])
    k = _head_rms(proj[:, o1:o2], pk_ref, kg_ref[...])
    v = proj[:, o2:o3]
    if ctx_seq:
        kt_ref, vt_ref = refs[pos:]
        for s in range(x_ref.shape[0] // ctx_seq):
            kt = k[s * ctx_seq:(s + 1) * ctx_seq].T
            vt = v[s * ctx_seq:(s + 1) * ctx_seq].T
            for kv in range(ATT_KV_HEADS):
                kt_ref[s, kv] = kt[kv * ATT_HEAD_DIM:(kv + 1) * ATT_HEAD_DIM]
                vt_ref[s, kv] = vt[kv * ATT_HEAD_DIM:(kv + 1) * ATT_HEAD_DIM]
    if use_rope:
        cos = cos_ref[...]
        sin = sin_ref[...]
        k = _rope64(k, cos, sin)
        reps = ATT_WIDTH // LANES
        q = _rope64(q, jnp.concatenate([cos] * reps, axis=1), jnp.concatenate([sin] * reps, axis=1))
    q_ref[...] = (q * (ATT_HEAD_DIM ** -0.5)).astype(BF16)
    kb = k.astype(BF16)
    lane = lax.broadcasted_iota(jnp.int32, (1, LANES), 1)
    ones_col = jnp.where(lane == ATT_HEAD_DIM, 1.0, 0.0)
    for kv in range(ATT_KV_HEADS):
        ka_ref[kv] = kb[:, kv * ATT_HEAD_DIM:(kv + 1) * ATT_HEAD_DIM]
        vv = v if kv == 0 else pltpu.roll(v, (LANES - kv * ATT_HEAD_DIM) % LANES, 1)
        va_ref[kv] = jnp.where(lane < ATT_HEAD_DIM, vv, ones_col).astype(BF16)
    z_ref[...] = proj[:, o3:o4].astype(z_ref.dtype)
    xbc_ref[...] = proj[:, o4:o5]
    dt_ref[...] = proj[:, o5:]


def _l0_in(x, mod, rows_per_mod, p, rope, tm, ctx_seq):
    m = x.shape[0]
    use_rope = rope is not None
    row = lambda w: pl.BlockSpec((tm, w), lambda i: (i, 0))
    in_specs = ([row(D_MODEL), _const_spec((1, D_MODEL))] + _mod_specs(tm, rows_per_mod, (0, 1))
                + [_resident_spec((D_MODEL, L0_IN_PAD)), _const_spec((1, ATT_WIDTH)), _const_spec((1, ATT_KV_WIDTH)),
                   _resident_spec((ATT_WIDTH, ATT_WIDTH)), _resident_spec((ATT_KV_WIDTH, ATT_KV_WIDTH))])
    args = [x, p["norm_mix"], mod, mod, p["w_in"], p["q_gain"], p["k_gain"], p["pq"], p["pk"]]
    if use_rope:
        rows = rope[0].shape[0]
        in_specs += [pl.BlockSpec((tm, LANES), lambda i: (i % (rows // tm), 0))] * 2
        args += list(rope)
    head = lambda w: pl.BlockSpec((ATT_KV_HEADS, tm, w), lambda i: (0, i, 0))
    out_specs = [row(ATT_WIDTH), head(ATT_HEAD_DIM), head(LANES), row(SSD_WIDTH), row(SSD_XBC), row(LANES)]
    out_shape = [jax.ShapeDtypeStruct((m, ATT_WIDTH), BF16),
                 jax.ShapeDtypeStruct((ATT_KV_HEADS, m, ATT_HEAD_DIM), BF16),
                 jax.ShapeDtypeStruct((ATT_KV_HEADS, m, LANES), BF16),
                 jax.ShapeDtypeStruct((m, SSD_WIDTH), BF16), jax.ShapeDtypeStruct((m, SSD_XBC), F32),
                 jax.ShapeDtypeStruct((m, LANES), F32)]
    if ctx_seq:
        assert tm % ctx_seq == 0
        cache_block = (tm // ctx_seq, ATT_KV_HEADS, ATT_HEAD_DIM, ctx_seq)
        out_specs += [pl.BlockSpec(cache_block, lambda i: (i, 0, 0, 0))] * 2
        out_shape += [jax.ShapeDtypeStruct((m // ctx_seq,) + cache_block[1:], F32)] * 2
    return pl.pallas_call(
        functools.partial(_l0_in_kernel, use_rope=use_rope, ctx_seq=ctx_seq),
        grid=(m // tm,),
        in_specs=in_specs,
        out_specs=out_specs,
        out_shape=out_shape,
        compiler_params=_params(1),
        name="l0_in",
    )(*args)


def _attn_kernel(*refs, ck, sb, has_cache, kv_blk):
    if has_cache:
        q_ref, k_ref, v_ref, kc_ref, vc_ref, o_ref, s_all_ref, m_all_ref = refs
    else:
        q_ref, k_ref, v_ref, o_ref, s_all_ref, m_all_ref = refs
    gw = q_ref.shape[2] // kv_blk
    tt = sb // ATT_GROUP
    nsub = q_ref.shape[1] // tt
    slots = [(s_all_ref.at[n], m_all_ref.at[n]) for n in range(s_all_ref.shape[0])]

    def chunks(kv):
        out = []
        if has_cache:
            out += [(kc_ref.at[0, kv], vc_ref.at[0, kv], j * ck) for j in range(kc_ref.shape[2] // ck)]
        return out + [(k_ref.at[kv], v_ref.at[kv], j * ck) for j in range(k_ref.shape[1] // ck)]

    def token_rows(i):
        if isinstance(i, int):
            return slice(i * tt, (i + 1) * tt)
        return pl.ds(pl.multiple_of(i * tt, tt), tt)

    def scores(kv, i, slot):
        s_ref, m_ref = slots[slot]
        q4 = q_ref[0, token_rows(i), kv * gw:(kv + 1) * gw]
        q = jnp.concatenate([q4[:, g * ATT_HEAD_DIM:(g + 1) * ATT_HEAD_DIM] for g in range(ATT_GROUP)], axis=0)
        mx = None
        for j, (kr, _, r0) in enumerate(chunks(kv)):
            s = _dot_nt(q, kr[r0:r0 + ck, :])
            s_ref[j] = s
            for t in range(ck // LANES):
                part = s[:, t * LANES:(t + 1) * LANES]
                mx = part if mx is None else jnp.maximum(mx, part)
        m_ref[...] = jnp.broadcast_to(jnp.max(mx, axis=1, keepdims=True), (sb, LANES))

    def values(kv, i, slot):
        s_ref, m_ref = slots[slot]
        m = jnp.concatenate([m_ref[...]] * (ck // LANES), axis=1)
        acc = None
        for j, (_, vr, r0) in enumerate(chunks(kv)):
            p = jnp.exp(s_ref[j] - m).astype(BF16)
            part = _dot(p, vr[r0:r0 + ck, :])
            acc = part if acc is None else acc + part
        out = acc[:, :ATT_HEAD_DIM] / acc[:, ATT_HEAD_DIM:ATT_HEAD_DIM + 1]
        out = jnp.concatenate([out[g * tt:(g + 1) * tt] for g in range(ATT_GROUP)], axis=1)
        o_ref[0, token_rows(i), kv * gw:(kv + 1) * gw] = out.astype(o_ref.dtype)

    if len(slots) > 2:
        items = [(kv, i) for kv in range(kv_blk) for i in range(nsub)]
        for n, item in enumerate(items):
            scores(*item, n)
        for n, item in enumerate(items):
            values(*item, n)
        return

    for kv in range(kv_blk):
        scores(kv, 0, 0)

        def body(h, carry, kv=kv):
            scores(kv, 2 * h + 1, 1)
            values(kv, 2 * h, 0)
            scores(kv, 2 * h + 2, 0)
            values(kv, 2 * h + 1, 1)
            return carry

        lax.fori_loop(0, nsub // 2 - 1, body, 0)
        scores(kv, nsub - 1, 1)
        values(kv, nsub - 2, 0)
        values(kv, nsub - 1, 1)


def _attention(q, k, v, cache, ck, sb):
    b, seq, width = q.shape
    nkv = k.shape[0]
    gw = width // nkv
    nsub = seq * ATT_GROUP // sb
    flat = nkv * nsub <= ATT_MAX_UNROLL
    kv_blk = nkv if flat else 1
    n_slots = kv_blk * nsub if flat else 2
    lk = seq
    in_specs = [pl.BlockSpec((1, seq, gw * kv_blk), lambda i, j: (i, 0, j)),
                pl.BlockSpec((kv_blk, seq, ATT_HEAD_DIM), lambda i, j: (j, i, 0)),
                pl.BlockSpec((kv_blk, seq, LANES), lambda i, j: (j, i, 0))]
    args = [q, k, v]
    if cache is not None:
        past = cache[0].shape[2]
        assert past % ck == 0
        lk += past
        in_specs += [pl.BlockSpec((1, kv_blk, past, ATT_HEAD_DIM), lambda i, j: (i, j, 0, 0)),
                     pl.BlockSpec((1, kv_blk, past, LANES), lambda i, j: (i, j, 0, 0))]
        args += list(cache)
    assert (seq * ATT_GROUP) % (2 * sb) == 0 and seq % ck == 0
    return pl.pallas_call(
        functools.partial(_attn_kernel, ck=ck, sb=sb, has_cache=cache is not None, kv_blk=kv_blk),
        grid=(b, nkv // kv_blk),
        in_specs=in_specs,
        out_specs=pl.BlockSpec((1, seq, gw * kv_blk), lambda i, j: (i, 0, j)),
        out_shape=jax.ShapeDtypeStruct(q.shape, BF16),
        scratch_shapes=[pltpu.VMEM((n_slots, lk // ck, sb, ck), F32), pltpu.VMEM((n_slots, sb, LANES), F32)],
        compiler_params=_params(2),
        name="attention",
    )(*args)


def _ssd_group_rows(pair):
    g = pair // (SSD_HEADS // 2 // SSD_GROUPS)
    return slice(g * SSD_STATE, (g + 1) * SSD_STATE)


def _ssd_conv(cur_ref, prev_ref, next_ref, is_first, is_last, ext_ref, cw_ref, cb_ref, shift_ref):
    cn = SSD_CHUNK
    ext_ref[0:HALO, :] = jnp.where(is_first, 0.0, prev_ref[...])
    ext_ref[HALO:HALO + cn, :] = cur_ref[...]
    ext_ref[HALO + cn:, :] = jnp.where(is_last, 0.0, next_ref[...])
    shifted = _dot(shift_ref[...], ext_ref[...].astype(BF16))
    centre = SSD_CONV // 2
    u = cb_ref[...] + cw_ref[centre:centre + 1, :] * cur_ref[...]
    for n, t in enumerate(t for t in range(SSD_CONV) if t != centre):
        u = u + cw_ref[t:t + 1, :] * shifted[n * cn:(n + 1) * cn]
    return _silu(u)


def _ssd_scalars(dt_ref, dtb_ref, alog_ref, *, forward):
    cn = SSD_CHUNK
    nh = 2 * SSD_HEADS
    dt_t = _softplus(dt_ref[...].T[:nh] + dtb_ref[...])
    a_t = dt_t * (-jnp.exp(alog_ref[...]))
    ii = lax.broadcasted_iota(jnp.int32, (cn, cn), 0)
    jj = lax.broadcasted_iota(jnp.int32, (cn, cn), 1)
    keep = (ii >= jj) if forward else (ii <= jj)
    tri_t = ((ii <= jj) if forward else (ii >= jj)).astype(BF16)
    c3 = _dot(jnp.concatenate(_split3(a_t), axis=0), tri_t)
    cum_t = c3[:nh] + c3[nh:2 * nh] + c3[2 * nh:]
    cols = jnp.concatenate([dt_t, cum_t, jnp.zeros((LANES - 2 * nh, cn), F32)], axis=0).T
    lane = lax.broadcasted_iota(jnp.int32, (1, LANES), 1)
    return dict(cum_t=cum_t, cols=cols, keep=keep, lo=lane < SSD_HEAD_DIM, forward=forward)


def _ssd_mats(act, prep):
    lo = prep["lo"]
    xs = act[:, :SSD_WIDTH]
    bm = act[:, SSD_WIDTH:SSD_WIDTH + LANES]
    cm = act[:, SSD_WIDTH + LANES:]
    bm_b = bm.astype(BF16)
    gmat = []
    bg_b = []
    for g in range(SSD_GROUPS):
        in_group = lo if g == 0 else jnp.logical_not(lo)
        gmat.append(_dot_nt(jnp.where(in_group, cm, 0.0).astype(BF16), bm_b))
        bg_b.append(jnp.where(in_group, bm, 0.0).astype(BF16))
    return dict(prep, xs=xs, cm_b=cm.astype(BF16), gmat=gmat, bg_b=bg_b)


def _ssd_pair(prep, pair, s_ref, y_ref, dsk_ref):
    cn = SSD_CHUNK
    nh = 2 * SSD_HEADS
    forward, cols, cum_t, keep, lo = prep["forward"], prep["cols"], prep["cum_t"], prep["keep"], prep["lo"]

    def col(lane_idx):
        return jnp.broadcast_to(cols[:, lane_idx:lane_idx + 1], (cn, LANES))

    off = 0 if forward else SSD_HEADS
    tot_col = cn - 1 if forward else 0
    g = pair // (SSD_HEADS // 2 // SSD_GROUPS)
    gmat = prep["gmat"][g]
    h0 = off + 2 * pair
    h1 = h0 + 1
    ci0 = col(nh + h0)
    ci1 = col(nh + h1)
    cip = jnp.where(lo, ci0, ci1)
    m0 = (gmat * jnp.exp(jnp.where(keep, ci0 - cum_t[h0:h0 + 1, :], -1e30))).astype(BF16)
    m1 = (gmat * jnp.exp(jnp.where(keep, ci1 - cum_t[h1:h1 + 1, :], -1e30))).astype(BF16)
    xs_p = prep["xs"][:, pair * LANES:(pair + 1) * LANES]
    vp = xs_p * jnp.where(lo, col(h0), col(h1))
    v0 = jnp.where(lo, vp, 0.0).astype(BF16)
    v1 = jnp.where(lo, 0.0, vp).astype(BF16)
    s_old = s_ref[pair]
    y = (_dot(jnp.concatenate([m0, m1], axis=1), jnp.concatenate([v0, v1], axis=0))
         + jnp.exp(cip) * _dot(prep["cm_b"], s_old.astype(BF16)))
    totp = jnp.where(lo, cum_t[h0:h0 + 1, tot_col:tot_col + 1], cum_t[h1:h1 + 1, tot_col:tot_col + 1])
    s_ref[pair] = s_old * jnp.exp(totp) + _dot_tn(prep["bg_b"][g], (vp * jnp.exp(totp - cip)).astype(BF16))
    if forward:
        y = y + dsk_ref[:, pair * LANES:(pair + 1) * LANES] * xs_p
    y_ref[:, pair * LANES:(pair + 1) * LANES] = y.astype(y_ref.dtype)


def _ssd_kernel(*refs, nc, ns, has_init, want_fin):
    refs = list(refs)
    cf_ref, pf_ref, nf_ref, dtf_ref, cb_ref, pb_ref, nb_ref, dtb_ref = refs[:8]
    cw_ref, cbias_ref, dtbias_ref, alog_ref, dsk_ref, shift_ref = refs[8:14]
    pos = 14
    s0_ref = None
    if has_init:
        s0_ref = refs[pos]
        pos += 1
    yf_ref, yb_ref = refs[pos:pos + 2]
    pos += 2
    sfin_ref = None
    if want_fin:
        sfin_ref = refs[pos]
        pos += 1
    ext_ref, act_ref, sf_ref, sb_ref = refs[pos:]
    c = pl.program_id(1)
    c_fwd = c
    c_bwd = nc - 1 - c

    @pl.when(c == 0)
    def _():
        sf_ref[...] = jnp.zeros_like(sf_ref)
        sb_ref[...] = jnp.zeros_like(sb_ref)
        if has_init:
            for s in range(ns):
                for pair in range(SSD_HEADS // 2):
                    rows = _ssd_group_rows(pair)
                    sf_ref[s, pair, rows, :] = s0_ref[s, 0, pair]
                    sb_ref[s, pair, rows, :] = s0_ref[s, 1, pair]

    def scalars():
        out = []
        for s in range(ns):
            out.append(_ssd_scalars(dtf_ref.at[0, s], dtbias_ref, alog_ref, forward=True))
            out.append(_ssd_scalars(dtb_ref.at[0, s], dtbias_ref, alog_ref, forward=False))
        return out

    def streams(preps, acts):
        work = []
        for s in range(ns):
            work.append((_ssd_mats(acts[2 * s], preps[2 * s]), sf_ref.at[s], yf_ref.at[0, s]))
            work.append((_ssd_mats(acts[2 * s + 1], preps[2 * s + 1]), sb_ref.at[s], yb_ref.at[0, s]))
        for pair in range(SSD_HEADS // 2):
            for prep, s_ref, y_ref in work:
                _ssd_pair(prep, pair, s_ref, y_ref, dsk_ref)

    @pl.when(c < nc // 2)
    def _():
        preps = scalars()
        acts = []
        for s in range(ns):
            act_f = _ssd_conv(cf_ref.at[0, s], pf_ref.at[0, s], nf_ref.at[0, s], c_fwd == 0, c_fwd == nc - 1,
                              ext_ref.at[2 * s], cw_ref, cbias_ref, shift_ref)
            act_b = _ssd_conv(cb_ref.at[0, s], pb_ref.at[0, s], nb_ref.at[0, s], c_bwd == 0, c_bwd == nc - 1,
                              ext_ref.at[2 * s + 1], cw_ref, cbias_ref, shift_ref)
            act_ref[s, c_fwd] = act_f
            act_ref[s, c_bwd] = act_b
            acts += [act_f, act_b]
        streams(preps, acts)

    @pl.when(c >= nc // 2)
    def _():
        acts = []
        for s in range(ns):
            acts += [act_ref[s, c_fwd], act_ref[s, c_bwd]]
        streams(scalars(), acts)

    if want_fin:
        @pl.when(c == nc - 1)
        def _():
            for s in range(ns):
                for pair in range(SSD_HEADS // 2):
                    rows = _ssd_group_rows(pair)
                    for d, st_ref in enumerate((sf_ref, sb_ref)):
                        both = st_ref[s, pair, rows, :]
                        sfin_ref[s, d, 2 * pair] = both[:, :SSD_HEAD_DIM]
                        sfin_ref[s, d, 2 * pair + 1] = both[:, SSD_HEAD_DIM:]


def _ssd(xbc, dt, p, s0, batch, want_fin):
    m = xbc.shape[0]
    cn = SSD_CHUNK
    ns = SSD_SEQS
    seq = m // batch
    nc = seq // cn
    assert nc % 2 == 0
    assert batch % ns == 0
    per = cn // HALO
    n_halo = seq // HALO
    has_init = s0 is not None
    npair = SSD_HEADS // 2
    view = lambda a: a.reshape(batch // ns, ns, seq, a.shape[-1])

    def fwd(c):
        return c

    def bwd(c):
        return nc - 1 - c

    def stream_specs(chunk):
        return [pl.BlockSpec((1, ns, cn, SSD_XBC), lambda b, c: (b, 0, chunk(c), 0)),
                pl.BlockSpec((1, ns, HALO, SSD_XBC), lambda b, c: (b, 0, jnp.maximum(chunk(c) * per - 1, 0), 0)),
                pl.BlockSpec((1, ns, HALO, SSD_XBC),
                             lambda b, c: (b, 0, jnp.minimum(chunk(c) * per + per, n_halo - 1), 0)),
                pl.BlockSpec((1, ns, cn, LANES), lambda b, c: (b, 0, chunk(c), 0))]

    in_specs = stream_specs(fwd) + stream_specs(bwd) + [
        _const_spec((HALO, SSD_XBC)), _const_spec((1, SSD_XBC)), _const_spec((2 * SSD_HEADS, cn)),
        _const_spec((2 * SSD_HEADS, cn)), _const_spec((1, SSD_WIDTH)),
        _const_spec(((SSD_CONV - 1) * cn, cn + 2 * HALO))]
    taps = [t for t in range(SSD_CONV) if t != SSD_CONV // 2]
    src = np.concatenate([HALO + np.arange(cn) + (t - SSD_CONV // 2) for t in taps])
    shift = jnp.asarray(src[:, None] == np.arange(cn + 2 * HALO)[None, :], BF16)
    xv, dv = view(xbc), view(dt)
    args = [xv, xv, xv, dv, xv, xv, xv, dv, p["conv_w"], p["conv_b"], p["dt_bias"], p["a_log"], p["d_skip"], shift]
    state_block = (ns, 2, npair, SSD_STATE, LANES)
    if has_init:
        in_specs.append(pl.BlockSpec(state_block, lambda b, c: (b, 0, 0, 0, 0)))
        args.append(s0)
    out_specs = [pl.BlockSpec((1, ns, cn, SSD_WIDTH), lambda b, c: (b, 0, fwd(c), 0)),
                 pl.BlockSpec((1, ns, cn, SSD_WIDTH), lambda b, c: (b, 0, bwd(c), 0))]
    out_shape = [jax.ShapeDtypeStruct((batch // ns, ns, seq, SSD_WIDTH), BF16)] * 2
    if want_fin:
        fin_block = (ns, 2, SSD_HEADS, SSD_STATE, SSD_HEAD_DIM)
        out_specs.append(pl.BlockSpec(fin_block, lambda b, c: (b, 0, 0, 0, 0)))
        out_shape.append(jax.ShapeDtypeStruct((batch,) + fin_block[1:], F32))
    out = pl.pallas_call(
        functools.partial(_ssd_kernel, nc=nc, ns=ns, has_init=has_init, want_fin=want_fin),
        grid=(batch // ns, nc),
        in_specs=in_specs,
        out_specs=out_specs,
        out_shape=out_shape,
        scratch_shapes=[pltpu.VMEM((2 * ns, cn + 2 * HALO, SSD_XBC), F32), pltpu.VMEM((ns, nc, cn, SSD_XBC), F32),
                        pltpu.VMEM((ns, npair, LANES, LANES), F32), pltpu.VMEM((ns, npair, LANES, LANES), F32)],
        compiler_params=_params(2),
        name="ssd",
    )(*args)
    return [out[0].reshape(m, SSD_WIDTH), out[1].reshape(m, SSD_WIDTH)] + list(out[2:])


def _ssd_state_to_pairs(s):
    b = s.shape[0]
    npair = SSD_HEADS // 2
    s = s.reshape(b, 2, npair, 2, SSD_STATE, SSD_HEAD_DIM).transpose(0, 1, 2, 4, 3, 5)
    return s.reshape(b, 2, npair, SSD_STATE, 2 * SSD_HEAD_DIM)


def _l1_in_kernel(*refs, use_rope):
    if use_rope:
        x_ref, nw_ref, sh_ref, sc_ref, w_ref, cos_ref, sin_ref, q_ref, k_ref, v_ref, g_ref = refs
    else:
        x_ref, nw_ref, sh_ref, sc_ref, w_ref, q_ref, k_ref, v_ref, g_ref = refs
    h = (_rms(x_ref[...]) * nw_ref[...] * (1.0 + sc_ref[0]) + sh_ref[0]).astype(BF16)
    half = RET_DK // 2

    def rope(t):
        if not use_rope:
            return t
        cos = cos_ref[...]
        sin = sin_ref[...]
        parts = []
        for hd in range(RET_HEADS):
            x1 = t[:, hd * RET_DK:hd * RET_DK + half]
            x2 = t[:, hd * RET_DK + half:(hd + 1) * RET_DK]
            parts += [x1 * cos - x2 * sin, x2 * cos + x1 * sin]
        return jnp.concatenate(parts, axis=1)

    o1 = RET_QK_WIDTH
    o2 = 2 * RET_QK_WIDTH
    o3 = o2 + RET_V_WIDTH
    q_ref[...] = rope(_dot(h, w_ref[:, :o1])).astype(BF16)
    k_ref[...] = (rope(_dot(h, w_ref[:, o1:o2])) * (RET_DK ** -0.5)).astype(k_ref.dtype)
    v_ref[...] = _dot(h, w_ref[:, o2:o3]).astype(BF16)
    g_ref[...] = _silu(_dot(h, w_ref[:, o3:])).astype(g_ref.dtype)


def _l1_in(x, mod, rows_per_mod, p, rope, tm):
    m = x.shape[0]
    use_rope = rope is not None
    row = lambda w: pl.BlockSpec((tm, w), lambda i: (i, 0))
    n = 2 * RET_QK_WIDTH + 2 * RET_V_WIDTH
    in_specs = ([row(D_MODEL), _const_spec((1, D_MODEL))] + _mod_specs(tm, rows_per_mod, (0, 1))
                + [_resident_spec((D_MODEL, n))])
    args = [x, p["norm_mix"], mod, mod, p["w_in"]]
    if use_rope:
        rows = rope[0].shape[0]
        in_specs += [pl.BlockSpec((tm, LANES), lambda i: (i % (rows // tm), 0))] * 2
        args += list(rope)
    widths = [(RET_QK_WIDTH, BF16), (RET_QK_WIDTH, BF16), (RET_V_WIDTH, BF16), (RET_V_WIDTH, BF16)]
    return pl.pallas_call(
        functools.partial(_l1_in_kernel, use_rope=use_rope),
        grid=(m // tm,),
        in_specs=in_specs,
        out_specs=[row(w) for w, _ in widths],
        out_shape=[jax.ShapeDtypeStruct((m, w), dt) for w, dt in widths],
        compiler_params=_params(1),
        name="l1_in",
    )(*args)


def _ret_kernel(*refs, cn, nc, has_init, want_fin):
    stateless = (not has_init) and nc == 1
    refs = list(refs)
    dec_ref, qf_ref, kf_ref, vf_ref = refs[:4]
    pos = 4
    if stateless:
        qb_ref, kb_ref, vb_ref = qf_ref, kf_ref, vf_ref
    else:
        qb_ref, kb_ref, vb_ref = refs[pos:pos + 3]
        pos += 3
    s0_ref = None
    if has_init:
        s0_ref = refs[pos]
        pos += 1
    yf_ref = refs[pos]
    pos += 1
    yb_ref = None
    if not stateless:
        yb_ref = refs[pos]
        pos += 1
    sfin_ref = None
    if want_fin:
        sfin_ref = refs[pos]
        pos += 1
    decay_ref, sf_ref, sb_ref = refs[pos:]
    c = pl.program_id(1)
    log_g = -jnp.exp(dec_ref[...])

    @pl.when((pl.program_id(0) == 0) & (c == 0))
    def _():
        ii = lax.broadcasted_iota(jnp.int32, (cn, cn), 0)
        jj = lax.broadcasted_iota(jnp.int32, (cn, cn), 1)
        dist = (ii - jj).astype(F32)
        for hd in range(RET_HEADS):
            gf = log_g[0:1, hd:hd + 1]
            gb = log_g[1:2, hd:hd + 1]
            decay_ref[hd] = (jnp.where(dist >= 0, jnp.exp(gf * jnp.maximum(dist, 0.0)), 0.0)
                             + jnp.where(dist <= 0, jnp.exp(gb * jnp.maximum(-dist, 0.0)), 0.0))

    if not stateless:
        @pl.when(c == 0)
        def _():
            if has_init:
                sf_ref[...] = s0_ref[0, 0]
                sb_ref[...] = s0_ref[0, 1]
            else:
                sf_ref[...] = jnp.zeros_like(sf_ref)
                sb_ref[...] = jnp.zeros_like(sb_ref)

    ri = lax.broadcasted_iota(jnp.int32, (cn, 1), 0).astype(F32)
    for hd in range(RET_HEADS):
        gf = log_g[0:1, hd:hd + 1]
        gb = log_g[1:2, hd:hd + 1]
        qs = slice(hd * RET_DK, (hd + 1) * RET_DK)
        vs = slice(hd * RET_DV, (hd + 1) * RET_DV)
        q = qf_ref[:, qs]
        k = kf_ref[:, qs]
        v = vf_ref[:, vs]
        y = _dot((_dot_nt(q, k.astype(BF16)) * decay_ref[hd]).astype(BF16), v)
        upd_f = _dot_tn((k * jnp.exp(gf * (cn - 1.0 - ri))).astype(BF16), v)
        if stateless:
            new_f = upd_f
        else:
            s_old = sf_ref[hd]
            y = y + jnp.exp(gf * (ri + 1.0)) * _dot(q, s_old.astype(BF16))
            new_f = s_old * jnp.exp(gf * cn) + upd_f
            sf_ref[hd] = new_f
        yf_ref[:, vs] = y.astype(yf_ref.dtype)
        q = qb_ref[:, qs]
        k = kb_ref[:, qs]
        v = vb_ref[:, vs]
        upd_b = _dot_tn((k * jnp.exp(gb * ri)).astype(BF16), v)
        if stateless:
            new_b = upd_b
        else:
            s_old = sb_ref[hd]
            yb_ref[:, vs] = (jnp.exp(gb * (cn - ri)) * _dot(q, s_old.astype(BF16))).astype(yb_ref.dtype)
            new_b = s_old * jnp.exp(gb * cn) + upd_b
            sb_ref[hd] = new_b
        if want_fin:
            if stateless:
                sfin_ref[0, 0, hd] = new_f
                sfin_ref[0, 1, hd] = new_b
            else:
                @pl.when(c == nc - 1)
                def _(new_f=new_f, new_b=new_b, hd=hd):
                    sfin_ref[0, 0, hd] = new_f
                    sfin_ref[0, 1, hd] = new_b


def _retention(q, k, v, decay, s0, batch, cn, want_fin):
    m = q.shape[0]
    nc = m // batch // cn
    has_init = s0 is not None
    stateless = (not has_init) and nc == 1

    def fwd(b, c):
        return b * nc + c

    def bwd(b, c):
        return b * nc + nc - 1 - c

    def stream_specs(chunk):
        return [pl.BlockSpec((cn, RET_QK_WIDTH), lambda b, c: (chunk(b, c), 0)),
                pl.BlockSpec((cn, RET_QK_WIDTH), lambda b, c: (chunk(b, c), 0)),
                pl.BlockSpec((cn, RET_V_WIDTH), lambda b, c: (chunk(b, c), 0))]

    in_specs = [_const_spec((8, LANES))] + stream_specs(fwd)
    args = [decay, q, k, v]
    if not stateless:
        in_specs += stream_specs(bwd)
        args += [q, k, v]
    state_block = (1, 2, RET_HEADS, RET_DK, RET_DV)
    if has_init:
        in_specs.append(pl.BlockSpec(state_block, lambda b, c: (b, 0, 0, 0, 0)))
        args.append(s0)
    out_specs = [pl.BlockSpec((cn, RET_V_WIDTH), lambda b, c: (fwd(b, c), 0))]
    if not stateless:
        out_specs.append(pl.BlockSpec((cn, RET_V_WIDTH), lambda b, c: (bwd(b, c), 0)))
    out_shape = [jax.ShapeDtypeStruct((m, RET_V_WIDTH), BF16)] * len(out_specs)
    if want_fin:
        out_specs.append(pl.BlockSpec(state_block, lambda b, c: (b, 0, 0, 0, 0)))
        out_shape.append(jax.ShapeDtypeStruct((batch,) + state_block[1:], F32))
    return pl.pallas_call(
        functools.partial(_ret_kernel, cn=cn, nc=nc, has_init=has_init, want_fin=want_fin),
        grid=(batch, nc),
        in_specs=in_specs,
        out_specs=out_specs,
        out_shape=out_shape,
        scratch_shapes=[pltpu.VMEM((RET_HEADS, cn, cn), F32)] + [pltpu.VMEM((RET_HEADS, RET_DK, RET_DV), F32)] * 2,
        compiler_params=_params(2),
        name="retention",
    )(*args)


def _post_kernel(*refs, mixer, final):
    refs = list(refs)
    x_ref, g1_ref, nw_ref, sh2_ref, sc2_ref, g2_ref, wout_ref, wg_ref, wu_ref, wd_ref = refs[:10]
    pos = 10
    fn_ref = None
    if final:
        fn_ref = refs[pos]
        pos += 1
    if mixer == "ab":
        att_ref, yf_ref, yb_ref, z_ref, gain_ref, o_ref = refs[pos:]
        y = (yf_ref[...].astype(F32) + yb_ref[...].astype(F32)) * _silu(z_ref[...].astype(F32))
        y = _rms(y) * gain_ref[...]
        mix = _dot(jnp.concatenate([att_ref[...], y.astype(BF16)], axis=1), wout_ref[...])
    else:
        parts = refs[pos:-3]
        gate_ref, gain_ref, o_ref = refs[-3:]
        mix = None
        for hd in range(RET_HEADS):
            vs = slice(hd * RET_DV, (hd + 1) * RET_DV)
            y = parts[0][:, vs].astype(F32)
            for extra in parts[1:]:
                y = y + extra[:, vs].astype(F32)
            y = _rms(y) * gain_ref[:, vs]
            part = _dot((gate_ref[:, vs].astype(F32) * y).astype(BF16), wout_ref[vs, :])
            mix = part if mix is None else mix + part
    x1 = x_ref[...] + g1_ref[0] * mix
    h = (_rms(x1) * nw_ref[...] * (1.0 + sc2_ref[0]) + sh2_ref[0]).astype(BF16)
    act = (_silu(_dot(h, wg_ref[...])) * _dot(h, wu_ref[...])).astype(BF16)
    x2 = x1 + g2_ref[0] * _dot(act, wd_ref[...])
    if final:
        x2 = _rms(x2) * fn_ref[...]
    o_ref[...] = x2


def _post(x, mod, rows_per_mod, p, mixer, mix_inputs, gain, final_norm, tm):
    m = x.shape[0]
    row = lambda w: pl.BlockSpec((tm, w), lambda i: (i, 0))
    mixw = p["w_out"].shape[0]
    (g1,) = _mod_specs(tm, rows_per_mod, (2,))
    sh2, sc2, g2 = _mod_specs(tm, rows_per_mod, (3, 4, 5))
    in_specs = [row(D_MODEL), g1, _const_spec((1, D_MODEL)), sh2, sc2, g2,
                _resident_spec((mixw, D_MODEL)), _resident_spec((D_MODEL, D_FF)), _resident_spec((D_MODEL, D_FF)),
                _resident_spec((D_FF, D_MODEL))]
    args = [x, mod, p["norm_ffn"], mod, mod, mod, p["w_out"], p["w_gate"], p["w_up"], p["w_down"]]
    final = final_norm is not None
    if final:
        in_specs.append(_const_spec((1, D_MODEL)))
        args.append(final_norm)
    in_specs += [row(a.shape[1]) for a in mix_inputs] + [_const_spec(gain.shape)]
    args += list(mix_inputs) + [gain]
    return pl.pallas_call(
        functools.partial(_post_kernel, mixer=mixer, final=final),
        grid=(m // tm,),
        in_specs=in_specs,
        out_specs=row(D_MODEL),
        out_shape=jax.ShapeDtypeStruct((m, D_MODEL), F32),
        compiler_params=_params(1),
        name="post_" + mixer,
    )(*args)


def _axial_angles(n_tokens, dim):
    rows = n_tokens // GRID_W
    row = np.repeat(np.arange(rows), GRID_W).astype(np.float64)
    col = np.tile(np.arange(GRID_W), rows).astype(np.float64)
    n_freq = dim // 4
    inv = ROPE_THETA ** (-np.arange(n_freq, dtype=np.float64) / n_freq)
    return np.concatenate([row[:, None] * inv, col[:, None] * inv], axis=-1)


def _head_mean_matrix(width, head):
    idx = jnp.arange(width) // head
    return jnp.where(idx[:, None] == idx[None, :], 1.0 / head, 0.0).astype(BF16)


def _rows_bcast(v, width):
    return jnp.broadcast_to(v.reshape(-1, 1), (v.size, width))


def _trunk(x, mods, rows_per_mod, p0, p1, final_norm, rope_att, rope_ret, caches, seq):
    m = x.shape[0]
    batch = m // seq
    sample = caches is not None
    tm = 512
    l0 = _l0_in(x, mods[0], rows_per_mod, p0, rope_att, tm, None if sample else seq)
    q, ka, va, z, xbc, dt = l0[:6]
    s0_ssd = None
    s0_ret = None
    kv_cache = None
    if sample:
        cache_k, cache_v, state_ssd, state_ret = caches
        ck = cache_k.astype(BF16).transpose(0, 2, 1, 3)
        cv = cache_v.astype(BF16).transpose(0, 2, 1, 3)
        ones_col = jnp.zeros(cv.shape[:3] + (LANES - ATT_HEAD_DIM,), BF16).at[..., 0].set(1.0)
        kv_cache = (ck, jnp.concatenate([cv, ones_col], axis=-1))
        s0_ssd = _ssd_state_to_pairs(state_ssd)
        s0_ret = state_ret
    att = _attention(q.reshape(batch, seq, ATT_WIDTH), ka, va, kv_cache, 256, 512)
    att = att.reshape(m, ATT_WIDTH)
    ssd_out = _ssd(xbc, dt, p0, s0_ssd, batch, want_fin=not sample)
    x = _post(x, mods[0], rows_per_mod, p0, "ab", [att, ssd_out[0], ssd_out[1], z], p0["ssd_gain"], None, tm)
    q1, k1, v1, g1 = _l1_in(x, mods[1], rows_per_mod, p1, rope_ret, tm)
    ret_out = _retention(q1, k1, v1, p1["decay"], s0_ret, batch, RET_CHUNK, want_fin=not sample)
    y_parts = ret_out if sample else ret_out[:-1]
    y = _post(x, mods[1], rows_per_mod, p1, "c", list(y_parts) + [g1], p1["ret_gain"], final_norm, tm)
    if sample:
        return y, None
    new_k = l0[6].transpose(0, 3, 1, 2)
    new_v = l0[7].transpose(0, 3, 1, 2)
    return y, (new_k, new_v, ssd_out[2], ret_out[-1])


def kernel(x_prompt, x_sample, c, cache_k0, cache_v0, state_ssd0, state_ret1, c_ctx, l0_w_ada, l0_b_ada, l0_norm_mix, l0_norm_ffn, l0_w_in, l0_w_out, l0_q_gain, l0_k_gain, l0_conv_w, l0_conv_b, l0_dt_bias, l0_a_log, l0_d_skip, l0_ssd_gain, l0_w_gate, l0_w_up, l0_w_down, l1_w_ada, l1_b_ada, l1_norm_mix, l1_norm_ffn, l1_w_in, l1_w_out, l1_decay, l1_ret_gain, l1_w_gate, l1_w_up, l1_w_down, final_norm):
    b_ctx, seq_ctx, d = x_prompt.shape
    b_lat, seq_lat, _ = x_sample.shape
    assert d == D_MODEL and l0_w_in.shape == (D_MODEL, L0_IN) and l0_w_gate.shape == (D_MODEL, D_FF)
    row = lambda v: v.reshape(1, -1)

    p0 = dict(
        norm_mix=row(l0_norm_mix), norm_ffn=row(l0_norm_ffn),
        w_in=jnp.pad(l0_w_in, ((0, 0), (0, L0_IN_PAD - L0_IN))).astype(BF16),
        w_out=l0_w_out.astype(BF16), w_gate=l0_w_gate.astype(BF16), w_up=l0_w_up.astype(BF16),
        w_down=l0_w_down.astype(BF16),
        q_gain=row(jnp.tile(l0_q_gain, ATT_HEADS)), k_gain=row(jnp.tile(l0_k_gain, ATT_KV_HEADS)),
        pq=_head_mean_matrix(ATT_WIDTH, ATT_HEAD_DIM), pk=_head_mean_matrix(ATT_KV_WIDTH, ATT_HEAD_DIM),
        conv_w=jnp.pad(l0_conv_w, ((0, HALO - SSD_CONV), (0, 0))), conv_b=row(l0_conv_b),
        dt_bias=_rows_bcast(l0_dt_bias, SSD_CHUNK), a_log=_rows_bcast(l0_a_log, SSD_CHUNK),
        d_skip=row(jnp.repeat(l0_d_skip, SSD_HEAD_DIM)), ssd_gain=row(l0_ssd_gain),
    )
    p1 = dict(
        norm_mix=row(l1_norm_mix), norm_ffn=row(l1_norm_ffn),
        w_in=l1_w_in.astype(BF16), w_out=l1_w_out.astype(BF16), w_gate=l1_w_gate.astype(BF16),
        w_up=l1_w_up.astype(BF16), w_down=l1_w_down.astype(BF16),
        decay=jnp.pad(l1_decay, ((0, 8 - l1_decay.shape[0]), (0, LANES - l1_decay.shape[1]))),
        ret_gain=row(l1_ret_gain),
    )
    fnorm = row(final_norm)

    n_cond = 8
    conds = jnp.concatenate([c_ctx[None, :], c, jnp.zeros((n_cond - 1 - b_lat, d), F32)], axis=0)
    mod0 = _ada(conds, l0_w_ada, l0_b_ada)
    mod1 = _ada(conds, l1_w_ada, l1_b_ada)
    mods_ctx = [mod[0:1].reshape(1, 1, 6 * d) for mod in (mod0, mod1)]
    mods_lat = [mod[1:1 + b_lat].reshape(b_lat, 1, 6 * d) for mod in (mod0, mod1)]

    m_ctx = b_ctx * seq_ctx
    y_prompt, ctx = _trunk(x_prompt.reshape(m_ctx, d), mods_ctx, m_ctx, p0, p1, fnorm, None, None, None, seq_ctx)
    new_k0, new_v0, new_ssd0, new_ret1 = ctx
    ang = _axial_angles(seq_lat, ATT_HEAD_DIM)
    cos, sin = np.cos(ang), np.sin(ang)
    reps = LANES // ATT_HEAD_DIM
    rope_att = (jnp.asarray(np.tile(np.concatenate([cos, cos], axis=1), (1, reps)), F32),
                jnp.asarray(np.tile(np.concatenate([-sin, sin], axis=1), (1, reps)), F32))
    ang = _axial_angles(seq_lat, RET_DK)
    rope_ret = (jnp.asarray(np.cos(ang), F32), jnp.asarray(np.sin(ang), F32))
    caches = (cache_k0, cache_v0, state_ssd0, state_ret1)
    y_sample, _ = _trunk(x_sample.reshape(b_lat * seq_lat, d), mods_lat, seq_lat, p0, p1, fnorm, rope_att, rope_ret,
                         caches, seq_lat)
    return (y_prompt.reshape(b_ctx, seq_ctx, d), y_sample.reshape(b_lat, seq_lat, d),
            new_k0, new_v0, new_ssd0, new_ret1)
```

```python
import functools

import jax
import jax.numpy as jnp
import numpy as np
from jax import lax
from jax.experimental import pallas as pl
from jax.experimental.pallas import tpu as pltpu

F32 = jnp.float32
BF16 = jnp.bfloat16

EPS = 1e-6
ROPE_THETA = 10000.0
GRID_W = 64
D_MODEL = 1024
ATT_HEAD_DIM = 64
ATT_HEADS = 8
ATT_KV_HEADS = 2
ATT_GROUP = ATT_HEADS // ATT_KV_HEADS
ATT_WIDTH = ATT_HEADS * ATT_HEAD_DIM
ATT_KV_WIDTH = ATT_KV_HEADS * ATT_HEAD_DIM
SSD_WIDTH = 512
SSD_HEADS = 8
SSD_HEAD_DIM = 64
SSD_STATE = 64
SSD_GROUPS = 2
SSD_CONV = 5
SSD_XBC = SSD_WIDTH + 2 * SSD_GROUPS * SSD_STATE
L0_IN = ATT_WIDTH + 2 * ATT_KV_WIDTH + SSD_WIDTH + SSD_XBC + 2 * SSD_HEADS
RET_HEADS = 4
RET_DK = 256
RET_DV = 512
RET_QK_WIDTH = RET_HEADS * RET_DK
RET_V_WIDTH = RET_HEADS * RET_DV
D_FF = 2816

LANES = 128
HALO = 8
L0_IN_PAD = -(-L0_IN // LANES) * LANES
MXU_WIDTH = 256
ROW_TILE = 512
ADA_COLS = 1536
ATT_KEY_CHUNK = MXU_WIDTH
ATT_SUB_ROWS = 512
ATT_MAX_UNROLL = 8
ATT_LOOP_UNROLL = 4
SSD_CHUNK = 128
SSD_SEQS = 2
RET_CHUNK = 256
VMEM_LIMIT = 56 * 1024 * 1024


def _dot(a, b):
    return jnp.dot(a, b, preferred_element_type=F32)


def _dot_nt(a, b):
    return lax.dot_general(a, b, (((1,), (1,)), ((), ())), preferred_element_type=F32)


def _dot_tn(a, b):
    return lax.dot_general(a, b, (((0,), (0,)), ((), ())), preferred_element_type=F32)


def _silu(x):
    half = 0.5 * x
    return half + half * jnp.tanh(half)


def _softplus(x):
    return jnp.maximum(x, 0.0) + jnp.log1p(jnp.exp(-jnp.abs(x)))


def _rms(x):
    return x * lax.rsqrt(jnp.mean(x * x, axis=-1, keepdims=True) + EPS)


def _split3(x):
    hi = x.astype(BF16)
    r = x - hi.astype(F32)
    mid = r.astype(BF16)
    lo = (r - mid.astype(F32)).astype(BF16)
    return hi, mid, lo


def _const_spec(shape):
    return pl.BlockSpec(shape, lambda *_: (0,) * len(shape))


def _resident_spec(shape):
    return pl.BlockSpec(shape, lambda *_: (0,) * len(shape), pipeline_mode=pl.Buffered(1))


def _params(n_axes, vmem=VMEM_LIMIT):
    return pltpu.CompilerParams(dimension_semantics=("arbitrary",) * n_axes, vmem_limit_bytes=vmem)


def _ada_kernel(c_ref, w_ref, b_ref, o_ref):
    s = _silu(c_ref[...])
    o_ref[...] = _dot(s.astype(BF16), w_ref[...].astype(BF16)) + b_ref[...]


def _ada(conds, w, b):
    n = w.shape[1]
    tn = ADA_COLS
    return pl.pallas_call(
        _ada_kernel,
        grid=(n // tn,),
        in_specs=[_const_spec(conds.shape),
                  pl.BlockSpec((D_MODEL, tn), lambda j: (0, j)),
                  pl.BlockSpec((1, tn), lambda j: (0, j))],
        out_specs=pl.BlockSpec((conds.shape[0], tn), lambda j: (0, j)),
        out_shape=jax.ShapeDtypeStruct((conds.shape[0], n), F32),
        compiler_params=_params(1),
        name="ada",
    )(conds, w, b.reshape(1, n))


def _mod_specs(tm, rows_per_mod, which):
    return [pl.BlockSpec((1, 1, D_MODEL), lambda i, j=j: ((i * tm) // rows_per_mod, 0, j)) for j in which]


def _head_rms(x, p_ref, gain):
    x2 = x * x
    hi = x2.astype(BF16)
    lo = (x2 - hi.astype(F32)).astype(BF16)
    ms = _dot(hi, p_ref[...]) + _dot(lo, p_ref[...])
    return x * lax.rsqrt(ms + EPS) * gain


def _rope64(x, cos, sin):
    n = x.shape[1]
    lane = lax.broadcasted_iota(jnp.int32, x.shape, 1)
    first_half = (lane % ATT_HEAD_DIM) < (ATT_HEAD_DIM // 2)
    partner = jnp.where(first_half, pltpu.roll(x, n - ATT_HEAD_DIM // 2, 1), pltpu.roll(x, ATT_HEAD_DIM // 2, 1))
    return x * cos + partner * sin


def _l0_in_kernel(*refs, use_rope, ctx_seq):
    refs = list(refs)
    x_ref, nw_ref, sh_ref, sc_ref, w_ref, qg_ref, kg_ref, pq_ref, pk_ref = refs[:9]
    pos = 9
    if use_rope:
        cos_ref, sin_ref = refs[pos:pos + 2]
        pos += 2
    q_ref, ka_ref, va_ref, z_ref, xbc_ref, dt_ref = refs[pos:pos + 6]
    pos += 6
    h = _rms(x_ref[...]) * nw_ref[...] * (1.0 + sc_ref[0]) + sh_ref[0]
    proj = _dot(h.astype(BF16), w_ref[...])
    o1 = ATT_WIDTH
    o2 = o1 + ATT_KV_WIDTH
    o3 = o2 + ATT_KV_WIDTH
    o4 = o3 + SSD_WIDTH
    o5 = o4 + SSD_XBC
    q = _head_rms(proj[:, :o1], pq_ref, qg_ref[...])
    k = _head_rms(proj[:, o1:o2], pk_ref, kg_ref[...])
    v = proj[:, o2:o3]
    if ctx_seq:
        kt_ref, vt_ref = refs[pos:]
        for s in range(x_ref.shape[0] // ctx_seq):
            kt = k[s * ctx_seq:(s + 1) * ctx_seq].T
            vt = v[s * ctx_seq:(s + 1) * ctx_seq].T
            for kv in range(ATT_KV_HEADS):
                kt_ref[s, kv] = kt[kv * ATT_HEAD_DIM:(kv + 1) * ATT_HEAD_DIM]
                vt_ref[s, kv] = vt[kv * ATT_HEAD_DIM:(kv + 1) * ATT_HEAD_DIM]
    if use_rope:
        cos = cos_ref[...]
        sin = sin_ref[...]
        k = _rope64(k, cos, sin)
        reps = ATT_WIDTH // LANES
        q = _rope64(q, jnp.concatenate([cos] * reps, axis=1), jnp.concatenate([sin] * reps, axis=1))
    q_ref[...] = (q * (ATT_HEAD_DIM ** -0.5)).astype(BF16)
    kb = k.astype(BF16)
    lane = lax.broadcasted_iota(jnp.int32, (1, LANES), 1)
    ones_col = jnp.where(lane == ATT_HEAD_DIM, 1.0, 0.0)
    for kv in range(ATT_KV_HEADS):
        ka_ref[kv] = kb[:, kv * ATT_HEAD_DIM:(kv + 1) * ATT_HEAD_DIM]
        vv = v if kv == 0 else pltpu.roll(v, (LANES - kv * ATT_HEAD_DIM) % LANES, 1)
        va_ref[kv] = jnp.where(lane < ATT_HEAD_DIM, vv, ones_col).astype(BF16)
    z_ref[...] = proj[:, o3:o4].astype(z_ref.dtype)
    xbc_ref[...] = proj[:, o4:o5]
    dt_ref[...] = proj[:, o5:]


def _l0_in(x, mod, rows_per_mod, p, rope, tm, ctx_seq):
    m = x.shape[0]
    use_rope = rope is not None
    row = lambda w: pl.BlockSpec((tm, w), lambda i: (i, 0))
    in_specs = ([row(D_MODEL), _const_spec((1, D_MODEL))] + _mod_specs(tm, rows_per_mod, (0, 1))
                + [_resident_spec((D_MODEL, L0_IN_PAD)), _const_spec((1, ATT_WIDTH)), _const_spec((1, ATT_KV_WIDTH)),
                   _resident_spec((ATT_WIDTH, ATT_WIDTH)), _resident_spec((ATT_KV_WIDTH, ATT_KV_WIDTH))])
    args = [x, p["norm_mix"], mod, mod, p["w_in"], p["q_gain"], p["k_gain"], p["pq"], p["pk"]]
    if use_rope:
        rows = rope[0].shape[0]
        in_specs += [pl.BlockSpec((tm, LANES), lambda i: (i % (rows // tm), 0))] * 2
        args += list(rope)
    head = lambda w: pl.BlockSpec((ATT_KV_HEADS, tm, w), lambda i: (0, i, 0))
    out_specs = [row(ATT_WIDTH), head(ATT_HEAD_DIM), head(LANES), row(SSD_WIDTH), row(SSD_XBC), row(LANES)]
    out_shape = [jax.ShapeDtypeStruct((m, ATT_WIDTH), BF16),
                 jax.ShapeDtypeStruct((ATT_KV_HEADS, m, ATT_HEAD_DIM), BF16),
                 jax.ShapeDtypeStruct((ATT_KV_HEADS, m, LANES), BF16),
                 jax.ShapeDtypeStruct((m, SSD_WIDTH), BF16), jax.ShapeDtypeStruct((m, SSD_XBC), F32),
                 jax.ShapeDtypeStruct((m, LANES), F32)]
    if ctx_seq:
        assert tm % ctx_seq == 0
        cache_block = (tm // ctx_seq, ATT_KV_HEADS, ATT_HEAD_DIM, ctx_seq)
        out_specs += [pl.BlockSpec(cache_block, lambda i: (i, 0, 0, 0))] * 2
        out_shape += [jax.ShapeDtypeStruct((m // ctx_seq,) + cache_block[1:], F32)] * 2
    return pl.pallas_call(
        functools.partial(_l0_in_kernel, use_rope=use_rope, ctx_seq=ctx_seq),
        grid=(m // tm,),
        in_specs=in_specs,
        out_specs=out_specs,
        out_shape=out_shape,
        compiler_params=_params(1),
        name="l0_in",
    )(*args)


def _attn_kernel(*refs, ck, sb, has_cache, kv_blk):
    if has_cache:
        q_ref, k_ref, v_ref, kc_ref, vc_ref, o_ref, s_all_ref, m_all_ref = refs
    else:
        q_ref, k_ref, v_ref, o_ref, s_all_ref, m_all_ref = refs
    gw = q_ref.shape[2] // kv_blk
    tt = sb // ATT_GROUP
    nsub = q_ref.shape[1] // tt
    slots = [(s_all_ref.at[n], m_all_ref.at[n]) for n in range(s_all_ref.shape[0])]

    def chunks(kv):
        out = []
        if has_cache:
            out += [(kc_ref.at[0, kv], vc_ref.at[0, kv], j * ck) for j in range(kc_ref.shape[2] // ck)]
        return out + [(k_ref.at[kv], v_ref.at[kv], j * ck) for j in range(k_ref.shape[1] // ck)]

    def token_rows(i):
        if isinstance(i, int):
            return slice(i * tt, (i + 1) * tt)
        return pl.ds(pl.multiple_of(i * tt, tt), tt)

    def scores(kv, i, slot):
        s_ref, m_ref = slots[slot]
        q4 = q_ref[0, token_rows(i), kv * gw:(kv + 1) * gw]
        q = jnp.concatenate([q4[:, g * ATT_HEAD_DIM:(g + 1) * ATT_HEAD_DIM] for g in range(ATT_GROUP)], axis=0)
        mx = None
        for j, (kr, _, r0) in enumerate(chunks(kv)):
            s = _dot_nt(q, kr[r0:r0 + ck, :])
            s_ref[j] = s
            for t in range(ck // LANES):
                part = s[:, t * LANES:(t + 1) * LANES]
                mx = part if mx is None else jnp.maximum(mx, part)
        m_ref[...] = jnp.broadcast_to(jnp.max(mx, axis=1, keepdims=True), (sb, LANES))

    def values(kv, i, slot):
        s_ref, m_ref = slots[slot]
        m = jnp.concatenate([m_ref[...]] * (ck // LANES), axis=1)
        acc = None
        for j, (_, vr, r0) in enumerate(chunks(kv)):
            p = jnp.exp(s_ref[j] - m).astype(BF16)
            part = _dot(p, vr[r0:r0 + ck, :])
            acc = part if acc is None else acc + part
        out = acc[:, :ATT_HEAD_DIM] / acc[:, ATT_HEAD_DIM:ATT_HEAD_DIM + 1]
        out = jnp.concatenate([out[g * tt:(g + 1) * tt] for g in range(ATT_GROUP)], axis=1)
        o_ref[0, token_rows(i), kv * gw:(kv + 1) * gw] = out.astype(o_ref.dtype)

    if len(slots) > 2:
        items = [(kv, i) for kv in range(kv_blk) for i in range(nsub)]
        for n, item in enumerate(items):
            scores(*item, n)
        for n, item in enumerate(items):
            values(*item, n)
        return

    unroll = ATT_LOOP_UNROLL
    for kv in range(kv_blk):
        scores(kv, 0, 0)

        def body(h, carry, kv=kv):
            for u in range(unroll):
                scores(kv, unroll * h + u + 1, (u + 1) % 2)
                values(kv, unroll * h + u, u % 2)
            return carry

        trips = (nsub - 1) // unroll
        lax.fori_loop(0, trips, body, 0)
        for n in range(trips * unroll, nsub - 1):
            scores(kv, n + 1, (n + 1) % 2)
            values(kv, n, n % 2)
        values(kv, nsub - 1, (nsub - 1) % 2)


def _attention(q, k, v, cache, ck, sb):
    b, seq, width = q.shape
    nkv = k.shape[0]
    gw = width // nkv
    nsub = seq * ATT_GROUP // sb
    flat = nkv * nsub <= ATT_MAX_UNROLL
    kv_blk = nkv if flat else 1
    n_slots = kv_blk * nsub if flat else 2
    lk = seq
    in_specs = [pl.BlockSpec((1, seq, gw * kv_blk), lambda i, j: (i, 0, j)),
                pl.BlockSpec((kv_blk, seq, ATT_HEAD_DIM), lambda i, j: (j, i, 0)),
                pl.BlockSpec((kv_blk, seq, LANES), lambda i, j: (j, i, 0))]
    args = [q, k, v]
    if cache is not None:
        past = cache[0].shape[2]
        assert past % ck == 0
        lk += past
        in_specs += [pl.BlockSpec((1, kv_blk, past, ATT_HEAD_DIM), lambda i, j: (i, j, 0, 0)),
                     pl.BlockSpec((1, kv_blk, past, LANES), lambda i, j: (i, j, 0, 0))]
        args += list(cache)
    assert (seq * ATT_GROUP) % (2 * sb) == 0 and seq % ck == 0
    return pl.pallas_call(
        functools.partial(_attn_kernel, ck=ck, sb=sb, has_cache=cache is not None, kv_blk=kv_blk),
        grid=(b, nkv // kv_blk),
        in_specs=in_specs,
        out_specs=pl.BlockSpec((1, seq, gw * kv_blk), lambda i, j: (i, 0, j)),
        out_shape=jax.ShapeDtypeStruct(q.shape, BF16),
        scratch_shapes=[pltpu.VMEM((n_slots, lk // ck, sb, ck), F32), pltpu.VMEM((n_slots, sb, LANES), F32)],
        compiler_params=_params(2),
        name="attention",
    )(*args)


def _ssd_group_rows(pair):
    g = pair // (SSD_HEADS // 2 // SSD_GROUPS)
    return slice(g * SSD_STATE, (g + 1) * SSD_STATE)


def _ssd_conv(cur_ref, prev_ref, next_ref, is_first, is_last, ext_ref, cw_ref, cb_ref, shift_ref):
    cn = SSD_CHUNK
    ext_ref[0:HALO, :] = jnp.where(is_first, 0.0, prev_ref[...])
    ext_ref[HALO:HALO + cn, :] = cur_ref[...]
    ext_ref[HALO + cn:, :] = jnp.where(is_last, 0.0, next_ref[...])
    shifted = _dot(shift_ref[...], ext_ref[...].astype(BF16))
    centre = SSD_CONV // 2
    u = cb_ref[...] + cw_ref[centre:centre + 1, :] * cur_ref[...]
    for n, t in enumerate(t for t in range(SSD_CONV) if t != centre):
        u = u + cw_ref[t:t + 1, :] * shifted[n * cn:(n + 1) * cn]
    return _silu(u)


def _ssd_scalars(dt_ref, dtb_ref, alog_ref, *, forward):
    cn = SSD_CHUNK
    nh = 2 * SSD_HEADS
    dt_t = _softplus(dt_ref[...].T[:nh] + dtb_ref[...])
    a_t = dt_t * (-jnp.exp(alog_ref[...]))
    ii = lax.broadcasted_iota(jnp.int32, (cn, cn), 0)
    jj = lax.broadcasted_iota(jnp.int32, (cn, cn), 1)
    keep = (ii >= jj) if forward else (ii <= jj)
    tri_t = ((ii <= jj) if forward else (ii >= jj)).astype(BF16)
    c3 = _dot(jnp.concatenate(_split3(a_t), axis=0), tri_t)
    cum_t = c3[:nh] + c3[nh:2 * nh] + c3[2 * nh:]
    cols = jnp.concatenate([dt_t, cum_t, jnp.zeros((LANES - 2 * nh, cn), F32)], axis=0).T
    lane = lax.broadcasted_iota(jnp.int32, (1, LANES), 1)
    return dict(cum_t=cum_t, cols=cols, keep=keep, lo=lane < SSD_HEAD_DIM, forward=forward)


def _ssd_mats(act, prep):
    lo = prep["lo"]
    xs = act[:, :SSD_WIDTH]
    bm = act[:, SSD_WIDTH:SSD_WIDTH + LANES]
    cm = act[:, SSD_WIDTH + LANES:]
    bm_b = bm.astype(BF16)
    gmat = []
    bg_b = []
    for g in range(SSD_GROUPS):
        in_group = lo if g == 0 else jnp.logical_not(lo)
        gmat.append(_dot_nt(jnp.where(in_group, cm, 0.0).astype(BF16), bm_b))
        bg_b.append(jnp.where(in_group, bm, 0.0).astype(BF16))
    return dict(prep, xs=xs, cm_b=cm.astype(BF16), gmat=gmat, bg_b=bg_b)


def _ssd_pair(prep, pair, s_ref, y_ref, dsk_ref):
    cn = SSD_CHUNK
    nh = 2 * SSD_HEADS
    forward, cols, cum_t, keep, lo = prep["forward"], prep["cols"], prep["cum_t"], prep["keep"], prep["lo"]

    def col(lane_idx):
        return jnp.broadcast_to(cols[:, lane_idx:lane_idx + 1], (cn, LANES))

    off = 0 if forward else SSD_HEADS
    tot_col = cn - 1 if forward else 0
    g = pair // (SSD_HEADS // 2 // SSD_GROUPS)
    gmat = prep["gmat"][g]
    h0 = off + 2 * pair
    h1 = h0 + 1
    ci0 = col(nh + h0)
    ci1 = col(nh + h1)
    cip = jnp.where(lo, ci0, ci1)
    m0 = (gmat * jnp.exp(jnp.where(keep, ci0 - cum_t[h0:h0 + 1, :], -1e30))).astype(BF16)
    m1 = (gmat * jnp.exp(jnp.where(keep, ci1 - cum_t[h1:h1 + 1, :], -1e30))).astype(BF16)
    xs_p = prep["xs"][:, pair * LANES:(pair + 1) * LANES]
    vp = xs_p * jnp.where(lo, col(h0), col(h1))
    v0 = jnp.where(lo, vp, 0.0).astype(BF16)
    v1 = jnp.where(lo, 0.0, vp).astype(BF16)
    s_old = s_ref[pair]
    y = (_dot(jnp.concatenate([m0, m1], axis=1), jnp.concatenate([v0, v1], axis=0))
         + jnp.exp(cip) * _dot(prep["cm_b"], s_old.astype(BF16)))
    totp = jnp.where(lo, cum_t[h0:h0 + 1, tot_col:tot_col + 1], cum_t[h1:h1 + 1, tot_col:tot_col + 1])
    s_ref[pair] = s_old * jnp.exp(totp) + _dot_tn(prep["bg_b"][g], (vp * jnp.exp(totp - cip)).astype(BF16))
    if forward:
        y = y + dsk_ref[:, pair * LANES:(pair + 1) * LANES] * xs_p
    y_ref[:, pair * LANES:(pair + 1) * LANES] = y.astype(y_ref.dtype)


def _ssd_kernel(*refs, nc, ns, has_init, want_fin):
    refs = list(refs)
    cf_ref, pf_ref, nf_ref, dtf_ref, cb_ref, pb_ref, nb_ref, dtb_ref = refs[:8]
    cw_ref, cbias_ref, dtbias_ref, alog_ref, dsk_ref, shift_ref = refs[8:14]
    pos = 14
    s0_ref = None
    if has_init:
        s0_ref = refs[pos]
        pos += 1
    yf_ref, yb_ref = refs[pos:pos + 2]
    pos += 2
    sfin_ref = None
    if want_fin:
        sfin_ref = refs[pos]
        pos += 1
    ext_ref, act_ref, sf_ref, sb_ref = refs[pos:]
    c = pl.program_id(1)
    c_fwd = c
    c_bwd = nc - 1 - c

    @pl.when(c == 0)
    def _():
        sf_ref[...] = jnp.zeros_like(sf_ref)
        sb_ref[...] = jnp.zeros_like(sb_ref)
        if has_init:
            for s in range(ns):
                for pair in range(SSD_HEADS // 2):
                    rows = _ssd_group_rows(pair)
                    sf_ref[s, pair, rows, :] = s0_ref[s, 0, pair]
                    sb_ref[s, pair, rows, :] = s0_ref[s, 1, pair]

    def scalars():
        out = []
        for s in range(ns):
            out.append(_ssd_scalars(dtf_ref.at[0, s], dtbias_ref, alog_ref, forward=True))
            out.append(_ssd_scalars(dtb_ref.at[0, s], dtbias_ref, alog_ref, forward=False))
        return out

    def streams(preps, acts):
        work = []
        for s in range(ns):
            work.append((_ssd_mats(acts[2 * s], preps[2 * s]), sf_ref.at[s], yf_ref.at[0, s]))
            work.append((_ssd_mats(acts[2 * s + 1], preps[2 * s + 1]), sb_ref.at[s], yb_ref.at[0, s]))
        for pair in range(SSD_HEADS // 2):
            for prep, s_ref, y_ref in work:
                _ssd_pair(prep, pair, s_ref, y_ref, dsk_ref)

    @pl.when(c < nc // 2)
    def _():
        preps = scalars()
        acts = []
        for s in range(ns):
            act_f = _ssd_conv(cf_ref.at[0, s], pf_ref.at[0, s], nf_ref.at[0, s], c_fwd == 0, c_fwd == nc - 1,
                              ext_ref.at[2 * s], cw_ref, cbias_ref, shift_ref)
            act_b = _ssd_conv(cb_ref.at[0, s], pb_ref.at[0, s], nb_ref.at[0, s], c_bwd == 0, c_bwd == nc - 1,
                              ext_ref.at[2 * s + 1], cw_ref, cbias_ref, shift_ref)
            act_ref[s, c_fwd] = act_f
            act_ref[s, c_bwd] = act_b
            acts += [act_f, act_b]
        streams(preps, acts)

    @pl.when(c >= nc // 2)
    def _():
        acts = []
        for s in range(ns):
            acts += [act_ref[s, c_fwd], act_ref[s, c_bwd]]
        streams(scalars(), acts)

    if want_fin:
        @pl.when(c == nc - 1)
        def _():
            for s in range(ns):
                for pair in range(SSD_HEADS // 2):
                    rows = _ssd_group_rows(pair)
                    for d, st_ref in enumerate((sf_ref, sb_ref)):
                        both = st_ref[s, pair, rows, :]
                        sfin_ref[s, d, 2 * pair] = both[:, :SSD_HEAD_DIM]
                        sfin_ref[s, d, 2 * pair + 1] = both[:, SSD_HEAD_DIM:]


def _ssd(xbc, dt, p, s0, batch, want_fin):
    m = xbc.shape[0]
    cn = SSD_CHUNK
    ns = SSD_SEQS
    seq = m // batch
    nc = seq // cn
    assert nc % 2 == 0
    assert batch % ns == 0
    per = cn // HALO
    n_halo = seq // HALO
    has_init = s0 is not None
    npair = SSD_HEADS // 2
    view = lambda a: a.reshape(batch // ns, ns, seq, a.shape[-1])

    def fwd(c):
        return c

    def bwd(c):
        return nc - 1 - c

    def stream_specs(chunk):
        return [pl.BlockSpec((1, ns, cn, SSD_XBC), lambda b, c: (b, 0, chunk(c), 0)),
                pl.BlockSpec((1, ns, HALO, SSD_XBC), lambda b, c: (b, 0, jnp.maximum(chunk(c) * per - 1, 0), 0)),
                pl.BlockSpec((1, ns, HALO, SSD_XBC),
                             lambda b, c: (b, 0, jnp.minimum(chunk(c) * per + per, n_halo - 1), 0)),
                pl.BlockSpec((1, ns, cn, LANES), lambda b, c: (b, 0, chunk(c), 0))]

    in_specs = stream_specs(fwd) + stream_specs(bwd) + [
        _const_spec((HALO, SSD_XBC)), _const_spec((1, SSD_XBC)), _const_spec((2 * SSD_HEADS, cn)),
        _const_spec((2 * SSD_HEADS, cn)), _const_spec((1, SSD_WIDTH)),
        _const_spec(((SSD_CONV - 1) * cn, cn + 2 * HALO))]
    taps = [t for t in range(SSD_CONV) if t != SSD_CONV // 2]
    src = np.concatenate([HALO + np.arange(cn) + (t - SSD_CONV // 2) for t in taps])
    shift = jnp.asarray(src[:, None] == np.arange(cn + 2 * HALO)[None, :], BF16)
    xv, dv = view(xbc), view(dt)
    args = [xv, xv, xv, dv, xv, xv, xv, dv, p["conv_w"], p["conv_b"], p["dt_bias"], p["a_log"], p["d_skip"], shift]
    state_block = (ns, 2, npair, SSD_STATE, LANES)
    if has_init:
        in_specs.append(pl.BlockSpec(state_block, lambda b, c: (b, 0, 0, 0, 0)))
        args.append(s0)
    out_specs = [pl.BlockSpec((1, ns, cn, SSD_WIDTH), lambda b, c: (b, 0, fwd(c), 0)),
                 pl.BlockSpec((1, ns, cn, SSD_WIDTH), lambda b, c: (b, 0, bwd(c), 0))]
    out_shape = [jax.ShapeDtypeStruct((batch // ns, ns, seq, SSD_WIDTH), BF16)] * 2
    if want_fin:
        fin_block = (ns, 2, SSD_HEADS, SSD_STATE, SSD_HEAD_DIM)
        out_specs.append(pl.BlockSpec(fin_block, lambda b, c: (b, 0, 0, 0, 0)))
        out_shape.append(jax.ShapeDtypeStruct((batch,) + fin_block[1:], F32))
    out = pl.pallas_call(
        functools.partial(_ssd_kernel, nc=nc, ns=ns, has_init=has_init, want_fin=want_fin),
        grid=(batch // ns, nc),
        in_specs=in_specs,
        out_specs=out_specs,
        out_shape=out_shape,
        scratch_shapes=[pltpu.VMEM((2 * ns, cn + 2 * HALO, SSD_XBC), F32), pltpu.VMEM((ns, nc, cn, SSD_XBC), F32),
                        pltpu.VMEM((ns, npair, LANES, LANES), F32), pltpu.VMEM((ns, npair, LANES, LANES), F32)],
        compiler_params=_params(2),
        name="ssd",
    )(*args)
    return [out[0].reshape(m, SSD_WIDTH), out[1].reshape(m, SSD_WIDTH)] + list(out[2:])


def _ssd_state_to_pairs(s):
    b = s.shape[0]
    npair = SSD_HEADS // 2
    s = s.reshape(b, 2, npair, 2, SSD_STATE, SSD_HEAD_DIM).transpose(0, 1, 2, 4, 3, 5)
    return s.reshape(b, 2, npair, SSD_STATE, 2 * SSD_HEAD_DIM)


def _l1_in_kernel(*refs, use_rope):
    if use_rope:
        x_ref, nw_ref, sh_ref, sc_ref, w_ref, cos_ref, sin_ref, q_ref, k_ref, v_ref, g_ref = refs
    else:
        x_ref, nw_ref, sh_ref, sc_ref, w_ref, q_ref, k_ref, v_ref, g_ref = refs
    h = (_rms(x_ref[...]) * nw_ref[...] * (1.0 + sc_ref[0]) + sh_ref[0]).astype(BF16)
    half = RET_DK // 2

    def rope(t):
        if not use_rope:
            return t
        cos = cos_ref[...]
        sin = sin_ref[...]
        parts = []
        for hd in range(RET_HEADS):
            x1 = t[:, hd * RET_DK:hd * RET_DK + half]
            x2 = t[:, hd * RET_DK + half:(hd + 1) * RET_DK]
            parts += [x1 * cos - x2 * sin, x2 * cos + x1 * sin]
        return jnp.concatenate(parts, axis=1)

    o1 = RET_QK_WIDTH
    o2 = 2 * RET_QK_WIDTH
    o3 = o2 + RET_V_WIDTH
    q_ref[...] = rope(_dot(h, w_ref[:, :o1])).astype(BF16)
    k_ref[...] = (rope(_dot(h, w_ref[:, o1:o2])) * (RET_DK ** -0.5)).astype(k_ref.dtype)
    v_ref[...] = _dot(h, w_ref[:, o2:o3]).astype(BF16)
    g_ref[...] = _silu(_dot(h, w_ref[:, o3:])).astype(g_ref.dtype)


def _l1_in(x, mod, rows_per_mod, p, rope, tm):
    m = x.shape[0]
    use_rope = rope is not None
    row = lambda w: pl.BlockSpec((tm, w), lambda i: (i, 0))
    n = 2 * RET_QK_WIDTH + 2 * RET_V_WIDTH
    in_specs = ([row(D_MODEL), _const_spec((1, D_MODEL))] + _mod_specs(tm, rows_per_mod, (0, 1))
                + [_resident_spec((D_MODEL, n))])
    args = [x, p["norm_mix"], mod, mod, p["w_in"]]
    if use_rope:
        rows = rope[0].shape[0]
        in_specs += [pl.BlockSpec((tm, LANES), lambda i: (i % (rows // tm), 0))] * 2
        args += list(rope)
    widths = [(RET_QK_WIDTH, BF16), (RET_QK_WIDTH, BF16), (RET_V_WIDTH, BF16), (RET_V_WIDTH, BF16)]
    return pl.pallas_call(
        functools.partial(_l1_in_kernel, use_rope=use_rope),
        grid=(m // tm,),
        in_specs=in_specs,
        out_specs=[row(w) for w, _ in widths],
        out_shape=[jax.ShapeDtypeStruct((m, w), dt) for w, dt in widths],
        compiler_params=_params(1),
        name="l1_in",
    )(*args)


def _ret_kernel(*refs, cn, nc, has_init, want_fin):
    stateless = (not has_init) and nc == 1
    refs = list(refs)
    dec_ref, qf_ref, kf_ref, vf_ref = refs[:4]
    pos = 4
    if stateless:
        qb_ref, kb_ref, vb_ref = qf_ref, kf_ref, vf_ref
    else:
        qb_ref, kb_ref, vb_ref = refs[pos:pos + 3]
        pos += 3
    s0_ref = None
    if has_init:
        s0_ref = refs[pos]
        pos += 1
    yf_ref = refs[pos]
    pos += 1
    yb_ref = None
    if not stateless:
        yb_ref = refs[pos]
        pos += 1
    sfin_ref = None
    if want_fin:
        sfin_ref = refs[pos]
        pos += 1
    decay_ref, sf_ref, sb_ref = refs[pos:]
    c = pl.program_id(1)
    log_g = -jnp.exp(dec_ref[...])

    @pl.when((pl.program_id(0) == 0) & (c == 0))
    def _():
        ii = lax.broadcasted_iota(jnp.int32, (cn, cn), 0)
        jj = lax.broadcasted_iota(jnp.int32, (cn, cn), 1)
        dist = (ii - jj).astype(F32)
        for hd in range(RET_HEADS):
            gf = log_g[0:1, hd:hd + 1]
            gb = log_g[1:2, hd:hd + 1]
            decay_ref[hd] = (jnp.where(dist >= 0, jnp.exp(gf * jnp.maximum(dist, 0.0)), 0.0)
                             + jnp.where(dist <= 0, jnp.exp(gb * jnp.maximum(-dist, 0.0)), 0.0))

    if not stateless:
        @pl.when(c == 0)
        def _():
            if has_init:
                sf_ref[...] = s0_ref[0, 0]
                sb_ref[...] = s0_ref[0, 1]
            else:
                sf_ref[...] = jnp.zeros_like(sf_ref)
                sb_ref[...] = jnp.zeros_like(sb_ref)

    ri = lax.broadcasted_iota(jnp.int32, (cn, 1), 0).astype(F32)
    for hd in range(RET_HEADS):
        gf = log_g[0:1, hd:hd + 1]
        gb = log_g[1:2, hd:hd + 1]
        qs = slice(hd * RET_DK, (hd + 1) * RET_DK)
        vs = slice(hd * RET_DV, (hd + 1) * RET_DV)
        q = qf_ref[:, qs]
        k = kf_ref[:, qs]
        v = vf_ref[:, vs]
        y = _dot((_dot_nt(q, k.astype(BF16)) * decay_ref[hd]).astype(BF16), v)
        upd_f = _dot_tn((k * jnp.exp(gf * (cn - 1.0 - ri))).astype(BF16), v)
        if stateless:
            new_f = upd_f
        else:
            s_old = sf_ref[hd]
            y = y + jnp.exp(gf * (ri + 1.0)) * _dot(q, s_old.astype(BF16))
            new_f = s_old * jnp.exp(gf * cn) + upd_f
            sf_ref[hd] = new_f
        yf_ref[:, vs] = y.astype(yf_ref.dtype)
        q = qb_ref[:, qs]
        k = kb_ref[:, qs]
        v = vb_ref[:, vs]
        upd_b = _dot_tn((k * jnp.exp(gb * ri)).astype(BF16), v)
        if stateless:
            new_b = upd_b
        else:
            s_old = sb_ref[hd]
            yb_ref[:, vs] = (jnp.exp(gb * (cn - ri)) * _dot(q, s_old.astype(BF16))).astype(yb_ref.dtype)
            new_b = s_old * jnp.exp(gb * cn) + upd_b
            sb_ref[hd] = new_b
        if want_fin:
            if stateless:
                sfin_ref[0, 0, hd] = new_f
                sfin_ref[0, 1, hd] = new_b
            else:
                @pl.when(c == nc - 1)
                def _(new_f=new_f, new_b=new_b, hd=hd):
                    sfin_ref[0, 0, hd] = new_f
                    sfin_ref[0, 1, hd] = new_b


def _retention(q, k, v, decay, s0, batch, cn, want_fin):
    m = q.shape[0]
    nc = m // batch // cn
    has_init = s0 is not None
    stateless = (not has_init) and nc == 1

    def fwd(b, c):
        return b * nc + c

    def bwd(b, c):
        return b * nc + nc - 1 - c

    def stream_specs(chunk):
        return [pl.BlockSpec((cn, RET_QK_WIDTH), lambda b, c: (chunk(b, c), 0)),
                pl.BlockSpec((cn, RET_QK_WIDTH), lambda b, c: (chunk(b, c), 0)),
                pl.BlockSpec((cn, RET_V_WIDTH), lambda b, c: (chunk(b, c), 0))]

    in_specs = [_const_spec((8, LANES))] + stream_specs(fwd)
    args = [decay, q, k, v]
    if not stateless:
        in_specs += stream_specs(bwd)
        args += [q, k, v]
    state_block = (1, 2, RET_HEADS, RET_DK, RET_DV)
    if has_init:
        in_specs.append(pl.BlockSpec(state_block, lambda b, c: (b, 0, 0, 0, 0)))
        args.append(s0)
    out_specs = [pl.BlockSpec((cn, RET_V_WIDTH), lambda b, c: (fwd(b, c), 0))]
    if not stateless:
        out_specs.append(pl.BlockSpec((cn, RET_V_WIDTH), lambda b, c: (bwd(b, c), 0)))
    out_shape = [jax.ShapeDtypeStruct((m, RET_V_WIDTH), BF16)] * len(out_specs)
    if want_fin:
        out_specs.append(pl.BlockSpec(state_block, lambda b, c: (b, 0, 0, 0, 0)))
        out_shape.append(jax.ShapeDtypeStruct((batch,) + state_block[1:], F32))
    return pl.pallas_call(
        functools.partial(_ret_kernel, cn=cn, nc=nc, has_init=has_init, want_fin=want_fin),
        grid=(batch, nc),
        in_specs=in_specs,
        out_specs=out_specs,
        out_shape=out_shape,
        scratch_shapes=[pltpu.VMEM((RET_HEADS, cn, cn), F32)] + [pltpu.VMEM((RET_HEADS, RET_DK, RET_DV), F32)] * 2,
        compiler_params=_params(2),
        name="retention",
    )(*args)


def _post_kernel(*refs, mixer, final):
    refs = list(refs)
    x_ref, g1_ref, nw_ref, sh2_ref, sc2_ref, g2_ref, wout_ref, wg_ref, wu_ref, wd_ref = refs[:10]
    pos = 10
    fn_ref = None
    if final:
        fn_ref = refs[pos]
        pos += 1
    if mixer == "ab":
        att_ref, yf_ref, yb_ref, z_ref, gain_ref, o_ref = refs[pos:]
        y = (yf_ref[...].astype(F32) + yb_ref[...].astype(F32)) * _silu(z_ref[...].astype(F32))
        y = _rms(y) * gain_ref[...]
        mix = _dot(jnp.concatenate([att_ref[...], y.astype(BF16)], axis=1), wout_ref[...])
    else:
        parts = refs[pos:-3]
        gate_ref, gain_ref, o_ref = refs[-3:]
        mix = None
        for hd in range(RET_HEADS):
            vs = slice(hd * RET_DV, (hd + 1) * RET_DV)
            y = parts[0][:, vs].astype(F32)
            for extra in parts[1:]:
                y = y + extra[:, vs].astype(F32)
            y = _rms(y) * gain_ref[:, vs]
            part = _dot((gate_ref[:, vs].astype(F32) * y).astype(BF16), wout_ref[vs, :])
            mix = part if mix is None else mix + part
    x1 = x_ref[...] + g1_ref[0] * mix
    h = (_rms(x1) * nw_ref[...] * (1.0 + sc2_ref[0]) + sh2_ref[0]).astype(BF16)
    act = (_silu(_dot(h, wg_ref[...])) * _dot(h, wu_ref[...])).astype(BF16)
    x2 = x1 + g2_ref[0] * _dot(act, wd_ref[...])
    if final:
        x2 = _rms(x2) * fn_ref[...]
    o_ref[...] = x2


def _post(x, mod, rows_per_mod, p, mixer, mix_inputs, gain, final_norm, tm):
    m = x.shape[0]
    row = lambda w: pl.BlockSpec((tm, w), lambda i: (i, 0))
    mixw = p["w_out"].shape[0]
    (g1,) = _mod_specs(tm, rows_per_mod, (2,))
    sh2, sc2, g2 = _mod_specs(tm, rows_per_mod, (3, 4, 5))
    in_specs = [row(D_MODEL), g1, _const_spec((1, D_MODEL)), sh2, sc2, g2,
                _resident_spec((mixw, D_MODEL)), _resident_spec((D_MODEL, D_FF)), _resident_spec((D_MODEL, D_FF)),
                _resident_spec((D_FF, D_MODEL))]
    args = [x, mod, p["norm_ffn"], mod, mod, mod, p["w_out"], p["w_gate"], p["w_up"], p["w_down"]]
    final = final_norm is not None
    if final:
        in_specs.append(_const_spec((1, D_MODEL)))
        args.append(final_norm)
    in_specs += [row(a.shape[1]) for a in mix_inputs] + [_const_spec(gain.shape)]
    args += list(mix_inputs) + [gain]
    return pl.pallas_call(
        functools.partial(_post_kernel, mixer=mixer, final=final),
        grid=(m // tm,),
        in_specs=in_specs,
        out_specs=row(D_MODEL),
        out_shape=jax.ShapeDtypeStruct((m, D_MODEL), F32),
        compiler_params=_params(1),
        name="post_" + mixer,
    )(*args)


def _axial_angles(n_tokens, dim):
    rows = n_tokens // GRID_W
    row = np.repeat(np.arange(rows), GRID_W).astype(np.float64)
    col = np.tile(np.arange(GRID_W), rows).astype(np.float64)
    n_freq = dim // 4
    inv = ROPE_THETA ** (-np.arange(n_freq, dtype=np.float64) / n_freq)
    return np.concatenate([row[:, None] * inv, col[:, None] * inv], axis=-1)


def _head_mean_matrix(width, head):
    idx = jnp.arange(width) // head
    return jnp.where(idx[:, None] == idx[None, :], 1.0 / head, 0.0).astype(BF16)


def _rows_bcast(v, width):
    return jnp.broadcast_to(v.reshape(-1, 1), (v.size, width))


def _trunk(x, mods, rows_per_mod, p0, p1, final_norm, rope_att, rope_ret, caches, seq):
    m = x.shape[0]
    batch = m // seq
    sample = caches is not None
    tm = ROW_TILE
    l0 = _l0_in(x, mods[0], rows_per_mod, p0, rope_att, tm, None if sample else seq)
    q, ka, va, z, xbc, dt = l0[:6]
    s0_ssd = None
    s0_ret = None
    kv_cache = None
    if sample:
        cache_k, cache_v, state_ssd, state_ret = caches
        ck = cache_k.astype(BF16).transpose(0, 2, 1, 3)
        cv = cache_v.astype(BF16).transpose(0, 2, 1, 3)
        ones_col = jnp.zeros(cv.shape[:3] + (LANES - ATT_HEAD_DIM,), BF16).at[..., 0].set(1.0)
        kv_cache = (ck, jnp.concatenate([cv, ones_col], axis=-1))
        s0_ssd = _ssd_state_to_pairs(state_ssd)
        s0_ret = state_ret
    att = _attention(q.reshape(batch, seq, ATT_WIDTH), ka, va, kv_cache, ATT_KEY_CHUNK, ATT_SUB_ROWS)
    att = att.reshape(m, ATT_WIDTH)
    ssd_out = _ssd(xbc, dt, p0, s0_ssd, batch, want_fin=not sample)
    x = _post(x, mods[0], rows_per_mod, p0, "ab", [att, ssd_out[0], ssd_out[1], z], p0["ssd_gain"], None, tm)
    q1, k1, v1, g1 = _l1_in(x, mods[1], rows_per_mod, p1, rope_ret, tm)
    ret_out = _retention(q1, k1, v1, p1["decay"], s0_ret, batch, RET_CHUNK, want_fin=not sample)
    y_parts = ret_out if sample else ret_out[:-1]
    y = _post(x, mods[1], rows_per_mod, p1, "c", list(y_parts) + [g1], p1["ret_gain"], final_norm, tm)
    if sample:
        return y, None
    new_k = l0[6].transpose(0, 3, 1, 2)
    new_v = l0[7].transpose(0, 3, 1, 2)
    return y, (new_k, new_v, ssd_out[2], ret_out[-1])


def kernel(x_prompt, x_sample, c, cache_k0, cache_v0, state_ssd0, state_ret1, c_ctx, l0_w_ada, l0_b_ada, l0_norm_mix, l0_norm_ffn, l0_w_in, l0_w_out, l0_q_gain, l0_k_gain, l0_conv_w, l0_conv_b, l0_dt_bias, l0_a_log, l0_d_skip, l0_ssd_gain, l0_w_gate, l0_w_up, l0_w_down, l1_w_ada, l1_b_ada, l1_norm_mix, l1_norm_ffn, l1_w_in, l1_w_out, l1_decay, l1_ret_gain, l1_w_gate, l1_w_up, l1_w_down, final_norm):
    b_ctx, seq_ctx, d = x_prompt.shape
    b_lat, seq_lat, _ = x_sample.shape
    assert d == D_MODEL and l0_w_in.shape == (D_MODEL, L0_IN) and l0_w_gate.shape == (D_MODEL, D_FF)
    row = lambda v: v.reshape(1, -1)

    p0 = dict(
        norm_mix=row(l0_norm_mix), norm_ffn=row(l0_norm_ffn),
        w_in=jnp.pad(l0_w_in, ((0, 0), (0, L0_IN_PAD - L0_IN))).astype(BF16),
        w_out=l0_w_out.astype(BF16), w_gate=l0_w_gate.astype(BF16), w_up=l0_w_up.astype(BF16),
        w_down=l0_w_down.astype(BF16),
        q_gain=row(jnp.tile(l0_q_gain, ATT_HEADS)), k_gain=row(jnp.tile(l0_k_gain, ATT_KV_HEADS)),
        pq=_head_mean_matrix(ATT_WIDTH, ATT_HEAD_DIM), pk=_head_mean_matrix(ATT_KV_WIDTH, ATT_HEAD_DIM),
        conv_w=jnp.pad(l0_conv_w, ((0, HALO - SSD_CONV), (0, 0))), conv_b=row(l0_conv_b),
        dt_bias=_rows_bcast(l0_dt_bias, SSD_CHUNK), a_log=_rows_bcast(l0_a_log, SSD_CHUNK),
        d_skip=row(jnp.repeat(l0_d_skip, SSD_HEAD_DIM)), ssd_gain=row(l0_ssd_gain),
    )
    p1 = dict(
        norm_mix=row(l1_norm_mix), norm_ffn=row(l1_norm_ffn),
        w_in=l1_w_in.astype(BF16), w_out=l1_w_out.astype(BF16), w_gate=l1_w_gate.astype(BF16),
        w_up=l1_w_up.astype(BF16), w_down=l1_w_down.astype(BF16),
        decay=jnp.pad(l1_decay, ((0, 8 - l1_decay.shape[0]), (0, LANES - l1_decay.shape[1]))),
        ret_gain=row(l1_ret_gain),
    )
    fnorm = row(final_norm)

    n_cond = 8
    conds = jnp.concatenate([c_ctx[None, :], c, jnp.zeros((n_cond - 1 - b_lat, d), F32)], axis=0)
    mod0 = _ada(conds, l0_w_ada, l0_b_ada)
    mod1 = _ada(conds, l1_w_ada, l1_b_ada)
    mods_ctx = [mod[0:1].reshape(1, 1, 6 * d) for mod in (mod0, mod1)]
    mods_lat = [mod[1:1 + b_lat].reshape(b_lat, 1, 6 * d) for mod in (mod0, mod1)]

    m_ctx = b_ctx * seq_ctx
    y_prompt, ctx = _trunk(x_prompt.reshape(m_ctx, d), mods_ctx, m_ctx, p0, p1, fnorm, None, None, None, seq_ctx)
    new_k0, new_v0, new_ssd0, new_ret1 = ctx
    ang = _axial_angles(seq_lat, ATT_HEAD_DIM)
    cos, sin = np.cos(ang), np.sin(ang)
    reps = LANES // ATT_HEAD_DIM
    rope_att = (jnp.asarray(np.tile(np.concatenate([cos, cos], axis=1), (1, reps)), F32),
                jnp.asarray(np.tile(np.concatenate([-sin, sin], axis=1), (1, reps)), F32))
    ang = _axial_angles(seq_lat, RET_DK)
    rope_ret = (jnp.asarray(np.cos(ang), F32), jnp.asarray(np.sin(ang), F32))
    caches = (cache_k0, cache_v0, state_ssd0, state_ret1)
    y_sample, _ = _trunk(x_sample.reshape(b_lat * seq_lat, d), mods_lat, seq_lat, p0, p1, fnorm, rope_att, rope_ret,
                         caches, seq_lat)
    return (y_prompt.reshape(b_ctx, seq_ctx, d), y_sample.reshape(b_lat, seq_lat, d),
            new_k0, new_v0, new_ssd0, new_ret1)
```

```python
import functools

import jax
import jax.numpy as jnp
import numpy as np
from jax import lax
from jax.experimental import pallas as pl
from jax.experimental.pallas import tpu as pltpu

F32 = jnp.float32
BF16 = jnp.bfloat16

EPS = 1e-6
ROPE_THETA = 10000.0
GRID_W = 64
D_MODEL = 1024
ATT_HEAD_DIM = 64
ATT_HEADS = 8
ATT_KV_HEADS = 2
ATT_GROUP = ATT_HEADS // ATT_KV_HEADS
ATT_WIDTH = ATT_HEADS * ATT_HEAD_DIM
ATT_KV_WIDTH = ATT_KV_HEADS * ATT_HEAD_DIM
SSD_WIDTH = 512
SSD_HEADS = 8
SSD_HEAD_DIM = 64
SSD_STATE = 64
SSD_GROUPS = 2
SSD_CONV = 5
SSD_XBC = SSD_WIDTH + 2 * SSD_GROUPS * SSD_STATE
L0_IN = ATT_WIDTH + 2 * ATT_KV_WIDTH + SSD_WIDTH + SSD_XBC + 2 * SSD_HEADS
RET_HEADS = 4
RET_DK = 256
RET_DV = 512
RET_QK_WIDTH = RET_HEADS * RET_DK
RET_V_WIDTH = RET_HEADS * RET_DV
D_FF = 2816

LANES = 128
HALO = 8
L0_IN_PAD = -(-L0_IN // LANES) * LANES
MXU_WIDTH = 256
ROW_TILE = 512
ADA_COLS = 1536
ATT_KEY_CHUNK = MXU_WIDTH
ATT_SUB_ROWS = 512
ATT_MAX_UNROLL = 8
ATT_LOOP_UNROLL = 4
SSD_CHUNK = 128
SSD_SEQS = 2
RET_CHUNK = 256
VMEM_LIMIT = 56 * 1024 * 1024


def _dot(a, b):
    return jnp.dot(a, b, preferred_element_type=F32)


def _dot_nt(a, b):
    return lax.dot_general(a, b, (((1,), (1,)), ((), ())), preferred_element_type=F32)


def _dot_tn(a, b):
    return lax.dot_general(a, b, (((0,), (0,)), ((), ())), preferred_element_type=F32)


def _silu(x):
    half = 0.5 * x
    return half + half * jnp.tanh(half)


def _softplus(x):
    return jnp.maximum(x, 0.0) + jnp.log1p(jnp.exp(-jnp.abs(x)))


def _rms(x):
    return x * lax.rsqrt(jnp.mean(x * x, axis=-1, keepdims=True) + EPS)


def _split3(x):
    hi = x.astype(BF16)
    r = x - hi.astype(F32)
    mid = r.astype(BF16)
    lo = (r - mid.astype(F32)).astype(BF16)
    return hi, mid, lo


def _const_spec(shape):
    return pl.BlockSpec(shape, lambda *_: (0,) * len(shape))


def _resident_spec(shape):
    return pl.BlockSpec(shape, lambda *_: (0,) * len(shape), pipeline_mode=pl.Buffered(1))


def _params(n_axes, vmem=VMEM_LIMIT):
    return pltpu.CompilerParams(dimension_semantics=("arbitrary",) * n_axes, vmem_limit_bytes=vmem)


def _ada_kernel(c_ref, w_ref, b_ref, o_ref):
    s = _silu(c_ref[...])
    o_ref[...] = _dot(s.astype(BF16), w_ref[...].astype(BF16)) + b_ref[...]


def _ada(conds, w, b):
    n = w.shape[1]
    tn = ADA_COLS
    return pl.pallas_call(
        _ada_kernel,
        grid=(n // tn,),
        in_specs=[_const_spec(conds.shape),
                  pl.BlockSpec((D_MODEL, tn), lambda j: (0, j)),
                  pl.BlockSpec((1, tn), lambda j: (0, j))],
        out_specs=pl.BlockSpec((conds.shape[0], tn), lambda j: (0, j)),
        out_shape=jax.ShapeDtypeStruct((conds.shape[0], n), F32),
        compiler_params=_params(1),
        name="ada",
    )(conds, w, b.reshape(1, n))


def _mod_specs(tm, rows_per_mod, which):
    return [pl.BlockSpec((1, 1, D_MODEL), lambda i, j=j: ((i * tm) // rows_per_mod, 0, j)) for j in which]


def _head_rms(x, p_ref, gain):
    x2 = x * x
    hi = x2.astype(BF16)
    lo = (x2 - hi.astype(F32)).astype(BF16)
    ms = _dot(hi, p_ref[...]) + _dot(lo, p_ref[...])
    return x * lax.rsqrt(ms + EPS) * gain


def _rope64(x, cos, sin):
    n = x.shape[1]
    lane = lax.broadcasted_iota(jnp.int32, x.shape, 1)
    first_half = (lane % ATT_HEAD_DIM) < (ATT_HEAD_DIM // 2)
    partner = jnp.where(first_half, pltpu.roll(x, n - ATT_HEAD_DIM // 2, 1), pltpu.roll(x, ATT_HEAD_DIM // 2, 1))
    return x * cos + partner * sin


def _l0_in_kernel(*refs, use_rope, ctx_seq):
    refs = list(refs)
    x_ref, nw_ref, sh_ref, sc_ref, w_ref, qg_ref, kg_ref, pq_ref, pk_ref = refs[:9]
    pos = 9
    if use_rope:
        cos_ref, sin_ref = refs[pos:pos + 2]
        pos += 2
    q_ref, ka_ref, va_ref, z_ref, xbc_ref, dt_ref = refs[pos:pos + 6]
    pos += 6
    h = _rms(x_ref[...]) * nw_ref[...] * (1.0 + sc_ref[0]) + sh_ref[0]
    proj = _dot(h.astype(BF16), w_ref[...])
    o1 = ATT_WIDTH
    o2 = o1 + ATT_KV_WIDTH
    o3 = o2 + ATT_KV_WIDTH
    o4 = o3 + SSD_WIDTH
    o5 = o4 + SSD_XBC
    q = _head_rms(proj[:, :o1], pq_ref, qg_ref[...])
    k = _head_rms(proj[:, o1:o2], pk_ref, kg_ref[...])
    v = proj[:, o2:o3]
    if ctx_seq:
        kt_ref, vt_ref = refs[pos:]
        for s in range(x_ref.shape[0] // ctx_seq):
            kt = k[s * ctx_seq:(s + 1) * ctx_seq].T
            vt = v[s * ctx_seq:(s + 1) * ctx_seq].T
            for kv in range(ATT_KV_HEADS):
                kt_ref[s, kv] = kt[kv * ATT_HEAD_DIM:(kv + 1) * ATT_HEAD_DIM]
                vt_ref[s, kv] = vt[kv * ATT_HEAD_DIM:(kv + 1) * ATT_HEAD_DIM]
    if use_rope:
        cos = cos_ref[...]
        sin = sin_ref[...]
        k = _rope64(k, cos, sin)
        reps = ATT_WIDTH // LANES
        q = _rope64(q, jnp.concatenate([cos] * reps, axis=1), jnp.concatenate([sin] * reps, axis=1))
    q_ref[...] = (q * (ATT_HEAD_DIM ** -0.5)).astype(BF16)
    kb = k.astype(BF16)
    lane = lax.broadcasted_iota(jnp.int32, (1, LANES), 1)
    ones_col = jnp.where(lane == ATT_HEAD_DIM, 1.0, 0.0)
    for kv in range(ATT_KV_HEADS):
        ka_ref[kv] = kb[:, kv * ATT_HEAD_DIM:(kv + 1) * ATT_HEAD_DIM]
        vv = v if kv == 0 else pltpu.roll(v, (LANES - kv * ATT_HEAD_DIM) % LANES, 1)
        va_ref[kv] = jnp.where(lane < ATT_HEAD_DIM, vv, ones_col).astype(BF16)
    z_ref[...] = proj[:, o3:o4].astype(z_ref.dtype)
    xbc_ref[...] = proj[:, o4:o5]
    dt_ref[...] = proj[:, o5:]


def _l0_in(x, mod, rows_per_mod, p, rope, tm, ctx_seq):
    m = x.shape[0]
    use_rope = rope is not None
    row = lambda w: pl.BlockSpec((tm, w), lambda i: (i, 0))
    in_specs = ([row(D_MODEL), _const_spec((1, D_MODEL))] + _mod_specs(tm, rows_per_mod, (0, 1))
                + [_resident_spec((D_MODEL, L0_IN_PAD)), _const_spec((1, ATT_WIDTH)), _const_spec((1, ATT_KV_WIDTH)),
                   _resident_spec((ATT_WIDTH, ATT_WIDTH)), _resident_spec((ATT_KV_WIDTH, ATT_KV_WIDTH))])
    args = [x, p["norm_mix"], mod, mod, p["w_in"], p["q_gain"], p["k_gain"], p["pq"], p["pk"]]
    if use_rope:
        rows = rope[0].shape[0]
        in_specs += [pl.BlockSpec((tm, LANES), lambda i: (i % (rows // tm), 0))] * 2
        args += list(rope)
    head = lambda w: pl.BlockSpec((ATT_KV_HEADS, tm, w), lambda i: (0, i, 0))
    out_specs = [row(ATT_WIDTH), head(ATT_HEAD_DIM), head(LANES), row(SSD_WIDTH), row(SSD_XBC), row(LANES)]
    out_shape = [jax.ShapeDtypeStruct((m, ATT_WIDTH), BF16),
                 jax.ShapeDtypeStruct((ATT_KV_HEADS, m, ATT_HEAD_DIM), BF16),
                 jax.ShapeDtypeStruct((ATT_KV_HEADS, m, LANES), BF16),
                 jax.ShapeDtypeStruct((m, SSD_WIDTH), BF16), jax.ShapeDtypeStruct((m, SSD_XBC), F32),
                 jax.ShapeDtypeStruct((m, LANES), F32)]
    if ctx_seq:
        assert tm % ctx_seq == 0
        cache_block = (tm // ctx_seq, ATT_KV_HEADS, ATT_HEAD_DIM, ctx_seq)
        out_specs += [pl.BlockSpec(cache_block, lambda i: (i, 0, 0, 0))] * 2
        out_shape += [jax.ShapeDtypeStruct((m // ctx_seq,) + cache_block[1:], F32)] * 2
    return pl.pallas_call(
        functools.partial(_l0_in_kernel, use_rope=use_rope, ctx_seq=ctx_seq),
        grid=(m // tm,),
        in_specs=in_specs,
        out_specs=out_specs,
        out_shape=out_shape,
        compiler_params=_params(1),
        name="l0_in",
    )(*args)


def _attn_kernel(*refs, ck, sb, has_cache, kv_blk):
    if has_cache:
        q_ref, k_ref, v_ref, kc_ref, vc_ref, o_ref, s_all_ref, m_all_ref = refs
    else:
        q_ref, k_ref, v_ref, o_ref, s_all_ref, m_all_ref = refs
    gw = q_ref.shape[2] // kv_blk
    tt = sb // ATT_GROUP
    nsub = q_ref.shape[1] // tt
    slots = [(s_all_ref.at[n], m_all_ref.at[n]) for n in range(s_all_ref.shape[0])]

    def chunks(kv):
        out = []
        if has_cache:
            out += [(kc_ref.at[0, kv], vc_ref.at[0, kv], j * ck) for j in range(kc_ref.shape[2] // ck)]
        return out + [(k_ref.at[kv], v_ref.at[kv], j * ck) for j in range(k_ref.shape[1] // ck)]

    def token_rows(i):
        if isinstance(i, int):
            return slice(i * tt, (i + 1) * tt)
        return pl.ds(pl.multiple_of(i * tt, tt), tt)

    def scores(kv, i, slot):
        s_ref, m_ref = slots[slot]
        q4 = q_ref[0, token_rows(i), kv * gw:(kv + 1) * gw]
        q = jnp.concatenate([q4[:, g * ATT_HEAD_DIM:(g + 1) * ATT_HEAD_DIM] for g in range(ATT_GROUP)], axis=0)
        mx = None
        for j, (kr, _, r0) in enumerate(chunks(kv)):
            s = _dot_nt(q, kr[r0:r0 + ck, :])
            s_ref[j] = s
            for t in range(ck // LANES):
                part = s[:, t * LANES:(t + 1) * LANES]
                mx = part if mx is None else jnp.maximum(mx, part)
        m_ref[...] = jnp.broadcast_to(jnp.max(mx, axis=1, keepdims=True), (sb, LANES))

    def values(kv, i, slot):
        s_ref, m_ref = slots[slot]
        m = jnp.concatenate([m_ref[...]] * (ck // LANES), axis=1)
        acc = None
        for j, (_, vr, r0) in enumerate(chunks(kv)):
            p = jnp.exp(s_ref[j] - m).astype(BF16)
            part = _dot(p, vr[r0:r0 + ck, :])
            acc = part if acc is None else acc + part
        out = acc[:, :ATT_HEAD_DIM] / acc[:, ATT_HEAD_DIM:ATT_HEAD_DIM + 1]
        out = jnp.concatenate([out[g * tt:(g + 1) * tt] for g in range(ATT_GROUP)], axis=1)
        o_ref[0, token_rows(i), kv * gw:(kv + 1) * gw] = out.astype(o_ref.dtype)

    if len(slots) > 2:
        items = [(kv, i) for kv in range(kv_blk) for i in range(nsub)]
        for n, item in enumerate(items):
            scores(*item, n)
        for n, item in enumerate(items):
            values(*item, n)
        return

    unroll = ATT_LOOP_UNROLL
    for kv in range(kv_blk):
        scores(kv, 0, 0)

        def body(h, carry, kv=kv):
            for u in range(unroll):
                scores(kv, unroll * h + u + 1, (u + 1) % 2)
                values(kv, unroll * h + u, u % 2)
            return carry

        trips = (nsub - 1) // unroll
        lax.fori_loop(0, trips, body, 0)
        for n in range(trips * unroll, nsub - 1):
            scores(kv, n + 1, (n + 1) % 2)
            values(kv, n, n % 2)
        values(kv, nsub - 1, (nsub - 1) % 2)


def _attention(q, k, v, cache, ck, sb):
    b, seq, width = q.shape
    nkv = k.shape[0]
    gw = width // nkv
    nsub = seq * ATT_GROUP // sb
    flat = nkv * nsub <= ATT_MAX_UNROLL
    kv_blk = nkv if flat else 1
    n_slots = kv_blk * nsub if flat else 2
    lk = seq
    in_specs = [pl.BlockSpec((1, seq, gw * kv_blk), lambda i, j: (i, 0, j)),
                pl.BlockSpec((kv_blk, seq, ATT_HEAD_DIM), lambda i, j: (j, i, 0)),
                pl.BlockSpec((kv_blk, seq, LANES), lambda i, j: (j, i, 0))]
    args = [q, k, v]
    if cache is not None:
        past = cache[0].shape[2]
        assert past % ck == 0
        lk += past
        in_specs += [pl.BlockSpec((1, kv_blk, past, ATT_HEAD_DIM), lambda i, j: (i, j, 0, 0)),
                     pl.BlockSpec((1, kv_blk, past, LANES), lambda i, j: (i, j, 0, 0))]
        args += list(cache)
    assert (seq * ATT_GROUP) % (2 * sb) == 0 and seq % ck == 0
    return pl.pallas_call(
        functools.partial(_attn_kernel, ck=ck, sb=sb, has_cache=cache is not None, kv_blk=kv_blk),
        grid=(b, nkv // kv_blk),
        in_specs=in_specs,
        out_specs=pl.BlockSpec((1, seq, gw * kv_blk), lambda i, j: (i, 0, j)),
        out_shape=jax.ShapeDtypeStruct(q.shape, BF16),
        scratch_shapes=[pltpu.VMEM((n_slots, lk // ck, sb, ck), F32), pltpu.VMEM((n_slots, sb, LANES), F32)],
        compiler_params=_params(2),
        name="attention",
    )(*args)


def _ssd_group_rows(pair):
    g = pair // (SSD_HEADS // 2 // SSD_GROUPS)
    return slice(g * SSD_STATE, (g + 1) * SSD_STATE)


def _ssd_conv(cur_ref, prev_ref, next_ref, is_first, is_last, ext_ref, cw_ref, cb_ref, shift_ref):
    cn = SSD_CHUNK
    ext_ref[0:HALO, :] = jnp.where(is_first, 0.0, prev_ref[...])
    ext_ref[HALO:HALO + cn, :] = cur_ref[...]
    ext_ref[HALO + cn:, :] = jnp.where(is_last, 0.0, next_ref[...])
    shifted = _dot(shift_ref[...], ext_ref[...].astype(BF16))
    centre = SSD_CONV // 2
    u = cb_ref[...] + cw_ref[centre:centre + 1, :] * cur_ref[...]
    for n, t in enumerate(t for t in range(SSD_CONV) if t != centre):
        u = u + cw_ref[t:t + 1, :] * shifted[n * cn:(n + 1) * cn]
    return _silu(u)


def _ssd_scalars(dt_ref, dtb_ref, alog_ref, *, forward):
    cn = SSD_CHUNK
    nh = 2 * SSD_HEADS
    dt_t = _softplus(dt_ref[...].T[:nh] + dtb_ref[...])
    a_t = dt_t * (-jnp.exp(alog_ref[...]))
    ii = lax.broadcasted_iota(jnp.int32, (cn, cn), 0)
    jj = lax.broadcasted_iota(jnp.int32, (cn, cn), 1)
    keep = (ii >= jj) if forward else (ii <= jj)
    tri_t = ((ii <= jj) if forward else (ii >= jj)).astype(BF16)
    c3 = _dot(jnp.concatenate(_split3(a_t), axis=0), tri_t)
    cum_t = c3[:nh] + c3[nh:2 * nh] + c3[2 * nh:]
    cols = jnp.concatenate([dt_t, cum_t, jnp.zeros((LANES - 2 * nh, cn), F32)], axis=0).T
    lane = lax.broadcasted_iota(jnp.int32, (1, LANES), 1)
    return dict(cum_t=cum_t, cols=cols, keep=keep, lo=lane < SSD_HEAD_DIM, forward=forward)


def _ssd_mats(act, prep):
    lo = prep["lo"]
    xs = act[:, :SSD_WIDTH]
    bm = act[:, SSD_WIDTH:SSD_WIDTH + LANES]
    cm = act[:, SSD_WIDTH + LANES:]
    bm_b = bm.astype(BF16)
    gmat = []
    bg_b = []
    for g in range(SSD_GROUPS):
        in_group = lo if g == 0 else jnp.logical_not(lo)
        gmat.append(_dot_nt(jnp.where(in_group, cm, 0.0).astype(BF16), bm_b))
        bg_b.append(jnp.where(in_group, bm, 0.0).astype(BF16))
    return dict(prep, xs=xs, cm_b=cm.astype(BF16), gmat=gmat, bg_b=bg_b)


def _ssd_pair(prep, pair, s_ref, y_ref, dsk_ref):
    cn = SSD_CHUNK
    nh = 2 * SSD_HEADS
    forward, cols, cum_t, keep, lo = prep["forward"], prep["cols"], prep["cum_t"], prep["keep"], prep["lo"]

    def col(lane_idx):
        return jnp.broadcast_to(cols[:, lane_idx:lane_idx + 1], (cn, LANES))

    off = 0 if forward else SSD_HEADS
    tot_col = cn - 1 if forward else 0
    g = pair // (SSD_HEADS // 2 // SSD_GROUPS)
    gmat = prep["gmat"][g]
    h0 = off + 2 * pair
    h1 = h0 + 1
    ci0 = col(nh + h0)
    ci1 = col(nh + h1)
    cip = jnp.where(lo, ci0, ci1)
    m0 = (gmat * jnp.exp(jnp.where(keep, ci0 - cum_t[h0:h0 + 1, :], -1e30))).astype(BF16)
    m1 = (gmat * jnp.exp(jnp.where(keep, ci1 - cum_t[h1:h1 + 1, :], -1e30))).astype(BF16)
    xs_p = prep["xs"][:, pair * LANES:(pair + 1) * LANES]
    vp = xs_p * jnp.where(lo, col(h0), col(h1))
    v0 = jnp.where(lo, vp, 0.0).astype(BF16)
    v1 = jnp.where(lo, 0.0, vp).astype(BF16)
    s_old = s_ref[pair]
    y = (_dot(jnp.concatenate([m0, m1], axis=1), jnp.concatenate([v0, v1], axis=0))
         + jnp.exp(cip) * _dot(prep["cm_b"], s_old.astype(BF16)))
    totp = jnp.where(lo, cum_t[h0:h0 + 1, tot_col:tot_col + 1], cum_t[h1:h1 + 1, tot_col:tot_col + 1])
    s_ref[pair] = s_old * jnp.exp(totp) + _dot_tn(prep["bg_b"][g], (vp * jnp.exp(totp - cip)).astype(BF16))
    if forward:
        y = y + dsk_ref[:, pair * LANES:(pair + 1) * LANES] * xs_p
    y_ref[:, pair * LANES:(pair + 1) * LANES] = y.astype(y_ref.dtype)


def _ssd_kernel(*refs, nc, ns, has_init, want_fin):
    refs = list(refs)
    cf_ref, pf_ref, nf_ref, dtf_ref, cb_ref, pb_ref, nb_ref, dtb_ref = refs[:8]
    cw_ref, cbias_ref, dtbias_ref, alog_ref, dsk_ref, shift_ref = refs[8:14]
    pos = 14
    s0_ref = None
    if has_init:
        s0_ref = refs[pos]
        pos += 1
    yf_ref, yb_ref = refs[pos:pos + 2]
    pos += 2
    sfin_ref = None
    if want_fin:
        sfin_ref = refs[pos]
        pos += 1
    ext_ref, act_ref, sf_ref, sb_ref = refs[pos:]
    c = pl.program_id(1)
    c_fwd = c
    c_bwd = nc - 1 - c

    @pl.when(c == 0)
    def _():
        sf_ref[...] = jnp.zeros_like(sf_ref)
        sb_ref[...] = jnp.zeros_like(sb_ref)
        if has_init:
            for s in range(ns):
                for pair in range(SSD_HEADS // 2):
                    rows = _ssd_group_rows(pair)
                    sf_ref[s, pair, rows, :] = s0_ref[s, 0, pair]
                    sb_ref[s, pair, rows, :] = s0_ref[s, 1, pair]

    def scalars():
        out = []
        for s in range(ns):
            out.append(_ssd_scalars(dtf_ref.at[0, s], dtbias_ref, alog_ref, forward=True))
            out.append(_ssd_scalars(dtb_ref.at[0, s], dtbias_ref, alog_ref, forward=False))
        return out

    def streams(preps, acts):
        work = []
        for s in range(ns):
            work.append((_ssd_mats(acts[2 * s], preps[2 * s]), sf_ref.at[s], yf_ref.at[0, s]))
            work.append((_ssd_mats(acts[2 * s + 1], preps[2 * s + 1]), sb_ref.at[s], yb_ref.at[0, s]))
        for pair in range(SSD_HEADS // 2):
            for prep, s_ref, y_ref in work:
                _ssd_pair(prep, pair, s_ref, y_ref, dsk_ref)

    @pl.when(c < nc // 2)
    def _():
        preps = scalars()
        acts = []
        for s in range(ns):
            act_f = _ssd_conv(cf_ref.at[0, s], pf_ref.at[0, s], nf_ref.at[0, s], c_fwd == 0, c_fwd == nc - 1,
                              ext_ref.at[2 * s], cw_ref, cbias_ref, shift_ref)
            act_b = _ssd_conv(cb_ref.at[0, s], pb_ref.at[0, s], nb_ref.at[0, s], c_bwd == 0, c_bwd == nc - 1,
                              ext_ref.at[2 * s + 1], cw_ref, cbias_ref, shift_ref)
            act_ref[s, c_fwd] = act_f
            act_ref[s, c_bwd] = act_b
            acts += [act_f, act_b]
        streams(preps, acts)

    @pl.when(c >= nc // 2)
    def _():
        acts = []
        for s in range(ns):
            acts += [act_ref[s, c_fwd], act_ref[s, c_bwd]]
        streams(scalars(), acts)

    if want_fin:
        @pl.when(c == nc - 1)
        def _():
            for s in range(ns):
                for pair in range(SSD_HEADS // 2):
                    rows = _ssd_group_rows(pair)
                    for d, st_ref in enumerate((sf_ref, sb_ref)):
                        both = st_ref[s, pair, rows, :]
                        sfin_ref[s, d, 2 * pair] = both[:, :SSD_HEAD_DIM]
                        sfin_ref[s, d, 2 * pair + 1] = both[:, SSD_HEAD_DIM:]


def _ssd(xbc, dt, p, s0, batch, want_fin):
    m = xbc.shape[0]
    cn = SSD_CHUNK
    ns = SSD_SEQS
    seq = m // batch
    nc = seq // cn
    assert nc % 2 == 0
    assert batch % ns == 0
    per = cn // HALO
    n_halo = seq // HALO
    has_init = s0 is not None
    npair = SSD_HEADS // 2
    view = lambda a: a.reshape(batch // ns, ns, seq, a.shape[-1])

    def fwd(c):
        return c

    def bwd(c):
        return nc - 1 - c

    def stream_specs(chunk):
        return [pl.BlockSpec((1, ns, cn, SSD_XBC), lambda b, c: (b, 0, chunk(c), 0)),
                pl.BlockSpec((1, ns, HALO, SSD_XBC), lambda b, c: (b, 0, jnp.maximum(chunk(c) * per - 1, 0), 0)),
                pl.BlockSpec((1, ns, HALO, SSD_XBC),
                             lambda b, c: (b, 0, jnp.minimum(chunk(c) * per + per, n_halo - 1), 0)),
                pl.BlockSpec((1, ns, cn, LANES), lambda b, c: (b, 0, chunk(c), 0))]

    in_specs = stream_specs(fwd) + stream_specs(bwd) + [
        _const_spec((HALO, SSD_XBC)), _const_spec((1, SSD_XBC)), _const_spec((2 * SSD_HEADS, cn)),
        _const_spec((2 * SSD_HEADS, cn)), _const_spec((1, SSD_WIDTH)),
        _const_spec(((SSD_CONV - 1) * cn, cn + 2 * HALO))]
    taps = [t for t in range(SSD_CONV) if t != SSD_CONV // 2]
    src = np.concatenate([HALO + np.arange(cn) + (t - SSD_CONV // 2) for t in taps])
    shift = jnp.asarray(src[:, None] == np.arange(cn + 2 * HALO)[None, :], BF16)
    xv, dv = view(xbc), view(dt)
    args = [xv, xv, xv, dv, xv, xv, xv, dv, p["conv_w"], p["conv_b"], p["dt_bias"], p["a_log"], p["d_skip"], shift]
    state_block = (ns, 2, npair, SSD_STATE, LANES)
    if has_init:
        in_specs.append(pl.BlockSpec(state_block, lambda b, c: (b, 0, 0, 0, 0)))
        args.append(s0)
    out_specs = [pl.BlockSpec((1, ns, cn, SSD_WIDTH), lambda b, c: (b, 0, fwd(c), 0)),
                 pl.BlockSpec((1, ns, cn, SSD_WIDTH), lambda b, c: (b, 0, bwd(c), 0))]
    out_shape = [jax.ShapeDtypeStruct((batch // ns, ns, seq, SSD_WIDTH), BF16)] * 2
    if want_fin:
        fin_block = (ns, 2, SSD_HEADS, SSD_STATE, SSD_HEAD_DIM)
        out_specs.append(pl.BlockSpec(fin_block, lambda b, c: (b, 0, 0, 0, 0)))
        out_shape.append(jax.ShapeDtypeStruct((batch,) + fin_block[1:], F32))
    out = pl.pallas_call(
        functools.partial(_ssd_kernel, nc=nc, ns=ns, has_init=has_init, want_fin=want_fin),
        grid=(batch // ns, nc),
        in_specs=in_specs,
        out_specs=out_specs,
        out_shape=out_shape,
        scratch_shapes=[pltpu.VMEM((2 * ns, cn + 2 * HALO, SSD_XBC), F32), pltpu.VMEM((ns, nc, cn, SSD_XBC), F32),
                        pltpu.VMEM((ns, npair, LANES, LANES), F32), pltpu.VMEM((ns, npair, LANES, LANES), F32)],
        compiler_params=_params(2),
        name="ssd",
    )(*args)
    return [out[0].reshape(m, SSD_WIDTH), out[1].reshape(m, SSD_WIDTH)] + list(out[2:])


def _ssd_state_to_pairs(s):
    b = s.shape[0]
    npair = SSD_HEADS // 2
    s = s.reshape(b, 2, npair, 2, SSD_STATE, SSD_HEAD_DIM).transpose(0, 1, 2, 4, 3, 5)
    return s.reshape(b, 2, npair, SSD_STATE, 2 * SSD_HEAD_DIM)


def _l1_in_kernel(*refs, use_rope):
    if use_rope:
        x_ref, nw_ref, sh_ref, sc_ref, w_ref, cos_ref, sin_ref, q_ref, k_ref, v_ref, g_ref = refs
    else:
        x_ref, nw_ref, sh_ref, sc_ref, w_ref, q_ref, k_ref, v_ref, g_ref = refs
    h = (_rms(x_ref[...]) * nw_ref[...] * (1.0 + sc_ref[0]) + sh_ref[0]).astype(BF16)
    half = RET_DK // 2

    def rope(t):
        if not use_rope:
            return t
        cos = cos_ref[...]
        sin = sin_ref[...]
        parts = []
        for hd in range(RET_HEADS):
            x1 = t[:, hd * RET_DK:hd * RET_DK + half]
            x2 = t[:, hd * RET_DK + half:(hd + 1) * RET_DK]
            parts += [x1 * cos - x2 * sin, x2 * cos + x1 * sin]
        return jnp.concatenate(parts, axis=1)

    o1 = RET_QK_WIDTH
    o2 = 2 * RET_QK_WIDTH
    o3 = o2 + RET_V_WIDTH
    q_ref[...] = rope(_dot(h, w_ref[:, :o1])).astype(BF16)
    k_ref[...] = (rope(_dot(h, w_ref[:, o1:o2])) * (RET_DK ** -0.5)).astype(k_ref.dtype)
    v_ref[...] = _dot(h, w_ref[:, o2:o3]).astype(BF16)
    g_ref[...] = _silu(_dot(h, w_ref[:, o3:])).astype(g_ref.dtype)


def _l1_in(x, mod, rows_per_mod, p, rope, tm):
    m = x.shape[0]
    use_rope = rope is not None
    row = lambda w: pl.BlockSpec((tm, w), lambda i: (i, 0))
    n = 2 * RET_QK_WIDTH + 2 * RET_V_WIDTH
    in_specs = ([row(D_MODEL), _const_spec((1, D_MODEL))] + _mod_specs(tm, rows_per_mod, (0, 1))
                + [_resident_spec((D_MODEL, n))])
    args = [x, p["norm_mix"], mod, mod, p["w_in"]]
    if use_rope:
        rows = rope[0].shape[0]
        in_specs += [pl.BlockSpec((tm, LANES), lambda i: (i % (rows // tm), 0))] * 2
        args += list(rope)
    widths = [(RET_QK_WIDTH, BF16), (RET_QK_WIDTH, BF16), (RET_V_WIDTH, BF16), (RET_V_WIDTH, BF16)]
    return pl.pallas_call(
        functools.partial(_l1_in_kernel, use_rope=use_rope),
        grid=(m // tm,),
        in_specs=in_specs,
        out_specs=[row(w) for w, _ in widths],
        out_shape=[jax.ShapeDtypeStruct((m, w), dt) for w, dt in widths],
        compiler_params=_params(1),
        name="l1_in",
    )(*args)


def _ret_kernel(*refs, cn, nc, has_init, want_fin):
    stateless = (not has_init) and nc == 1
    refs = list(refs)
    dec_ref, qf_ref, kf_ref, vf_ref = refs[:4]
    pos = 4
    if stateless:
        qb_ref, kb_ref, vb_ref = qf_ref, kf_ref, vf_ref
    else:
        qb_ref, kb_ref, vb_ref = refs[pos:pos + 3]
        pos += 3
    s0_ref = None
    if has_init:
        s0_ref = refs[pos]
        pos += 1
    yf_ref = refs[pos]
    pos += 1
    yb_ref = None
    if not stateless:
        yb_ref = refs[pos]
        pos += 1
    sfin_ref = None
    if want_fin:
        sfin_ref = refs[pos]
        pos += 1
    decay_ref, rowdec_ref, sf_ref, sb_ref = refs[pos:]
    c = pl.program_id(1)
    log_g = -jnp.exp(dec_ref[...])

    @pl.when((pl.program_id(0) == 0) & (c == 0))
    def _():
        ii = lax.broadcasted_iota(jnp.int32, (cn, cn), 0)
        jj = lax.broadcasted_iota(jnp.int32, (cn, cn), 1)
        dist = (ii - jj).astype(F32)
        ri = lax.broadcasted_iota(jnp.int32, (cn, LANES), 0).astype(F32)
        for hd in range(RET_HEADS):
            gf = log_g[0:1, hd:hd + 1]
            gb = log_g[1:2, hd:hd + 1]
            decay_ref[hd] = (jnp.where(dist >= 0, jnp.exp(gf * jnp.maximum(dist, 0.0)), 0.0)
                             + jnp.where(dist <= 0, jnp.exp(gb * jnp.maximum(-dist, 0.0)), 0.0))
            rowdec_ref[hd, 0] = jnp.exp(gf * (ri + 1.0))
            rowdec_ref[hd, 1] = jnp.exp(gf * (cn - 1.0 - ri))
            rowdec_ref[hd, 2] = jnp.exp(gb * (cn - ri))
            rowdec_ref[hd, 3] = jnp.exp(gb * ri)

    def rowdec(hd, which, width):
        return jnp.concatenate([rowdec_ref[hd, which]] * (width // LANES), axis=1)

    if not stateless:
        @pl.when(c == 0)
        def _():
            if has_init:
                sf_ref[...] = s0_ref[0, 0]
                sb_ref[...] = s0_ref[0, 1]
            else:
                sf_ref[...] = jnp.zeros_like(sf_ref)
                sb_ref[...] = jnp.zeros_like(sb_ref)

    for hd in range(RET_HEADS):
        gf = log_g[0:1, hd:hd + 1]
        gb = log_g[1:2, hd:hd + 1]
        qs = slice(hd * RET_DK, (hd + 1) * RET_DK)
        vs = slice(hd * RET_DV, (hd + 1) * RET_DV)
        q = qf_ref[:, qs]
        k = kf_ref[:, qs]
        v = vf_ref[:, vs]
        y = _dot((_dot_nt(q, k.astype(BF16)) * decay_ref[hd]).astype(BF16), v)
        upd_f = _dot_tn((k * rowdec(hd, 1, RET_DK)).astype(BF16), v)
        if stateless:
            new_f = upd_f
        else:
            s_old = sf_ref[hd]
            y = y + rowdec(hd, 0, RET_DV) * _dot(q, s_old.astype(BF16))
            new_f = s_old * jnp.exp(gf * cn) + upd_f
            sf_ref[hd] = new_f
        yf_ref[:, vs] = y.astype(yf_ref.dtype)
        q = qb_ref[:, qs]
        k = kb_ref[:, qs]
        v = vb_ref[:, vs]
        upd_b = _dot_tn((k * rowdec(hd, 3, RET_DK)).astype(BF16), v)
        if stateless:
            new_b = upd_b
        else:
            s_old = sb_ref[hd]
            yb_ref[:, vs] = (rowdec(hd, 2, RET_DV) * _dot(q, s_old.astype(BF16))).astype(yb_ref.dtype)
            new_b = s_old * jnp.exp(gb * cn) + upd_b
            sb_ref[hd] = new_b
        if want_fin:
            if stateless:
                sfin_ref[0, 0, hd] = new_f
                sfin_ref[0, 1, hd] = new_b
            else:
                @pl.when(c == nc - 1)
                def _(new_f=new_f, new_b=new_b, hd=hd):
                    sfin_ref[0, 0, hd] = new_f
                    sfin_ref[0, 1, hd] = new_b


def _retention(q, k, v, decay, s0, batch, cn, want_fin):
    m = q.shape[0]
    nc = m // batch // cn
    has_init = s0 is not None
    stateless = (not has_init) and nc == 1

    def fwd(b, c):
        return b * nc + c

    def bwd(b, c):
        return b * nc + nc - 1 - c

    def stream_specs(chunk):
        return [pl.BlockSpec((cn, RET_QK_WIDTH), lambda b, c: (chunk(b, c), 0)),
                pl.BlockSpec((cn, RET_QK_WIDTH), lambda b, c: (chunk(b, c), 0)),
                pl.BlockSpec((cn, RET_V_WIDTH), lambda b, c: (chunk(b, c), 0))]

    in_specs = [_const_spec((8, LANES))] + stream_specs(fwd)
    args = [decay, q, k, v]
    if not stateless:
        in_specs += stream_specs(bwd)
        args += [q, k, v]
    state_block = (1, 2, RET_HEADS, RET_DK, RET_DV)
    if has_init:
        in_specs.append(pl.BlockSpec(state_block, lambda b, c: (b, 0, 0, 0, 0)))
        args.append(s0)
    out_specs = [pl.BlockSpec((cn, RET_V_WIDTH), lambda b, c: (fwd(b, c), 0))]
    if not stateless:
        out_specs.append(pl.BlockSpec((cn, RET_V_WIDTH), lambda b, c: (bwd(b, c), 0)))
    out_shape = [jax.ShapeDtypeStruct((m, RET_V_WIDTH), BF16)] * len(out_specs)
    if want_fin:
        out_specs.append(pl.BlockSpec(state_block, lambda b, c: (b, 0, 0, 0, 0)))
        out_shape.append(jax.ShapeDtypeStruct((batch,) + state_block[1:], F32))
    return pl.pallas_call(
        functools.partial(_ret_kernel, cn=cn, nc=nc, has_init=has_init, want_fin=want_fin),
        grid=(batch, nc),
        in_specs=in_specs,
        out_specs=out_specs,
        out_shape=out_shape,
        scratch_shapes=[pltpu.VMEM((RET_HEADS, cn, cn), F32), pltpu.VMEM((RET_HEADS, 4, cn, LANES), F32)]
        + [pltpu.VMEM((RET_HEADS, RET_DK, RET_DV), F32)] * 2,
        compiler_params=_params(2),
        name="retention",
    )(*args)


def _post_kernel(*refs, mixer, final):
    refs = list(refs)
    x_ref, g1_ref, nw_ref, sh2_ref, sc2_ref, g2_ref, wout_ref, wg_ref, wu_ref, wd_ref = refs[:10]
    pos = 10
    fn_ref = None
    if final:
        fn_ref = refs[pos]
        pos += 1
    if mixer == "ab":
        att_ref, yf_ref, yb_ref, z_ref, gain_ref, o_ref = refs[pos:]
        y = (yf_ref[...].astype(F32) + yb_ref[...].astype(F32)) * _silu(z_ref[...].astype(F32))
        y = _rms(y) * gain_ref[...]
        mix = _dot(jnp.concatenate([att_ref[...], y.astype(BF16)], axis=1), wout_ref[...])
    else:
        parts = refs[pos:-3]
        gate_ref, gain_ref, o_ref = refs[-3:]
        mix = None
        for hd in range(RET_HEADS):
            vs = slice(hd * RET_DV, (hd + 1) * RET_DV)
            y = parts[0][:, vs].astype(F32)
            for extra in parts[1:]:
                y = y + extra[:, vs].astype(F32)
            y = _rms(y) * gain_ref[:, vs]
            part = _dot((gate_ref[:, vs].astype(F32) * y).astype(BF16), wout_ref[vs, :])
            mix = part if mix is None else mix + part
    x1 = x_ref[...] + g1_ref[0] * mix
    h = (_rms(x1) * nw_ref[...] * (1.0 + sc2_ref[0]) + sh2_ref[0]).astype(BF16)
    act = (_silu(_dot(h, wg_ref[...])) * _dot(h, wu_ref[...])).astype(BF16)
    x2 = x1 + g2_ref[0] * _dot(act, wd_ref[...])
    if final:
        x2 = _rms(x2) * fn_ref[...]
    o_ref[...] = x2


def _post(x, mod, rows_per_mod, p, mixer, mix_inputs, gain, final_norm, tm):
    m = x.shape[0]
    row = lambda w: pl.BlockSpec((tm, w), lambda i: (i, 0))
    mixw = p["w_out"].shape[0]
    (g1,) = _mod_specs(tm, rows_per_mod, (2,))
    sh2, sc2, g2 = _mod_specs(tm, rows_per_mod, (3, 4, 5))
    in_specs = [row(D_MODEL), g1, _const_spec((1, D_MODEL)), sh2, sc2, g2,
                _resident_spec((mixw, D_MODEL)), _resident_spec((D_MODEL, D_FF)), _resident_spec((D_MODEL, D_FF)),
                _resident_spec((D_FF, D_MODEL))]
    args = [x, mod, p["norm_ffn"], mod, mod, mod, p["w_out"], p["w_gate"], p["w_up"], p["w_down"]]
    final = final_norm is not None
    if final:
        in_specs.append(_const_spec((1, D_MODEL)))
        args.append(final_norm)
    in_specs += [row(a.shape[1]) for a in mix_inputs] + [_const_spec(gain.shape)]
    args += list(mix_inputs) + [gain]
    return pl.pallas_call(
        functools.partial(_post_kernel, mixer=mixer, final=final),
        grid=(m // tm,),
        in_specs=in_specs,
        out_specs=row(D_MODEL),
        out_shape=jax.ShapeDtypeStruct((m, D_MODEL), F32),
        compiler_params=_params(1),
        name="post_" + mixer,
    )(*args)


def _axial_angles(n_tokens, dim):
    rows = n_tokens // GRID_W
    row = np.repeat(np.arange(rows), GRID_W).astype(np.float64)
    col = np.tile(np.arange(GRID_W), rows).astype(np.float64)
    n_freq = dim // 4
    inv = ROPE_THETA ** (-np.arange(n_freq, dtype=np.float64) / n_freq)
    return np.concatenate([row[:, None] * inv, col[:, None] * inv], axis=-1)


def _head_mean_matrix(width, head):
    idx = jnp.arange(width) // head
    return jnp.where(idx[:, None] == idx[None, :], 1.0 / head, 0.0).astype(BF16)


def _rows_bcast(v, width):
    return jnp.broadcast_to(v.reshape(-1, 1), (v.size, width))


def _trunk(x, mods, rows_per_mod, p0, p1, final_norm, rope_att, rope_ret, caches, seq):
    m = x.shape[0]
    batch = m // seq
    sample = caches is not None
    tm = ROW_TILE
    l0 = _l0_in(x, mods[0], rows_per_mod, p0, rope_att, tm, None if sample else seq)
    q, ka, va, z, xbc, dt = l0[:6]
    s0_ssd = None
    s0_ret = None
    kv_cache = None
    if sample:
        cache_k, cache_v, state_ssd, state_ret = caches
        ck = cache_k.astype(BF16).transpose(0, 2, 1, 3)
        cv = cache_v.astype(BF16).transpose(0, 2, 1, 3)
        ones_col = jnp.zeros(cv.shape[:3] + (LANES - ATT_HEAD_DIM,), BF16).at[..., 0].set(1.0)
        kv_cache = (ck, jnp.concatenate([cv, ones_col], axis=-1))
        s0_ssd = _ssd_state_to_pairs(state_ssd)
        s0_ret = state_ret
    att = _attention(q.reshape(batch, seq, ATT_WIDTH), ka, va, kv_cache, ATT_KEY_CHUNK, ATT_SUB_ROWS)
    att = att.reshape(m, ATT_WIDTH)
    ssd_out = _ssd(xbc, dt, p0, s0_ssd, batch, want_fin=not sample)
    x = _post(x, mods[0], rows_per_mod, p0, "ab", [att, ssd_out[0], ssd_out[1], z], p0["ssd_gain"], None, tm)
    q1, k1, v1, g1 = _l1_in(x, mods[1], rows_per_mod, p1, rope_ret, tm)
    ret_out = _retention(q1, k1, v1, p1["decay"], s0_ret, batch, RET_CHUNK, want_fin=not sample)
    y_parts = ret_out if sample else ret_out[:-1]
    y = _post(x, mods[1], rows_per_mod, p1, "c", list(y_parts) + [g1], p1["ret_gain"], final_norm, tm)
    if sample:
        return y, None
    new_k = l0[6].transpose(0, 3, 1, 2)
    new_v = l0[7].transpose(0, 3, 1, 2)
    return y, (new_k, new_v, ssd_out[2], ret_out[-1])


def kernel(x_prompt, x_sample, c, cache_k0, cache_v0, state_ssd0, state_ret1, c_ctx, l0_w_ada, l0_b_ada, l0_norm_mix, l0_norm_ffn, l0_w_in, l0_w_out, l0_q_gain, l0_k_gain, l0_conv_w, l0_conv_b, l0_dt_bias, l0_a_log, l0_d_skip, l0_ssd_gain, l0_w_gate, l0_w_up, l0_w_down, l1_w_ada, l1_b_ada, l1_norm_mix, l1_norm_ffn, l1_w_in, l1_w_out, l1_decay, l1_ret_gain, l1_w_gate, l1_w_up, l1_w_down, final_norm):
    b_ctx, seq_ctx, d = x_prompt.shape
    b_lat, seq_lat, _ = x_sample.shape
    assert d == D_MODEL and l0_w_in.shape == (D_MODEL, L0_IN) and l0_w_gate.shape == (D_MODEL, D_FF)
    row = lambda v: v.reshape(1, -1)

    p0 = dict(
        norm_mix=row(l0_norm_mix), norm_ffn=row(l0_norm_ffn),
        w_in=jnp.pad(l0_w_in, ((0, 0), (0, L0_IN_PAD - L0_IN))).astype(BF16),
        w_out=l0_w_out.astype(BF16), w_gate=l0_w_gate.astype(BF16), w_up=l0_w_up.astype(BF16),
        w_down=l0_w_down.astype(BF16),
        q_gain=row(jnp.tile(l0_q_gain, ATT_HEADS)), k_gain=row(jnp.tile(l0_k_gain, ATT_KV_HEADS)),
        pq=_head_mean_matrix(ATT_WIDTH, ATT_HEAD_DIM), pk=_head_mean_matrix(ATT_KV_WIDTH, ATT_HEAD_DIM),
        conv_w=jnp.pad(l0_conv_w, ((0, HALO - SSD_CONV), (0, 0))), conv_b=row(l0_conv_b),
        dt_bias=_rows_bcast(l0_dt_bias, SSD_CHUNK), a_log=_rows_bcast(l0_a_log, SSD_CHUNK),
        d_skip=row(jnp.repeat(l0_d_skip, SSD_HEAD_DIM)), ssd_gain=row(l0_ssd_gain),
    )
    p1 = dict(
        norm_mix=row(l1_norm_mix), norm_ffn=row(l1_norm_ffn),
        w_in=l1_w_in.astype(BF16), w_out=l1_w_out.astype(BF16), w_gate=l1_w_gate.astype(BF16),
        w_up=l1_w_up.astype(BF16), w_down=l1_w_down.astype(BF16),
        decay=jnp.pad(l1_decay, ((0, 8 - l1_decay.shape[0]), (0, LANES - l1_decay.shape[1]))),
        ret_gain=row(l1_ret_gain),
    )
    fnorm = row(final_norm)

    n_cond = 8
    conds = jnp.concatenate([c_ctx[None, :], c, jnp.zeros((n_cond - 1 - b_lat, d), F32)], axis=0)
    mod0 = _ada(conds, l0_w_ada, l0_b_ada)
    mod1 = _ada(conds, l1_w_ada, l1_b_ada)
    mods_ctx = [mod[0:1].reshape(1, 1, 6 * d) for mod in (mod0, mod1)]
    mods_lat = [mod[1:1 + b_lat].reshape(b_lat, 1, 6 * d) for mod in (mod0, mod1)]

    m_ctx = b_ctx * seq_ctx
    y_prompt, ctx = _trunk(x_prompt.reshape(m_ctx, d), mods_ctx, m_ctx, p0, p1, fnorm, None, None, None, seq_ctx)
    new_k0, new_v0, new_ssd0, new_ret1 = ctx
    ang = _axial_angles(seq_lat, ATT_HEAD_DIM)
    cos, sin = np.cos(ang), np.sin(ang)
    reps = LANES // ATT_HEAD_DIM
    rope_att = (jnp.asarray(np.tile(np.concatenate([cos, cos], axis=1), (1, reps)), F32),
                jnp.asarray(np.tile(np.concatenate([-sin, sin], axis=1), (1, reps)), F32))
    ang = _axial_angles(seq_lat, RET_DK)
    rope_ret = (jnp.asarray(np.cos(ang), F32), jnp.asarray(np.sin(ang), F32))
    caches = (cache_k0, cache_v0, state_ssd0, state_ret1)
    y_sample, _ = _trunk(x_sample.reshape(b_lat * seq_lat, d), mods_lat, seq_lat, p0, p1, fnorm, rope_att, rope_ret,
                         caches, seq_lat)
    return (y_prompt.reshape(b_ctx, seq_ctx, d), y_sample.reshape(b_lat, seq_lat, d),
            new_k0, new_v0, new_ssd0, new_ret1)
```

```python
import functools

import jax
import jax.numpy as jnp
import numpy as np
from jax import lax
from jax.experimental import pallas as pl
from jax.experimental.pallas import tpu as pltpu

F32 = jnp.float32
BF16 = jnp.bfloat16

EPS = 1e-6
ROPE_THETA = 10000.0
GRID_W = 64
D_MODEL = 1024
ATT_HEAD_DIM = 64
ATT_HEADS = 8
ATT_KV_HEADS = 2
ATT_GROUP = ATT_HEADS // ATT_KV_HEADS
ATT_WIDTH = ATT_HEADS * ATT_HEAD_DIM
ATT_KV_WIDTH = ATT_KV_HEADS * ATT_HEAD_DIM
SSD_WIDTH = 512
SSD_HEADS = 8
SSD_HEAD_DIM = 64
SSD_STATE = 64
SSD_GROUPS = 2
SSD_CONV = 5
SSD_XBC = SSD_WIDTH + 2 * SSD_GROUPS * SSD_STATE
L0_IN = ATT_WIDTH + 2 * ATT_KV_WIDTH + SSD_WIDTH + SSD_XBC + 2 * SSD_HEADS
RET_HEADS = 4
RET_DK = 256
RET_DV = 512
RET_QK_WIDTH = RET_HEADS * RET_DK
RET_V_WIDTH = RET_HEADS * RET_DV
D_FF = 2816

LANES = 128
HALO = 8
L0_IN_PAD = -(-L0_IN // LANES) * LANES
MXU_WIDTH = 256
ROW_TILE = 512
ADA_COLS = 1536
CAST_STEPS = 16
ATT_KEY_CHUNK = MXU_WIDTH
ATT_SUB_ROWS = 512
ATT_MAX_UNROLL = 8
ATT_LOOP_UNROLL = 4
SSD_CHUNK = 128
SSD_SEQS = 2
RET_CHUNK = 256
VMEM_LIMIT = 56 * 1024 * 1024


def _dot(a, b):
    return jnp.dot(a, b, preferred_element_type=F32)


def _dot_nt(a, b):
    return lax.dot_general(a, b, (((1,), (1,)), ((), ())), preferred_element_type=F32)


def _dot_tn(a, b):
    return lax.dot_general(a, b, (((0,), (0,)), ((), ())), preferred_element_type=F32)


def _silu(x):
    half = 0.5 * x
    return half + half * jnp.tanh(half)


def _softplus(x):
    return jnp.maximum(x, 0.0) + jnp.log1p(jnp.exp(-jnp.abs(x)))


def _rms(x):
    return x * lax.rsqrt(jnp.mean(x * x, axis=-1, keepdims=True) + EPS)


def _split3(x):
    hi = x.astype(BF16)
    r = x - hi.astype(F32)
    mid = r.astype(BF16)
    lo = (r - mid.astype(F32)).astype(BF16)
    return hi, mid, lo


def _const_spec(shape):
    return pl.BlockSpec(shape, lambda *_: (0,) * len(shape))


def _resident_spec(shape):
    return pl.BlockSpec(shape, lambda *_: (0,) * len(shape), pipeline_mode=pl.Buffered(1))


def _params(n_axes, vmem=VMEM_LIMIT):
    return pltpu.CompilerParams(dimension_semantics=("arbitrary",) * n_axes, vmem_limit_bytes=vmem)


def _cast_kernel(*refs):
    n = len(refs) // 2
    for x_ref, o_ref in zip(refs[:n], refs[n:]):
        width = x_ref.shape[1]
        if o_ref.shape[1] == width:
            o_ref[...] = x_ref[...].astype(BF16)
        else:
            o_ref[:, :width] = x_ref[...].astype(BF16)
            o_ref[:, width:] = jnp.zeros((o_ref.shape[0], o_ref.shape[1] - width), BF16)


def _cast_weights(weights):
    steps = CAST_STEPS
    in_specs, out_specs, out_shape = [], [], []
    for w in weights:
        rows, width = w.shape
        assert rows % (steps * 16) == 0
        padded = -(-width // LANES) * LANES
        in_specs.append(pl.BlockSpec((rows // steps, width), lambda i: (i, 0)))
        out_specs.append(pl.BlockSpec((rows // steps, padded), lambda i: (i, 0)))
        out_shape.append(jax.ShapeDtypeStruct((rows, padded), BF16))
    return pl.pallas_call(
        _cast_kernel,
        grid=(steps,),
        in_specs=in_specs,
        out_specs=out_specs,
        out_shape=out_shape,
        compiler_params=_params(1),
        name="cast_weights",
    )(*weights)


def _ada_kernel(c_ref, w_ref, b_ref, o_ref):
    s = _silu(c_ref[...])
    o_ref[...] = _dot(s.astype(BF16), w_ref[...].astype(BF16)) + b_ref[...]


def _ada(conds, w, b):
    n = w.shape[1]
    tn = ADA_COLS
    return pl.pallas_call(
        _ada_kernel,
        grid=(n // tn,),
        in_specs=[_const_spec(conds.shape),
                  pl.BlockSpec((D_MODEL, tn), lambda j: (0, j)),
                  pl.BlockSpec((1, tn), lambda j: (0, j))],
        out_specs=pl.BlockSpec((conds.shape[0], tn), lambda j: (0, j)),
        out_shape=jax.ShapeDtypeStruct((conds.shape[0], n), F32),
        compiler_params=_params(1),
        name="ada",
    )(conds, w, b.reshape(1, n))


def _mod_specs(tm, rows_per_mod, which):
    return [pl.BlockSpec((1, 1, D_MODEL), lambda i, j=j: ((i * tm) // rows_per_mod, 0, j)) for j in which]


def _head_rms(x, p_ref, gain):
    x2 = x * x
    hi = x2.astype(BF16)
    lo = (x2 - hi.astype(F32)).astype(BF16)
    ms = _dot(hi, p_ref[...]) + _dot(lo, p_ref[...])
    return x * lax.rsqrt(ms + EPS) * gain


def _rope64(x, cos, sin):
    n = x.shape[1]
    lane = lax.broadcasted_iota(jnp.int32, x.shape, 1)
    first_half = (lane % ATT_HEAD_DIM) < (ATT_HEAD_DIM // 2)
    partner = jnp.where(first_half, pltpu.roll(x, n - ATT_HEAD_DIM // 2, 1), pltpu.roll(x, ATT_HEAD_DIM // 2, 1))
    return x * cos + partner * sin


def _l0_in_kernel(*refs, use_rope, ctx_seq):
    refs = list(refs)
    x_ref, nw_ref, sh_ref, sc_ref, w_ref, qg_ref, kg_ref, pq_ref, pk_ref = refs[:9]
    pos = 9
    if use_rope:
        cos_ref, sin_ref = refs[pos:pos + 2]
        pos += 2
    q_ref, ka_ref, va_ref, z_ref, xbc_ref, dt_ref = refs[pos:pos + 6]
    pos += 6
    h = _rms(x_ref[...]) * nw_ref[...] * (1.0 + sc_ref[0]) + sh_ref[0]
    proj = _dot(h.astype(BF16), w_ref[...])
    o1 = ATT_WIDTH
    o2 = o1 + ATT_KV_WIDTH
    o3 = o2 + ATT_KV_WIDTH
    o4 = o3 + SSD_WIDTH
    o5 = o4 + SSD_XBC
    q = _head_rms(proj[:, :o1], pq_ref, qg_ref[...])
    k = _head_rms(proj[:, o1:o2], pk_ref, kg_ref[...])
    v = proj[:, o2:o3]
    if ctx_seq:
        kt_ref, vt_ref = refs[pos:]
        for s in range(x_ref.shape[0] // ctx_seq):
            kt = k[s * ctx_seq:(s + 1) * ctx_seq].T
            vt = v[s * ctx_seq:(s + 1) * ctx_seq].T
            for kv in range(ATT_KV_HEADS):
                kt_ref[s, kv] = kt[kv * ATT_HEAD_DIM:(kv + 1) * ATT_HEAD_DIM]
                vt_ref[s, kv] = vt[kv * ATT_HEAD_DIM:(kv + 1) * ATT_HEAD_DIM]
    if use_rope:
        cos = cos_ref[...]
        sin = sin_ref[...]
        k = _rope64(k, cos, sin)
        reps = ATT_WIDTH // LANES
        q = _rope64(q, jnp.concatenate([cos] * reps, axis=1), jnp.concatenate([sin] * reps, axis=1))
    q_ref[...] = (q * (ATT_HEAD_DIM ** -0.5)).astype(BF16)
    kb = k.astype(BF16)
    lane = lax.broadcasted_iota(jnp.int32, (1, LANES), 1)
    ones_col = jnp.where(lane == ATT_HEAD_DIM, 1.0, 0.0)
    for kv in range(ATT_KV_HEADS):
        ka_ref[kv] = kb[:, kv * ATT_HEAD_DIM:(kv + 1) * ATT_HEAD_DIM]
        vv = v if kv == 0 else pltpu.roll(v, (LANES - kv * ATT_HEAD_DIM) % LANES, 1)
        va_ref[kv] = jnp.where(lane < ATT_HEAD_DIM, vv, ones_col).astype(BF16)
    z_ref[...] = proj[:, o3:o4].astype(z_ref.dtype)
    xbc_ref[...] = proj[:, o4:o5]
    dt_ref[...] = proj[:, o5:]


def _l0_in(x, mod, rows_per_mod, p, rope, tm, ctx_seq):
    m = x.shape[0]
    use_rope = rope is not None
    row = lambda w: pl.BlockSpec((tm, w), lambda i: (i, 0))
    in_specs = ([row(D_MODEL), _const_spec((1, D_MODEL))] + _mod_specs(tm, rows_per_mod, (0, 1))
                + [_resident_spec((D_MODEL, L0_IN_PAD)), _const_spec((1, ATT_WIDTH)), _const_spec((1, ATT_KV_WIDTH)),
                   _resident_spec((ATT_WIDTH, ATT_WIDTH)), _resident_spec((ATT_KV_WIDTH, ATT_KV_WIDTH))])
    args = [x, p["norm_mix"], mod, mod, p["w_in"], p["q_gain"], p["k_gain"], p["pq"], p["pk"]]
    if use_rope:
        rows = rope[0].shape[0]
        in_specs += [pl.BlockSpec((tm, LANES), lambda i: (i % (rows // tm), 0))] * 2
        args += list(rope)
    head = lambda w: pl.BlockSpec((ATT_KV_HEADS, tm, w), lambda i: (0, i, 0))
    out_specs = [row(ATT_WIDTH), head(ATT_HEAD_DIM), head(LANES), row(SSD_WIDTH), row(SSD_XBC), row(LANES)]
    out_shape = [jax.ShapeDtypeStruct((m, ATT_WIDTH), BF16),
                 jax.ShapeDtypeStruct((ATT_KV_HEADS, m, ATT_HEAD_DIM), BF16),
                 jax.ShapeDtypeStruct((ATT_KV_HEADS, m, LANES), BF16),
                 jax.ShapeDtypeStruct((m, SSD_WIDTH), BF16), jax.ShapeDtypeStruct((m, SSD_XBC), F32),
                 jax.ShapeDtypeStruct((m, LANES), F32)]
    if ctx_seq:
        assert tm % ctx_seq == 0
        cache_block = (tm // ctx_seq, ATT_KV_HEADS, ATT_HEAD_DIM, ctx_seq)
        out_specs += [pl.BlockSpec(cache_block, lambda i: (i, 0, 0, 0))] * 2
        out_shape += [jax.ShapeDtypeStruct((m // ctx_seq,) + cache_block[1:], F32)] * 2
    return pl.pallas_call(
        functools.partial(_l0_in_kernel, use_rope=use_rope, ctx_seq=ctx_seq),
        grid=(m // tm,),
        in_specs=in_specs,
        out_specs=out_specs,
        out_shape=out_shape,
        compiler_params=_params(1),
        name="l0_in",
    )(*args)


def _attn_kernel(*refs, ck, sb, has_cache, kv_blk):
    if has_cache:
        q_ref, k_ref, v_ref, kc_ref, vc_ref, o_ref, s_all_ref, m_all_ref = refs
    else:
        q_ref, k_ref, v_ref, o_ref, s_all_ref, m_all_ref = refs
    gw = q_ref.shape[2] // kv_blk
    tt = sb // ATT_GROUP
    nsub = q_ref.shape[1] // tt
    slots = [(s_all_ref.at[n], m_all_ref.at[n]) for n in range(s_all_ref.shape[0])]

    def chunks(kv):
        out = []
        if has_cache:
            out += [(kc_ref.at[0, kv], vc_ref.at[0, kv], j * ck) for j in range(kc_ref.shape[2] // ck)]
        return out + [(k_ref.at[kv], v_ref.at[kv], j * ck) for j in range(k_ref.shape[1] // ck)]

    def token_rows(i):
        if isinstance(i, int):
            return slice(i * tt, (i + 1) * tt)
        return pl.ds(pl.multiple_of(i * tt, tt), tt)

    def scores(kv, i, slot):
        s_ref, m_ref = slots[slot]
        q4 = q_ref[0, token_rows(i), kv * gw:(kv + 1) * gw]
        q = jnp.concatenate([q4[:, g * ATT_HEAD_DIM:(g + 1) * ATT_HEAD_DIM] for g in range(ATT_GROUP)], axis=0)
        mx = None
        for j, (kr, _, r0) in enumerate(chunks(kv)):
            s = _dot_nt(q, kr[r0:r0 + ck, :])
            s_ref[j] = s
            for t in range(ck // LANES):
                part = s[:, t * LANES:(t + 1) * LANES]
                mx = part if mx is None else jnp.maximum(mx, part)
        m_ref[...] = jnp.broadcast_to(jnp.max(mx, axis=1, keepdims=True), (sb, LANES))

    def values(kv, i, slot):
        s_ref, m_ref = slots[slot]
        m = jnp.concatenate([m_ref[...]] * (ck // LANES), axis=1)
        acc = None
        for j, (_, vr, r0) in enumerate(chunks(kv)):
            p = jnp.exp(s_ref[j] - m).astype(BF16)
            part = _dot(p, vr[r0:r0 + ck, :])
            acc = part if acc is None else acc + part
        out = acc[:, :ATT_HEAD_DIM] / acc[:, ATT_HEAD_DIM:ATT_HEAD_DIM + 1]
        out = jnp.concatenate([out[g * tt:(g + 1) * tt] for g in range(ATT_GROUP)], axis=1)
        o_ref[0, token_rows(i), kv * gw:(kv + 1) * gw] = out.astype(o_ref.dtype)

    if len(slots) > 2:
        items = [(kv, i) for kv in range(kv_blk) for i in range(nsub)]
        for n, item in enumerate(items):
            scores(*item, n)
        for n, item in enumerate(items):
            values(*item, n)
        return

    unroll = ATT_LOOP_UNROLL
    for kv in range(kv_blk):
        scores(kv, 0, 0)

        def body(h, carry, kv=kv):
            for u in range(unroll):
                scores(kv, unroll * h + u + 1, (u + 1) % 2)
                values(kv, unroll * h + u, u % 2)
            return carry

        trips = (nsub - 1) // unroll
        lax.fori_loop(0, trips, body, 0)
        for n in range(trips * unroll, nsub - 1):
            scores(kv, n + 1, (n + 1) % 2)
            values(kv, n, n % 2)
        values(kv, nsub - 1, (nsub - 1) % 2)


def _attention(q, k, v, cache, ck, sb):
    b, seq, width = q.shape
    nkv = k.shape[0]
    gw = width // nkv
    nsub = seq * ATT_GROUP // sb
    flat = nkv * nsub <= ATT_MAX_UNROLL
    kv_blk = nkv if flat else 1
    n_slots = kv_blk * nsub if flat else 2
    lk = seq
    in_specs = [pl.BlockSpec((1, seq, gw * kv_blk), lambda i, j: (i, 0, j)),
                pl.BlockSpec((kv_blk, seq, ATT_HEAD_DIM), lambda i, j: (j, i, 0)),
                pl.BlockSpec((kv_blk, seq, LANES), lambda i, j: (j, i, 0))]
    args = [q, k, v]
    if cache is not None:
        past = cache[0].shape[2]
        assert past % ck == 0
        lk += past
        in_specs += [pl.BlockSpec((1, kv_blk, past, ATT_HEAD_DIM), lambda i, j: (i, j, 0, 0)),
                     pl.BlockSpec((1, kv_blk, past, LANES), lambda i, j: (i, j, 0, 0))]
        args += list(cache)
    assert (seq * ATT_GROUP) % (2 * sb) == 0 and seq % ck == 0
    return pl.pallas_call(
        functools.partial(_attn_kernel, ck=ck, sb=sb, has_cache=cache is not None, kv_blk=kv_blk),
        grid=(b, nkv // kv_blk),
        in_specs=in_specs,
        out_specs=pl.BlockSpec((1, seq, gw * kv_blk), lambda i, j: (i, 0, j)),
        out_shape=jax.ShapeDtypeStruct(q.shape, BF16),
        scratch_shapes=[pltpu.VMEM((n_slots, lk // ck, sb, ck), F32), pltpu.VMEM((n_slots, sb, LANES), F32)],
        compiler_params=_params(2),
        name="attention",
    )(*args)


def _ssd_group_rows(pair):
    g = pair // (SSD_HEADS // 2 // SSD_GROUPS)
    return slice(g * SSD_STATE, (g + 1) * SSD_STATE)


def _ssd_conv(cur_ref, prev_ref, next_ref, is_first, is_last, ext_ref, cw_ref, cb_ref, shift_ref):
    cn = SSD_CHUNK
    ext_ref[0:HALO, :] = jnp.where(is_first, 0.0, prev_ref[...])
    ext_ref[HALO:HALO + cn, :] = cur_ref[...]
    ext_ref[HALO + cn:, :] = jnp.where(is_last, 0.0, next_ref[...])
    shifted = _dot(shift_ref[...], ext_ref[...].astype(BF16))
    centre = SSD_CONV // 2
    u = cb_ref[...] + cw_ref[centre:centre + 1, :] * cur_ref[...]
    for n, t in enumerate(t for t in range(SSD_CONV) if t != centre):
        u = u + cw_ref[t:t + 1, :] * shifted[n * cn:(n + 1) * cn]
    return _silu(u)


def _ssd_scalars(dt_ref, dtb_ref, alog_ref, *, forward):
    cn = SSD_CHUNK
    nh = 2 * SSD_HEADS
    dt_t = _softplus(dt_ref[...].T[:nh] + dtb_ref[...])
    a_t = dt_t * (-jnp.exp(alog_ref[...]))
    ii = lax.broadcasted_iota(jnp.int32, (cn, cn), 0)
    jj = lax.broadcasted_iota(jnp.int32, (cn, cn), 1)
    keep = (ii >= jj) if forward else (ii <= jj)
    tri_t = ((ii <= jj) if forward else (ii >= jj)).astype(BF16)
    c3 = _dot(jnp.concatenate(_split3(a_t), axis=0), tri_t)
    cum_t = c3[:nh] + c3[nh:2 * nh] + c3[2 * nh:]
    cols = jnp.concatenate([dt_t, cum_t, jnp.zeros((LANES - 2 * nh, cn), F32)], axis=0).T
    lane = lax.broadcasted_iota(jnp.int32, (1, LANES), 1)
    return dict(cum_t=cum_t, cols=cols, keep=keep, lo=lane < SSD_HEAD_DIM, forward=forward)


def _ssd_mats(act, prep):
    lo = prep["lo"]
    xs = act[:, :SSD_WIDTH]
    bm = act[:, SSD_WIDTH:SSD_WIDTH + LANES]
    cm = act[:, SSD_WIDTH + LANES:]
    bm_b = bm.astype(BF16)
    gmat = []
    bg_b = []
    for g in range(SSD_GROUPS):
        in_group = lo if g == 0 else jnp.logical_not(lo)
        gmat.append(_dot_nt(jnp.where(in_group, cm, 0.0).astype(BF16), bm_b))
        bg_b.append(jnp.where(in_group, bm, 0.0).astype(BF16))
    return dict(prep, xs=xs, cm_b=cm.astype(BF16), gmat=gmat, bg_b=bg_b)


def _ssd_pair(prep, pair, s_ref, y_ref, dsk_ref):
    cn = SSD_CHUNK
    nh = 2 * SSD_HEADS
    forward, cols, cum_t, keep, lo = prep["forward"], prep["cols"], prep["cum_t"], prep["keep"], prep["lo"]

    def col(lane_idx):
        return jnp.broadcast_to(cols[:, lane_idx:lane_idx + 1], (cn, LANES))

    off = 0 if forward else SSD_HEADS
    tot_col = cn - 1 if forward else 0
    g = pair // (SSD_HEADS // 2 // SSD_GROUPS)
    gmat = prep["gmat"][g]
    h0 = off + 2 * pair
    h1 = h0 + 1
    ci0 = col(nh + h0)
    ci1 = col(nh + h1)
    cip = jnp.where(lo, ci0, ci1)
    m0 = (gmat * jnp.exp(jnp.where(keep, ci0 - cum_t[h0:h0 + 1, :], -1e30))).astype(BF16)
    m1 = (gmat * jnp.exp(jnp.where(keep, ci1 - cum_t[h1:h1 + 1, :], -1e30))).astype(BF16)
    xs_p = prep["xs"][:, pair * LANES:(pair + 1) * LANES]
    vp = xs_p * jnp.where(lo, col(h0), col(h1))
    v0 = jnp.where(lo, vp, 0.0).astype(BF16)
    v1 = jnp.where(lo, 0.0, vp).astype(BF16)
    s_old = s_ref[pair]
    y = (_dot(jnp.concatenate([m0, m1], axis=1), jnp.concatenate([v0, v1], axis=0))
         + jnp.exp(cip) * _dot(prep["cm_b"], s_old.astype(BF16)))
    totp = jnp.where(lo, cum_t[h0:h0 + 1, tot_col:tot_col + 1], cum_t[h1:h1 + 1, tot_col:tot_col + 1])
    s_ref[pair] = s_old * jnp.exp(totp) + _dot_tn(prep["bg_b"][g], (vp * jnp.exp(totp - cip)).astype(BF16))
    if forward:
        y = y + dsk_ref[:, pair * LANES:(pair + 1) * LANES] * xs_p
    y_ref[:, pair * LANES:(pair + 1) * LANES] = y.astype(y_ref.dtype)


def _ssd_kernel(*refs, nc, ns, has_init, want_fin):
    refs = list(refs)
    cf_ref, pf_ref, nf_ref, dtf_ref, cb_ref, pb_ref, nb_ref, dtb_ref = refs[:8]
    cw_ref, cbias_ref, dtbias_ref, alog_ref, dsk_ref, shift_ref = refs[8:14]
    pos = 14
    s0_ref = None
    if has_init:
        s0_ref = refs[pos]
        pos += 1
    yf_ref, yb_ref = refs[pos:pos + 2]
    pos += 2
    sfin_ref = None
    if want_fin:
        sfin_ref = refs[pos]
        pos += 1
    ext_ref, act_ref, sf_ref, sb_ref = refs[pos:]
    c = pl.program_id(1)
    c_fwd = c
    c_bwd = nc - 1 - c

    @pl.when(c == 0)
    def _():
        sf_ref[...] = jnp.zeros_like(sf_ref)
        sb_ref[...] = jnp.zeros_like(sb_ref)
        if has_init:
            for s in range(ns):
                for pair in range(SSD_HEADS // 2):
                    rows = _ssd_group_rows(pair)
                    sf_ref[s, pair, rows, :] = s0_ref[s, 0, pair]
                    sb_ref[s, pair, rows, :] = s0_ref[s, 1, pair]

    def scalars():
        out = []
        for s in range(ns):
            out.append(_ssd_scalars(dtf_ref.at[0, s], dtbias_ref, alog_ref, forward=True))
            out.append(_ssd_scalars(dtb_ref.at[0, s], dtbias_ref, alog_ref, forward=False))
        return out

    def streams(preps, acts):
        work = []
        for s in range(ns):
            work.append((_ssd_mats(acts[2 * s], preps[2 * s]), sf_ref.at[s], yf_ref.at[0, s]))
            work.append((_ssd_mats(acts[2 * s + 1], preps[2 * s + 1]), sb_ref.at[s], yb_ref.at[0, s]))
        for pair in range(SSD_HEADS // 2):
            for prep, s_ref, y_ref in work:
                _ssd_pair(prep, pair, s_ref, y_ref, dsk_ref)

    @pl.when(c < nc // 2)
    def _():
        preps = scalars()
        acts = []
        for s in range(ns):
            act_f = _ssd_conv(cf_ref.at[0, s], pf_ref.at[0, s], nf_ref.at[0, s], c_fwd == 0, c_fwd == nc - 1,
                              ext_ref.at[2 * s], cw_ref, cbias_ref, shift_ref)
            act_b = _ssd_conv(cb_ref.at[0, s], pb_ref.at[0, s], nb_ref.at[0, s], c_bwd == 0, c_bwd == nc - 1,
                              ext_ref.at[2 * s + 1], cw_ref, cbias_ref, shift_ref)
            act_ref[s, c_fwd] = act_f
            act_ref[s, c_bwd] = act_b
            acts += [act_f, act_b]
        streams(preps, acts)

    @pl.when(c >= nc // 2)
    def _():
        acts = []
        for s in range(ns):
            acts += [act_ref[s, c_fwd], act_ref[s, c_bwd]]
        streams(scalars(), acts)

    if want_fin:
        @pl.when(c == nc - 1)
        def _():
            for s in range(ns):
                for pair in range(SSD_HEADS // 2):
                    rows = _ssd_group_rows(pair)
                    for d, st_ref in enumerate((sf_ref, sb_ref)):
                        both = st_ref[s, pair, rows, :]
                        sfin_ref[s, d, 2 * pair] = both[:, :SSD_HEAD_DIM]
                        sfin_ref[s, d, 2 * pair + 1] = both[:, SSD_HEAD_DIM:]


def _ssd(xbc, dt, p, s0, batch, want_fin):
    m = xbc.shape[0]
    cn = SSD_CHUNK
    ns = SSD_SEQS
    seq = m // batch
    nc = seq // cn
    assert nc % 2 == 0
    assert batch % ns == 0
    per = cn // HALO
    n_halo = seq // HALO
    has_init = s0 is not None
    npair = SSD_HEADS // 2
    view = lambda a: a.reshape(batch // ns, ns, seq, a.shape[-1])

    def fwd(c):
        return c

    def bwd(c):
        return nc - 1 - c

    def stream_specs(chunk):
        return [pl.BlockSpec((1, ns, cn, SSD_XBC), lambda b, c: (b, 0, chunk(c), 0)),
                pl.BlockSpec((1, ns, HALO, SSD_XBC), lambda b, c: (b, 0, jnp.maximum(chunk(c) * per - 1, 0), 0)),
                pl.BlockSpec((1, ns, HALO, SSD_XBC),
                             lambda b, c: (b, 0, jnp.minimum(chunk(c) * per + per, n_halo - 1), 0)),
                pl.BlockSpec((1, ns, cn, LANES), lambda b, c: (b, 0, chunk(c), 0))]

    in_specs = stream_specs(fwd) + stream_specs(bwd) + [
        _const_spec((HALO, SSD_XBC)), _const_spec((1, SSD_XBC)), _const_spec((2 * SSD_HEADS, cn)),
        _const_spec((2 * SSD_HEADS, cn)), _const_spec((1, SSD_WIDTH)),
        _const_spec(((SSD_CONV - 1) * cn, cn + 2 * HALO))]
    taps = [t for t in range(SSD_CONV) if t != SSD_CONV // 2]
    src = np.concatenate([HALO + np.arange(cn) + (t - SSD_CONV // 2) for t in taps])
    shift = jnp.asarray(src[:, None] == np.arange(cn + 2 * HALO)[None, :], BF16)
    xv, dv = view(xbc), view(dt)
    args = [xv, xv, xv, dv, xv, xv, xv, dv, p["conv_w"], p["conv_b"], p["dt_bias"], p["a_log"], p["d_skip"], shift]
    state_block = (ns, 2, npair, SSD_STATE, LANES)
    if has_init:
        in_specs.append(pl.BlockSpec(state_block, lambda b, c: (b, 0, 0, 0, 0)))
        args.append(s0)
    out_specs = [pl.BlockSpec((1, ns, cn, SSD_WIDTH), lambda b, c: (b, 0, fwd(c), 0)),
                 pl.BlockSpec((1, ns, cn, SSD_WIDTH), lambda b, c: (b, 0, bwd(c), 0))]
    out_shape = [jax.ShapeDtypeStruct((batch // ns, ns, seq, SSD_WIDTH), BF16)] * 2
    if want_fin:
        fin_block = (ns, 2, SSD_HEADS, SSD_STATE, SSD_HEAD_DIM)
        out_specs.append(pl.BlockSpec(fin_block, lambda b, c: (b, 0, 0, 0, 0)))
        out_shape.append(jax.ShapeDtypeStruct((batch,) + fin_block[1:], F32))
    out = pl.pallas_call(
        functools.partial(_ssd_kernel, nc=nc, ns=ns, has_init=has_init, want_fin=want_fin),
        grid=(batch // ns, nc),
        in_specs=in_specs,
        out_specs=out_specs,
        out_shape=out_shape,
        scratch_shapes=[pltpu.VMEM((2 * ns, cn + 2 * HALO, SSD_XBC), F32), pltpu.VMEM((ns, nc, cn, SSD_XBC), F32),
                        pltpu.VMEM((ns, npair, LANES, LANES), F32), pltpu.VMEM((ns, npair, LANES, LANES), F32)],
        compiler_params=_params(2),
        name="ssd",
    )(*args)
    return [out[0].reshape(m, SSD_WIDTH), out[1].reshape(m, SSD_WIDTH)] + list(out[2:])


def _ssd_state_to_pairs(s):
    b = s.shape[0]
    npair = SSD_HEADS // 2
    s = s.reshape(b, 2, npair, 2, SSD_STATE, SSD_HEAD_DIM).transpose(0, 1, 2, 4, 3, 5)
    return s.reshape(b, 2, npair, SSD_STATE, 2 * SSD_HEAD_DIM)


def _l1_in_kernel(*refs, use_rope):
    if use_rope:
        x_ref, nw_ref, sh_ref, sc_ref, w_ref, cos_ref, sin_ref, q_ref, k_ref, v_ref, g_ref = refs
    else:
        x_ref, nw_ref, sh_ref, sc_ref, w_ref, q_ref, k_ref, v_ref, g_ref = refs
    h = (_rms(x_ref[...]) * nw_ref[...] * (1.0 + sc_ref[0]) + sh_ref[0]).astype(BF16)
    half = RET_DK // 2

    def rope(t):
        if not use_rope:
            return t
        cos = cos_ref[...]
        sin = sin_ref[...]
        parts = []
        for hd in range(RET_HEADS):
            x1 = t[:, hd * RET_DK:hd * RET_DK + half]
            x2 = t[:, hd * RET_DK + half:(hd + 1) * RET_DK]
            parts += [x1 * cos - x2 * sin, x2 * cos + x1 * sin]
        return jnp.concatenate(parts, axis=1)

    o1 = RET_QK_WIDTH
    o2 = 2 * RET_QK_WIDTH
    o3 = o2 + RET_V_WIDTH
    q_ref[...] = rope(_dot(h, w_ref[:, :o1])).astype(BF16)
    k_ref[...] = (rope(_dot(h, w_ref[:, o1:o2])) * (RET_DK ** -0.5)).astype(k_ref.dtype)
    v_ref[...] = _dot(h, w_ref[:, o2:o3]).astype(BF16)
    g_ref[...] = _silu(_dot(h, w_ref[:, o3:])).astype(g_ref.dtype)


def _l1_in(x, mod, rows_per_mod, p, rope, tm):
    m = x.shape[0]
    use_rope = rope is not None
    row = lambda w: pl.BlockSpec((tm, w), lambda i: (i, 0))
    n = 2 * RET_QK_WIDTH + 2 * RET_V_WIDTH
    in_specs = ([row(D_MODEL), _const_spec((1, D_MODEL))] + _mod_specs(tm, rows_per_mod, (0, 1))
                + [_resident_spec((D_MODEL, n))])
    args = [x, p["norm_mix"], mod, mod, p["w_in"]]
    if use_rope:
        rows = rope[0].shape[0]
        in_specs += [pl.BlockSpec((tm, LANES), lambda i: (i % (rows // tm), 0))] * 2
        args += list(rope)
    widths = [(RET_QK_WIDTH, BF16), (RET_QK_WIDTH, BF16), (RET_V_WIDTH, BF16), (RET_V_WIDTH, BF16)]
    return pl.pallas_call(
        functools.partial(_l1_in_kernel, use_rope=use_rope),
        grid=(m // tm,),
        in_specs=in_specs,
        out_specs=[row(w) for w, _ in widths],
        out_shape=[jax.ShapeDtypeStruct((m, w), dt) for w, dt in widths],
        compiler_params=_params(1),
        name="l1_in",
    )(*args)


def _ret_kernel(*refs, cn, nc, has_init, want_fin):
    stateless = (not has_init) and nc == 1
    refs = list(refs)
    dec_ref, qf_ref, kf_ref, vf_ref = refs[:4]
    pos = 4
    if stateless:
        qb_ref, kb_ref, vb_ref = qf_ref, kf_ref, vf_ref
    else:
        qb_ref, kb_ref, vb_ref = refs[pos:pos + 3]
        pos += 3
    s0_ref = None
    if has_init:
        s0_ref = refs[pos]
        pos += 1
    yf_ref = refs[pos]
    pos += 1
    yb_ref = None
    if not stateless:
        yb_ref = refs[pos]
        pos += 1
    sfin_ref = None
    if want_fin:
        sfin_ref = refs[pos]
        pos += 1
    decay_ref, rowdec_ref, sf_ref, sb_ref = refs[pos:]
    c = pl.program_id(1)
    log_g = -jnp.exp(dec_ref[...])

    @pl.when((pl.program_id(0) == 0) & (c == 0))
    def _():
        ii = lax.broadcasted_iota(jnp.int32, (cn, cn), 0)
        jj = lax.broadcasted_iota(jnp.int32, (cn, cn), 1)
        dist = (ii - jj).astype(F32)
        ri = lax.broadcasted_iota(jnp.int32, (cn, LANES), 0).astype(F32)
        for hd in range(RET_HEADS):
            gf = log_g[0:1, hd:hd + 1]
            gb = log_g[1:2, hd:hd + 1]
            decay_ref[hd] = (jnp.where(dist >= 0, jnp.exp(gf * jnp.maximum(dist, 0.0)), 0.0)
                             + jnp.where(dist <= 0, jnp.exp(gb * jnp.maximum(-dist, 0.0)), 0.0))
            rowdec_ref[hd, 0] = jnp.exp(gf * (ri + 1.0))
            rowdec_ref[hd, 1] = jnp.exp(gf * (cn - 1.0 - ri))
            rowdec_ref[hd, 2] = jnp.exp(gb * (cn - ri))
            rowdec_ref[hd, 3] = jnp.exp(gb * ri)

    def rowdec(hd, which, width):
        return jnp.concatenate([rowdec_ref[hd, which]] * (width // LANES), axis=1)

    if not stateless:
        @pl.when(c == 0)
        def _():
            if has_init:
                sf_ref[...] = s0_ref[0, 0]
                sb_ref[...] = s0_ref[0, 1]
            else:
                sf_ref[...] = jnp.zeros_like(sf_ref)
                sb_ref[...] = jnp.zeros_like(sb_ref)

    for hd in range(RET_HEADS):
        gf = log_g[0:1, hd:hd + 1]
        gb = log_g[1:2, hd:hd + 1]
        qs = slice(hd * RET_DK, (hd + 1) * RET_DK)
        vs = slice(hd * RET_DV, (hd + 1) * RET_DV)
        q = qf_ref[:, qs]
        k = kf_ref[:, qs]
        v = vf_ref[:, vs]
        y = _dot((_dot_nt(q, k.astype(BF16)) * decay_ref[hd]).astype(BF16), v)
        upd_f = _dot_tn((k * rowdec(hd, 1, RET_DK)).astype(BF16), v)
        if stateless:
            new_f = upd_f
        else:
            s_old = sf_ref[hd]
            y = y + rowdec(hd, 0, RET_DV) * _dot(q, s_old.astype(BF16))
            new_f = s_old * jnp.exp(gf * cn) + upd_f
            sf_ref[hd] = new_f
        yf_ref[:, vs] = y.astype(yf_ref.dtype)
        q = qb_ref[:, qs]
        k = kb_ref[:, qs]
        v = vb_ref[:, vs]
        upd_b = _dot_tn((k * rowdec(hd, 3, RET_DK)).astype(BF16), v)
        if stateless:
            new_b = upd_b
        else:
            s_old = sb_ref[hd]
            yb_ref[:, vs] = (rowdec(hd, 2, RET_DV) * _dot(q, s_old.astype(BF16))).astype(yb_ref.dtype)
            new_b = s_old * jnp.exp(gb * cn) + upd_b
            sb_ref[hd] = new_b
        if want_fin:
            if stateless:
                sfin_ref[0, 0, hd] = new_f
                sfin_ref[0, 1, hd] = new_b
            else:
                @pl.when(c == nc - 1)
                def _(new_f=new_f, new_b=new_b, hd=hd):
                    sfin_ref[0, 0, hd] = new_f
                    sfin_ref[0, 1, hd] = new_b


def _retention(q, k, v, decay, s0, batch, cn, want_fin):
    m = q.shape[0]
    nc = m // batch // cn
    has_init = s0 is not None
    stateless = (not has_init) and nc == 1

    def fwd(b, c):
        return b * nc + c

    def bwd(b, c):
        return b * nc + nc - 1 - c

    def stream_specs(chunk):
        return [pl.BlockSpec((cn, RET_QK_WIDTH), lambda b, c: (chunk(b, c), 0)),
                pl.BlockSpec((cn, RET_QK_WIDTH), lambda b, c: (chunk(b, c), 0)),
                pl.BlockSpec((cn, RET_V_WIDTH), lambda b, c: (chunk(b, c), 0))]

    in_specs = [_const_spec((8, LANES))] + stream_specs(fwd)
    args = [decay, q, k, v]
    if not stateless:
        in_specs += stream_specs(bwd)
        args += [q, k, v]
    state_block = (1, 2, RET_HEADS, RET_DK, RET_DV)
    if has_init:
        in_specs.append(pl.BlockSpec(state_block, lambda b, c: (b, 0, 0, 0, 0)))
        args.append(s0)
    out_specs = [pl.BlockSpec((cn, RET_V_WIDTH), lambda b, c: (fwd(b, c), 0))]
    if not stateless:
        out_specs.append(pl.BlockSpec((cn, RET_V_WIDTH), lambda b, c: (bwd(b, c), 0)))
    out_shape = [jax.ShapeDtypeStruct((m, RET_V_WIDTH), BF16)] * len(out_specs)
    if want_fin:
        out_specs.append(pl.BlockSpec(state_block, lambda b, c: (b, 0, 0, 0, 0)))
        out_shape.append(jax.ShapeDtypeStruct((batch,) + state_block[1:], F32))
    return pl.pallas_call(
        functools.partial(_ret_kernel, cn=cn, nc=nc, has_init=has_init, want_fin=want_fin),
        grid=(batch, nc),
        in_specs=in_specs,
        out_specs=out_specs,
        out_shape=out_shape,
        scratch_shapes=[pltpu.VMEM((RET_HEADS, cn, cn), F32), pltpu.VMEM((RET_HEADS, 4, cn, LANES), F32)]
        + [pltpu.VMEM((RET_HEADS, RET_DK, RET_DV), F32)] * 2,
        compiler_params=_params(2),
        name="retention",
    )(*args)


def _post_kernel(*refs, mixer, final):
    refs = list(refs)
    x_ref, g1_ref, nw_ref, sh2_ref, sc2_ref, g2_ref, wout_ref, wg_ref, wu_ref, wd_ref = refs[:10]
    pos = 10
    fn_ref = None
    if final:
        fn_ref = refs[pos]
        pos += 1
    if mixer == "ab":
        att_ref, yf_ref, yb_ref, z_ref, gain_ref, o_ref = refs[pos:]
        y = (yf_ref[...].astype(F32) + yb_ref[...].astype(F32)) * _silu(z_ref[...].astype(F32))
        y = _rms(y) * gain_ref[...]
        mix = _dot(jnp.concatenate([att_ref[...], y.astype(BF16)], axis=1), wout_ref[...])
    else:
        parts = refs[pos:-3]
        gate_ref, gain_ref, o_ref = refs[-3:]
        mix = None
        for hd in range(RET_HEADS):
            vs = slice(hd * RET_DV, (hd + 1) * RET_DV)
            y = parts[0][:, vs].astype(F32)
            for extra in parts[1:]:
                y = y + extra[:, vs].astype(F32)
            y = _rms(y) * gain_ref[:, vs]
            part = _dot((gate_ref[:, vs].astype(F32) * y).astype(BF16), wout_ref[vs, :])
            mix = part if mix is None else mix + part
    x1 = x_ref[...] + g1_ref[0] * mix
    h = (_rms(x1) * nw_ref[...] * (1.0 + sc2_ref[0]) + sh2_ref[0]).astype(BF16)
    act = (_silu(_dot(h, wg_ref[...])) * _dot(h, wu_ref[...])).astype(BF16)
    x2 = x1 + g2_ref[0] * _dot(act, wd_ref[...])
    if final:
        x2 = _rms(x2) * fn_ref[...]
    o_ref[...] = x2


def _post(x, mod, rows_per_mod, p, mixer, mix_inputs, gain, final_norm, tm):
    m = x.shape[0]
    row = lambda w: pl.BlockSpec((tm, w), lambda i: (i, 0))
    mixw = p["w_out"].shape[0]
    (g1,) = _mod_specs(tm, rows_per_mod, (2,))
    sh2, sc2, g2 = _mod_specs(tm, rows_per_mod, (3, 4, 5))
    in_specs = [row(D_MODEL), g1, _const_spec((1, D_MODEL)), sh2, sc2, g2,
                _resident_spec((mixw, D_MODEL)), _resident_spec((D_MODEL, D_FF)), _resident_spec((D_MODEL, D_FF)),
                _resident_spec((D_FF, D_MODEL))]
    args = [x, mod, p["norm_ffn"], mod, mod, mod, p["w_out"], p["w_gate"], p["w_up"], p["w_down"]]
    final = final_norm is not None
    if final:
        in_specs.append(_const_spec((1, D_MODEL)))
        args.append(final_norm)
    in_specs += [row(a.shape[1]) for a in mix_inputs] + [_const_spec(gain.shape)]
    args += list(mix_inputs) + [gain]
    return pl.pallas_call(
        functools.partial(_post_kernel, mixer=mixer, final=final),
        grid=(m // tm,),
        in_specs=in_specs,
        out_specs=row(D_MODEL),
        out_shape=jax.ShapeDtypeStruct((m, D_MODEL), F32),
        compiler_params=_params(1),
        name="post_" + mixer,
    )(*args)


def _axial_angles(n_tokens, dim):
    rows = n_tokens // GRID_W
    row = np.repeat(np.arange(rows), GRID_W).astype(np.float64)
    col = np.tile(np.arange(GRID_W), rows).astype(np.float64)
    n_freq = dim // 4
    inv = ROPE_THETA ** (-np.arange(n_freq, dtype=np.float64) / n_freq)
    return np.concatenate([row[:, None] * inv, col[:, None] * inv], axis=-1)


def _head_mean_matrix(width, head):
    idx = jnp.arange(width) // head
    return jnp.where(idx[:, None] == idx[None, :], 1.0 / head, 0.0).astype(BF16)


def _rows_bcast(v, width):
    return jnp.broadcast_to(v.reshape(-1, 1), (v.size, width))


def _trunk(x, mods, rows_per_mod, p0, p1, final_norm, rope_att, rope_ret, caches, seq):
    m = x.shape[0]
    batch = m // seq
    sample = caches is not None
    tm = ROW_TILE
    l0 = _l0_in(x, mods[0], rows_per_mod, p0, rope_att, 2 * tm, None if sample else seq)
    q, ka, va, z, xbc, dt = l0[:6]
    s0_ssd = None
    s0_ret = None
    kv_cache = None
    if sample:
        cache_k, cache_v, state_ssd, state_ret = caches
        ck = cache_k.astype(BF16).transpose(0, 2, 1, 3)
        cv = cache_v.astype(BF16).transpose(0, 2, 1, 3)
        ones_col = jnp.zeros(cv.shape[:3] + (LANES - ATT_HEAD_DIM,), BF16).at[..., 0].set(1.0)
        kv_cache = (ck, jnp.concatenate([cv, ones_col], axis=-1))
        s0_ssd = _ssd_state_to_pairs(state_ssd)
        s0_ret = state_ret
    att = _attention(q.reshape(batch, seq, ATT_WIDTH), ka, va, kv_cache, ATT_KEY_CHUNK, ATT_SUB_ROWS)
    att = att.reshape(m, ATT_WIDTH)
    ssd_out = _ssd(xbc, dt, p0, s0_ssd, batch, want_fin=not sample)
    x = _post(x, mods[0], rows_per_mod, p0, "ab", [att, ssd_out[0], ssd_out[1], z], p0["ssd_gain"], None, tm)
    q1, k1, v1, g1 = _l1_in(x, mods[1], rows_per_mod, p1, rope_ret, 2 * tm)
    ret_out = _retention(q1, k1, v1, p1["decay"], s0_ret, batch, RET_CHUNK, want_fin=not sample)
    y_parts = ret_out if sample else ret_out[:-1]
    y = _post(x, mods[1], rows_per_mod, p1, "c", list(y_parts) + [g1], p1["ret_gain"], final_norm, tm)
    if sample:
        return y, None
    new_k = l0[6].transpose(0, 3, 1, 2)
    new_v = l0[7].transpose(0, 3, 1, 2)
    return y, (new_k, new_v, ssd_out[2], ret_out[-1])


def kernel(x_prompt, x_sample, c, cache_k0, cache_v0, state_ssd0, state_ret1, c_ctx, l0_w_ada, l0_b_ada, l0_norm_mix, l0_norm_ffn, l0_w_in, l0_w_out, l0_q_gain, l0_k_gain, l0_conv_w, l0_conv_b, l0_dt_bias, l0_a_log, l0_d_skip, l0_ssd_gain, l0_w_gate, l0_w_up, l0_w_down, l1_w_ada, l1_b_ada, l1_norm_mix, l1_norm_ffn, l1_w_in, l1_w_out, l1_decay, l1_ret_gain, l1_w_gate, l1_w_up, l1_w_down, final_norm):
    b_ctx, seq_ctx, d = x_prompt.shape
    b_lat, seq_lat, _ = x_sample.shape
    assert d == D_MODEL and l0_w_in.shape == (D_MODEL, L0_IN) and l0_w_gate.shape == (D_MODEL, D_FF)
    row = lambda v: v.reshape(1, -1)

    (w0_in, w0_out, w0_gate, w0_up, w0_down, w1_in, w1_out, w1_gate, w1_up, w1_down) = _cast_weights(
        [l0_w_in, l0_w_out, l0_w_gate, l0_w_up, l0_w_down, l1_w_in, l1_w_out, l1_w_gate, l1_w_up, l1_w_down])
    p0 = dict(
        norm_mix=row(l0_norm_mix), norm_ffn=row(l0_norm_ffn),
        w_in=w0_in, w_out=w0_out, w_gate=w0_gate, w_up=w0_up, w_down=w0_down,
        q_gain=row(jnp.tile(l0_q_gain, ATT_HEADS)), k_gain=row(jnp.tile(l0_k_gain, ATT_KV_HEADS)),
        pq=_head_mean_matrix(ATT_WIDTH, ATT_HEAD_DIM), pk=_head_mean_matrix(ATT_KV_WIDTH, ATT_HEAD_DIM),
        conv_w=jnp.pad(l0_conv_w, ((0, HALO - SSD_CONV), (0, 0))), conv_b=row(l0_conv_b),
        dt_bias=_rows_bcast(l0_dt_bias, SSD_CHUNK), a_log=_rows_bcast(l0_a_log, SSD_CHUNK),
        d_skip=row(jnp.repeat(l0_d_skip, SSD_HEAD_DIM)), ssd_gain=row(l0_ssd_gain),
    )
    p1 = dict(
        norm_mix=row(l1_norm_mix), norm_ffn=row(l1_norm_ffn),
        w_in=w1_in, w_out=w1_out, w_gate=w1_gate, w_up=w1_up, w_down=w1_down,
        decay=jnp.pad(l1_decay, ((0, 8 - l1_decay.shape[0]), (0, LANES - l1_decay.shape[1]))),
        ret_gain=row(l1_ret_gain),
    )
    fnorm = row(final_norm)

    n_cond = 8
    conds = jnp.concatenate([c_ctx[None, :], c, jnp.zeros((n_cond - 1 - b_lat, d), F32)], axis=0)
    mod0 = _ada(conds, l0_w_ada, l0_b_ada)
    mod1 = _ada(conds, l1_w_ada, l1_b_ada)
    mods_ctx = [mod[0:1].reshape(1, 1, 6 * d) for mod in (mod0, mod1)]
    mods_lat = [mod[1:1 + b_lat].reshape(b_lat, 1, 6 * d) for mod in (mod0, mod1)]

    m_ctx = b_ctx * seq_ctx
    y_prompt, ctx = _trunk(x_prompt.reshape(m_ctx, d), mods_ctx, m_ctx, p0, p1, fnorm, None, None, None, seq_ctx)
    new_k0, new_v0, new_ssd0, new_ret1 = ctx
    ang = _axial_angles(seq_lat, ATT_HEAD_DIM)
    cos, sin = np.cos(ang), np.sin(ang)
    reps = LANES // ATT_HEAD_DIM
    rope_att = (jnp.asarray(np.tile(np.concatenate([cos, cos], axis=1), (1, reps)), F32),
                jnp.asarray(np.tile(np.concatenate([-sin, sin], axis=1), (1, reps)), F32))
    ang = _axial_angles(seq_lat, RET_DK)
    rope_ret = (jnp.asarray(np.cos(ang), F32), jnp.asarray(np.sin(ang), F32))
    caches = (cache_k0, cache_v0, state_ssd0, state_ret1)
    y_sample, _ = _trunk(x_sample.reshape(b_lat * seq_lat, d), mods_lat, seq_lat, p0, p1, fnorm, rope_att, rope_ret,
                         caches, seq_lat)
    return (y_prompt.reshape(b_ctx, seq_ctx, d), y_sample.reshape(b_lat, seq_lat, d),
            new_k0, new_v0, new_ssd0, new_ret1)
```

```python
import functools

import jax
import jax.numpy as jnp
import numpy as np
from jax import lax
from jax.experimental import pallas as pl
from jax.experimental.pallas import tpu as pltpu

F32 = jnp.float32
BF16 = jnp.bfloat16

EPS = 1e-6
ROPE_THETA = 10000.0
GRID_W = 64
D_MODEL = 1024
ATT_HEAD_DIM = 64
ATT_HEADS = 8
ATT_KV_HEADS = 2
ATT_GROUP = ATT_HEADS // ATT_KV_HEADS
ATT_WIDTH = ATT_HEADS * ATT_HEAD_DIM
ATT_KV_WIDTH = ATT_KV_HEADS * ATT_HEAD_DIM
SSD_WIDTH = 512
SSD_HEADS = 8
SSD_HEAD_DIM = 64
SSD_STATE = 64
SSD_GROUPS = 2
SSD_CONV = 5
SSD_XBC = SSD_WIDTH + 2 * SSD_GROUPS * SSD_STATE
L0_IN = ATT_WIDTH + 2 * ATT_KV_WIDTH + SSD_WIDTH + SSD_XBC + 2 * SSD_HEADS
RET_HEADS = 4
RET_DK = 256
RET_DV = 512
RET_QK_WIDTH = RET_HEADS * RET_DK
RET_V_WIDTH = RET_HEADS * RET_DV
D_FF = 2816

LANES = 128
BF16_SUBLANES = 16
HALO = 8
L0_IN_PAD = -(-L0_IN // LANES) * LANES
MXU_WIDTH = 256
ROW_TILE = 512
ADA_COLS = 1536
CAST_STEPS = 16
ATT_KEY_CHUNK = MXU_WIDTH
ATT_SUB_ROWS = 512
ATT_MAX_UNROLL = 8
ATT_LOOP_UNROLL = 6
SSD_CHUNK = 128
SSD_SEQS = 2
RET_CHUNK = 256
VMEM_LIMIT = 56 * 1024 * 1024


def _dot(a, b):
    return jnp.dot(a, b, preferred_element_type=F32)


def _dot_nt(a, b):
    return lax.dot_general(a, b, (((1,), (1,)), ((), ())), preferred_element_type=F32)


def _dot_tn(a, b):
    return lax.dot_general(a, b, (((0,), (0,)), ((), ())), preferred_element_type=F32)


def _silu(x):
    half = 0.5 * x
    return half + half * jnp.tanh(half)


def _softplus(x):
    return jnp.maximum(x, 0.0) + jnp.log1p(jnp.exp(-jnp.abs(x)))


def _rms(x):
    return x * lax.rsqrt(jnp.mean(x * x, axis=-1, keepdims=True) + EPS)


def _split3(x):
    hi = x.astype(BF16)
    r = x - hi.astype(F32)
    mid = r.astype(BF16)
    lo = (r - mid.astype(F32)).astype(BF16)
    return hi, mid, lo


def _const_spec(shape):
    return pl.BlockSpec(shape, lambda *_: (0,) * len(shape))


def _resident_spec(shape):
    return pl.BlockSpec(shape, lambda *_: (0,) * len(shape), pipeline_mode=pl.Buffered(1))


def _params(n_axes):
    return pltpu.CompilerParams(dimension_semantics=("arbitrary",) * n_axes, vmem_limit_bytes=VMEM_LIMIT)


def _cast_kernel(*refs):
    n = len(refs) // 2
    for x_ref, o_ref in zip(refs[:n], refs[n:]):
        width = x_ref.shape[1]
        if o_ref.shape[1] == width:
            o_ref[...] = x_ref[...].astype(BF16)
        else:
            o_ref[:, :width] = x_ref[...].astype(BF16)
            o_ref[:, width:] = jnp.zeros((o_ref.shape[0], o_ref.shape[1] - width), BF16)


def _cast_weights(weights):
    steps = CAST_STEPS
    in_specs, out_specs, out_shape = [], [], []
    for w in weights:
        rows, width = w.shape
        assert rows % (steps * BF16_SUBLANES) == 0
        padded = -(-width // LANES) * LANES
        in_specs.append(pl.BlockSpec((rows // steps, width), lambda i: (i, 0)))
        out_specs.append(pl.BlockSpec((rows // steps, padded), lambda i: (i, 0)))
        out_shape.append(jax.ShapeDtypeStruct((rows, padded), BF16))
    return pl.pallas_call(
        _cast_kernel,
        grid=(steps,),
        in_specs=in_specs,
        out_specs=out_specs,
        out_shape=out_shape,
        compiler_params=_params(1),
        name="cast_weights",
    )(*weights)


def _ada_kernel(c_ref, w_ref, b_ref, o_ref):
    s = _silu(c_ref[...])
    o_ref[...] = _dot(s.astype(BF16), w_ref[...].astype(BF16)) + b_ref[...]


def _ada(conds, w, b):
    n = w.shape[1]
    tn = ADA_COLS
    return pl.pallas_call(
        _ada_kernel,
        grid=(n // tn,),
        in_specs=[_const_spec(conds.shape),
                  pl.BlockSpec((D_MODEL, tn), lambda j: (0, j)),
                  pl.BlockSpec((1, tn), lambda j: (0, j))],
        out_specs=pl.BlockSpec((conds.shape[0], tn), lambda j: (0, j)),
        out_shape=jax.ShapeDtypeStruct((conds.shape[0], n), F32),
        compiler_params=_params(1),
        name="ada",
    )(conds, w, b.reshape(1, n))


def _mod_specs(tm, rows_per_mod, which):
    return [pl.BlockSpec((1, 1, D_MODEL), lambda i, j=j: ((i * tm) // rows_per_mod, 0, j)) for j in which]


def _head_rms(x, p_ref, gain):
    x2 = x * x
    hi = x2.astype(BF16)
    lo = (x2 - hi.astype(F32)).astype(BF16)
    ms = _dot(hi, p_ref[...]) + _dot(lo, p_ref[...])
    return x * lax.rsqrt(ms + EPS) * gain


def _rope64(x, cos, sin):
    n = x.shape[1]
    lane = lax.broadcasted_iota(jnp.int32, x.shape, 1)
    first_half = (lane % ATT_HEAD_DIM) < (ATT_HEAD_DIM // 2)
    partner = jnp.where(first_half, pltpu.roll(x, n - ATT_HEAD_DIM // 2, 1), pltpu.roll(x, ATT_HEAD_DIM // 2, 1))
    return x * cos + partner * sin


def _l0_in_kernel(*refs, use_rope, ctx_seq):
    refs = list(refs)
    x_ref, nw_ref, sh_ref, sc_ref, w_ref, qg_ref, kg_ref, pq_ref, pk_ref = refs[:9]
    pos = 9
    if use_rope:
        cos_ref, sin_ref = refs[pos:pos + 2]
        pos += 2
    q_ref, ka_ref, va_ref, z_ref, xbc_ref, dt_ref = refs[pos:pos + 6]
    pos += 6
    h = _rms(x_ref[...]) * nw_ref[...] * (1.0 + sc_ref[0]) + sh_ref[0]
    proj = _dot(h.astype(BF16), w_ref[...])
    o1 = ATT_WIDTH
    o2 = o1 + ATT_KV_WIDTH
    o3 = o2 + ATT_KV_WIDTH
    o4 = o3 + SSD_WIDTH
    o5 = o4 + SSD_XBC
    q = _head_rms(proj[:, :o1], pq_ref, qg_ref[...])
    k = _head_rms(proj[:, o1:o2], pk_ref, kg_ref[...])
    v = proj[:, o2:o3]
    if ctx_seq:
        kt_ref, vt_ref = refs[pos:]
        for s in range(x_ref.shape[0] // ctx_seq):
            kt = k[s * ctx_seq:(s + 1) * ctx_seq].T
            vt = v[s * ctx_seq:(s + 1) * ctx_seq].T
            for kv in range(ATT_KV_HEADS):
                kt_ref[s, kv] = kt[kv * ATT_HEAD_DIM:(kv + 1) * ATT_HEAD_DIM]
                vt_ref[s, kv] = vt[kv * ATT_HEAD_DIM:(kv + 1) * ATT_HEAD_DIM]
    if use_rope:
        cos = cos_ref[...]
        sin = sin_ref[...]
        k = _rope64(k, cos, sin)
        reps = ATT_WIDTH // LANES
        q = _rope64(q, jnp.concatenate([cos] * reps, axis=1), jnp.concatenate([sin] * reps, axis=1))
    q_ref[...] = (q * (ATT_HEAD_DIM ** -0.5)).astype(BF16)
    kb = k.astype(BF16)
    lane = lax.broadcasted_iota(jnp.int32, (1, LANES), 1)
    ones_col = jnp.where(lane == ATT_HEAD_DIM, 1.0, 0.0)
    for kv in range(ATT_KV_HEADS):
        ka_ref[kv] = kb[:, kv * ATT_HEAD_DIM:(kv + 1) * ATT_HEAD_DIM]
        vv = v if kv == 0 else pltpu.roll(v, (LANES - kv * ATT_HEAD_DIM) % LANES, 1)
        va_ref[kv] = jnp.where(lane < ATT_HEAD_DIM, vv, ones_col).astype(BF16)
    z_ref[...] = proj[:, o3:o4].astype(z_ref.dtype)
    xbc_ref[...] = proj[:, o4:o5]
    dt_ref[...] = proj[:, o5:]


def _l0_in(x, mod, rows_per_mod, p, rope, tm, ctx_seq):
    m = x.shape[0]
    use_rope = rope is not None
    row = lambda w: pl.BlockSpec((tm, w), lambda i: (i, 0))
    in_specs = ([row(D_MODEL), _const_spec((1, D_MODEL))] + _mod_specs(tm, rows_per_mod, (0, 1))
                + [_resident_spec((D_MODEL, L0_IN_PAD)), _const_spec((1, ATT_WIDTH)), _const_spec((1, ATT_KV_WIDTH)),
                   _resident_spec((ATT_WIDTH, ATT_WIDTH)), _resident_spec((ATT_KV_WIDTH, ATT_KV_WIDTH))])
    args = [x, p["norm_mix"], mod, mod, p["w_in"], p["q_gain"], p["k_gain"], p["pq"], p["pk"]]
    if use_rope:
        rows = rope[0].shape[0]
        in_specs += [pl.BlockSpec((tm, LANES), lambda i: (i % (rows // tm), 0))] * 2
        args += list(rope)
    head = lambda w: pl.BlockSpec((ATT_KV_HEADS, tm, w), lambda i: (0, i, 0))
    out_specs = [row(ATT_WIDTH), head(ATT_HEAD_DIM), head(LANES), row(SSD_WIDTH), row(SSD_XBC), row(LANES)]
    out_shape = [jax.ShapeDtypeStruct((m, ATT_WIDTH), BF16),
                 jax.ShapeDtypeStruct((ATT_KV_HEADS, m, ATT_HEAD_DIM), BF16),
                 jax.ShapeDtypeStruct((ATT_KV_HEADS, m, LANES), BF16),
                 jax.ShapeDtypeStruct((m, SSD_WIDTH), BF16), jax.ShapeDtypeStruct((m, SSD_XBC), F32),
                 jax.ShapeDtypeStruct((m, LANES), F32)]
    if ctx_seq:
        assert tm % ctx_seq == 0
        cache_block = (tm // ctx_seq, ATT_KV_HEADS, ATT_HEAD_DIM, ctx_seq)
        out_specs += [pl.BlockSpec(cache_block, lambda i: (i, 0, 0, 0))] * 2
        out_shape += [jax.ShapeDtypeStruct((m // ctx_seq,) + cache_block[1:], F32)] * 2
    return pl.pallas_call(
        functools.partial(_l0_in_kernel, use_rope=use_rope, ctx_seq=ctx_seq),
        grid=(m // tm,),
        in_specs=in_specs,
        out_specs=out_specs,
        out_shape=out_shape,
        compiler_params=_params(1),
        name="l0_in",
    )(*args)


def _attn_kernel(*refs, ck, sb, has_cache, kv_blk):
    if has_cache:
        q_ref, k_ref, v_ref, kc_ref, vc_ref, o_ref, s_all_ref, m_all_ref = refs
    else:
        q_ref, k_ref, v_ref, o_ref, s_all_ref, m_all_ref = refs
    gw = q_ref.shape[2] // kv_blk
    tt = sb // ATT_GROUP
    nsub = q_ref.shape[1] // tt
    slots = [(s_all_ref.at[n], m_all_ref.at[n]) for n in range(s_all_ref.shape[0])]

    def chunks(kv):
        out = []
        if has_cache:
            out += [(kc_ref.at[0, kv], vc_ref.at[0, kv], j * ck) for j in range(kc_ref.shape[2] // ck)]
        return out + [(k_ref.at[kv], v_ref.at[kv], j * ck) for j in range(k_ref.shape[1] // ck)]

    def token_rows(i):
        if isinstance(i, int):
            return slice(i * tt, (i + 1) * tt)
        return pl.ds(pl.multiple_of(i * tt, tt), tt)

    def scores(kv, i, slot):
        s_ref, m_ref = slots[slot]
        q4 = q_ref[0, token_rows(i), kv * gw:(kv + 1) * gw]
        q = jnp.concatenate([q4[:, g * ATT_HEAD_DIM:(g + 1) * ATT_HEAD_DIM] for g in range(ATT_GROUP)], axis=0)
        mx = None
        for j, (kr, _, r0) in enumerate(chunks(kv)):
            s = _dot_nt(q, kr[r0:r0 + ck, :])
            s_ref[j] = s
            for t in range(ck // LANES):
                part = s[:, t * LANES:(t + 1) * LANES]
                mx = part if mx is None else jnp.maximum(mx, part)
        m_ref[...] = jnp.broadcast_to(jnp.max(mx, axis=1, keepdims=True), (sb, LANES))

    def values(kv, i, slot):
        s_ref, m_ref = slots[slot]
        m = jnp.concatenate([m_ref[...]] * (ck // LANES), axis=1)
        acc = None
        for j, (_, vr, r0) in enumerate(chunks(kv)):
            p = jnp.exp(s_ref[j] - m).astype(BF16)
            part = _dot(p, vr[r0:r0 + ck, :])
            acc = part if acc is None else acc + part
        out = acc[:, :ATT_HEAD_DIM] / acc[:, ATT_HEAD_DIM:ATT_HEAD_DIM + 1]
        out = jnp.concatenate([out[g * tt:(g + 1) * tt] for g in range(ATT_GROUP)], axis=1)
        o_ref[0, token_rows(i), kv * gw:(kv + 1) * gw] = out.astype(o_ref.dtype)

    if len(slots) > 2:
        items = [(kv, i) for kv in range(kv_blk) for i in range(nsub)]
        for n, item in enumerate(items):
            scores(*item, n)
        for n, item in enumerate(items):
            values(*item, n)
        return

    unroll = ATT_LOOP_UNROLL
    for kv in range(kv_blk):
        scores(kv, 0, 0)

        def body(h, carry, kv=kv):
            for u in range(unroll):
                scores(kv, unroll * h + u + 1, (u + 1) % 2)
                values(kv, unroll * h + u, u % 2)
            return carry

        trips = (nsub - 1) // unroll
        lax.fori_loop(0, trips, body, 0)
        for n in range(trips * unroll, nsub - 1):
            scores(kv, n + 1, (n + 1) % 2)
            values(kv, n, n % 2)
        values(kv, nsub - 1, (nsub - 1) % 2)


def _attention(q, k, v, cache, ck, sb):
    b, seq, width = q.shape
    nkv = k.shape[0]
    gw = width // nkv
    nsub = seq * ATT_GROUP // sb
    flat = nkv * nsub <= ATT_MAX_UNROLL
    kv_blk = nkv if flat else 1
    n_slots = kv_blk * nsub if flat else 2
    lk = seq
    in_specs = [pl.BlockSpec((1, seq, gw * kv_blk), lambda i, j: (i, 0, j)),
                pl.BlockSpec((kv_blk, seq, ATT_HEAD_DIM), lambda i, j: (j, i, 0)),
                pl.BlockSpec((kv_blk, seq, LANES), lambda i, j: (j, i, 0))]
    args = [q, k, v]
    if cache is not None:
        past = cache[0].shape[2]
        assert past % ck == 0
        lk += past
        in_specs += [pl.BlockSpec((1, kv_blk, past, ATT_HEAD_DIM), lambda i, j: (i, j, 0, 0)),
                     pl.BlockSpec((1, kv_blk, past, LANES), lambda i, j: (i, j, 0, 0))]
        args += list(cache)
    assert (seq * ATT_GROUP) % (2 * sb) == 0 and seq % ck == 0
    return pl.pallas_call(
        functools.partial(_attn_kernel, ck=ck, sb=sb, has_cache=cache is not None, kv_blk=kv_blk),
        grid=(b, nkv // kv_blk),
        in_specs=in_specs,
        out_specs=pl.BlockSpec((1, seq, gw * kv_blk), lambda i, j: (i, 0, j)),
        out_shape=jax.ShapeDtypeStruct(q.shape, BF16),
        scratch_shapes=[pltpu.VMEM((n_slots, lk // ck, sb, ck), F32), pltpu.VMEM((n_slots, sb, LANES), F32)],
        compiler_params=_params(2),
        name="attention",
    )(*args)


def _ssd_group_rows(pair):
    g = pair // (SSD_HEADS // 2 // SSD_GROUPS)
    return slice(g * SSD_STATE, (g + 1) * SSD_STATE)


def _ssd_conv(cur_ref, prev_ref, next_ref, is_first, is_last, ext_ref, cw_ref, cb_ref, shift_ref):
    cn = SSD_CHUNK
    ext_ref[0:HALO, :] = jnp.where(is_first, 0.0, prev_ref[...])
    ext_ref[HALO:HALO + cn, :] = cur_ref[...]
    ext_ref[HALO + cn:, :] = jnp.where(is_last, 0.0, next_ref[...])
    shifted = _dot(shift_ref[...], ext_ref[...].astype(BF16))
    centre = SSD_CONV // 2
    u = cb_ref[...] + cw_ref[centre:centre + 1, :] * cur_ref[...]
    for n, t in enumerate(t for t in range(SSD_CONV) if t != centre):
        u = u + cw_ref[t:t + 1, :] * shifted[n * cn:(n + 1) * cn]
    return _silu(u)


def _ssd_scalars(dt_ref, dtb_ref, alog_ref, *, forward):
    cn = SSD_CHUNK
    nh = 2 * SSD_HEADS
    dt_t = _softplus(dt_ref[...].T[:nh] + dtb_ref[...])
    a_t = dt_t * (-jnp.exp(alog_ref[...]))
    ii = lax.broadcasted_iota(jnp.int32, (cn, cn), 0)
    jj = lax.broadcasted_iota(jnp.int32, (cn, cn), 1)
    keep = (ii >= jj) if forward else (ii <= jj)
    tri_t = ((ii <= jj) if forward else (ii >= jj)).astype(BF16)
    c3 = _dot(jnp.concatenate(_split3(a_t), axis=0), tri_t)
    cum_t = c3[:nh] + c3[nh:2 * nh] + c3[2 * nh:]
    cols = jnp.concatenate([dt_t, cum_t, jnp.zeros((LANES - 2 * nh, cn), F32)], axis=0).T
    lane = lax.broadcasted_iota(jnp.int32, (1, LANES), 1)
    return dict(cum_t=cum_t, cols=cols, keep=keep, lo=lane < SSD_HEAD_DIM, forward=forward)


def _ssd_mats(act, prep):
    lo = prep["lo"]
    xs = act[:, :SSD_WIDTH]
    bm = act[:, SSD_WIDTH:SSD_WIDTH + LANES]
    cm = act[:, SSD_WIDTH + LANES:]
    bm_b = bm.astype(BF16)
    gmat = []
    bg_b = []
    for g in range(SSD_GROUPS):
        in_group = lo if g == 0 else jnp.logical_not(lo)
        gmat.append(_dot_nt(jnp.where(in_group, cm, 0.0).astype(BF16), bm_b))
        bg_b.append(jnp.where(in_group, bm, 0.0).astype(BF16))
    return dict(prep, xs=xs, cm_b=cm.astype(BF16), gmat=gmat, bg_b=bg_b)


def _ssd_pair(prep, pair, s_ref, y_ref, dsk_ref):
    cn = SSD_CHUNK
    nh = 2 * SSD_HEADS
    forward, cols, cum_t, keep, lo = prep["forward"], prep["cols"], prep["cum_t"], prep["keep"], prep["lo"]

    def col(lane_idx):
        return jnp.broadcast_to(cols[:, lane_idx:lane_idx + 1], (cn, LANES))

    off = 0 if forward else SSD_HEADS
    tot_col = cn - 1 if forward else 0
    g = pair // (SSD_HEADS // 2 // SSD_GROUPS)
    gmat = prep["gmat"][g]
    h0 = off + 2 * pair
    h1 = h0 + 1
    ci0 = col(nh + h0)
    ci1 = col(nh + h1)
    cip = jnp.where(lo, ci0, ci1)
    m0 = (gmat * jnp.exp(jnp.where(keep, ci0 - cum_t[h0:h0 + 1, :], -1e30))).astype(BF16)
    m1 = (gmat * jnp.exp(jnp.where(keep, ci1 - cum_t[h1:h1 + 1, :], -1e30))).astype(BF16)
    xs_p = prep["xs"][:, pair * LANES:(pair + 1) * LANES]
    vp = xs_p * jnp.where(lo, col(h0), col(h1))
    v0 = jnp.where(lo, vp, 0.0).astype(BF16)
    v1 = jnp.where(lo, 0.0, vp).astype(BF16)
    s_old = s_ref[pair]
    y = (_dot(jnp.concatenate([m0, m1], axis=1), jnp.concatenate([v0, v1], axis=0))
         + jnp.exp(cip) * _dot(prep["cm_b"], s_old.astype(BF16)))
    totp = jnp.where(lo, cum_t[h0:h0 + 1, tot_col:tot_col + 1], cum_t[h1:h1 + 1, tot_col:tot_col + 1])
    s_ref[pair] = s_old * jnp.exp(totp) + _dot_tn(prep["bg_b"][g], (vp * jnp.exp(totp - cip)).astype(BF16))
    if forward:
        y = y + dsk_ref[:, pair * LANES:(pair + 1) * LANES] * xs_p
    y_ref[:, pair * LANES:(pair + 1) * LANES] = y.astype(y_ref.dtype)


def _ssd_kernel(*refs, nc, ns, has_init, want_fin):
    refs = list(refs)
    cf_ref, pf_ref, nf_ref, dtf_ref, cb_ref, pb_ref, nb_ref, dtb_ref = refs[:8]
    cw_ref, cbias_ref, dtbias_ref, alog_ref, dsk_ref, shift_ref = refs[8:14]
    pos = 14
    s0_ref = None
    if has_init:
        s0_ref = refs[pos]
        pos += 1
    yf_ref, yb_ref = refs[pos:pos + 2]
    pos += 2
    sfin_ref = None
    if want_fin:
        sfin_ref = refs[pos]
        pos += 1
    ext_ref, act_ref, sf_ref, sb_ref = refs[pos:]
    c = pl.program_id(1)
    c_fwd = c
    c_bwd = nc - 1 - c

    @pl.when(c == 0)
    def _():
        sf_ref[...] = jnp.zeros_like(sf_ref)
        sb_ref[...] = jnp.zeros_like(sb_ref)
        if has_init:
            for s in range(ns):
                for pair in range(SSD_HEADS // 2):
                    rows = _ssd_group_rows(pair)
                    sf_ref[s, pair, rows, :] = s0_ref[s, 0, pair]
                    sb_ref[s, pair, rows, :] = s0_ref[s, 1, pair]

    def scalars():
        out = []
        for s in range(ns):
            out.append(_ssd_scalars(dtf_ref.at[0, s], dtbias_ref, alog_ref, forward=True))
            out.append(_ssd_scalars(dtb_ref.at[0, s], dtbias_ref, alog_ref, forward=False))
        return out

    def streams(preps, acts):
        work = []
        for s in range(ns):
            work.append((_ssd_mats(acts[2 * s], preps[2 * s]), sf_ref.at[s], yf_ref.at[0, s]))
            work.append((_ssd_mats(acts[2 * s + 1], preps[2 * s + 1]), sb_ref.at[s], yb_ref.at[0, s]))
        for pair in range(SSD_HEADS // 2):
            for prep, s_ref, y_ref in work:
                _ssd_pair(prep, pair, s_ref, y_ref, dsk_ref)

    @pl.when(c < nc // 2)
    def _():
        preps = scalars()
        acts = []
        for s in range(ns):
            act_f = _ssd_conv(cf_ref.at[0, s], pf_ref.at[0, s], nf_ref.at[0, s], c_fwd == 0, c_fwd == nc - 1,
                              ext_ref.at[2 * s], cw_ref, cbias_ref, shift_ref)
            act_b = _ssd_conv(cb_ref.at[0, s], pb_ref.at[0, s], nb_ref.at[0, s], c_bwd == 0, c_bwd == nc - 1,
                              ext_ref.at[2 * s + 1], cw_ref, cbias_ref, shift_ref)
            act_ref[s, c_fwd] = act_f
            act_ref[s, c_bwd] = act_b
            acts += [act_f, act_b]
        streams(preps, acts)

    @pl.when(c >= nc // 2)
    def _():
        acts = []
        for s in range(ns):
            acts += [act_ref[s, c_fwd], act_ref[s, c_bwd]]
        streams(scalars(), acts)

    if want_fin:
        @pl.when(c == nc - 1)
        def _():
            for s in range(ns):
                for pair in range(SSD_HEADS // 2):
                    rows = _ssd_group_rows(pair)
                    for d, st_ref in enumerate((sf_ref, sb_ref)):
                        both = st_ref[s, pair, rows, :]
                        sfin_ref[s, d, 2 * pair] = both[:, :SSD_HEAD_DIM]
                        sfin_ref[s, d, 2 * pair + 1] = both[:, SSD_HEAD_DIM:]


def _ssd(xbc, dt, p, s0, batch, want_fin):
    m = xbc.shape[0]
    cn = SSD_CHUNK
    ns = SSD_SEQS
    seq = m // batch
    nc = seq // cn
    assert nc % 2 == 0
    assert batch % ns == 0
    per = cn // HALO
    n_halo = seq // HALO
    has_init = s0 is not None
    npair = SSD_HEADS // 2
    view = lambda a: a.reshape(batch // ns, ns, seq, a.shape[-1])

    def fwd(c):
        return c

    def bwd(c):
        return nc - 1 - c

    def stream_specs(chunk):
        return [pl.BlockSpec((1, ns, cn, SSD_XBC), lambda b, c: (b, 0, chunk(c), 0)),
                pl.BlockSpec((1, ns, HALO, SSD_XBC), lambda b, c: (b, 0, jnp.maximum(chunk(c) * per - 1, 0), 0)),
                pl.BlockSpec((1, ns, HALO, SSD_XBC),
                             lambda b, c: (b, 0, jnp.minimum(chunk(c) * per + per, n_halo - 1), 0)),
                pl.BlockSpec((1, ns, cn, LANES), lambda b, c: (b, 0, chunk(c), 0))]

    in_specs = stream_specs(fwd) + stream_specs(bwd) + [
        _const_spec((HALO, SSD_XBC)), _const_spec((1, SSD_XBC)), _const_spec((2 * SSD_HEADS, cn)),
        _const_spec((2 * SSD_HEADS, cn)), _const_spec((1, SSD_WIDTH)),
        _const_spec(((SSD_CONV - 1) * cn, cn + 2 * HALO))]
    taps = [t for t in range(SSD_CONV) if t != SSD_CONV // 2]
    src = np.concatenate([HALO + np.arange(cn) + (t - SSD_CONV // 2) for t in taps])
    shift = jnp.asarray(src[:, None] == np.arange(cn + 2 * HALO)[None, :], BF16)
    xv, dv = view(xbc), view(dt)
    args = [xv, xv, xv, dv, xv, xv, xv, dv, p["conv_w"], p["conv_b"], p["dt_bias"], p["a_log"], p["d_skip"], shift]
    state_block = (ns, 2, npair, SSD_STATE, LANES)
    if has_init:
        in_specs.append(pl.BlockSpec(state_block, lambda b, c: (b, 0, 0, 0, 0)))
        args.append(s0)
    out_specs = [pl.BlockSpec((1, ns, cn, SSD_WIDTH), lambda b, c: (b, 0, fwd(c), 0)),
                 pl.BlockSpec((1, ns, cn, SSD_WIDTH), lambda b, c: (b, 0, bwd(c), 0))]
    out_shape = [jax.ShapeDtypeStruct((batch // ns, ns, seq, SSD_WIDTH), BF16)] * 2
    if want_fin:
        fin_block = (ns, 2, SSD_HEADS, SSD_STATE, SSD_HEAD_DIM)
        out_specs.append(pl.BlockSpec(fin_block, lambda b, c: (b, 0, 0, 0, 0)))
        out_shape.append(jax.ShapeDtypeStruct((batch,) + fin_block[1:], F32))
    out = pl.pallas_call(
        functools.partial(_ssd_kernel, nc=nc, ns=ns, has_init=has_init, want_fin=want_fin),
        grid=(batch // ns, nc),
        in_specs=in_specs,
        out_specs=out_specs,
        out_shape=out_shape,
        scratch_shapes=[pltpu.VMEM((2 * ns, cn + 2 * HALO, SSD_XBC), F32), pltpu.VMEM((ns, nc, cn, SSD_XBC), F32),
                        pltpu.VMEM((ns, npair, LANES, LANES), F32), pltpu.VMEM((ns, npair, LANES, LANES), F32)],
        compiler_params=_params(2),
        name="ssd",
    )(*args)
    return [out[0].reshape(m, SSD_WIDTH), out[1].reshape(m, SSD_WIDTH)] + list(out[2:])


def _ssd_state_to_pairs(s):
    b = s.shape[0]
    npair = SSD_HEADS // 2
    s = s.reshape(b, 2, npair, 2, SSD_STATE, SSD_HEAD_DIM).transpose(0, 1, 2, 4, 3, 5)
    return s.reshape(b, 2, npair, SSD_STATE, 2 * SSD_HEAD_DIM)


def _l1_in_kernel(*refs, use_rope):
    if use_rope:
        x_ref, nw_ref, sh_ref, sc_ref, w_ref, cos_ref, sin_ref, q_ref, k_ref, v_ref, g_ref = refs
    else:
        x_ref, nw_ref, sh_ref, sc_ref, w_ref, q_ref, k_ref, v_ref, g_ref = refs
    h = (_rms(x_ref[...]) * nw_ref[...] * (1.0 + sc_ref[0]) + sh_ref[0]).astype(BF16)
    half = RET_DK // 2

    def rope(t):
        if not use_rope:
            return t
        cos = cos_ref[...]
        sin = sin_ref[...]
        parts = []
        for hd in range(RET_HEADS):
            x1 = t[:, hd * RET_DK:hd * RET_DK + half]
            x2 = t[:, hd * RET_DK + half:(hd + 1) * RET_DK]
            parts += [x1 * cos - x2 * sin, x2 * cos + x1 * sin]
        return jnp.concatenate(parts, axis=1)

    o1 = RET_QK_WIDTH
    o2 = 2 * RET_QK_WIDTH
    o3 = o2 + RET_V_WIDTH
    q_ref[...] = rope(_dot(h, w_ref[:, :o1])).astype(BF16)
    k_ref[...] = (rope(_dot(h, w_ref[:, o1:o2])) * (RET_DK ** -0.5)).astype(k_ref.dtype)
    v_ref[...] = _dot(h, w_ref[:, o2:o3]).astype(BF16)
    g_ref[...] = _silu(_dot(h, w_ref[:, o3:])).astype(g_ref.dtype)


def _l1_in(x, mod, rows_per_mod, p, rope, tm):
    m = x.shape[0]
    use_rope = rope is not None
    row = lambda w: pl.BlockSpec((tm, w), lambda i: (i, 0))
    n = 2 * RET_QK_WIDTH + 2 * RET_V_WIDTH
    in_specs = ([row(D_MODEL), _const_spec((1, D_MODEL))] + _mod_specs(tm, rows_per_mod, (0, 1))
                + [_resident_spec((D_MODEL, n))])
    args = [x, p["norm_mix"], mod, mod, p["w_in"]]
    if use_rope:
        rows = rope[0].shape[0]
        in_specs += [pl.BlockSpec((tm, LANES), lambda i: (i % (rows // tm), 0))] * 2
        args += list(rope)
    widths = [(RET_QK_WIDTH, BF16), (RET_QK_WIDTH, BF16), (RET_V_WIDTH, BF16), (RET_V_WIDTH, BF16)]
    return pl.pallas_call(
        functools.partial(_l1_in_kernel, use_rope=use_rope),
        grid=(m // tm,),
        in_specs=in_specs,
        out_specs=[row(w) for w, _ in widths],
        out_shape=[jax.ShapeDtypeStruct((m, w), dt) for w, dt in widths],
        compiler_params=_params(1),
        name="l1_in",
    )(*args)


def _ret_kernel(*refs, cn, nc, has_init, want_fin):
    stateless = (not has_init) and nc == 1
    refs = list(refs)
    dec_ref, qf_ref, kf_ref, vf_ref = refs[:4]
    pos = 4
    if stateless:
        qb_ref, kb_ref, vb_ref = qf_ref, kf_ref, vf_ref
    else:
        qb_ref, kb_ref, vb_ref = refs[pos:pos + 3]
        pos += 3
    s0_ref = None
    if has_init:
        s0_ref = refs[pos]
        pos += 1
    yf_ref = refs[pos]
    pos += 1
    yb_ref = None
    if not stateless:
        yb_ref = refs[pos]
        pos += 1
    sfin_ref = None
    if want_fin:
        sfin_ref = refs[pos]
        pos += 1
    decay_ref, rowdec_ref, sf_ref, sb_ref = refs[pos:]
    c = pl.program_id(1)
    log_g = -jnp.exp(dec_ref[...])

    @pl.when((pl.program_id(0) == 0) & (c == 0))
    def _():
        ii = lax.broadcasted_iota(jnp.int32, (cn, cn), 0)
        jj = lax.broadcasted_iota(jnp.int32, (cn, cn), 1)
        dist = (ii - jj).astype(F32)
        ri = lax.broadcasted_iota(jnp.int32, (cn, LANES), 0).astype(F32)
        for hd in range(RET_HEADS):
            gf = log_g[0:1, hd:hd + 1]
            gb = log_g[1:2, hd:hd + 1]
            decay_ref[hd] = (jnp.where(dist >= 0, jnp.exp(gf * jnp.maximum(dist, 0.0)), 0.0)
                             + jnp.where(dist <= 0, jnp.exp(gb * jnp.maximum(-dist, 0.0)), 0.0))
            rowdec_ref[hd, 0] = jnp.exp(gf * (ri + 1.0))
            rowdec_ref[hd, 1] = jnp.exp(gf * (cn - 1.0 - ri))
            rowdec_ref[hd, 2] = jnp.exp(gb * (cn - ri))
            rowdec_ref[hd, 3] = jnp.exp(gb * ri)

    def rowdec(hd, which, width):
        return jnp.concatenate([rowdec_ref[hd, which]] * (width // LANES), axis=1)

    if not stateless:
        @pl.when(c == 0)
        def _():
            if has_init:
                sf_ref[...] = s0_ref[0, 0]
                sb_ref[...] = s0_ref[0, 1]
            else:
                sf_ref[...] = jnp.zeros_like(sf_ref)
                sb_ref[...] = jnp.zeros_like(sb_ref)

    for hd in range(RET_HEADS):
        gf = log_g[0:1, hd:hd + 1]
        gb = log_g[1:2, hd:hd + 1]
        qs = slice(hd * RET_DK, (hd + 1) * RET_DK)
        vs = slice(hd * RET_DV, (hd + 1) * RET_DV)
        q = qf_ref[:, qs]
        k = kf_ref[:, qs]
        v = vf_ref[:, vs]
        y = _dot((_dot_nt(q, k.astype(BF16)) * decay_ref[hd]).astype(BF16), v)
        upd_f = _dot_tn((k * rowdec(hd, 1, RET_DK)).astype(BF16), v)
        if stateless:
            new_f = upd_f
        else:
            s_old = sf_ref[hd]
            y = y + rowdec(hd, 0, RET_DV) * _dot(q, s_old.astype(BF16))
            new_f = s_old * jnp.exp(gf * cn) + upd_f
            sf_ref[hd] = new_f
        yf_ref[:, vs] = y.astype(yf_ref.dtype)
        q = qb_ref[:, qs]
        k = kb_ref[:, qs]
        v = vb_ref[:, vs]
        upd_b = _dot_tn((k * rowdec(hd, 3, RET_DK)).astype(BF16), v)
        if stateless:
            new_b = upd_b
        else:
            s_old = sb_ref[hd]
            yb_ref[:, vs] = (rowdec(hd, 2, RET_DV) * _dot(q, s_old.astype(BF16))).astype(yb_ref.dtype)
            new_b = s_old * jnp.exp(gb * cn) + upd_b
            sb_ref[hd] = new_b
        if want_fin:
            if stateless:
                sfin_ref[0, 0, hd] = new_f
                sfin_ref[0, 1, hd] = new_b
            else:
                @pl.when(c == nc - 1)
                def _(new_f=new_f, new_b=new_b, hd=hd):
                    sfin_ref[0, 0, hd] = new_f
                    sfin_ref[0, 1, hd] = new_b


def _retention(q, k, v, decay, s0, batch, cn, want_fin):
    m = q.shape[0]
    nc = m // batch // cn
    has_init = s0 is not None
    stateless = (not has_init) and nc == 1

    def fwd(b, c):
        return b * nc + c

    def bwd(b, c):
        return b * nc + nc - 1 - c

    def stream_specs(chunk):
        return [pl.BlockSpec((cn, RET_QK_WIDTH), lambda b, c: (chunk(b, c), 0)),
                pl.BlockSpec((cn, RET_QK_WIDTH), lambda b, c: (chunk(b, c), 0)),
                pl.BlockSpec((cn, RET_V_WIDTH), lambda b, c: (chunk(b, c), 0))]

    in_specs = [_const_spec((8, LANES))] + stream_specs(fwd)
    args = [decay, q, k, v]
    if not stateless:
        in_specs += stream_specs(bwd)
        args += [q, k, v]
    state_block = (1, 2, RET_HEADS, RET_DK, RET_DV)
    if has_init:
        in_specs.append(pl.BlockSpec(state_block, lambda b, c: (b, 0, 0, 0, 0)))
        args.append(s0)
    out_specs = [pl.BlockSpec((cn, RET_V_WIDTH), lambda b, c: (fwd(b, c), 0))]
    if not stateless:
        out_specs.append(pl.BlockSpec((cn, RET_V_WIDTH), lambda b, c: (bwd(b, c), 0)))
    out_shape = [jax.ShapeDtypeStruct((m, RET_V_WIDTH), BF16)] * len(out_specs)
    if want_fin:
        out_specs.append(pl.BlockSpec(state_block, lambda b, c: (b, 0, 0, 0, 0)))
        out_shape.append(jax.ShapeDtypeStruct((batch,) + state_block[1:], F32))
    return pl.pallas_call(
        functools.partial(_ret_kernel, cn=cn, nc=nc, has_init=has_init, want_fin=want_fin),
        grid=(batch, nc),
        in_specs=in_specs,
        out_specs=out_specs,
        out_shape=out_shape,
        scratch_shapes=[pltpu.VMEM((RET_HEADS, cn, cn), F32), pltpu.VMEM((RET_HEADS, 4, cn, LANES), F32)]
        + [pltpu.VMEM((RET_HEADS, RET_DK, RET_DV), F32)] * 2,
        compiler_params=_params(2),
        name="retention",
    )(*args)


def _post_kernel(*refs, mixer, final):
    refs = list(refs)
    x_ref, g1_ref, nw_ref, sh2_ref, sc2_ref, g2_ref, wout_ref, wg_ref, wu_ref, wd_ref = refs[:10]
    pos = 10
    fn_ref = None
    if final:
        fn_ref = refs[pos]
        pos += 1
    if mixer == "ab":
        att_ref, yf_ref, yb_ref, z_ref, gain_ref, o_ref = refs[pos:]
        y = (yf_ref[...].astype(F32) + yb_ref[...].astype(F32)) * _silu(z_ref[...].astype(F32))
        y = _rms(y) * gain_ref[...]
        mix = _dot(jnp.concatenate([att_ref[...], y.astype(BF16)], axis=1), wout_ref[...])
    else:
        parts = refs[pos:-3]
        gate_ref, gain_ref, o_ref = refs[-3:]
        mix = None
        for hd in range(RET_HEADS):
            vs = slice(hd * RET_DV, (hd + 1) * RET_DV)
            y = parts[0][:, vs].astype(F32)
            for extra in parts[1:]:
                y = y + extra[:, vs].astype(F32)
            y = _rms(y) * gain_ref[:, vs]
            part = _dot((gate_ref[:, vs].astype(F32) * y).astype(BF16), wout_ref[vs, :])
            mix = part if mix is None else mix + part
    x1 = x_ref[...] + g1_ref[0] * mix
    h = (_rms(x1) * nw_ref[...] * (1.0 + sc2_ref[0]) + sh2_ref[0]).astype(BF16)
    act = (_silu(_dot(h, wg_ref[...])) * _dot(h, wu_ref[...])).astype(BF16)
    x2 = x1 + g2_ref[0] * _dot(act, wd_ref[...])
    if final:
        x2 = _rms(x2) * fn_ref[...]
    o_ref[...] = x2


def _post(x, mod, rows_per_mod, p, mixer, mix_inputs, gain, final_norm, tm):
    m = x.shape[0]
    row = lambda w: pl.BlockSpec((tm, w), lambda i: (i, 0))
    mixw = p["w_out"].shape[0]
    (g1,) = _mod_specs(tm, rows_per_mod, (2,))
    sh2, sc2, g2 = _mod_specs(tm, rows_per_mod, (3, 4, 5))
    in_specs = [row(D_MODEL), g1, _const_spec((1, D_MODEL)), sh2, sc2, g2,
                _resident_spec((mixw, D_MODEL)), _resident_spec((D_MODEL, D_FF)), _resident_spec((D_MODEL, D_FF)),
                _resident_spec((D_FF, D_MODEL))]
    args = [x, mod, p["norm_ffn"], mod, mod, mod, p["w_out"], p["w_gate"], p["w_up"], p["w_down"]]
    final = final_norm is not None
    if final:
        in_specs.append(_const_spec((1, D_MODEL)))
        args.append(final_norm)
    in_specs += [row(a.shape[1]) for a in mix_inputs] + [_const_spec(gain.shape)]
    args += list(mix_inputs) + [gain]
    return pl.pallas_call(
        functools.partial(_post_kernel, mixer=mixer, final=final),
        grid=(m // tm,),
        in_specs=in_specs,
        out_specs=row(D_MODEL),
        out_shape=jax.ShapeDtypeStruct((m, D_MODEL), F32),
        compiler_params=_params(1),
        name="post_" + mixer,
    )(*args)


def _axial_angles(n_tokens, dim):
    rows = n_tokens // GRID_W
    row = np.repeat(np.arange(rows), GRID_W).astype(np.float64)
    col = np.tile(np.arange(GRID_W), rows).astype(np.float64)
    n_freq = dim // 4
    inv = ROPE_THETA ** (-np.arange(n_freq, dtype=np.float64) / n_freq)
    return np.concatenate([row[:, None] * inv, col[:, None] * inv], axis=-1)


def _head_mean_matrix(width, head):
    idx = jnp.arange(width) // head
    return jnp.where(idx[:, None] == idx[None, :], 1.0 / head, 0.0).astype(BF16)


def _rows_bcast(v, width):
    return jnp.broadcast_to(v.reshape(-1, 1), (v.size, width))


def _trunk(x, mods, rows_per_mod, p0, p1, final_norm, rope_att, rope_ret, caches, seq):
    m = x.shape[0]
    batch = m // seq
    sample = caches is not None
    tm = ROW_TILE
    l0 = _l0_in(x, mods[0], rows_per_mod, p0, rope_att, tm, None if sample else seq)
    q, ka, va, z, xbc, dt = l0[:6]
    s0_ssd = None
    s0_ret = None
    kv_cache = None
    if sample:
        cache_k, cache_v, state_ssd, state_ret = caches
        ck = cache_k.astype(BF16).transpose(0, 2, 1, 3)
        cv = cache_v.astype(BF16).transpose(0, 2, 1, 3)
        ones_col = jnp.zeros(cv.shape[:3] + (LANES - ATT_HEAD_DIM,), BF16).at[..., 0].set(1.0)
        kv_cache = (ck, jnp.concatenate([cv, ones_col], axis=-1))
        s0_ssd = _ssd_state_to_pairs(state_ssd)
        s0_ret = state_ret
    att = _attention(q.reshape(batch, seq, ATT_WIDTH), ka, va, kv_cache, ATT_KEY_CHUNK, ATT_SUB_ROWS)
    att = att.reshape(m, ATT_WIDTH)
    ssd_out = _ssd(xbc, dt, p0, s0_ssd, batch, want_fin=not sample)
    x = _post(x, mods[0], rows_per_mod, p0, "ab", [att, ssd_out[0], ssd_out[1], z], p0["ssd_gain"], None, tm)
    q1, k1, v1, g1 = _l1_in(x, mods[1], rows_per_mod, p1, rope_ret, tm)
    ret_out = _retention(q1, k1, v1, p1["decay"], s0_ret, batch, RET_CHUNK, want_fin=not sample)
    y_parts = ret_out if sample else ret_out[:-1]
    y = _post(x, mods[1], rows_per_mod, p1, "c", list(y_parts) + [g1], p1["ret_gain"], final_norm, tm)
    if sample:
        return y, None
    new_k = l0[6].transpose(0, 3, 1, 2)
    new_v = l0[7].transpose(0, 3, 1, 2)
    return y, (new_k, new_v, ssd_out[2], ret_out[-1])


def kernel(x_prompt, x_sample, c, cache_k0, cache_v0, state_ssd0, state_ret1, c_ctx, l0_w_ada, l0_b_ada, l0_norm_mix, l0_norm_ffn, l0_w_in, l0_w_out, l0_q_gain, l0_k_gain, l0_conv_w, l0_conv_b, l0_dt_bias, l0_a_log, l0_d_skip, l0_ssd_gain, l0_w_gate, l0_w_up, l0_w_down, l1_w_ada, l1_b_ada, l1_norm_mix, l1_norm_ffn, l1_w_in, l1_w_out, l1_decay, l1_ret_gain, l1_w_gate, l1_w_up, l1_w_down, final_norm):
    b_ctx, seq_ctx, d = x_prompt.shape
    b_lat, seq_lat, _ = x_sample.shape
    assert d == D_MODEL and l0_w_in.shape == (D_MODEL, L0_IN) and l0_w_gate.shape == (D_MODEL, D_FF)
    row = lambda v: v.reshape(1, -1)

    (w0_in, w0_out, w0_gate, w0_up, w0_down, w1_in, w1_out, w1_gate, w1_up, w1_down) = _cast_weights(
        [l0_w_in, l0_w_out, l0_w_gate, l0_w_up, l0_w_down, l1_w_in, l1_w_out, l1_w_gate, l1_w_up, l1_w_down])
    p0 = dict(
        norm_mix=row(l0_norm_mix), norm_ffn=row(l0_norm_ffn),
        w_in=w0_in, w_out=w0_out, w_gate=w0_gate, w_up=w0_up, w_down=w0_down,
        q_gain=row(jnp.tile(l0_q_gain, ATT_HEADS)), k_gain=row(jnp.tile(l0_k_gain, ATT_KV_HEADS)),
        pq=_head_mean_matrix(ATT_WIDTH, ATT_HEAD_DIM), pk=_head_mean_matrix(ATT_KV_WIDTH, ATT_HEAD_DIM),
        conv_w=jnp.pad(l0_conv_w, ((0, HALO - SSD_CONV), (0, 0))), conv_b=row(l0_conv_b),
        dt_bias=_rows_bcast(l0_dt_bias, SSD_CHUNK), a_log=_rows_bcast(l0_a_log, SSD_CHUNK),
        d_skip=row(jnp.repeat(l0_d_skip, SSD_HEAD_DIM)), ssd_gain=row(l0_ssd_gain),
    )
    p1 = dict(
        norm_mix=row(l1_norm_mix), norm_ffn=row(l1_norm_ffn),
        w_in=w1_in, w_out=w1_out, w_gate=w1_gate, w_up=w1_up, w_down=w1_down,
        decay=jnp.pad(l1_decay, ((0, 8 - l1_decay.shape[0]), (0, LANES - l1_decay.shape[1]))),
        ret_gain=row(l1_ret_gain),
    )
    fnorm = row(final_norm)

    n_cond = 8
    conds = jnp.concatenate([c_ctx[None, :], c, jnp.zeros((n_cond - 1 - b_lat, d), F32)], axis=0)
    mod0 = _ada(conds, l0_w_ada, l0_b_ada)
    mod1 = _ada(conds, l1_w_ada, l1_b_ada)
    mods_ctx = [mod[0:1].reshape(1, 1, 6 * d) for mod in (mod0, mod1)]
    mods_lat = [mod[1:1 + b_lat].reshape(b_lat, 1, 6 * d) for mod in (mod0, mod1)]

    m_ctx = b_ctx * seq_ctx
    y_prompt, ctx = _trunk(x_prompt.reshape(m_ctx, d), mods_ctx, m_ctx, p0, p1, fnorm, None, None, None, seq_ctx)
    new_k0, new_v0, new_ssd0, new_ret1 = ctx
    ang = _axial_angles(seq_lat, ATT_HEAD_DIM)
    cos, sin = np.cos(ang), np.sin(ang)
    reps = LANES // ATT_HEAD_DIM
    rope_att = (jnp.asarray(np.tile(np.concatenate([cos, cos], axis=1), (1, reps)), F32),
                jnp.asarray(np.tile(np.concatenate([-sin, sin], axis=1), (1, reps)), F32))
    ang = _axial_angles(seq_lat, RET_DK)
    rope_ret = (jnp.asarray(np.cos(ang), F32), jnp.asarray(np.sin(ang), F32))
    caches = (cache_k0, cache_v0, state_ssd0, state_ret1)
    y_sample, _ = _trunk(x_sample.reshape(b_lat * seq_lat, d), mods_lat, seq_lat, p0, p1, fnorm, rope_att, rope_ret,
                         caches, seq_lat)
    return (y_prompt.reshape(b_ctx, seq_ctx, d), y_sample.reshape(b_lat, seq_lat, d),
            new_k0, new_v0, new_ssd0, new_ret1)
```

```python
import functools

import jax
import jax.numpy as jnp
import numpy as np
from jax import lax
from jax.experimental import pallas as pl
from jax.experimental.pallas import tpu as pltpu

F32 = jnp.float32
BF16 = jnp.bfloat16

EPS = 1e-6
ROPE_THETA = 10000.0
GRID_W = 64
D_MODEL = 1024
ATT_HEAD_DIM = 64
ATT_HEADS = 8
ATT_KV_HEADS = 2
ATT_GROUP = ATT_HEADS // ATT_KV_HEADS
ATT_WIDTH = ATT_HEADS * ATT_HEAD_DIM
ATT_KV_WIDTH = ATT_KV_HEADS * ATT_HEAD_DIM
SSD_WIDTH = 512
SSD_HEADS = 8
SSD_HEAD_DIM = 64
SSD_STATE = 64
SSD_GROUPS = 2
SSD_CONV = 5
SSD_XBC = SSD_WIDTH + 2 * SSD_GROUPS * SSD_STATE
L0_IN = ATT_WIDTH + 2 * ATT_KV_WIDTH + SSD_WIDTH + SSD_XBC + 2 * SSD_HEADS
RET_HEADS = 4
RET_DK = 256
RET_DV = 512
RET_QK_WIDTH = RET_HEADS * RET_DK
RET_V_WIDTH = RET_HEADS * RET_DV
D_FF = 2816

LANES = 128
BF16_SUBLANES = 16
HALO = 8
L0_IN_PAD = -(-L0_IN // LANES) * LANES
MXU_WIDTH = 256
ROW_TILE = 512
ADA_COLS = 1536
CAST_STEPS = 16
ATT_KEY_CHUNK = MXU_WIDTH
ATT_SUB_ROWS = 512
ATT_MAX_UNROLL = 8
ATT_LOOP_UNROLL = 10
SSD_CHUNK = 128
SSD_SEQS = 2
RET_CHUNK = 256
VMEM_LIMIT = 56 * 1024 * 1024


def _dot(a, b):
    return jnp.dot(a, b, preferred_element_type=F32)


def _dot_nt(a, b):
    return lax.dot_general(a, b, (((1,), (1,)), ((), ())), preferred_element_type=F32)


def _dot_tn(a, b):
    return lax.dot_general(a, b, (((0,), (0,)), ((), ())), preferred_element_type=F32)


def _silu(x):
    half = 0.5 * x
    return half + half * jnp.tanh(half)


def _softplus(x):
    return jnp.maximum(x, 0.0) + jnp.log1p(jnp.exp(-jnp.abs(x)))


def _rms(x):
    return x * lax.rsqrt(jnp.mean(x * x, axis=-1, keepdims=True) + EPS)


def _split3(x):
    hi = x.astype(BF16)
    r = x - hi.astype(F32)
    mid = r.astype(BF16)
    lo = (r - mid.astype(F32)).astype(BF16)
    return hi, mid, lo


def _const_spec(shape):
    return pl.BlockSpec(shape, lambda *_: (0,) * len(shape))


def _resident_spec(shape):
    return pl.BlockSpec(shape, lambda *_: (0,) * len(shape), pipeline_mode=pl.Buffered(1))


def _params(n_axes):
    return pltpu.CompilerParams(dimension_semantics=("arbitrary",) * n_axes, vmem_limit_bytes=VMEM_LIMIT)


def _cast_kernel(*refs):
    n = len(refs) // 2
    for x_ref, o_ref in zip(refs[:n], refs[n:]):
        width = x_ref.shape[1]
        if o_ref.shape[1] == width:
            o_ref[...] = x_ref[...].astype(BF16)
        else:
            o_ref[:, :width] = x_ref[...].astype(BF16)
            o_ref[:, width:] = jnp.zeros((o_ref.shape[0], o_ref.shape[1] - width), BF16)


def _cast_weights(weights):
    steps = CAST_STEPS
    in_specs, out_specs, out_shape = [], [], []
    for w in weights:
        rows, width = w.shape
        assert rows % (steps * BF16_SUBLANES) == 0
        padded = -(-width // LANES) * LANES
        in_specs.append(pl.BlockSpec((rows // steps, width), lambda i: (i, 0)))
        out_specs.append(pl.BlockSpec((rows // steps, padded), lambda i: (i, 0)))
        out_shape.append(jax.ShapeDtypeStruct((rows, padded), BF16))
    return pl.pallas_call(
        _cast_kernel,
        grid=(steps,),
        in_specs=in_specs,
        out_specs=out_specs,
        out_shape=out_shape,
        compiler_params=_params(1),
        name="cast_weights",
    )(*weights)


def _ada_kernel(c_ref, w_ref, b_ref, o_ref):
    s = _silu(c_ref[...])
    o_ref[...] = _dot(s.astype(BF16), w_ref[...].astype(BF16)) + b_ref[...]


def _ada(conds, w, b):
    n = w.shape[1]
    tn = ADA_COLS
    return pl.pallas_call(
        _ada_kernel,
        grid=(n // tn,),
        in_specs=[_const_spec(conds.shape),
                  pl.BlockSpec((D_MODEL, tn), lambda j: (0, j)),
                  pl.BlockSpec((1, tn), lambda j: (0, j))],
        out_specs=pl.BlockSpec((conds.shape[0], tn), lambda j: (0, j)),
        out_shape=jax.ShapeDtypeStruct((conds.shape[0], n), F32),
        compiler_params=_params(1),
        name="ada",
    )(conds, w, b.reshape(1, n))


def _mod_specs(tm, rows_per_mod, which):
    return [pl.BlockSpec((1, 1, D_MODEL), lambda i, j=j: ((i * tm) // rows_per_mod, 0, j)) for j in which]


def _head_rms(x, p_ref, gain):
    x2 = x * x
    hi = x2.astype(BF16)
    lo = (x2 - hi.astype(F32)).astype(BF16)
    ms = _dot(hi, p_ref[...]) + _dot(lo, p_ref[...])
    return x * lax.rsqrt(ms + EPS) * gain


def _rope64(x, cos, sin):
    n = x.shape[1]
    lane = lax.broadcasted_iota(jnp.int32, x.shape, 1)
    first_half = (lane % ATT_HEAD_DIM) < (ATT_HEAD_DIM // 2)
    partner = jnp.where(first_half, pltpu.roll(x, n - ATT_HEAD_DIM // 2, 1), pltpu.roll(x, ATT_HEAD_DIM // 2, 1))
    return x * cos + partner * sin


def _l0_in_kernel(*refs, use_rope, ctx_seq):
    refs = list(refs)
    x_ref, nw_ref, sh_ref, sc_ref, w_ref, qg_ref, kg_ref, pq_ref, pk_ref = refs[:9]
    pos = 9
    if use_rope:
        cos_ref, sin_ref = refs[pos:pos + 2]
        pos += 2
    q_ref, ka_ref, va_ref, z_ref, xbc_ref, dt_ref = refs[pos:pos + 6]
    pos += 6
    h = _rms(x_ref[...]) * nw_ref[...] * (1.0 + sc_ref[0]) + sh_ref[0]
    proj = _dot(h.astype(BF16), w_ref[...])
    o1 = ATT_WIDTH
    o2 = o1 + ATT_KV_WIDTH
    o3 = o2 + ATT_KV_WIDTH
    o4 = o3 + SSD_WIDTH
    o5 = o4 + SSD_XBC
    q = _head_rms(proj[:, :o1], pq_ref, qg_ref[...])
    k = _head_rms(proj[:, o1:o2], pk_ref, kg_ref[...])
    v = proj[:, o2:o3]
    if ctx_seq:
        kt_ref, vt_ref = refs[pos:]
        for s in range(x_ref.shape[0] // ctx_seq):
            kt = k[s * ctx_seq:(s + 1) * ctx_seq].T
            vt = v[s * ctx_seq:(s + 1) * ctx_seq].T
            for kv in range(ATT_KV_HEADS):
                kt_ref[s, kv] = kt[kv * ATT_HEAD_DIM:(kv + 1) * ATT_HEAD_DIM]
                vt_ref[s, kv] = vt[kv * ATT_HEAD_DIM:(kv + 1) * ATT_HEAD_DIM]
    if use_rope:
        cos = cos_ref[...]
        sin = sin_ref[...]
        k = _rope64(k, cos, sin)
        reps = ATT_WIDTH // LANES
        q = _rope64(q, jnp.concatenate([cos] * reps, axis=1), jnp.concatenate([sin] * reps, axis=1))
    q_ref[...] = (q * (ATT_HEAD_DIM ** -0.5)).astype(BF16)
    kb = k.astype(BF16)
    lane = lax.broadcasted_iota(jnp.int32, (1, LANES), 1)
    ones_col = jnp.where(lane == ATT_HEAD_DIM, 1.0, 0.0)
    for kv in range(ATT_KV_HEADS):
        ka_ref[kv] = kb[:, kv * ATT_HEAD_DIM:(kv + 1) * ATT_HEAD_DIM]
        vv = v if kv == 0 else pltpu.roll(v, (LANES - kv * ATT_HEAD_DIM) % LANES, 1)
        va_ref[kv] = jnp.where(lane < ATT_HEAD_DIM, vv, ones_col).astype(BF16)
    z_ref[...] = proj[:, o3:o4].astype(z_ref.dtype)
    xbc_ref[...] = proj[:, o4:o5]
    dt_ref[...] = proj[:, o5:]


def _l0_in(x, mod, rows_per_mod, p, rope, tm, ctx_seq):
    m = x.shape[0]
    use_rope = rope is not None
    row = lambda w: pl.BlockSpec((tm, w), lambda i: (i, 0))
    in_specs = ([row(D_MODEL), _const_spec((1, D_MODEL))] + _mod_specs(tm, rows_per_mod, (0, 1))
                + [_resident_spec((D_MODEL, L0_IN_PAD)), _const_spec((1, ATT_WIDTH)), _const_spec((1, ATT_KV_WIDTH)),
                   _resident_spec((ATT_WIDTH, ATT_WIDTH)), _resident_spec((ATT_KV_WIDTH, ATT_KV_WIDTH))])
    args = [x, p["norm_mix"], mod, mod, p["w_in"], p["q_gain"], p["k_gain"], p["pq"], p["pk"]]
    if use_rope:
        rows = rope[0].shape[0]
        in_specs += [pl.BlockSpec((tm, LANES), lambda i: (i % (rows // tm), 0))] * 2
        args += list(rope)
    head = lambda w: pl.BlockSpec((ATT_KV_HEADS, tm, w), lambda i: (0, i, 0))
    out_specs = [row(ATT_WIDTH), head(ATT_HEAD_DIM), head(LANES), row(SSD_WIDTH), row(SSD_XBC), row(LANES)]
    out_shape = [jax.ShapeDtypeStruct((m, ATT_WIDTH), BF16),
                 jax.ShapeDtypeStruct((ATT_KV_HEADS, m, ATT_HEAD_DIM), BF16),
                 jax.ShapeDtypeStruct((ATT_KV_HEADS, m, LANES), BF16),
                 jax.ShapeDtypeStruct((m, SSD_WIDTH), BF16), jax.ShapeDtypeStruct((m, SSD_XBC), F32),
                 jax.ShapeDtypeStruct((m, LANES), F32)]
    if ctx_seq:
        assert tm % ctx_seq == 0
        cache_block = (tm // ctx_seq, ATT_KV_HEADS, ATT_HEAD_DIM, ctx_seq)
        out_specs += [pl.BlockSpec(cache_block, lambda i: (i, 0, 0, 0))] * 2
        out_shape += [jax.ShapeDtypeStruct((m // ctx_seq,) + cache_block[1:], F32)] * 2
    return pl.pallas_call(
        functools.partial(_l0_in_kernel, use_rope=use_rope, ctx_seq=ctx_seq),
        grid=(m // tm,),
        in_specs=in_specs,
        out_specs=out_specs,
        out_shape=out_shape,
        compiler_params=_params(1),
        name="l0_in",
    )(*args)


def _attn_kernel(*refs, ck, sb, has_cache, kv_blk):
    if has_cache:
        q_ref, k_ref, v_ref, kc_ref, vc_ref, o_ref, s_all_ref, m_all_ref = refs
    else:
        q_ref, k_ref, v_ref, o_ref, s_all_ref, m_all_ref = refs
    gw = q_ref.shape[2] // kv_blk
    tt = sb // ATT_GROUP
    nsub = q_ref.shape[1] // tt
    slots = [(s_all_ref.at[n], m_all_ref.at[n]) for n in range(s_all_ref.shape[0])]

    def chunks(kv):
        out = []
        if has_cache:
            out += [(kc_ref.at[0, kv], vc_ref.at[0, kv], j * ck) for j in range(kc_ref.shape[2] // ck)]
        return out + [(k_ref.at[kv], v_ref.at[kv], j * ck) for j in range(k_ref.shape[1] // ck)]

    def token_rows(i):
        if isinstance(i, int):
            return slice(i * tt, (i + 1) * tt)
        return pl.ds(pl.multiple_of(i * tt, tt), tt)

    def scores(kv, i, slot):
        s_ref, m_ref = slots[slot]
        q4 = q_ref[0, token_rows(i), kv * gw:(kv + 1) * gw]
        q = jnp.concatenate([q4[:, g * ATT_HEAD_DIM:(g + 1) * ATT_HEAD_DIM] for g in range(ATT_GROUP)], axis=0)
        mx = None
        for j, (kr, _, r0) in enumerate(chunks(kv)):
            s = _dot_nt(q, kr[r0:r0 + ck, :])
            s_ref[j] = s
            for t in range(ck // LANES):
                part = s[:, t * LANES:(t + 1) * LANES]
                mx = part if mx is None else jnp.maximum(mx, part)
        m_ref[...] = jnp.broadcast_to(jnp.max(mx, axis=1, keepdims=True), (sb, LANES))

    def values(kv, i, slot):
        s_ref, m_ref = slots[slot]
        m = jnp.concatenate([m_ref[...]] * (ck // LANES), axis=1)
        acc = None
        for j, (_, vr, r0) in enumerate(chunks(kv)):
            p = jnp.exp(s_ref[j] - m).astype(BF16)
            part = _dot(p, vr[r0:r0 + ck, :])
            acc = part if acc is None else acc + part
        out = acc[:, :ATT_HEAD_DIM] / acc[:, ATT_HEAD_DIM:ATT_HEAD_DIM + 1]
        out = jnp.concatenate([out[g * tt:(g + 1) * tt] for g in range(ATT_GROUP)], axis=1)
        o_ref[0, token_rows(i), kv * gw:(kv + 1) * gw] = out.astype(o_ref.dtype)

    if len(slots) > 2:
        items = [(kv, i) for kv in range(kv_blk) for i in range(nsub)]
        for n, item in enumerate(items):
            scores(*item, n)
        for n, item in enumerate(items):
            values(*item, n)
        return

    unroll = ATT_LOOP_UNROLL
    for kv in range(kv_blk):
        scores(kv, 0, 0)

        def body(h, carry, kv=kv):
            for u in range(unroll):
                scores(kv, unroll * h + u + 1, (u + 1) % 2)
                values(kv, unroll * h + u, u % 2)
            return carry

        trips = (nsub - 1) // unroll
        lax.fori_loop(0, trips, body, 0)
        for n in range(trips * unroll, nsub - 1):
            scores(kv, n + 1, (n + 1) % 2)
            values(kv, n, n % 2)
        values(kv, nsub - 1, (nsub - 1) % 2)


def _attention(q, k, v, cache, ck, sb):
    b, seq, width = q.shape
    nkv = k.shape[0]
    gw = width // nkv
    nsub = seq * ATT_GROUP // sb
    flat = nkv * nsub <= ATT_MAX_UNROLL
    kv_blk = nkv if flat else 1
    n_slots = kv_blk * nsub if flat else 2
    lk = seq
    in_specs = [pl.BlockSpec((1, seq, gw * kv_blk), lambda i, j: (i, 0, j)),
                pl.BlockSpec((kv_blk, seq, ATT_HEAD_DIM), lambda i, j: (j, i, 0)),
                pl.BlockSpec((kv_blk, seq, LANES), lambda i, j: (j, i, 0))]
    args = [q, k, v]
    if cache is not None:
        past = cache[0].shape[2]
        assert past % ck == 0
        lk += past
        in_specs += [pl.BlockSpec((1, kv_blk, past, ATT_HEAD_DIM), lambda i, j: (i, j, 0, 0)),
                     pl.BlockSpec((1, kv_blk, past, LANES), lambda i, j: (i, j, 0, 0))]
        args += list(cache)
    assert (seq * ATT_GROUP) % (2 * sb) == 0 and seq % ck == 0
    return pl.pallas_call(
        functools.partial(_attn_kernel, ck=ck, sb=sb, has_cache=cache is not None, kv_blk=kv_blk),
        grid=(b, nkv // kv_blk),
        in_specs=in_specs,
        out_specs=pl.BlockSpec((1, seq, gw * kv_blk), lambda i, j: (i, 0, j)),
        out_shape=jax.ShapeDtypeStruct(q.shape, BF16),
        scratch_shapes=[pltpu.VMEM((n_slots, lk // ck, sb, ck), F32), pltpu.VMEM((n_slots, sb, LANES), F32)],
        compiler_params=_params(2),
        name="attention",
    )(*args)


def _ssd_group_rows(pair):
    g = pair // (SSD_HEADS // 2 // SSD_GROUPS)
    return slice(g * SSD_STATE, (g + 1) * SSD_STATE)


def _ssd_conv(cur_ref, prev_ref, next_ref, is_first, is_last, ext_ref, cw_ref, cb_ref, shift_ref):
    cn = SSD_CHUNK
    ext_ref[0:HALO, :] = jnp.where(is_first, 0.0, prev_ref[...])
    ext_ref[HALO:HALO + cn, :] = cur_ref[...]
    ext_ref[HALO + cn:, :] = jnp.where(is_last, 0.0, next_ref[...])
    shifted = _dot(shift_ref[...], ext_ref[...].astype(BF16))
    centre = SSD_CONV // 2
    u = cb_ref[...] + cw_ref[centre:centre + 1, :] * cur_ref[...]
    for n, t in enumerate(t for t in range(SSD_CONV) if t != centre):
        u = u + cw_ref[t:t + 1, :] * shifted[n * cn:(n + 1) * cn]
    return _silu(u)


def _ssd_scalars(dt_ref, dtb_ref, alog_ref, *, forward):
    cn = SSD_CHUNK
    nh = 2 * SSD_HEADS
    dt_t = _softplus(dt_ref[...].T[:nh] + dtb_ref[...])
    a_t = dt_t * (-jnp.exp(alog_ref[...]))
    ii = lax.broadcasted_iota(jnp.int32, (cn, cn), 0)
    jj = lax.broadcasted_iota(jnp.int32, (cn, cn), 1)
    keep = (ii >= jj) if forward else (ii <= jj)
    tri_t = ((ii <= jj) if forward else (ii >= jj)).astype(BF16)
    c3 = _dot(jnp.concatenate(_split3(a_t), axis=0), tri_t)
    cum_t = c3[:nh] + c3[nh:2 * nh] + c3[2 * nh:]
    cols = jnp.concatenate([dt_t, cum_t, jnp.zeros((LANES - 2 * nh, cn), F32)], axis=0).T
    lane = lax.broadcasted_iota(jnp.int32, (1, LANES), 1)
    return dict(cum_t=cum_t, cols=cols, keep=keep, lo=lane < SSD_HEAD_DIM, forward=forward)


def _ssd_mats(act, prep):
    lo = prep["lo"]
    xs = act[:, :SSD_WIDTH]
    bm = act[:, SSD_WIDTH:SSD_WIDTH + LANES]
    cm = act[:, SSD_WIDTH + LANES:]
    bm_b = bm.astype(BF16)
    gmat = []
    bg_b = []
    for g in range(SSD_GROUPS):
        in_group = lo if g == 0 else jnp.logical_not(lo)
        gmat.append(_dot_nt(jnp.where(in_group, cm, 0.0).astype(BF16), bm_b))
        bg_b.append(jnp.where(in_group, bm, 0.0).astype(BF16))
    return dict(prep, xs=xs, cm_b=cm.astype(BF16), gmat=gmat, bg_b=bg_b)


def _ssd_pair(prep, pair, s_ref, y_ref, dsk_ref):
    cn = SSD_CHUNK
    nh = 2 * SSD_HEADS
    forward, cols, cum_t, keep, lo = prep["forward"], prep["cols"], prep["cum_t"], prep["keep"], prep["lo"]

    def col(lane_idx):
        return jnp.broadcast_to(cols[:, lane_idx:lane_idx + 1], (cn, LANES))

    off = 0 if forward else SSD_HEADS
    tot_col = cn - 1 if forward else 0
    g = pair // (SSD_HEADS // 2 // SSD_GROUPS)
    gmat = prep["gmat"][g]
    h0 = off + 2 * pair
    h1 = h0 + 1
    ci0 = col(nh + h0)
    ci1 = col(nh + h1)
    cip = jnp.where(lo, ci0, ci1)
    m0 = (gmat * jnp.exp(jnp.where(keep, ci0 - cum_t[h0:h0 + 1, :], -1e30))).astype(BF16)
    m1 = (gmat * jnp.exp(jnp.where(keep, ci1 - cum_t[h1:h1 + 1, :], -1e30))).astype(BF16)
    xs_p = prep["xs"][:, pair * LANES:(pair + 1) * LANES]
    vp = xs_p * jnp.where(lo, col(h0), col(h1))
    v0 = jnp.where(lo, vp, 0.0).astype(BF16)
    v1 = jnp.where(lo, 0.0, vp).astype(BF16)
    s_old = s_ref[pair]
    y = (_dot(jnp.concatenate([m0, m1], axis=1), jnp.concatenate([v0, v1], axis=0))
         + jnp.exp(cip) * _dot(prep["cm_b"], s_old.astype(BF16)))
    totp = jnp.where(lo, cum_t[h0:h0 + 1, tot_col:tot_col + 1], cum_t[h1:h1 + 1, tot_col:tot_col + 1])
    s_ref[pair] = s_old * jnp.exp(totp) + _dot_tn(prep["bg_b"][g], (vp * jnp.exp(totp - cip)).astype(BF16))
    if forward:
        y = y + dsk_ref[:, pair * LANES:(pair + 1) * LANES] * xs_p
    y_ref[:, pair * LANES:(pair + 1) * LANES] = y.astype(y_ref.dtype)


def _ssd_kernel(*refs, nc, ns, has_init, want_fin):
    refs = list(refs)
    cf_ref, pf_ref, nf_ref, dtf_ref, cb_ref, pb_ref, nb_ref, dtb_ref = refs[:8]
    cw_ref, cbias_ref, dtbias_ref, alog_ref, dsk_ref, shift_ref = refs[8:14]
    pos = 14
    s0_ref = None
    if has_init:
        s0_ref = refs[pos]
        pos += 1
    yf_ref, yb_ref = refs[pos:pos + 2]
    pos += 2
    sfin_ref = None
    if want_fin:
        sfin_ref = refs[pos]
        pos += 1
    ext_ref, act_ref, sf_ref, sb_ref = refs[pos:]
    c = pl.program_id(1)
    c_fwd = c
    c_bwd = nc - 1 - c

    @pl.when(c == 0)
    def _():
        sf_ref[...] = jnp.zeros_like(sf_ref)
        sb_ref[...] = jnp.zeros_like(sb_ref)
        if has_init:
            for s in range(ns):
                for pair in range(SSD_HEADS // 2):
                    rows = _ssd_group_rows(pair)
                    sf_ref[s, pair, rows, :] = s0_ref[s, 0, pair]
                    sb_ref[s, pair, rows, :] = s0_ref[s, 1, pair]

    def scalars():
        out = []
        for s in range(ns):
            out.append(_ssd_scalars(dtf_ref.at[0, s], dtbias_ref, alog_ref, forward=True))
            out.append(_ssd_scalars(dtb_ref.at[0, s], dtbias_ref, alog_ref, forward=False))
        return out

    def streams(preps, acts):
        work = []
        for s in range(ns):
            work.append((_ssd_mats(acts[2 * s], preps[2 * s]), sf_ref.at[s], yf_ref.at[0, s]))
            work.append((_ssd_mats(acts[2 * s + 1], preps[2 * s + 1]), sb_ref.at[s], yb_ref.at[0, s]))
        for pair in range(SSD_HEADS // 2):
            for prep, s_ref, y_ref in work:
                _ssd_pair(prep, pair, s_ref, y_ref, dsk_ref)

    @pl.when(c < nc // 2)
    def _():
        preps = scalars()
        acts = []
        for s in range(ns):
            act_f = _ssd_conv(cf_ref.at[0, s], pf_ref.at[0, s], nf_ref.at[0, s], c_fwd == 0, c_fwd == nc - 1,
                              ext_ref.at[2 * s], cw_ref, cbias_ref, shift_ref)
            act_b = _ssd_conv(cb_ref.at[0, s], pb_ref.at[0, s], nb_ref.at[0, s], c_bwd == 0, c_bwd == nc - 1,
                              ext_ref.at[2 * s + 1], cw_ref, cbias_ref, shift_ref)
            act_ref[s, c_fwd] = act_f
            act_ref[s, c_bwd] = act_b
            acts += [act_f, act_b]
        streams(preps, acts)

    @pl.when(c >= nc // 2)
    def _():
        acts = []
        for s in range(ns):
            acts += [act_ref[s, c_fwd], act_ref[s, c_bwd]]
        streams(scalars(), acts)

    if want_fin:
        @pl.when(c == nc - 1)
        def _():
            for s in range(ns):
                for pair in range(SSD_HEADS // 2):
                    rows = _ssd_group_rows(pair)
                    for d, st_ref in enumerate((sf_ref, sb_ref)):
                        both = st_ref[s, pair, rows, :]
                        sfin_ref[s, d, 2 * pair] = both[:, :SSD_HEAD_DIM]
                        sfin_ref[s, d, 2 * pair + 1] = both[:, SSD_HEAD_DIM:]


def _ssd(xbc, dt, p, s0, batch, want_fin):
    m = xbc.shape[0]
    cn = SSD_CHUNK
    ns = SSD_SEQS
    seq = m // batch
    nc = seq // cn
    assert nc % 2 == 0
    assert batch % ns == 0
    per = cn // HALO
    n_halo = seq // HALO
    has_init = s0 is not None
    npair = SSD_HEADS // 2
    view = lambda a: a.reshape(batch // ns, ns, seq, a.shape[-1])

    def fwd(c):
        return c

    def bwd(c):
        return nc - 1 - c

    def stream_specs(chunk):
        return [pl.BlockSpec((1, ns, cn, SSD_XBC), lambda b, c: (b, 0, chunk(c), 0)),
                pl.BlockSpec((1, ns, HALO, SSD_XBC), lambda b, c: (b, 0, jnp.maximum(chunk(c) * per - 1, 0), 0)),
                pl.BlockSpec((1, ns, HALO, SSD_XBC),
                             lambda b, c: (b, 0, jnp.minimum(chunk(c) * per + per, n_halo - 1), 0)),
                pl.BlockSpec((1, ns, cn, LANES), lambda b, c: (b, 0, chunk(c), 0))]

    in_specs = stream_specs(fwd) + stream_specs(bwd) + [
        _const_spec((HALO, SSD_XBC)), _const_spec((1, SSD_XBC)), _const_spec((2 * SSD_HEADS, cn)),
        _const_spec((2 * SSD_HEADS, cn)), _const_spec((1, SSD_WIDTH)),
        _const_spec(((SSD_CONV - 1) * cn, cn + 2 * HALO))]
    taps = [t for t in range(SSD_CONV) if t != SSD_CONV // 2]
    src = np.concatenate([HALO + np.arange(cn) + (t - SSD_CONV // 2) for t in taps])
    shift = jnp.asarray(src[:, None] == np.arange(cn + 2 * HALO)[None, :], BF16)
    xv, dv = view(xbc), view(dt)
    args = [xv, xv, xv, dv, xv, xv, xv, dv, p["conv_w"], p["conv_b"], p["dt_bias"], p["a_log"], p["d_skip"], shift]
    state_block = (ns, 2, npair, SSD_STATE, LANES)
    if has_init:
        in_specs.append(pl.BlockSpec(state_block, lambda b, c: (b, 0, 0, 0, 0)))
        args.append(s0)
    out_specs = [pl.BlockSpec((1, ns, cn, SSD_WIDTH), lambda b, c: (b, 0, fwd(c), 0)),
                 pl.BlockSpec((1, ns, cn, SSD_WIDTH), lambda b, c: (b, 0, bwd(c), 0))]
    out_shape = [jax.ShapeDtypeStruct((batch // ns, ns, seq, SSD_WIDTH), BF16)] * 2
    if want_fin:
        fin_block = (ns, 2, SSD_HEADS, SSD_STATE, SSD_HEAD_DIM)
        out_specs.append(pl.BlockSpec(fin_block, lambda b, c: (b, 0, 0, 0, 0)))
        out_shape.append(jax.ShapeDtypeStruct((batch,) + fin_block[1:], F32))
    out = pl.pallas_call(
        functools.partial(_ssd_kernel, nc=nc, ns=ns, has_init=has_init, want_fin=want_fin),
        grid=(batch // ns, nc),
        in_specs=in_specs,
        out_specs=out_specs,
        out_shape=out_shape,
        scratch_shapes=[pltpu.VMEM((2 * ns, cn + 2 * HALO, SSD_XBC), F32), pltpu.VMEM((ns, nc, cn, SSD_XBC), F32),
                        pltpu.VMEM((ns, npair, LANES, LANES), F32), pltpu.VMEM((ns, npair, LANES, LANES), F32)],
        compiler_params=_params(2),
        name="ssd",
    )(*args)
    return [out[0].reshape(m, SSD_WIDTH), out[1].reshape(m, SSD_WIDTH)] + list(out[2:])


def _ssd_state_to_pairs(s):
    b = s.shape[0]
    npair = SSD_HEADS // 2
    s = s.reshape(b, 2, npair, 2, SSD_STATE, SSD_HEAD_DIM).transpose(0, 1, 2, 4, 3, 5)
    return s.reshape(b, 2, npair, SSD_STATE, 2 * SSD_HEAD_DIM)


def _l1_in_kernel(*refs, use_rope):
    if use_rope:
        x_ref, nw_ref, sh_ref, sc_ref, w_ref, cos_ref, sin_ref, q_ref, k_ref, v_ref, g_ref = refs
    else:
        x_ref, nw_ref, sh_ref, sc_ref, w_ref, q_ref, k_ref, v_ref, g_ref = refs
    h = (_rms(x_ref[...]) * nw_ref[...] * (1.0 + sc_ref[0]) + sh_ref[0]).astype(BF16)
    half = RET_DK // 2

    def rope(t):
        if not use_rope:
            return t
        cos = cos_ref[...]
        sin = sin_ref[...]
        parts = []
        for hd in range(RET_HEADS):
            x1 = t[:, hd * RET_DK:hd * RET_DK + half]
            x2 = t[:, hd * RET_DK + half:(hd + 1) * RET_DK]
            parts += [x1 * cos - x2 * sin, x2 * cos + x1 * sin]
        return jnp.concatenate(parts, axis=1)

    o1 = RET_QK_WIDTH
    o2 = 2 * RET_QK_WIDTH
    o3 = o2 + RET_V_WIDTH
    q_ref[...] = rope(_dot(h, w_ref[:, :o1])).astype(BF16)
    k_ref[...] = (rope(_dot(h, w_ref[:, o1:o2])) * (RET_DK ** -0.5)).astype(k_ref.dtype)
    v_ref[...] = _dot(h, w_ref[:, o2:o3]).astype(BF16)
    g_ref[...] = _silu(_dot(h, w_ref[:, o3:])).astype(g_ref.dtype)


def _l1_in(x, mod, rows_per_mod, p, rope, tm):
    m = x.shape[0]
    use_rope = rope is not None
    row = lambda w: pl.BlockSpec((tm, w), lambda i: (i, 0))
    n = 2 * RET_QK_WIDTH + 2 * RET_V_WIDTH
    in_specs = ([row(D_MODEL), _const_spec((1, D_MODEL))] + _mod_specs(tm, rows_per_mod, (0, 1))
                + [_resident_spec((D_MODEL, n))])
    args = [x, p["norm_mix"], mod, mod, p["w_in"]]
    if use_rope:
        rows = rope[0].shape[0]
        in_specs += [pl.BlockSpec((tm, LANES), lambda i: (i % (rows // tm), 0))] * 2
        args += list(rope)
    widths = [(RET_QK_WIDTH, BF16), (RET_QK_WIDTH, BF16), (RET_V_WIDTH, BF16), (RET_V_WIDTH, BF16)]
    return pl.pallas_call(
        functools.partial(_l1_in_kernel, use_rope=use_rope),
        grid=(m // tm,),
        in_specs=in_specs,
        out_specs=[row(w) for w, _ in widths],
        out_shape=[jax.ShapeDtypeStruct((m, w), dt) for w, dt in widths],
        compiler_params=_params(1),
        name="l1_in",
    )(*args)


def _ret_kernel(*refs, cn, nc, has_init, want_fin):
    stateless = (not has_init) and nc == 1
    refs = list(refs)
    dec_ref, qf_ref, kf_ref, vf_ref = refs[:4]
    pos = 4
    if stateless:
        qb_ref, kb_ref, vb_ref = qf_ref, kf_ref, vf_ref
    else:
        qb_ref, kb_ref, vb_ref = refs[pos:pos + 3]
        pos += 3
    s0_ref = None
    if has_init:
        s0_ref = refs[pos]
        pos += 1
    yf_ref = refs[pos]
    pos += 1
    yb_ref = None
    if not stateless:
        yb_ref = refs[pos]
        pos += 1
    sfin_ref = None
    if want_fin:
        sfin_ref = refs[pos]
        pos += 1
    decay_ref, rowdec_ref, sf_ref, sb_ref = refs[pos:]
    c = pl.program_id(1)
    log_g = -jnp.exp(dec_ref[...])

    @pl.when((pl.program_id(0) == 0) & (c == 0))
    def _():
        ii = lax.broadcasted_iota(jnp.int32, (cn, cn), 0)
        jj = lax.broadcasted_iota(jnp.int32, (cn, cn), 1)
        dist = (ii - jj).astype(F32)
        ri = lax.broadcasted_iota(jnp.int32, (cn, LANES), 0).astype(F32)
        for hd in range(RET_HEADS):
            gf = log_g[0:1, hd:hd + 1]
            gb = log_g[1:2, hd:hd + 1]
            decay_ref[hd] = (jnp.where(dist >= 0, jnp.exp(gf * jnp.maximum(dist, 0.0)), 0.0)
                             + jnp.where(dist <= 0, jnp.exp(gb * jnp.maximum(-dist, 0.0)), 0.0))
            rowdec_ref[hd, 0] = jnp.exp(gf * (ri + 1.0))
            rowdec_ref[hd, 1] = jnp.exp(gf * (cn - 1.0 - ri))
            rowdec_ref[hd, 2] = jnp.exp(gb * (cn - ri))
            rowdec_ref[hd, 3] = jnp.exp(gb * ri)

    def rowdec(hd, which, width):
        return jnp.concatenate([rowdec_ref[hd, which]] * (width // LANES), axis=1)

    if not stateless:
        @pl.when(c == 0)
        def _():
            if has_init:
                sf_ref[...] = s0_ref[0, 0]
                sb_ref[...] = s0_ref[0, 1]
            else:
                sf_ref[...] = jnp.zeros_like(sf_ref)
                sb_ref[...] = jnp.zeros_like(sb_ref)

    for hd in range(RET_HEADS):
        gf = log_g[0:1, hd:hd + 1]
        gb = log_g[1:2, hd:hd + 1]
        qs = slice(hd * RET_DK, (hd + 1) * RET_DK)
        vs = slice(hd * RET_DV, (hd + 1) * RET_DV)
        q = qf_ref[:, qs]
        k = kf_ref[:, qs]
        v = vf_ref[:, vs]
        y = _dot((_dot_nt(q, k.astype(BF16)) * decay_ref[hd]).astype(BF16), v)
        upd_f = _dot_tn((k * rowdec(hd, 1, RET_DK)).astype(BF16), v)
        if stateless:
            new_f = upd_f
        else:
            s_old = sf_ref[hd]
            y = y + rowdec(hd, 0, RET_DV) * _dot(q, s_old.astype(BF16))
            new_f = s_old * jnp.exp(gf * cn) + upd_f
            sf_ref[hd] = new_f
        yf_ref[:, vs] = y.astype(yf_ref.dtype)
        q = qb_ref[:, qs]
        k = kb_ref[:, qs]
        v = vb_ref[:, vs]
        upd_b = _dot_tn((k * rowdec(hd, 3, RET_DK)).astype(BF16), v)
        if stateless:
            new_b = upd_b
        else:
            s_old = sb_ref[hd]
            yb_ref[:, vs] = (rowdec(hd, 2, RET_DV) * _dot(q, s_old.astype(BF16))).astype(yb_ref.dtype)
            new_b = s_old * jnp.exp(gb * cn) + upd_b
            sb_ref[hd] = new_b
        if want_fin:
            if stateless:
                sfin_ref[0, 0, hd] = new_f
                sfin_ref[0, 1, hd] = new_b
            else:
                @pl.when(c == nc - 1)
                def _(new_f=new_f, new_b=new_b, hd=hd):
                    sfin_ref[0, 0, hd] = new_f
                    sfin_ref[0, 1, hd] = new_b


def _retention(q, k, v, decay, s0, batch, cn, want_fin):
    m = q.shape[0]
    nc = m // batch // cn
    has_init = s0 is not None
    stateless = (not has_init) and nc == 1

    def fwd(b, c):
        return b * nc + c

    def bwd(b, c):
        return b * nc + nc - 1 - c

    def stream_specs(chunk):
        return [pl.BlockSpec((cn, RET_QK_WIDTH), lambda b, c: (chunk(b, c), 0)),
                pl.BlockSpec((cn, RET_QK_WIDTH), lambda b, c: (chunk(b, c), 0)),
                pl.BlockSpec((cn, RET_V_WIDTH), lambda b, c: (chunk(b, c), 0))]

    in_specs = [_const_spec((8, LANES))] + stream_specs(fwd)
    args = [decay, q, k, v]
    if not stateless:
        in_specs += stream_specs(bwd)
        args += [q, k, v]
    state_block = (1, 2, RET_HEADS, RET_DK, RET_DV)
    if has_init:
        in_specs.append(pl.BlockSpec(state_block, lambda b, c: (b, 0, 0, 0, 0)))
        args.append(s0)
    out_specs = [pl.BlockSpec((cn, RET_V_WIDTH), lambda b, c: (fwd(b, c), 0))]
    if not stateless:
        out_specs.append(pl.BlockSpec((cn, RET_V_WIDTH), lambda b, c: (bwd(b, c), 0)))
    out_shape = [jax.ShapeDtypeStruct((m, RET_V_WIDTH), BF16)] * len(out_specs)
    if want_fin:
        out_specs.append(pl.BlockSpec(state_block, lambda b, c: (b, 0, 0, 0, 0)))
        out_shape.append(jax.ShapeDtypeStruct((batch,) + state_block[1:], F32))
    return pl.pallas_call(
        functools.partial(_ret_kernel, cn=cn, nc=nc, has_init=has_init, want_fin=want_fin),
        grid=(batch, nc),
        in_specs=in_specs,
        out_specs=out_specs,
        out_shape=out_shape,
        scratch_shapes=[pltpu.VMEM((RET_HEADS, cn, cn), F32), pltpu.VMEM((RET_HEADS, 4, cn, LANES), F32)]
        + [pltpu.VMEM((RET_HEADS, RET_DK, RET_DV), F32)] * 2,
        compiler_params=_params(2),
        name="retention",
    )(*args)


def _post_kernel(*refs, mixer, final):
    refs = list(refs)
    x_ref, g1_ref, nw_ref, sh2_ref, sc2_ref, g2_ref, wout_ref, wg_ref, wu_ref, wd_ref = refs[:10]
    pos = 10
    fn_ref = None
    if final:
        fn_ref = refs[pos]
        pos += 1
    if mixer == "ab":
        att_ref, yf_ref, yb_ref, z_ref, gain_ref, o_ref = refs[pos:]
        y = (yf_ref[...].astype(F32) + yb_ref[...].astype(F32)) * _silu(z_ref[...].astype(F32))
        y = _rms(y) * gain_ref[...]
        mix = _dot(jnp.concatenate([att_ref[...], y.astype(BF16)], axis=1), wout_ref[...])
    else:
        parts = refs[pos:-3]
        gate_ref, gain_ref, o_ref = refs[-3:]
        mix = None
        for hd in range(RET_HEADS):
            vs = slice(hd * RET_DV, (hd + 1) * RET_DV)
            y = parts[0][:, vs].astype(F32)
            for extra in parts[1:]:
                y = y + extra[:, vs].astype(F32)
            y = _rms(y) * gain_ref[:, vs]
            part = _dot((gate_ref[:, vs].astype(F32) * y).astype(BF16), wout_ref[vs, :])
            mix = part if mix is None else mix + part
    x1 = x_ref[...] + g1_ref[0] * mix
    h = (_rms(x1) * nw_ref[...] * (1.0 + sc2_ref[0]) + sh2_ref[0]).astype(BF16)
    act = (_silu(_dot(h, wg_ref[...])) * _dot(h, wu_ref[...])).astype(BF16)
    x2 = x1 + g2_ref[0] * _dot(act, wd_ref[...])
    if final:
        x2 = _rms(x2) * fn_ref[...]
    o_ref[...] = x2


def _post(x, mod, rows_per_mod, p, mixer, mix_inputs, gain, final_norm, tm):
    m = x.shape[0]
    row = lambda w: pl.BlockSpec((tm, w), lambda i: (i, 0))
    mixw = p["w_out"].shape[0]
    (g1,) = _mod_specs(tm, rows_per_mod, (2,))
    sh2, sc2, g2 = _mod_specs(tm, rows_per_mod, (3, 4, 5))
    in_specs = [row(D_MODEL), g1, _const_spec((1, D_MODEL)), sh2, sc2, g2,
                _resident_spec((mixw, D_MODEL)), _resident_spec((D_MODEL, D_FF)), _resident_spec((D_MODEL, D_FF)),
                _resident_spec((D_FF, D_MODEL))]
    args = [x, mod, p["norm_ffn"], mod, mod, mod, p["w_out"], p["w_gate"], p["w_up"], p["w_down"]]
    final = final_norm is not None
    if final:
        in_specs.append(_const_spec((1, D_MODEL)))
        args.append(final_norm)
    in_specs += [row(a.shape[1]) for a in mix_inputs] + [_const_spec(gain.shape)]
    args += list(mix_inputs) + [gain]
    return pl.pallas_call(
        functools.partial(_post_kernel, mixer=mixer, final=final),
        grid=(m // tm,),
        in_specs=in_specs,
        out_specs=row(D_MODEL),
        out_shape=jax.ShapeDtypeStruct((m, D_MODEL), F32),
        compiler_params=_params(1),
        name="post_" + mixer,
    )(*args)


def _axial_angles(n_tokens, dim):
    rows = n_tokens // GRID_W
    row = np.repeat(np.arange(rows), GRID_W).astype(np.float64)
    col = np.tile(np.arange(GRID_W), rows).astype(np.float64)
    n_freq = dim // 4
    inv = ROPE_THETA ** (-np.arange(n_freq, dtype=np.float64) / n_freq)
    return np.concatenate([row[:, None] * inv, col[:, None] * inv], axis=-1)


def _head_mean_matrix(width, head):
    idx = jnp.arange(width) // head
    return jnp.where(idx[:, None] == idx[None, :], 1.0 / head, 0.0).astype(BF16)


def _rows_bcast(v, width):
    return jnp.broadcast_to(v.reshape(-1, 1), (v.size, width))


def _trunk(x, mods, rows_per_mod, p0, p1, final_norm, rope_att, rope_ret, caches, seq):
    m = x.shape[0]
    batch = m // seq
    sample = caches is not None
    tm = ROW_TILE
    l0 = _l0_in(x, mods[0], rows_per_mod, p0, rope_att, tm, None if sample else seq)
    q, ka, va, z, xbc, dt = l0[:6]
    s0_ssd = None
    s0_ret = None
    kv_cache = None
    if sample:
        cache_k, cache_v, state_ssd, state_ret = caches
        ck = cache_k.astype(BF16).transpose(0, 2, 1, 3)
        cv = cache_v.astype(BF16).transpose(0, 2, 1, 3)
        ones_col = jnp.zeros(cv.shape[:3] + (LANES - ATT_HEAD_DIM,), BF16).at[..., 0].set(1.0)
        kv_cache = (ck, jnp.concatenate([cv, ones_col], axis=-1))
        s0_ssd = _ssd_state_to_pairs(state_ssd)
        s0_ret = state_ret
    att = _attention(q.reshape(batch, seq, ATT_WIDTH), ka, va, kv_cache, ATT_KEY_CHUNK, ATT_SUB_ROWS)
    att = att.reshape(m, ATT_WIDTH)
    ssd_out = _ssd(xbc, dt, p0, s0_ssd, batch, want_fin=not sample)
    x = _post(x, mods[0], rows_per_mod, p0, "ab", [att, ssd_out[0], ssd_out[1], z], p0["ssd_gain"], None, tm)
    q1, k1, v1, g1 = _l1_in(x, mods[1], rows_per_mod, p1, rope_ret, tm)
    ret_out = _retention(q1, k1, v1, p1["decay"], s0_ret, batch, RET_CHUNK, want_fin=not sample)
    y_parts = ret_out if sample else ret_out[:-1]
    y = _post(x, mods[1], rows_per_mod, p1, "c", list(y_parts) + [g1], p1["ret_gain"], final_norm, tm)
    if sample:
        return y, None
    new_k = l0[6].transpose(0, 3, 1, 2)
    new_v = l0[7].transpose(0, 3, 1, 2)
    return y, (new_k, new_v, ssd_out[2], ret_out[-1])


def kernel(x_prompt, x_sample, c, cache_k0, cache_v0, state_ssd0, state_ret1, c_ctx, l0_w_ada, l0_b_ada, l0_norm_mix, l0_norm_ffn, l0_w_in, l0_w_out, l0_q_gain, l0_k_gain, l0_conv_w, l0_conv_b, l0_dt_bias, l0_a_log, l0_d_skip, l0_ssd_gain, l0_w_gate, l0_w_up, l0_w_down, l1_w_ada, l1_b_ada, l1_norm_mix, l1_norm_ffn, l1_w_in, l1_w_out, l1_decay, l1_ret_gain, l1_w_gate, l1_w_up, l1_w_down, final_norm):
    b_ctx, seq_ctx, d = x_prompt.shape
    b_lat, seq_lat, _ = x_sample.shape
    assert d == D_MODEL and l0_w_in.shape == (D_MODEL, L0_IN) and l0_w_gate.shape == (D_MODEL, D_FF)
    row = lambda v: v.reshape(1, -1)

    (w0_in, w0_out, w0_gate, w0_up, w0_down, w1_in, w1_out, w1_gate, w1_up, w1_down) = _cast_weights(
        [l0_w_in, l0_w_out, l0_w_gate, l0_w_up, l0_w_down, l1_w_in, l1_w_out, l1_w_gate, l1_w_up, l1_w_down])
    p0 = dict(
        norm_mix=row(l0_norm_mix), norm_ffn=row(l0_norm_ffn),
        w_in=w0_in, w_out=w0_out, w_gate=w0_gate, w_up=w0_up, w_down=w0_down,
        q_gain=row(jnp.tile(l0_q_gain, ATT_HEADS)), k_gain=row(jnp.tile(l0_k_gain, ATT_KV_HEADS)),
        pq=_head_mean_matrix(ATT_WIDTH, ATT_HEAD_DIM), pk=_head_mean_matrix(ATT_KV_WIDTH, ATT_HEAD_DIM),
        conv_w=jnp.pad(l0_conv_w, ((0, HALO - SSD_CONV), (0, 0))), conv_b=row(l0_conv_b),
        dt_bias=_rows_bcast(l0_dt_bias, SSD_CHUNK), a_log=_rows_bcast(l0_a_log, SSD_CHUNK),
        d_skip=row(jnp.repeat(l0_d_skip, SSD_HEAD_DIM)), ssd_gain=row(l0_ssd_gain),
    )
    p1 = dict(
        norm_mix=row(l1_norm_mix), norm_ffn=row(l1_norm_ffn),
        w_in=w1_in, w_out=w1_out, w_gate=w1_gate, w_up=w1_up, w_down=w1_down,
        decay=jnp.pad(l1_decay, ((0, 8 - l1_decay.shape[0]), (0, LANES - l1_decay.shape[1]))),
        ret_gain=row(l1_ret_gain),
    )
    fnorm = row(final_norm)

    n_cond = 8
    conds = jnp.concatenate([c_ctx[None, :], c, jnp.zeros((n_cond - 1 - b_lat, d), F32)], axis=0)
    mod0 = _ada(conds, l0_w_ada, l0_b_ada)
    mod1 = _ada(conds, l1_w_ada, l1_b_ada)
    mods_ctx = [mod[0:1].reshape(1, 1, 6 * d) for mod in (mod0, mod1)]
    mods_lat = [mod[1:1 + b_lat].reshape(b_lat, 1, 6 * d) for mod in (mod0, mod1)]

    m_ctx = b_ctx * seq_ctx
    y_prompt, ctx = _trunk(x_prompt.reshape(m_ctx, d), mods_ctx, m_ctx, p0, p1, fnorm, None, None, None, seq_ctx)
    new_k0, new_v0, new_ssd0, new_ret1 = ctx
    ang = _axial_angles(seq_lat, ATT_HEAD_DIM)
    cos, sin = np.cos(ang), np.sin(ang)
    reps = LANES // ATT_HEAD_DIM
    rope_att = (jnp.asarray(np.tile(np.concatenate([cos, cos], axis=1), (1, reps)), F32),
                jnp.asarray(np.tile(np.concatenate([-sin, sin], axis=1), (1, reps)), F32))
    ang = _axial_angles(seq_lat, RET_DK)
    rope_ret = (jnp.asarray(np.cos(ang), F32), jnp.asarray(np.sin(ang), F32))
    caches = (cache_k0, cache_v0, state_ssd0, state_ret1)
    y_sample, _ = _trunk(x_sample.reshape(b_lat * seq_lat, d), mods_lat, seq_lat, p0, p1, fnorm, rope_att, rope_ret,
                         caches, seq_lat)
    return (y_prompt.reshape(b_ctx, seq_ctx, d), y_sample.reshape(b_lat, seq_lat, d),
            new_k0, new_v0, new_ssd0, new_ret1)
```

```python
import functools

import jax
import jax.numpy as jnp
import numpy as np
from jax import lax
from jax.experimental import pallas as pl
from jax.experimental.pallas import tpu as pltpu

F32 = jnp.float32
BF16 = jnp.bfloat16

EPS = 1e-6
ROPE_THETA = 10000.0
GRID_W = 64
D_MODEL = 1024
ATT_HEAD_DIM = 64
ATT_HEADS = 8
ATT_KV_HEADS = 2
ATT_GROUP = ATT_HEADS // ATT_KV_HEADS
ATT_WIDTH = ATT_HEADS * ATT_HEAD_DIM
ATT_KV_WIDTH = ATT_KV_HEADS * ATT_HEAD_DIM
SSD_WIDTH = 512
SSD_HEADS = 8
SSD_HEAD_DIM = 64
SSD_STATE = 64
SSD_GROUPS = 2
SSD_CONV = 5
SSD_XBC = SSD_WIDTH + 2 * SSD_GROUPS * SSD_STATE
L0_IN = ATT_WIDTH + 2 * ATT_KV_WIDTH + SSD_WIDTH + SSD_XBC + 2 * SSD_HEADS
RET_HEADS = 4
RET_DK = 256
RET_DV = 512
RET_QK_WIDTH = RET_HEADS * RET_DK
RET_V_WIDTH = RET_HEADS * RET_DV
D_FF = 2816

LANES = 128
BF16_SUBLANES = 16
HALO = 8
L0_IN_PAD = -(-L0_IN // LANES) * LANES
MXU_WIDTH = 256
ROW_TILE = 512
ADA_COLS = 1536
CAST_STEPS = 16
ATT_KEY_CHUNK = MXU_WIDTH
ATT_SUB_ROWS = 512
ATT_MAX_UNROLL = 8
ATT_LOOP_UNROLL = 6
SSD_CHUNK = 128
SSD_SEQS = 2
RET_CHUNK = 256
RET_V_TILE = MXU_WIDTH
VMEM_LIMIT = 56 * 1024 * 1024


def _dot(a, b):
    return jnp.dot(a, b, preferred_element_type=F32)


def _dot_nt(a, b):
    return lax.dot_general(a, b, (((1,), (1,)), ((), ())), preferred_element_type=F32)


def _dot_tn(a, b):
    return lax.dot_general(a, b, (((0,), (0,)), ((), ())), preferred_element_type=F32)


def _silu(x):
    half = 0.5 * x
    return half + half * jnp.tanh(half)


def _softplus(x):
    return jnp.maximum(x, 0.0) + jnp.log1p(jnp.exp(-jnp.abs(x)))


def _rms(x):
    return x * lax.rsqrt(jnp.mean(x * x, axis=-1, keepdims=True) + EPS)


def _split3(x):
    hi = x.astype(BF16)
    r = x - hi.astype(F32)
    mid = r.astype(BF16)
    lo = (r - mid.astype(F32)).astype(BF16)
    return hi, mid, lo


def _const_spec(shape):
    return pl.BlockSpec(shape, lambda *_: (0,) * len(shape))


def _resident_spec(shape):
    return pl.BlockSpec(shape, lambda *_: (0,) * len(shape), pipeline_mode=pl.Buffered(1))


def _params(n_axes):
    return pltpu.CompilerParams(dimension_semantics=("arbitrary",) * n_axes, vmem_limit_bytes=VMEM_LIMIT)


def _cast_kernel(*refs):
    n = len(refs) // 2
    for x_ref, o_ref in zip(refs[:n], refs[n:]):
        width = x_ref.shape[1]
        if o_ref.shape[1] == width:
            o_ref[...] = x_ref[...].astype(BF16)
        else:
            o_ref[:, :width] = x_ref[...].astype(BF16)
            o_ref[:, width:] = jnp.zeros((o_ref.shape[0], o_ref.shape[1] - width), BF16)


def _cast_weights(weights):
    steps = CAST_STEPS
    in_specs, out_specs, out_shape = [], [], []
    for w in weights:
        rows, width = w.shape
        assert rows % (steps * BF16_SUBLANES) == 0
        padded = -(-width // LANES) * LANES
        in_specs.append(pl.BlockSpec((rows // steps, width), lambda i: (i, 0)))
        out_specs.append(pl.BlockSpec((rows // steps, padded), lambda i: (i, 0)))
        out_shape.append(jax.ShapeDtypeStruct((rows, padded), BF16))
    return pl.pallas_call(
        _cast_kernel,
        grid=(steps,),
        in_specs=in_specs,
        out_specs=out_specs,
        out_shape=out_shape,
        compiler_params=_params(1),
        name="cast_weights",
    )(*weights)


def _ada_kernel(c_ref, w_ref, b_ref, o_ref):
    s = _silu(c_ref[...])
    o_ref[...] = _dot(s.astype(BF16), w_ref[...].astype(BF16)) + b_ref[...]


def _ada(conds, w, b):
    n = w.shape[1]
    tn = ADA_COLS
    return pl.pallas_call(
        _ada_kernel,
        grid=(n // tn,),
        in_specs=[_const_spec(conds.shape),
                  pl.BlockSpec((D_MODEL, tn), lambda j: (0, j)),
                  pl.BlockSpec((1, tn), lambda j: (0, j))],
        out_specs=pl.BlockSpec((conds.shape[0], tn), lambda j: (0, j)),
        out_shape=jax.ShapeDtypeStruct((conds.shape[0], n), F32),
        compiler_params=_params(1),
        name="ada",
    )(conds, w, b.reshape(1, n))


def _mod_specs(tm, rows_per_mod, which):
    return [pl.BlockSpec((1, 1, D_MODEL), lambda i, j=j: ((i * tm) // rows_per_mod, 0, j)) for j in which]


def _head_rms(x, p_ref, gain):
    x2 = x * x
    hi = x2.astype(BF16)
    lo = (x2 - hi.astype(F32)).astype(BF16)
    ms = _dot(hi, p_ref[...]) + _dot(lo, p_ref[...])
    return x * lax.rsqrt(ms + EPS) * gain


def _rope64(x, cos, sin):
    n = x.shape[1]
    lane = lax.broadcasted_iota(jnp.int32, x.shape, 1)
    first_half = (lane % ATT_HEAD_DIM) < (ATT_HEAD_DIM // 2)
    partner = jnp.where(first_half, pltpu.roll(x, n - ATT_HEAD_DIM // 2, 1), pltpu.roll(x, ATT_HEAD_DIM // 2, 1))
    return x * cos + partner * sin


def _l0_in_kernel(*refs, use_rope, ctx_seq):
    refs = list(refs)
    x_ref, nw_ref, sh_ref, sc_ref, w_ref, qg_ref, kg_ref, pq_ref, pk_ref = refs[:9]
    pos = 9
    if use_rope:
        cos_ref, sin_ref = refs[pos:pos + 2]
        pos += 2
    q_ref, ka_ref, va_ref, z_ref, xbc_ref, dt_ref = refs[pos:pos + 6]
    pos += 6
    h = _rms(x_ref[...]) * nw_ref[...] * (1.0 + sc_ref[0]) + sh_ref[0]
    proj = _dot(h.astype(BF16), w_ref[...])
    o1 = ATT_WIDTH
    o2 = o1 + ATT_KV_WIDTH
    o3 = o2 + ATT_KV_WIDTH
    o4 = o3 + SSD_WIDTH
    o5 = o4 + SSD_XBC
    q = _head_rms(proj[:, :o1], pq_ref, qg_ref[...])
    k = _head_rms(proj[:, o1:o2], pk_ref, kg_ref[...])
    v = proj[:, o2:o3]
    if ctx_seq:
        kt_ref, vt_ref = refs[pos:]
        for s in range(x_ref.shape[0] // ctx_seq):
            kt = k[s * ctx_seq:(s + 1) * ctx_seq].T
            vt = v[s * ctx_seq:(s + 1) * ctx_seq].T
            for kv in range(ATT_KV_HEADS):
                kt_ref[s, kv] = kt[kv * ATT_HEAD_DIM:(kv + 1) * ATT_HEAD_DIM]
                vt_ref[s, kv] = vt[kv * ATT_HEAD_DIM:(kv + 1) * ATT_HEAD_DIM]
    if use_rope:
        cos = cos_ref[...]
        sin = sin_ref[...]
        k = _rope64(k, cos, sin)
        reps = ATT_WIDTH // LANES
        q = _rope64(q, jnp.concatenate([cos] * reps, axis=1), jnp.concatenate([sin] * reps, axis=1))
    q_ref[...] = (q * (ATT_HEAD_DIM ** -0.5)).astype(BF16)
    kb = k.astype(BF16)
    lane = lax.broadcasted_iota(jnp.int32, (1, LANES), 1)
    ones_col = jnp.where(lane == ATT_HEAD_DIM, 1.0, 0.0)
    for kv in range(ATT_KV_HEADS):
        ka_ref[kv] = kb[:, kv * ATT_HEAD_DIM:(kv + 1) * ATT_HEAD_DIM]
        vv = v if kv == 0 else pltpu.roll(v, (LANES - kv * ATT_HEAD_DIM) % LANES, 1)
        va_ref[kv] = jnp.where(lane < ATT_HEAD_DIM, vv, ones_col).astype(BF16)
    z_ref[...] = proj[:, o3:o4].astype(z_ref.dtype)
    xbc_ref[...] = proj[:, o4:o5]
    dt_ref[...] = proj[:, o5:]


def _l0_in(x, mod, rows_per_mod, p, rope, tm, ctx_seq):
    m = x.shape[0]
    use_rope = rope is not None
    row = lambda w: pl.BlockSpec((tm, w), lambda i: (i, 0))
    in_specs = ([row(D_MODEL), _const_spec((1, D_MODEL))] + _mod_specs(tm, rows_per_mod, (0, 1))
                + [_resident_spec((D_MODEL, L0_IN_PAD)), _const_spec((1, ATT_WIDTH)), _const_spec((1, ATT_KV_WIDTH)),
                   _resident_spec((ATT_WIDTH, ATT_WIDTH)), _resident_spec((ATT_KV_WIDTH, ATT_KV_WIDTH))])
    args = [x, p["norm_mix"], mod, mod, p["w_in"], p["q_gain"], p["k_gain"], p["pq"], p["pk"]]
    if use_rope:
        rows = rope[0].shape[0]
        in_specs += [pl.BlockSpec((tm, LANES), lambda i: (i % (rows // tm), 0))] * 2
        args += list(rope)
    head = lambda w: pl.BlockSpec((ATT_KV_HEADS, tm, w), lambda i: (0, i, 0))
    out_specs = [row(ATT_WIDTH), head(ATT_HEAD_DIM), head(LANES), row(SSD_WIDTH), row(SSD_XBC), row(LANES)]
    out_shape = [jax.ShapeDtypeStruct((m, ATT_WIDTH), BF16),
                 jax.ShapeDtypeStruct((ATT_KV_HEADS, m, ATT_HEAD_DIM), BF16),
                 jax.ShapeDtypeStruct((ATT_KV_HEADS, m, LANES), BF16),
                 jax.ShapeDtypeStruct((m, SSD_WIDTH), BF16), jax.ShapeDtypeStruct((m, SSD_XBC), F32),
                 jax.ShapeDtypeStruct((m, LANES), F32)]
    if ctx_seq:
        assert tm % ctx_seq == 0
        cache_block = (tm // ctx_seq, ATT_KV_HEADS, ATT_HEAD_DIM, ctx_seq)
        out_specs += [pl.BlockSpec(cache_block, lambda i: (i, 0, 0, 0))] * 2
        out_shape += [jax.ShapeDtypeStruct((m // ctx_seq,) + cache_block[1:], F32)] * 2
    return pl.pallas_call(
        functools.partial(_l0_in_kernel, use_rope=use_rope, ctx_seq=ctx_seq),
        grid=(m // tm,),
        in_specs=in_specs,
        out_specs=out_specs,
        out_shape=out_shape,
        compiler_params=_params(1),
        name="l0_in",
    )(*args)


def _attn_kernel(*refs, ck, sb, has_cache, kv_blk):
    if has_cache:
        q_ref, k_ref, v_ref, kc_ref, vc_ref, o_ref, s_all_ref, m_all_ref = refs
    else:
        q_ref, k_ref, v_ref, o_ref, s_all_ref, m_all_ref = refs
    gw = q_ref.shape[2] // kv_blk
    tt = sb // ATT_GROUP
    nsub = q_ref.shape[1] // tt
    slots = [(s_all_ref.at[n], m_all_ref.at[n]) for n in range(s_all_ref.shape[0])]

    def chunks(kv):
        out = []
        if has_cache:
            out += [(kc_ref.at[0, kv], vc_ref.at[0, kv], j * ck) for j in range(kc_ref.shape[2] // ck)]
        return out + [(k_ref.at[kv], v_ref.at[kv], j * ck) for j in range(k_ref.shape[1] // ck)]

    def token_rows(i):
        if isinstance(i, int):
            return slice(i * tt, (i + 1) * tt)
        return pl.ds(pl.multiple_of(i * tt, tt), tt)

    def scores(kv, i, slot):
        s_ref, m_ref = slots[slot]
        q4 = q_ref[0, token_rows(i), kv * gw:(kv + 1) * gw]
        q = jnp.concatenate([q4[:, g * ATT_HEAD_DIM:(g + 1) * ATT_HEAD_DIM] for g in range(ATT_GROUP)], axis=0)
        mx = None
        for j, (kr, _, r0) in enumerate(chunks(kv)):
            s = _dot_nt(q, kr[r0:r0 + ck, :])
            s_ref[j] = s
            for t in range(ck // LANES):
                part = s[:, t * LANES:(t + 1) * LANES]
                mx = part if mx is None else jnp.maximum(mx, part)
        m_ref[...] = jnp.broadcast_to(jnp.max(mx, axis=1, keepdims=True), (sb, LANES))

    def values(kv, i, slot):
        s_ref, m_ref = slots[slot]
        m = jnp.concatenate([m_ref[...]] * (ck // LANES), axis=1)
        acc = None
        for j, (_, vr, r0) in enumerate(chunks(kv)):
            p = jnp.exp(s_ref[j] - m).astype(BF16)
            part = _dot(p, vr[r0:r0 + ck, :])
            acc = part if acc is None else acc + part
        out = acc[:, :ATT_HEAD_DIM] / acc[:, ATT_HEAD_DIM:ATT_HEAD_DIM + 1]
        out = jnp.concatenate([out[g * tt:(g + 1) * tt] for g in range(ATT_GROUP)], axis=1)
        o_ref[0, token_rows(i), kv * gw:(kv + 1) * gw] = out.astype(o_ref.dtype)

    if len(slots) > 2:
        items = [(kv, i) for kv in range(kv_blk) for i in range(nsub)]
        for n, item in enumerate(items):
            scores(*item, n)
        for n, item in enumerate(items):
            values(*item, n)
        return

    unroll = ATT_LOOP_UNROLL
    for kv in range(kv_blk):
        scores(kv, 0, 0)

        def body(h, carry, kv=kv):
            for u in range(unroll):
                scores(kv, unroll * h + u + 1, (u + 1) % 2)
                values(kv, unroll * h + u, u % 2)
            return carry

        trips = (nsub - 1) // unroll
        lax.fori_loop(0, trips, body, 0)
        for n in range(trips * unroll, nsub - 1):
            scores(kv, n + 1, (n + 1) % 2)
            values(kv, n, n % 2)
        values(kv, nsub - 1, (nsub - 1) % 2)


def _attention(q, k, v, cache, ck, sb):
    b, seq, width = q.shape
    nkv = k.shape[0]
    gw = width // nkv
    nsub = seq * ATT_GROUP // sb
    flat = nkv * nsub <= ATT_MAX_UNROLL
    kv_blk = nkv if flat else 1
    n_slots = kv_blk * nsub if flat else 2
    lk = seq
    in_specs = [pl.BlockSpec((1, seq, gw * kv_blk), lambda i, j: (i, 0, j)),
                pl.BlockSpec((kv_blk, seq, ATT_HEAD_DIM), lambda i, j: (j, i, 0)),
                pl.BlockSpec((kv_blk, seq, LANES), lambda i, j: (j, i, 0))]
    args = [q, k, v]
    if cache is not None:
        past = cache[0].shape[2]
        assert past % ck == 0
        lk += past
        in_specs += [pl.BlockSpec((1, kv_blk, past, ATT_HEAD_DIM), lambda i, j: (i, j, 0, 0)),
                     pl.BlockSpec((1, kv_blk, past, LANES), lambda i, j: (i, j, 0, 0))]
        args += list(cache)
    assert (seq * ATT_GROUP) % (2 * sb) == 0 and seq % ck == 0
    return pl.pallas_call(
        functools.partial(_attn_kernel, ck=ck, sb=sb, has_cache=cache is not None, kv_blk=kv_blk),
        grid=(b, nkv // kv_blk),
        in_specs=in_specs,
        out_specs=pl.BlockSpec((1, seq, gw * kv_blk), lambda i, j: (i, 0, j)),
        out_shape=jax.ShapeDtypeStruct(q.shape, BF16),
        scratch_shapes=[pltpu.VMEM((n_slots, lk // ck, sb, ck), F32), pltpu.VMEM((n_slots, sb, LANES), F32)],
        compiler_params=_params(2),
        name="attention",
    )(*args)


def _ssd_group_rows(pair):
    g = pair // (SSD_HEADS // 2 // SSD_GROUPS)
    return slice(g * SSD_STATE, (g + 1) * SSD_STATE)


def _ssd_conv(cur_ref, prev_ref, next_ref, is_first, is_last, ext_ref, cw_ref, cb_ref, shift_ref):
    cn = SSD_CHUNK
    ext_ref[0:HALO, :] = jnp.where(is_first, 0.0, prev_ref[...])
    ext_ref[HALO:HALO + cn, :] = cur_ref[...]
    ext_ref[HALO + cn:, :] = jnp.where(is_last, 0.0, next_ref[...])
    shifted = _dot(shift_ref[...], ext_ref[...].astype(BF16))
    centre = SSD_CONV // 2
    u = cb_ref[...] + cw_ref[centre:centre + 1, :] * cur_ref[...]
    for n, t in enumerate(t for t in range(SSD_CONV) if t != centre):
        u = u + cw_ref[t:t + 1, :] * shifted[n * cn:(n + 1) * cn]
    return _silu(u)


def _ssd_scalars(dt_ref, dtb_ref, alog_ref, *, forward):
    cn = SSD_CHUNK
    nh = 2 * SSD_HEADS
    dt_t = _softplus(dt_ref[...].T[:nh] + dtb_ref[...])
    a_t = dt_t * (-jnp.exp(alog_ref[...]))
    ii = lax.broadcasted_iota(jnp.int32, (cn, cn), 0)
    jj = lax.broadcasted_iota(jnp.int32, (cn, cn), 1)
    keep = (ii >= jj) if forward else (ii <= jj)
    tri_t = ((ii <= jj) if forward else (ii >= jj)).astype(BF16)
    c3 = _dot(jnp.concatenate(_split3(a_t), axis=0), tri_t)
    cum_t = c3[:nh] + c3[nh:2 * nh] + c3[2 * nh:]
    cols = jnp.concatenate([dt_t, cum_t, jnp.zeros((LANES - 2 * nh, cn), F32)], axis=0).T
    lane = lax.broadcasted_iota(jnp.int32, (1, LANES), 1)
    return dict(cum_t=cum_t, cols=cols, keep=keep, lo=lane < SSD_HEAD_DIM, forward=forward)


def _ssd_mats(act, prep):
    lo = prep["lo"]
    xs = act[:, :SSD_WIDTH]
    bm = act[:, SSD_WIDTH:SSD_WIDTH + LANES]
    cm = act[:, SSD_WIDTH + LANES:]
    bm_b = bm.astype(BF16)
    gmat = []
    bg_b = []
    for g in range(SSD_GROUPS):
        in_group = lo if g == 0 else jnp.logical_not(lo)
        gmat.append(_dot_nt(jnp.where(in_group, cm, 0.0).astype(BF16), bm_b))
        bg_b.append(jnp.where(in_group, bm, 0.0).astype(BF16))
    return dict(prep, xs=xs, cm_b=cm.astype(BF16), gmat=gmat, bg_b=bg_b)


def _ssd_pair(prep, pair, s_ref, y_ref, dsk_ref):
    cn = SSD_CHUNK
    nh = 2 * SSD_HEADS
    forward, cols, cum_t, keep, lo = prep["forward"], prep["cols"], prep["cum_t"], prep["keep"], prep["lo"]

    def col(lane_idx):
        return jnp.broadcast_to(cols[:, lane_idx:lane_idx + 1], (cn, LANES))

    off = 0 if forward else SSD_HEADS
    tot_col = cn - 1 if forward else 0
    g = pair // (SSD_HEADS // 2 // SSD_GROUPS)
    gmat = prep["gmat"][g]
    h0 = off + 2 * pair
    h1 = h0 + 1
    ci0 = col(nh + h0)
    ci1 = col(nh + h1)
    cip = jnp.where(lo, ci0, ci1)
    m0 = (gmat * jnp.exp(jnp.where(keep, ci0 - cum_t[h0:h0 + 1, :], -1e30))).astype(BF16)
    m1 = (gmat * jnp.exp(jnp.where(keep, ci1 - cum_t[h1:h1 + 1, :], -1e30))).astype(BF16)
    xs_p = prep["xs"][:, pair * LANES:(pair + 1) * LANES]
    vp = xs_p * jnp.where(lo, col(h0), col(h1))
    v0 = jnp.where(lo, vp, 0.0).astype(BF16)
    v1 = jnp.where(lo, 0.0, vp).astype(BF16)
    s_old = s_ref[pair]
    y = (_dot(jnp.concatenate([m0, m1], axis=1), jnp.concatenate([v0, v1], axis=0))
         + jnp.exp(cip) * _dot(prep["cm_b"], s_old.astype(BF16)))
    totp = jnp.where(lo, cum_t[h0:h0 + 1, tot_col:tot_col + 1], cum_t[h1:h1 + 1, tot_col:tot_col + 1])
    s_ref[pair] = s_old * jnp.exp(totp) + _dot_tn(prep["bg_b"][g], (vp * jnp.exp(totp - cip)).astype(BF16))
    if forward:
        y = y + dsk_ref[:, pair * LANES:(pair + 1) * LANES] * xs_p
    y_ref[:, pair * LANES:(pair + 1) * LANES] = y.astype(y_ref.dtype)


def _ssd_kernel(*refs, nc, ns, has_init, want_fin):
    refs = list(refs)
    cf_ref, pf_ref, nf_ref, dtf_ref, cb_ref, pb_ref, nb_ref, dtb_ref = refs[:8]
    cw_ref, cbias_ref, dtbias_ref, alog_ref, dsk_ref, shift_ref = refs[8:14]
    pos = 14
    s0_ref = None
    if has_init:
        s0_ref = refs[pos]
        pos += 1
    yf_ref, yb_ref = refs[pos:pos + 2]
    pos += 2
    sfin_ref = None
    if want_fin:
        sfin_ref = refs[pos]
        pos += 1
    ext_ref, act_ref, sf_ref, sb_ref = refs[pos:]
    c = pl.program_id(1)
    c_fwd = c
    c_bwd = nc - 1 - c

    @pl.when(c == 0)
    def _():
        sf_ref[...] = jnp.zeros_like(sf_ref)
        sb_ref[...] = jnp.zeros_like(sb_ref)
        if has_init:
            for s in range(ns):
                for pair in range(SSD_HEADS // 2):
                    rows = _ssd_group_rows(pair)
                    sf_ref[s, pair, rows, :] = s0_ref[s, 0, pair]
                    sb_ref[s, pair, rows, :] = s0_ref[s, 1, pair]

    def scalars():
        out = []
        for s in range(ns):
            out.append(_ssd_scalars(dtf_ref.at[0, s], dtbias_ref, alog_ref, forward=True))
            out.append(_ssd_scalars(dtb_ref.at[0, s], dtbias_ref, alog_ref, forward=False))
        return out

    def streams(preps, acts):
        work = []
        for s in range(ns):
            work.append((_ssd_mats(acts[2 * s], preps[2 * s]), sf_ref.at[s], yf_ref.at[0, s]))
            work.append((_ssd_mats(acts[2 * s + 1], preps[2 * s + 1]), sb_ref.at[s], yb_ref.at[0, s]))
        for pair in range(SSD_HEADS // 2):
            for prep, s_ref, y_ref in work:
                _ssd_pair(prep, pair, s_ref, y_ref, dsk_ref)

    @pl.when(c < nc // 2)
    def _():
        preps = scalars()
        acts = []
        for s in range(ns):
            act_f = _ssd_conv(cf_ref.at[0, s], pf_ref.at[0, s], nf_ref.at[0, s], c_fwd == 0, c_fwd == nc - 1,
                              ext_ref.at[2 * s], cw_ref, cbias_ref, shift_ref)
            act_b = _ssd_conv(cb_ref.at[0, s], pb_ref.at[0, s], nb_ref.at[0, s], c_bwd == 0, c_bwd == nc - 1,
                              ext_ref.at[2 * s + 1], cw_ref, cbias_ref, shift_ref)
            act_ref[s, c_fwd] = act_f
            act_ref[s, c_bwd] = act_b
            acts += [act_f, act_b]
        streams(preps, acts)

    @pl.when(c >= nc // 2)
    def _():
        acts = []
        for s in range(ns):
            acts += [act_ref[s, c_fwd], act_ref[s, c_bwd]]
        streams(scalars(), acts)

    if want_fin:
        @pl.when(c == nc - 1)
        def _():
            for s in range(ns):
                for pair in range(SSD_HEADS // 2):
                    rows = _ssd_group_rows(pair)
                    for d, st_ref in enumerate((sf_ref, sb_ref)):
                        both = st_ref[s, pair, rows, :]
                        sfin_ref[s, d, 2 * pair] = both[:, :SSD_HEAD_DIM]
                        sfin_ref[s, d, 2 * pair + 1] = both[:, SSD_HEAD_DIM:]


def _ssd(xbc, dt, p, s0, batch, want_fin):
    m = xbc.shape[0]
    cn = SSD_CHUNK
    ns = SSD_SEQS
    seq = m // batch
    nc = seq // cn
    assert nc % 2 == 0
    assert batch % ns == 0
    per = cn // HALO
    n_halo = seq // HALO
    has_init = s0 is not None
    npair = SSD_HEADS // 2
    view = lambda a: a.reshape(batch // ns, ns, seq, a.shape[-1])

    def fwd(c):
        return c

    def bwd(c):
        return nc - 1 - c

    def stream_specs(chunk):
        return [pl.BlockSpec((1, ns, cn, SSD_XBC), lambda b, c: (b, 0, chunk(c), 0)),
                pl.BlockSpec((1, ns, HALO, SSD_XBC), lambda b, c: (b, 0, jnp.maximum(chunk(c) * per - 1, 0), 0)),
                pl.BlockSpec((1, ns, HALO, SSD_XBC),
                             lambda b, c: (b, 0, jnp.minimum(chunk(c) * per + per, n_halo - 1), 0)),
                pl.BlockSpec((1, ns, cn, LANES), lambda b, c: (b, 0, chunk(c), 0))]

    in_specs = stream_specs(fwd) + stream_specs(bwd) + [
        _const_spec((HALO, SSD_XBC)), _const_spec((1, SSD_XBC)), _const_spec((2 * SSD_HEADS, cn)),
        _const_spec((2 * SSD_HEADS, cn)), _const_spec((1, SSD_WIDTH)),
        _const_spec(((SSD_CONV - 1) * cn, cn + 2 * HALO))]
    taps = [t for t in range(SSD_CONV) if t != SSD_CONV // 2]
    src = np.concatenate([HALO + np.arange(cn) + (t - SSD_CONV // 2) for t in taps])
    shift = jnp.asarray(src[:, None] == np.arange(cn + 2 * HALO)[None, :], BF16)
    xv, dv = view(xbc), view(dt)
    args = [xv, xv, xv, dv, xv, xv, xv, dv, p["conv_w"], p["conv_b"], p["dt_bias"], p["a_log"], p["d_skip"], shift]
    state_block = (ns, 2, npair, SSD_STATE, LANES)
    if has_init:
        in_specs.append(pl.BlockSpec(state_block, lambda b, c: (b, 0, 0, 0, 0)))
        args.append(s0)
    out_specs = [pl.BlockSpec((1, ns, cn, SSD_WIDTH), lambda b, c: (b, 0, fwd(c), 0)),
                 pl.BlockSpec((1, ns, cn, SSD_WIDTH), lambda b, c: (b, 0, bwd(c), 0))]
    out_shape = [jax.ShapeDtypeStruct((batch // ns, ns, seq, SSD_WIDTH), BF16)] * 2
    if want_fin:
        fin_block = (ns, 2, SSD_HEADS, SSD_STATE, SSD_HEAD_DIM)
        out_specs.append(pl.BlockSpec(fin_block, lambda b, c: (b, 0, 0, 0, 0)))
        out_shape.append(jax.ShapeDtypeStruct((batch,) + fin_block[1:], F32))
    out = pl.pallas_call(
        functools.partial(_ssd_kernel, nc=nc, ns=ns, has_init=has_init, want_fin=want_fin),
        grid=(batch // ns, nc),
        in_specs=in_specs,
        out_specs=out_specs,
        out_shape=out_shape,
        scratch_shapes=[pltpu.VMEM((2 * ns, cn + 2 * HALO, SSD_XBC), F32), pltpu.VMEM((ns, nc, cn, SSD_XBC), F32),
                        pltpu.VMEM((ns, npair, LANES, LANES), F32), pltpu.VMEM((ns, npair, LANES, LANES), F32)],
        compiler_params=_params(2),
        name="ssd",
    )(*args)
    return [out[0].reshape(m, SSD_WIDTH), out[1].reshape(m, SSD_WIDTH)] + list(out[2:])


def _ssd_state_to_pairs(s):
    b = s.shape[0]
    npair = SSD_HEADS // 2
    s = s.reshape(b, 2, npair, 2, SSD_STATE, SSD_HEAD_DIM).transpose(0, 1, 2, 4, 3, 5)
    return s.reshape(b, 2, npair, SSD_STATE, 2 * SSD_HEAD_DIM)


def _l1_in_kernel(*refs, use_rope):
    if use_rope:
        x_ref, nw_ref, sh_ref, sc_ref, w_ref, cos_ref, sin_ref, q_ref, k_ref, v_ref, g_ref = refs
    else:
        x_ref, nw_ref, sh_ref, sc_ref, w_ref, q_ref, k_ref, v_ref, g_ref = refs
    h = (_rms(x_ref[...]) * nw_ref[...] * (1.0 + sc_ref[0]) + sh_ref[0]).astype(BF16)
    half = RET_DK // 2

    def rope(t):
        if not use_rope:
            return t
        cos = cos_ref[...]
        sin = sin_ref[...]
        parts = []
        for hd in range(RET_HEADS):
            x1 = t[:, hd * RET_DK:hd * RET_DK + half]
            x2 = t[:, hd * RET_DK + half:(hd + 1) * RET_DK]
            parts += [x1 * cos - x2 * sin, x2 * cos + x1 * sin]
        return jnp.concatenate(parts, axis=1)

    o1 = RET_QK_WIDTH
    o2 = 2 * RET_QK_WIDTH
    o3 = o2 + RET_V_WIDTH
    q_ref[...] = rope(_dot(h, w_ref[:, :o1])).astype(BF16)
    k_ref[...] = (rope(_dot(h, w_ref[:, o1:o2])) * (RET_DK ** -0.5)).astype(k_ref.dtype)
    v_ref[...] = _dot(h, w_ref[:, o2:o3]).astype(BF16)
    g_ref[...] = _silu(_dot(h, w_ref[:, o3:])).astype(g_ref.dtype)


def _l1_in(x, mod, rows_per_mod, p, rope, tm):
    m = x.shape[0]
    use_rope = rope is not None
    row = lambda w: pl.BlockSpec((tm, w), lambda i: (i, 0))
    n = 2 * RET_QK_WIDTH + 2 * RET_V_WIDTH
    in_specs = ([row(D_MODEL), _const_spec((1, D_MODEL))] + _mod_specs(tm, rows_per_mod, (0, 1))
                + [_resident_spec((D_MODEL, n))])
    args = [x, p["norm_mix"], mod, mod, p["w_in"]]
    if use_rope:
        rows = rope[0].shape[0]
        in_specs += [pl.BlockSpec((tm, LANES), lambda i: (i % (rows // tm), 0))] * 2
        args += list(rope)
    widths = [(RET_QK_WIDTH, BF16), (RET_QK_WIDTH, BF16), (RET_V_WIDTH, BF16), (RET_V_WIDTH, BF16)]
    return pl.pallas_call(
        functools.partial(_l1_in_kernel, use_rope=use_rope),
        grid=(m // tm,),
        in_specs=in_specs,
        out_specs=[row(w) for w, _ in widths],
        out_shape=[jax.ShapeDtypeStruct((m, w), dt) for w, dt in widths],
        compiler_params=_params(1),
        name="l1_in",
    )(*args)


def _ret_kernel(*refs, cn, nc, has_init, want_fin):
    stateless = (not has_init) and nc == 1
    refs = list(refs)
    dec_ref, qf_ref, kf_ref, vf_ref = refs[:4]
    pos = 4
    if stateless:
        qb_ref, kb_ref, vb_ref = qf_ref, kf_ref, vf_ref
    else:
        qb_ref, kb_ref, vb_ref = refs[pos:pos + 3]
        pos += 3
    s0_ref = None
    if has_init:
        s0_ref = refs[pos]
        pos += 1
    yf_ref = refs[pos]
    pos += 1
    yb_ref = None
    if not stateless:
        yb_ref = refs[pos]
        pos += 1
    sfin_ref = None
    if want_fin:
        sfin_ref = refs[pos]
        pos += 1
    decay_ref, rowdec_ref, sf_ref, sb_ref = refs[pos:]
    c = pl.program_id(1)
    log_g = -jnp.exp(dec_ref[...])

    @pl.when((pl.program_id(0) == 0) & (c == 0))
    def _():
        ii = lax.broadcasted_iota(jnp.int32, (cn, cn), 0)
        jj = lax.broadcasted_iota(jnp.int32, (cn, cn), 1)
        dist = (ii - jj).astype(F32)
        ri = lax.broadcasted_iota(jnp.int32, (cn, LANES), 0).astype(F32)
        for hd in range(RET_HEADS):
            gf = log_g[0:1, hd:hd + 1]
            gb = log_g[1:2, hd:hd + 1]
            decay_ref[hd] = (jnp.where(dist >= 0, jnp.exp(gf * jnp.maximum(dist, 0.0)), 0.0)
                             + jnp.where(dist <= 0, jnp.exp(gb * jnp.maximum(-dist, 0.0)), 0.0))
            rowdec_ref[hd, 0] = jnp.exp(gf * (ri + 1.0))
            rowdec_ref[hd, 1] = jnp.exp(gf * (cn - 1.0 - ri))
            rowdec_ref[hd, 2] = jnp.exp(gb * (cn - ri))
            rowdec_ref[hd, 3] = jnp.exp(gb * ri)

    def rowdec(hd, which, width):
        return jnp.concatenate([rowdec_ref[hd, which]] * (width // LANES), axis=1)

    if not stateless:
        @pl.when(c == 0)
        def _():
            if has_init:
                sf_ref[...] = s0_ref[0, 0]
                sb_ref[...] = s0_ref[0, 1]
            else:
                sf_ref[...] = jnp.zeros_like(sf_ref)
                sb_ref[...] = jnp.zeros_like(sb_ref)

    for hd in range(RET_HEADS):
        gf = log_g[0:1, hd:hd + 1]
        gb = log_g[1:2, hd:hd + 1]
        qs = slice(hd * RET_DK, (hd + 1) * RET_DK)
        qf = qf_ref[:, qs]
        kf = kf_ref[:, qs]
        qb = qb_ref[:, qs]
        kb = kb_ref[:, qs]
        scores = (_dot_nt(qf, kf.astype(BF16)) * decay_ref[hd]).astype(BF16)
        kw_f = (kf * rowdec(hd, 1, RET_DK)).astype(BF16)
        kw_b = (kb * rowdec(hd, 3, RET_DK)).astype(BF16)
        for t in range(RET_DV // RET_V_TILE):
            cols = slice(t * RET_V_TILE, (t + 1) * RET_V_TILE)
            vs = slice(hd * RET_DV + t * RET_V_TILE, hd * RET_DV + (t + 1) * RET_V_TILE)
            v = vf_ref[:, vs]
            y = _dot(scores, v)
            upd_f = _dot_tn(kw_f, v)
            if stateless:
                new_f = upd_f
            else:
                s_old = sf_ref[hd, :, cols]
                y = y + rowdec(hd, 0, RET_V_TILE) * _dot(qf, s_old.astype(BF16))
                new_f = s_old * jnp.exp(gf * cn) + upd_f
                sf_ref[hd, :, cols] = new_f
            yf_ref[:, vs] = y.astype(yf_ref.dtype)
            v = vb_ref[:, vs]
            upd_b = _dot_tn(kw_b, v)
            if stateless:
                new_b = upd_b
            else:
                s_old = sb_ref[hd, :, cols]
                yb_ref[:, vs] = (rowdec(hd, 2, RET_V_TILE) * _dot(qb, s_old.astype(BF16))).astype(yb_ref.dtype)
                new_b = s_old * jnp.exp(gb * cn) + upd_b
                sb_ref[hd, :, cols] = new_b
            if want_fin:
                if stateless:
                    sfin_ref[0, 0, hd, :, cols] = new_f
                    sfin_ref[0, 1, hd, :, cols] = new_b
                else:
                    @pl.when(c == nc - 1)
                    def _(new_f=new_f, new_b=new_b, hd=hd, cols=cols):
                        sfin_ref[0, 0, hd, :, cols] = new_f
                        sfin_ref[0, 1, hd, :, cols] = new_b


def _retention(q, k, v, decay, s0, batch, cn, want_fin):
    m = q.shape[0]
    nc = m // batch // cn
    has_init = s0 is not None
    stateless = (not has_init) and nc == 1

    def fwd(b, c):
        return b * nc + c

    def bwd(b, c):
        return b * nc + nc - 1 - c

    def stream_specs(chunk):
        return [pl.BlockSpec((cn, RET_QK_WIDTH), lambda b, c: (chunk(b, c), 0)),
                pl.BlockSpec((cn, RET_QK_WIDTH), lambda b, c: (chunk(b, c), 0)),
                pl.BlockSpec((cn, RET_V_WIDTH), lambda b, c: (chunk(b, c), 0))]

    in_specs = [_const_spec((8, LANES))] + stream_specs(fwd)
    args = [decay, q, k, v]
    if not stateless:
        in_specs += stream_specs(bwd)
        args += [q, k, v]
    state_block = (1, 2, RET_HEADS, RET_DK, RET_DV)
    if has_init:
        in_specs.append(pl.BlockSpec(state_block, lambda b, c: (b, 0, 0, 0, 0)))
        args.append(s0)
    out_specs = [pl.BlockSpec((cn, RET_V_WIDTH), lambda b, c: (fwd(b, c), 0))]
    if not stateless:
        out_specs.append(pl.BlockSpec((cn, RET_V_WIDTH), lambda b, c: (bwd(b, c), 0)))
    out_shape = [jax.ShapeDtypeStruct((m, RET_V_WIDTH), BF16)] * len(out_specs)
    if want_fin:
        out_specs.append(pl.BlockSpec(state_block, lambda b, c: (b, 0, 0, 0, 0)))
        out_shape.append(jax.ShapeDtypeStruct((batch,) + state_block[1:], F32))
    return pl.pallas_call(
        functools.partial(_ret_kernel, cn=cn, nc=nc, has_init=has_init, want_fin=want_fin),
        grid=(batch, nc),
        in_specs=in_specs,
        out_specs=out_specs,
        out_shape=out_shape,
        scratch_shapes=[pltpu.VMEM((RET_HEADS, cn, cn), F32), pltpu.VMEM((RET_HEADS, 4, cn, LANES), F32)]
        + [pltpu.VMEM((RET_HEADS, RET_DK, RET_DV), F32)] * 2,
        compiler_params=_params(2),
        name="retention",
    )(*args)


def _post_kernel(*refs, mixer, final):
    refs = list(refs)
    x_ref, g1_ref, nw_ref, sh2_ref, sc2_ref, g2_ref, wout_ref, wg_ref, wu_ref, wd_ref = refs[:10]
    pos = 10
    fn_ref = None
    if final:
        fn_ref = refs[pos]
        pos += 1
    if mixer == "ab":
        att_ref, yf_ref, yb_ref, z_ref, gain_ref, o_ref = refs[pos:]
        y = (yf_ref[...].astype(F32) + yb_ref[...].astype(F32)) * _silu(z_ref[...].astype(F32))
        y = _rms(y) * gain_ref[...]
        mix = _dot(jnp.concatenate([att_ref[...], y.astype(BF16)], axis=1), wout_ref[...])
    else:
        parts = refs[pos:-3]
        gate_ref, gain_ref, o_ref = refs[-3:]
        mix = None
        for hd in range(RET_HEADS):
            vs = slice(hd * RET_DV, (hd + 1) * RET_DV)
            y = parts[0][:, vs].astype(F32)
            for extra in parts[1:]:
                y = y + extra[:, vs].astype(F32)
            y = _rms(y) * gain_ref[:, vs]
            part = _dot((gate_ref[:, vs].astype(F32) * y).astype(BF16), wout_ref[vs, :])
            mix = part if mix is None else mix + part
    x1 = x_ref[...] + g1_ref[0] * mix
    h = (_rms(x1) * nw_ref[...] * (1.0 + sc2_ref[0]) + sh2_ref[0]).astype(BF16)
    act = (_silu(_dot(h, wg_ref[...])) * _dot(h, wu_ref[...])).astype(BF16)
    x2 = x1 + g2_ref[0] * _dot(act, wd_ref[...])
    if final:
        x2 = _rms(x2) * fn_ref[...]
    o_ref[...] = x2


def _post(x, mod, rows_per_mod, p, mixer, mix_inputs, gain, final_norm, tm):
    m = x.shape[0]
    row = lambda w: pl.BlockSpec((tm, w), lambda i: (i, 0))
    mixw = p["w_out"].shape[0]
    (g1,) = _mod_specs(tm, rows_per_mod, (2,))
    sh2, sc2, g2 = _mod_specs(tm, rows_per_mod, (3, 4, 5))
    in_specs = [row(D_MODEL), g1, _const_spec((1, D_MODEL)), sh2, sc2, g2,
                _resident_spec((mixw, D_MODEL)), _resident_spec((D_MODEL, D_FF)), _resident_spec((D_MODEL, D_FF)),
                _resident_spec((D_FF, D_MODEL))]
    args = [x, mod, p["norm_ffn"], mod, mod, mod, p["w_out"], p["w_gate"], p["w_up"], p["w_down"]]
    final = final_norm is not None
    if final:
        in_specs.append(_const_spec((1, D_MODEL)))
        args.append(final_norm)
    in_specs += [row(a.shape[1]) for a in mix_inputs] + [_const_spec(gain.shape)]
    args += list(mix_inputs) + [gain]
    return pl.pallas_call(
        functools.partial(_post_kernel, mixer=mixer, final=final),
        grid=(m // tm,),
        in_specs=in_specs,
        out_specs=row(D_MODEL),
        out_shape=jax.ShapeDtypeStruct((m, D_MODEL), F32),
        compiler_params=_params(1),
        name="post_" + mixer,
    )(*args)


def _axial_angles(n_tokens, dim):
    rows = n_tokens // GRID_W
    row = np.repeat(np.arange(rows), GRID_W).astype(np.float64)
    col = np.tile(np.arange(GRID_W), rows).astype(np.float64)
    n_freq = dim // 4
    inv = ROPE_THETA ** (-np.arange(n_freq, dtype=np.float64) / n_freq)
    return np.concatenate([row[:, None] * inv, col[:, None] * inv], axis=-1)


def _head_mean_matrix(width, head):
    idx = jnp.arange(width) // head
    return jnp.where(idx[:, None] == idx[None, :], 1.0 / head, 0.0).astype(BF16)


def _rows_bcast(v, width):
    return jnp.broadcast_to(v.reshape(-1, 1), (v.size, width))


def _trunk(x, mods, rows_per_mod, p0, p1, final_norm, rope_att, rope_ret, caches, seq):
    m = x.shape[0]
    batch = m // seq
    sample = caches is not None
    tm = ROW_TILE
    l0 = _l0_in(x, mods[0], rows_per_mod, p0, rope_att, tm, None if sample else seq)
    q, ka, va, z, xbc, dt = l0[:6]
    s0_ssd = None
    s0_ret = None
    kv_cache = None
    if sample:
        cache_k, cache_v, state_ssd, state_ret = caches
        ck = cache_k.astype(BF16).transpose(0, 2, 1, 3)
        cv = cache_v.astype(BF16).transpose(0, 2, 1, 3)
        ones_col = jnp.zeros(cv.shape[:3] + (LANES - ATT_HEAD_DIM,), BF16).at[..., 0].set(1.0)
        kv_cache = (ck, jnp.concatenate([cv, ones_col], axis=-1))
        s0_ssd = _ssd_state_to_pairs(state_ssd)
        s0_ret = state_ret
    att = _attention(q.reshape(batch, seq, ATT_WIDTH), ka, va, kv_cache, ATT_KEY_CHUNK, ATT_SUB_ROWS)
    att = att.reshape(m, ATT_WIDTH)
    ssd_out = _ssd(xbc, dt, p0, s0_ssd, batch, want_fin=not sample)
    x = _post(x, mods[0], rows_per_mod, p0, "ab", [att, ssd_out[0], ssd_out[1], z], p0["ssd_gain"], None, tm)
    q1, k1, v1, g1 = _l1_in(x, mods[1], rows_per_mod, p1, rope_ret, tm)
    ret_out = _retention(q1, k1, v1, p1["decay"], s0_ret, batch, RET_CHUNK, want_fin=not sample)
    y_parts = ret_out if sample else ret_out[:-1]
    y = _post(x, mods[1], rows_per_mod, p1, "c", list(y_parts) + [g1], p1["ret_gain"], final_norm, tm)
    if sample:
        return y, None
    new_k = l0[6].transpose(0, 3, 1, 2)
    new_v = l0[7].transpose(0, 3, 1, 2)
    return y, (new_k, new_v, ssd_out[2], ret_out[-1])


def kernel(x_prompt, x_sample, c, cache_k0, cache_v0, state_ssd0, state_ret1, c_ctx, l0_w_ada, l0_b_ada, l0_norm_mix, l0_norm_ffn, l0_w_in, l0_w_out, l0_q_gain, l0_k_gain, l0_conv_w, l0_conv_b, l0_dt_bias, l0_a_log, l0_d_skip, l0_ssd_gain, l0_w_gate, l0_w_up, l0_w_down, l1_w_ada, l1_b_ada, l1_norm_mix, l1_norm_ffn, l1_w_in, l1_w_out, l1_decay, l1_ret_gain, l1_w_gate, l1_w_up, l1_w_down, final_norm):
    b_ctx, seq_ctx, d = x_prompt.shape
    b_lat, seq_lat, _ = x_sample.shape
    assert d == D_MODEL and l0_w_in.shape == (D_MODEL, L0_IN) and l0_w_gate.shape == (D_MODEL, D_FF)
    row = lambda v: v.reshape(1, -1)

    (w0_in, w0_out, w0_gate, w0_up, w0_down, w1_in, w1_out, w1_gate, w1_up, w1_down) = _cast_weights(
        [l0_w_in, l0_w_out, l0_w_gate, l0_w_up, l0_w_down, l1_w_in, l1_w_out, l1_w_gate, l1_w_up, l1_w_down])
    p0 = dict(
        norm_mix=row(l0_norm_mix), norm_ffn=row(l0_norm_ffn),
        w_in=w0_in, w_out=w0_out, w_gate=w0_gate, w_up=w0_up, w_down=w0_down,
        q_gain=row(jnp.tile(l0_q_gain, ATT_HEADS)), k_gain=row(jnp.tile(l0_k_gain, ATT_KV_HEADS)),
        pq=_head_mean_matrix(ATT_WIDTH, ATT_HEAD_DIM), pk=_head_mean_matrix(ATT_KV_WIDTH, ATT_HEAD_DIM),
        conv_w=jnp.pad(l0_conv_w, ((0, HALO - SSD_CONV), (0, 0))), conv_b=row(l0_conv_b),
        dt_bias=_rows_bcast(l0_dt_bias, SSD_CHUNK), a_log=_rows_bcast(l0_a_log, SSD_CHUNK),
        d_skip=row(jnp.repeat(l0_d_skip, SSD_HEAD_DIM)), ssd_gain=row(l0_ssd_gain),
    )
    p1 = dict(
        norm_mix=row(l1_norm_mix), norm_ffn=row(l1_norm_ffn),
        w_in=w1_in, w_out=w1_out, w_gate=w1_gate, w_up=w1_up, w_down=w1_down,
        decay=jnp.pad(l1_decay, ((0, 8 - l1_decay.shape[0]), (0, LANES - l1_decay.shape[1]))),
        ret_gain=row(l1_ret_gain),
    )
    fnorm = row(final_norm)

    n_cond = 8
    conds = jnp.concatenate([c_ctx[None, :], c, jnp.zeros((n_cond - 1 - b_lat, d), F32)], axis=0)
    mod0 = _ada(conds, l0_w_ada, l0_b_ada)
    mod1 = _ada(conds, l1_w_ada, l1_b_ada)
    mods_ctx = [mod[0:1].reshape(1, 1, 6 * d) for mod in (mod0, mod1)]
    mods_lat = [mod[1:1 + b_lat].reshape(b_lat, 1, 6 * d) for mod in (mod0, mod1)]

    m_ctx = b_ctx * seq_ctx
    y_prompt, ctx = _trunk(x_prompt.reshape(m_ctx, d), mods_ctx, m_ctx, p0, p1, fnorm, None, None, None, seq_ctx)
    new_k0, new_v0, new_ssd0, new_ret1 = ctx
    ang = _axial_angles(seq_lat, ATT_HEAD_DIM)
    cos, sin = np.cos(ang), np.sin(ang)
    reps = LANES // ATT_HEAD_DIM
    rope_att = (jnp.asarray(np.tile(np.concatenate([cos, cos], axis=1), (1, reps)), F32),
                jnp.asarray(np.tile(np.concatenate([-sin, sin], axis=1), (1, reps)), F32))
    ang = _axial_angles(seq_lat, RET_DK)
    rope_ret = (jnp.asarray(np.cos(ang), F32), jnp.asarray(np.sin(ang), F32))
    caches = (cache_k0, cache_v0, state_ssd0, state_ret1)
    y_sample, _ = _trunk(x_sample.reshape(b_lat * seq_lat, d), mods_lat, seq_lat, p0, p1, fnorm, rope_att, rope_ret,
                         caches, seq_lat)
    return (y_prompt.reshape(b_ctx, seq_ctx, d), y_sample.reshape(b_lat, seq_lat, d),
            new_k0, new_v0, new_ssd0, new_ret1)
```
